```python
import math
import jax
import jax.numpy as jnp
from jax import lax
import numpy as np

D_MODEL = 4096
BATCH = 8
SEQ = 2048
DEPTH = 2

CHUNK = 64
N_BRANCHES = 4
BRANCH_WIDTH = D_MODEL // N_BRANCHES
DK_A = 128
H_A = BRANCH_WIDTH // DK_A
CONV_A = 4
DH_B = 128
H_B = BRANCH_WIDTH // DH_B
DC_B = 256
H_IDX = 8
D_IDX = 64
DSA_TOPK = 256
Q_BLOCK = 128
N_BUCKETS = 32
T5_MAX_DISTANCE = 128
CONV_C = 3
D_INNER = BRANCH_WIDTH
P_D = 64
H_D = D_INNER // P_D
N_GROUPS = 2
D_STATE = 128
CONV_D = 4
D_FF = 11008
N_EXPERTS = 8
TOP_K = 2
D_EXPERT = 4096

MIX_SPLITS = (
    BRANCH_WIDTH, BRANCH_WIDTH, BRANCH_WIDTH, BRANCH_WIDTH, H_A, H_A,
    H_B * DH_B, DC_B, H_IDX * D_IDX, D_IDX, H_IDX,
    BRANCH_WIDTH, BRANCH_WIDTH, BRANCH_WIDTH,
    D_INNER, D_INNER + 2 * N_GROUPS * D_STATE, H_D,
)
MIX_COLS = sum(MIX_SPLITS)
IN_COLS = MIX_COLS + N_BRANCHES * D_MODEL

kernel_name = 'hybrid_gated_branch_encoder'


def rms_norm(x, g, eps=1e-6):
    xf = x.astype(jnp.float32)
    y = xf * lax.rsqrt(jnp.mean(xf * xf, axis=-1, keepdims=True) + eps)
    return (y * g.astype(jnp.float32)).astype(x.dtype)


def l2_normalize(x, eps=1e-6):
    return x * lax.rsqrt(jnp.sum(x * x, axis=-1, keepdims=True) + eps)


def causal_dwconv(x, w):
    k, c = w.shape
    return lax.conv_general_dilated(x, w[:, None, :], window_strides=(1,), padding=[(k - 1, 0)],
                                    dimension_numbers=('NWC', 'WIO', 'NWC'), feature_group_count=c)


def to_chunks(t):
    bsz, seq = t.shape[:2]
    t = t.reshape(bsz, seq // CHUNK, CHUNK, *t.shape[2:])
    return jnp.moveaxis(t, 3, 1)


def from_chunks(t):
    t = jnp.moveaxis(t, 1, 3)
    s = t.shape
    return t.reshape(s[0], s[1] * s[2], *s[3:])


def segment_decay(cum):
    tril = jnp.tril(jnp.ones((CHUNK, CHUNK), bool))
    diff = cum[..., :, None] - cum[..., None, :]
    return jnp.where(tril, jnp.exp(jnp.where(tril, diff, 0.0)), 0.0)


def t5_bucket(rel):
    half = N_BUCKETS // 2
    max_exact = half // 2
    n = jnp.abs(rel)
    log_ratio = jnp.log(jnp.maximum(n, 1).astype(jnp.float32) / max_exact) / math.log(T5_MAX_DISTANCE / max_exact)
    large = jnp.minimum(max_exact + (log_ratio * (half - max_exact)).astype(jnp.int32), half - 1)
    return jnp.where(rel > 0, half, 0) + jnp.where(n < max_exact, n, large)


def gated_deltanet(q, k, v, z, a, b, conv_w, a_log, dt_bias, norm_g):
    bsz, seq, _ = q.shape
    out_dtype = z.dtype
    f32 = jnp.float32
    qkv = jax.nn.silu(causal_dwconv(jnp.concatenate([q, k, v], axis=-1), conv_w)).astype(f32)
    qkv = qkv.reshape(bsz, seq, 3, H_A, DK_A)
    qh = l2_normalize(qkv[:, :, 0]) * DK_A ** -0.5
    kh = l2_normalize(qkv[:, :, 1])
    vh = qkv[:, :, 2]
    beta = jax.nn.sigmoid(b.astype(f32))
    g = -jnp.exp(a_log.astype(f32)) * jax.nn.softplus(a.astype(f32) + dt_bias.astype(f32))
    qc, kc, vc = to_chunks(qh), to_chunks(kh), to_chunks(vh)
    bc = to_chunks(beta)[..., None]
    gc = jnp.cumsum(to_chunks(g), axis=-1)
    decay = segment_decay(gc)
    kb = kc * bc
    strict = jnp.tril(jnp.ones((CHUNK, CHUNK), bool), -1)
    eye = jnp.eye(CHUNK, dtype=f32)
    lower = jnp.where(strict, jnp.einsum('bhncd,bhnsd->bhncs', kb, kc) * decay, 0.0) + eye
    rhs = jnp.concatenate([vc * bc, kb * jnp.exp(gc)[..., None]], axis=-1)
    sol = lax.linalg.triangular_solve(lower, rhs, left_side=True, lower=True, unit_diagonal=True)
    u, w = sol[..., :DK_A], sol[..., DK_A:]
    intra = jnp.einsum('bhncd,bhnsd->bhncs', qc, kc) * decay
    q_dec = qc * jnp.exp(gc)[..., None]
    k_dec = kc * jnp.exp(gc[..., -1:] - gc)[..., None]
    g_last = jnp.exp(gc[..., -1])

    def step(state, inp):
        u_n, w_n, qd_n, kd_n, a_n, gl_n = inp
        v_new = u_n - jnp.einsum('bhck,bhkv->bhcv', w_n, state)
        out = jnp.einsum('bhck,bhkv->bhcv', qd_n, state) + jnp.einsum('bhcs,bhsv->bhcv', a_n, v_new)
        state = state * gl_n[..., None, None] + jnp.einsum('bhck,bhcv->bhkv', kd_n, v_new)
        return state, out

    xs = tuple(jnp.moveaxis(t, 2, 0) for t in (u, w, q_dec, k_dec, intra, g_last))
    s0 = jnp.zeros((bsz, H_A, DK_A, DK_A), f32)
    _, o = lax.scan(step, s0, xs)
    o = from_chunks(jnp.moveaxis(o, 0, 2))
    o = rms_norm(o, norm_g) * jax.nn.silu(z.reshape(bsz, seq, H_A, DK_A).astype(f32))
    return o.reshape(bsz, seq, BRANCH_WIDTH).astype(out_dtype)


def dsa_attention(q, ckv, q_idx, k_idx, w_idx, ckv_norm_g, w_uk, q_norm_g, k_norm_g, w_uv, rel_bias):
    bsz, seq, _ = q.shape
    f32 = jnp.float32
    topk = min(DSA_TOPK, seq // 4)
    c = rms_norm(ckv, ckv_norm_g)
    kk = rms_norm(c @ w_uk, k_norm_g)
    qh = rms_norm(q.reshape(bsz, seq, H_B, DH_B), q_norm_g)
    kv = jnp.concatenate([kk, c], axis=-1)
    qi = q_idx.reshape(bsz, seq, H_IDX, D_IDX).astype(f32)
    ki = k_idx.astype(f32)
    wi = w_idx.astype(f32) * H_IDX ** -0.5
    key_chunk = jnp.arange(seq) // CHUNK

    def one_block(blk):
        t0 = blk * Q_BLOCK
        tq = t0 + jnp.arange(Q_BLOCK)
        q_b = lax.dynamic_slice_in_dim(qh, t0, Q_BLOCK, axis=1)
        qi_b = lax.dynamic_slice_in_dim(qi, t0, Q_BLOCK, axis=1)
        wi_b = lax.dynamic_slice_in_dim(wi, t0, Q_BLOCK, axis=1)
        admissible = key_chunk[None, :] <= (tq // CHUNK)[:, None]
        head_scores = jax.nn.relu(jnp.einsum('bqhd,bsd->bqhs', qi_b, ki)) * D_IDX ** -0.5
        idx_scores = jnp.einsum('bqh,bqhs->bqs', wi_b, head_scores)
        idx_scores = jnp.where(admissible[None], idx_scores, -jnp.inf)
        top_scores, sel = lax.top_k(idx_scores, topk)
        valid = jnp.isfinite(top_scores)
        gathered = jax.vmap(lambda arr, i: arr[i])(kv, sel)
        k_sel, c_sel = gathered[..., :DH_B], gathered[..., DH_B:]
        logits = jnp.einsum('bqhd,bqkd->bqhk', q_b, k_sel).astype(f32) * DH_B ** -0.5
        bias = rel_bias[t5_bucket(sel - tq[None, :, None])].astype(f32)
        logits = logits + jnp.moveaxis(bias, 3, 2)
        logits = jnp.where(valid[:, :, None, :], logits, -jnp.inf)
        p = jax.nn.softmax(logits, axis=-1).astype(c.dtype)
        o_lat = jnp.einsum('bqhk,bqkc->bqhc', p, c_sel)
        return jnp.einsum('bqhc,hcd->bqhd', o_lat, w_uv).reshape(bsz, Q_BLOCK, H_B * DH_B)

    out = lax.map(one_block, jnp.arange(seq // Q_BLOCK))
    return jnp.moveaxis(out, 0, 1).reshape(bsz, seq, H_B * DH_B)


def short_conv_mixer(b_gate, c_gate, h, conv_w):
    return b_gate * causal_dwconv(c_gate * h, conv_w)


def mamba2_ssd(z, xbc, dt, conv_w, conv_b, a_log, dt_bias, d_skip, norm_g):
    bsz, seq, _ = z.shape
    out_dtype = z.dtype
    f32 = jnp.float32
    xbc = jax.nn.silu(causal_dwconv(xbc, conv_w) + conv_b).astype(f32)
    xs = xbc[..., :D_INNER].reshape(bsz, seq, H_D, P_D)
    bm = xbc[..., D_INNER:D_INNER + N_GROUPS * D_STATE].reshape(bsz, seq, N_GROUPS, D_STATE)
    cm = xbc[..., D_INNER + N_GROUPS * D_STATE:].reshape(bsz, seq, N_GROUPS, D_STATE)
    bm = jnp.repeat(bm, H_D // N_GROUPS, axis=2)
    cm = jnp.repeat(cm, H_D // N_GROUPS, axis=2)
    dt = jax.nn.softplus(dt.astype(f32) + dt_bias.astype(f32))
    acs = jnp.cumsum(to_chunks(dt * -jnp.exp(a_log.astype(f32))), axis=-1)
    xc = to_chunks(xs * dt[..., None])
    bc, cc = to_chunks(bm), to_chunks(cm)
    scores = jnp.einsum('bhnld,bhnsd->bhnls', cc, bc) * segment_decay(acs)
    y_diag = jnp.einsum('bhnls,bhnsp->bhnlp', scores, xc)
    states = jnp.einsum('bhnl,bhnld,bhnlp->bhnpd', jnp.exp(acs[..., -1:] - acs), bc, xc)

    def step(state, inp):
        st_n, dec_n = inp
        return state * dec_n[..., None, None] + st_n, state

    s0 = jnp.zeros((bsz, H_D, P_D, D_STATE), f32)
    _, prev = lax.scan(step, s0, (jnp.moveaxis(states, 2, 0), jnp.moveaxis(jnp.exp(acs[..., -1]), 2, 0)))
    prev = jnp.moveaxis(prev, 0, 2)
    y_off = jnp.einsum('bhnld,bhnpd->bhnlp', cc * jnp.exp(acs)[..., None], prev)
    y = from_chunks(y_diag + y_off) + d_skip.astype(f32)[:, None] * xs
    y = y.reshape(bsz, seq, D_INNER) * jax.nn.silu(z.astype(f32))
    y = rms_norm(y.reshape(bsz, seq, N_GROUPS, D_INNER // N_GROUPS), norm_g.reshape(N_GROUPS, D_INNER // N_GROUPS))
    return y.reshape(bsz, seq, D_INNER).astype(out_dtype)


def hybrid_mixer(xn, w_in, conv_a_w, a_log_a, dt_bias_a, out_norm_a_g, ckv_norm_g, w_uk, q_norm_b_g,
                 k_norm_b_g, w_uv, rel_bias, conv_c_w, conv_d_w, conv_d_b, a_log_d, dt_bias_d, d_skip,
                 out_norm_d_g, w_up, w_out):
    proj = xn @ w_in[:, :MIX_COLS]
    (a_q, a_k, a_v, a_z, a_a, a_b, b_q, b_ckv, b_qi, b_ki, b_wi, c_b, c_c, c_h, d_z, d_xbc, d_dt) = jnp.split(
        proj, np.cumsum(MIX_SPLITS)[:-1], axis=-1)
    branches = (
        gated_deltanet(a_q, a_k, a_v, a_z, a_a, a_b, conv_a_w, a_log_a, dt_bias_a, out_norm_a_g),
        dsa_attention(b_q, b_ckv, b_qi, b_ki, b_wi, ckv_norm_g, w_uk, q_norm_b_g, k_norm_b_g, w_uv, rel_bias),
        short_conv_mixer(c_b, c_c, c_h, conv_c_w),
        mamba2_ssd(d_z, d_xbc, d_dt, conv_d_w, conv_d_b, a_log_d, dt_bias_d, d_skip, out_norm_d_g),
    )
    merged = None
    for i, o in enumerate(branches):
        lo = MIX_COLS + i * D_MODEL
        gate = jax.nn.sigmoid(xn @ w_in[:, lo:lo + D_MODEL])
        term = gate * (o @ w_up[i])
        merged = term if merged is None else merged + term
    return merged @ w_out


def swiglu(x, w1, w3, w2):
    return (jax.nn.silu(x @ w1) * (x @ w3)) @ w2


def moe_swiglu(x, w_router, w1, w3, w2):
    logits = (x @ w_router).astype(jnp.float32)
    top_logits, top_idx = lax.top_k(logits, TOP_K)
    top_w = jax.nn.softmax(top_logits, axis=-1)
    gate = jnp.einsum('btk,btke->bte', top_w, jax.nn.one_hot(top_idx, N_EXPERTS, dtype=jnp.float32)).astype(x.dtype)
    out = gate[..., 0:1] * swiglu(x, w1[0], w3[0], w2[0])
    for e in range(1, N_EXPERTS):
        out = out + gate[..., e:e + 1] * swiglu(x, w1[e], w3[e], w2[e])
    return out


def setup_inputs(seed: int = 0) -> dict:
    key = jax.random.key(seed)
    keys = iter(jax.random.split(key, 40))
    f32 = jnp.float32
    n_dense = (DEPTH + 1) // 2
    n_moe = DEPTH // 2
    xbc_width = D_INNER + 2 * N_GROUPS * D_STATE

    def normal(shape, scale):
        return jax.random.normal(next(keys), shape, f32) * scale

    def gain(shape):
        return 1.0 + 0.1 * jax.random.normal(next(keys), shape, f32)

    def a_log_init(shape):
        return jnp.log(jax.random.uniform(next(keys), shape, f32, 1.0, 16.0))

    def dt_bias_init(shape):
        u = jax.random.uniform(next(keys), shape, f32)
        dt = jnp.exp(u * (math.log(0.1) - math.log(0.001)) + math.log(0.001))
        return dt + jnp.log(-jnp.expm1(-dt))

    return {
        'x': normal((BATCH, SEQ, D_MODEL), 1.0),
        'mix_norm_g': gain((DEPTH, D_MODEL)),
        'w_in': normal((DEPTH, D_MODEL, IN_COLS), D_MODEL ** -0.5),
        'conv_a_w': normal((DEPTH, CONV_A, 3 * BRANCH_WIDTH), CONV_A ** -0.5),
        'a_log_a': a_log_init((DEPTH, H_A)),
        'dt_bias_a': dt_bias_init((DEPTH, H_A)),
        'out_norm_a_g': gain((DEPTH, DK_A)),
        'ckv_norm_g': gain((DEPTH, DC_B)),
        'w_uk': normal((DEPTH, DC_B, DH_B), DC_B ** -0.5),
        'q_norm_b_g': gain((DEPTH, DH_B)),
        'k_norm_b_g': gain((DEPTH, DH_B)),
        'w_uv': normal((DEPTH, H_B, DC_B, DH_B), DC_B ** -0.5),
        'rel_bias': normal((N_BUCKETS, H_B), 0.5),
        'conv_c_w': normal((DEPTH, CONV_C, BRANCH_WIDTH), CONV_C ** -0.5),
        'conv_d_w': normal((DEPTH, CONV_D, xbc_width), CONV_D ** -0.5),
        'conv_d_b': normal((DEPTH, xbc_width), 0.02),
        'a_log_d': a_log_init((DEPTH, H_D)),
        'dt_bias_d': dt_bias_init((DEPTH, H_D)),
        'd_skip': gain((DEPTH, H_D)),
        'out_norm_d_g': gain((DEPTH, D_INNER)),
        'w_up': normal((DEPTH, N_BRANCHES, BRANCH_WIDTH, D_MODEL), BRANCH_WIDTH ** -0.5),
        'w_out': normal((DEPTH, D_MODEL, D_MODEL), D_MODEL ** -0.5),
        'ffn_norm_g': gain((DEPTH, D_MODEL)),
        'w1_dense': normal((n_dense, D_MODEL, D_FF), D_MODEL ** -0.5),
        'w3_dense': normal((n_dense, D_MODEL, D_FF), D_MODEL ** -0.5),
        'w2_dense': normal((n_dense, D_FF, D_MODEL), D_FF ** -0.5),
        'w_router': normal((n_moe, D_MODEL, N_EXPERTS), D_MODEL ** -0.5),
        'w1_moe': normal((n_moe, N_EXPERTS, D_MODEL, D_EXPERT), D_MODEL ** -0.5),
        'w3_moe': normal((n_moe, N_EXPERTS, D_MODEL, D_EXPERT), D_MODEL ** -0.5),
        'w2_moe': normal((n_moe, N_EXPERTS, D_EXPERT, D_MODEL), D_EXPERT ** -0.5),
    }


def reference(x, mix_norm_g, w_in, conv_a_w, a_log_a, dt_bias_a, out_norm_a_g, ckv_norm_g, w_uk,
              q_norm_b_g, k_norm_b_g, w_uv, rel_bias, conv_c_w, conv_d_w, conv_d_b, a_log_d, dt_bias_d,
              d_skip, out_norm_d_g, w_up, w_out, ffn_norm_g, w1_dense, w3_dense, w2_dense, w_router,
              w1_moe, w3_moe, w2_moe):
    for l in range(DEPTH):
        xn = rms_norm(x, mix_norm_g[l])
        x = x + hybrid_mixer(xn, w_in[l], conv_a_w[l], a_log_a[l], dt_bias_a[l], out_norm_a_g[l],
                             ckv_norm_g[l], w_uk[l], q_norm_b_g[l], k_norm_b_g[l], w_uv[l], rel_bias,
                             conv_c_w[l], conv_d_w[l], conv_d_b[l], a_log_d[l], dt_bias_d[l], d_skip[l],
                             out_norm_d_g[l], w_up[l], w_out[l])
        h = rms_norm(x, ffn_norm_g[l])
        j = l // 2
        if l % 2 == 0:
            x = x + swiglu(h, w1_dense[j], w3_dense[j], w2_dense[j])
        else:
            x = x + moe_swiglu(h, w_router[j], w1_moe[j], w3_moe[j], w2_moe[j])
    return x
```

```python
import functools
import math

import jax
import jax.numpy as jnp
from jax import lax
from jax.experimental import pallas as pl
from jax.experimental.pallas import tpu as pltpu

F32 = jnp.float32
BF16 = jnp.bfloat16
I32 = jnp.int32
HIGHEST = lax.Precision.HIGHEST

D_MODEL = 4096
CHUNK = 64
N_BRANCHES = 4
BRANCH_WIDTH = D_MODEL // N_BRANCHES
DK_A = 128
H_A = BRANCH_WIDTH // DK_A
CONV_A = 4
DH_B = 128
H_B = BRANCH_WIDTH // DH_B
DC_B = 256
H_IDX = 8
D_IDX = 64
DSA_TOPK = 256
Q_BLOCK = 128
N_BUCKETS = 32
T5_MAX_DISTANCE = 128
CONV_C = 3
D_INNER = BRANCH_WIDTH
P_D = 64
H_D = D_INNER // P_D
N_GROUPS = 2
D_STATE = 128
CONV_D = 4
XBC_WIDTH = D_INNER + 2 * N_GROUPS * D_STATE
N_EXPERTS = 8
TOP_K = 2
EPS = 1e-6

MIX_SPLITS = (
    BRANCH_WIDTH, BRANCH_WIDTH, BRANCH_WIDTH, BRANCH_WIDTH, H_A, H_A,
    H_B * DH_B, DC_B, H_IDX * D_IDX, D_IDX, H_IDX,
    BRANCH_WIDTH, BRANCH_WIDTH, BRANCH_WIDTH,
    D_INNER, XBC_WIDTH, H_D,
)
MIX_COLS = sum(MIX_SPLITS)
_OFF = [0]
for _w in MIX_SPLITS:
    _OFF.append(_OFF[-1] + _w)
(O_AQ, O_AK, O_AV, O_AZ, O_AA, O_AB, O_BQ, O_BCKV, O_BQI, O_BKI, O_BWI,
 O_CB, O_CC, O_CH, O_DZ, O_DXBC, O_DDT, _O_END) = _OFF

P_AQKV = 0
P_AZ = 3072
P_BQ = 4096
P_C = 5120
P_DZ = 8192
P_DXBC = 9216
P_BQI = 10752
P_BCKV = 11264
P_COLS = 11520
S_AA = 0
S_AB = 8
S_BKI = 16
S_BWI = 80
S_DDT = 88
S_COLS = 128

LANES = 128
VMEM_LIMIT_MB = 56


def _cparams(sem, vmem_mb=VMEM_LIMIT_MB):
    return pltpu.CompilerParams(dimension_semantics=sem, vmem_limit_bytes=vmem_mb * 2 ** 20)


def _softplus(x):
    return jnp.maximum(x, 0.0) + jnp.log(1.0 + jnp.exp(-jnp.abs(x)))


def _sigmoid(x):
    return 1.0 / (1.0 + jnp.exp(-x))


def _silu(x):
    return x * _sigmoid(x)


def _dot(a, b, precision=None):
    return jnp.dot(a, b, preferred_element_type=F32, precision=precision)


def _dot_nt(a, b, precision=None):
    return lax.dot_general(a, b, (((1,), (1,)), ((), ())), preferred_element_type=F32, precision=precision)


def _dot_tn(a, b, precision=None):
    return lax.dot_general(a, b, (((0,), (0,)), ((), ())), preferred_element_type=F32, precision=precision)


def _rmsnorm_kernel(x_ref, g_ref, o_ref):
    x = x_ref[...].astype(F32)
    ms = jnp.mean(x * x, axis=-1, keepdims=True)
    o_ref[...] = (x * lax.rsqrt(ms + EPS) * g_ref[...]).astype(o_ref.dtype)


def rmsnorm(x, g, out_dtype=BF16, tm=256):
    m, d = x.shape
    tm = min(tm, m)
    return pl.pallas_call(
        _rmsnorm_kernel,
        grid=(m // tm,),
        in_specs=[pl.BlockSpec((tm, d), lambda i: (i, 0)), pl.BlockSpec((1, d), lambda i: (0, 0))],
        out_specs=pl.BlockSpec((tm, d), lambda i: (i, 0)),
        out_shape=jax.ShapeDtypeStruct((m, d), out_dtype),
        compiler_params=_cparams(("parallel",)),
        name="rmsnorm",
    )(x, g.reshape(1, d).astype(F32))


def _mm_kernel(*refs, nk, has_res):
    if has_res:
        a_ref, b_ref, r_ref = refs[:3]
        rest = refs[3:]
    else:
        a_ref, b_ref = refs[:2]
        r_ref = None
        rest = refs[2:]
    o_ref = rest[0]
    if nk == 1:
        acc = _dot(a_ref[...], b_ref[...])
        if r_ref is not None:
            acc = acc + r_ref[...].astype(F32)
        o_ref[...] = acc.astype(o_ref.dtype)
        return
    acc_ref = rest[1]
    k = pl.program_id(2)

    @pl.when(k == 0)
    def _():
        acc_ref[...] = jnp.zeros_like(acc_ref)

    acc_ref[...] += _dot(a_ref[...], b_ref[...])

    @pl.when(k == nk - 1)
    def _():
        acc = acc_ref[...]
        if r_ref is not None:
            acc = acc + r_ref[...].astype(F32)
        o_ref[...] = acc.astype(o_ref.dtype)


def matmul(a, b, *, out_dtype, residual=None, tm=1024, tn=512, tk=None, name="matmul"):
    m, kdim = a.shape
    _, n = b.shape
    tm, tn = min(tm, m), min(tn, n)
    tk = kdim if tk is None else min(tk, kdim)
    nk = kdim // tk
    assert m % tm == 0 and n % tn == 0 and kdim % tk == 0, (a.shape, b.shape, tm, tn, tk)
    in_specs = [pl.BlockSpec((tm, tk), lambda i, j, k: (i, k)), pl.BlockSpec((tk, tn), lambda i, j, k: (k, j))]
    args = [a, b]
    if residual is not None:
        in_specs.append(pl.BlockSpec((tm, tn), lambda i, j, k: (i, j)))
        args.append(residual)
    scratch = [pltpu.VMEM((tm, tn), F32)] if nk > 1 else []
    return pl.pallas_call(
        functools.partial(_mm_kernel, nk=nk, has_res=residual is not None),
        grid=(m // tm, n // tn, nk),
        in_specs=in_specs,
        out_specs=pl.BlockSpec((tm, tn), lambda i, j, k: (i, j)),
        out_shape=jax.ShapeDtypeStruct((m, n), out_dtype),
        scratch_shapes=scratch,
        compiler_params=_cparams(("parallel", "parallel", "arbitrary")),
        name=name,
    )(*args)


_PAD_ROWS = 8


def _causal_conv(stage_ref, x, w_ref, ksize):
    t = x.shape[0]
    stage_ref[0:_PAD_ROWS, :] = jnp.zeros((_PAD_ROWS, x.shape[1]), F32)
    stage_ref[_PAD_ROWS:_PAD_ROWS + t, :] = x
    acc = x * w_ref[ksize - 1:ksize, :]
    for j in range(ksize - 1):
        s = ksize - 1 - j
        acc = acc + stage_ref[_PAD_ROWS - s:_PAD_ROWS - s + t, :] * w_ref[j:j + 1, :]
    return acc


def _conv_qkv_kernel(x_ref, w_ref, o_ref, stage_ref, *, tc):
    cb = pl.program_id(1)
    y = _silu(_causal_conv(stage_ref, x_ref[...].astype(F32), w_ref, CONV_A))
    col0 = cb * tc
    is_q = col0 < BRANCH_WIDTH
    is_qk = col0 < 2 * BRANCH_WIDTH
    for g in range(tc // DK_A):
        ys = y[:, g * DK_A:(g + 1) * DK_A]
        inv = lax.rsqrt(jnp.sum(ys * ys, axis=-1, keepdims=True) + EPS)
        inv = jnp.where(is_q, inv * DK_A ** -0.5, inv)
        inv = jnp.where(is_qk, inv, jnp.ones_like(inv))
        o_ref[:, g * DK_A:(g + 1) * DK_A] = ys * inv


def _conv_xbc_kernel(x_ref, w_ref, b_ref, o_ref, stage_ref):
    y = _causal_conv(stage_ref, x_ref[...].astype(F32), w_ref, CONV_D) + b_ref[...]
    o_ref[...] = _silu(y)


def _conv_gated_kernel(bg_ref, cg_ref, h_ref, w_ref, o_ref, stage_ref):
    u = cg_ref[...].astype(F32) * h_ref[...].astype(F32)
    y = _causal_conv(stage_ref, u, w_ref, CONV_C)
    o_ref[...] = (bg_ref[...].astype(F32) * y).astype(o_ref.dtype)


def _conv_call(kernel, p3, col_offsets, width, extra, out_dtype, tc, name):
    bsz, t, _ = p3.shape
    tc = min(tc, width)
    in_specs = [pl.BlockSpec((None, t, tc), functools.partial(lambda b, c, o: (b, 0, o + c), o=off // tc))
                for off in col_offsets]
    args = [p3] * len(col_offsets)
    for e in extra:
        in_specs.append(pl.BlockSpec((e.shape[0], tc), lambda b, c: (0, c)))
        args.append(e)
    return pl.pallas_call(
        kernel,
        grid=(bsz, width // tc),
        in_specs=in_specs,
        out_specs=pl.BlockSpec((None, t, tc), lambda b, c: (b, 0, c)),
        out_shape=jax.ShapeDtypeStruct((bsz, t, width), out_dtype),
        scratch_shapes=[pltpu.VMEM((t + _PAD_ROWS, tc), F32)],
        compiler_params=_cparams(("parallel", "parallel")),
        name=name,
    )(*args)


def _tril_mask(n, strict=False):
    r = lax.broadcasted_iota(I32, (n, n), 0)
    c = lax.broadcasted_iota(I32, (n, n), 1)
    return (r > c) if strict else (r >= c)


def _chunk_cumsum(x):
    tril = _tril_mask(CHUNK).astype(F32)
    cs = _dot(tril, x, HIGHEST)
    padded = jnp.concatenate([cs, jnp.zeros((LANES - CHUNK, LANES), F32)], axis=0)
    return cs, padded.T[:, :CHUNK]


def _segment_decay(cs, cs_t, c):
    tril = _tril_mask(CHUNK)
    diff = cs[:, c:c + 1] - cs_t[c:c + 1, :]
    return jnp.where(tril, jnp.exp(jnp.where(tril, diff, 0.0)), 0.0)


_INV_BLOCK = 16


def _unit_lower_inverse(low):
    r = lax.broadcasted_iota(I32, (CHUNK, CHUNK), 0)
    c = lax.broadcasted_iota(I32, (CHUNK, CHUNK), 1)
    same = (r // _INV_BLOCK) == (c // _INV_BLOCK)
    eye = (r == c).astype(F32)
    mm = functools.partial(_dot, precision=HIGHEST)
    ld = jnp.where(same, low, 0.0)
    lo = jnp.where(same, 0.0, low)
    p = eye - ld
    x = mm(ld, ld)
    p = p + mm(p, x)
    x = mm(x, x)
    p = p + mm(p, x)
    x = mm(x, x)
    p = p + mm(p, x)
    n = mm(p, lo)
    rr = eye - n
    rr = rr + mm(rr, mm(n, n))
    return mm(rr, p)


def _gdn_kernel(q_ref, k_ref, v_ref, z_ref, sm_ref, alog_ref, dtb_ref, ng_ref, o_ref, state_ref):
    @pl.when(pl.program_id(1) == 0)
    def _():
        state_ref[...] = jnp.zeros_like(state_ref)

    sm = sm_ref[...]
    lane = lax.broadcasted_iota(I32, (CHUNK, LANES), 1)
    g = -jnp.exp(alog_ref[...]) * _softplus(sm + dtb_ref[...])
    g = jnp.where(lane < S_AA + H_A, g, 0.0)
    beta = _sigmoid(sm)
    gc, gc_t = _chunk_cumsum(g)
    g_last = gc[CHUNK - 1:CHUNK, :]
    e_gc = jnp.exp(gc)
    e_rem = jnp.exp(g_last - gc)
    e_last = jnp.exp(g_last)
    strict = _tril_mask(CHUNK, strict=True)
    ng = ng_ref[...]
    hi = functools.partial(_dot, precision=HIGHEST)

    for h in range(H_A):
        sl = slice(h * DK_A, (h + 1) * DK_A)
        qh, kh, vh = q_ref[:, sl], k_ref[:, sl], v_ref[:, sl]
        ca, cb = S_AA + h, S_AB + h
        decay = _segment_decay(gc, gc_t, ca)
        bcol = beta[:, cb:cb + 1]
        kb = kh * bcol
        low = jnp.where(strict, _dot_nt(kb, kh) * decay, 0.0)
        tinv = _unit_lower_inverse(low)
        u = hi(tinv, vh * bcol)
        w = hi(tinv, kb * e_gc[:, ca:ca + 1])
        intra = _dot_nt(qh, kh) * decay
        s = state_ref[h]
        v_new = u - _dot(w, s)
        out = _dot(qh * e_gc[:, ca:ca + 1], s) + _dot(intra, v_new)
        state_ref[h] = s * e_last[:, ca:ca + 1] + _dot_tn(kh * e_rem[:, ca:ca + 1], v_new)
        ms = jnp.mean(out * out, axis=-1, keepdims=True)
        out = out * lax.rsqrt(ms + EPS) * ng
        o_ref[:, sl] = (out * _silu(z_ref[:, sl].astype(F32))).astype(o_ref.dtype)


def gated_deltanet(qkv3, p3, small3, a_log, dt_bias, norm_g):
    bsz, t, _ = qkv3.shape
    nchunks = t // CHUNK

    def row(vals, off):
        return jnp.zeros((1, S_COLS), F32).at[0, off:off + vals.shape[0]].set(vals.astype(F32))

    w = BRANCH_WIDTH
    return pl.pallas_call(
        _gdn_kernel,
        grid=(bsz, nchunks),
        in_specs=[
            pl.BlockSpec((None, CHUNK, w), lambda b, n: (b, n, 0)),
            pl.BlockSpec((None, CHUNK, w), lambda b, n: (b, n, 1)),
            pl.BlockSpec((None, CHUNK, w), lambda b, n: (b, n, 2)),
            pl.BlockSpec((None, CHUNK, w), lambda b, n: (b, n, P_AZ // w)),
            pl.BlockSpec((None, CHUNK, S_COLS), lambda b, n: (b, n, 0)),
            pl.BlockSpec((1, S_COLS), lambda b, n: (0, 0)),
            pl.BlockSpec((1, S_COLS), lambda b, n: (0, 0)),
            pl.BlockSpec((1, DK_A), lambda b, n: (0, 0)),
        ],
        out_specs=pl.BlockSpec((None, CHUNK, w), lambda b, n: (b, n, 0)),
        out_shape=jax.ShapeDtypeStruct((bsz, t, w), BF16),
        scratch_shapes=[pltpu.VMEM((H_A, DK_A, DK_A), F32)],
        compiler_params=_cparams(("parallel", "arbitrary")),
        name="gated_deltanet",
    )(qkv3, qkv3, qkv3, p3, small3, row(a_log, S_AA), row(dt_bias, S_AA), norm_g.reshape(1, DK_A).astype(F32))


def _ssd_kernel(xbc_ref, z_ref, sm_ref, alog_ref, dtb_ref, dskip_ref, ng_ref, o_ref, state_ref):
    @pl.when(pl.program_id(1) == 0)
    def _():
        state_ref[...] = jnp.zeros_like(state_ref)

    sm = sm_ref[...]
    lane = lax.broadcasted_iota(I32, (CHUNK, LANES), 1)
    dt = _softplus(sm + dtb_ref[...])
    a = jnp.where((lane >= S_DDT) & (lane < S_DDT + H_D), dt * -jnp.exp(alog_ref[...]), 0.0)
    acs, acs_t = _chunk_cumsum(a)
    a_last = acs[CHUNK - 1:CHUNK, :]
    e_acs = jnp.exp(acs)
    e_rem = jnp.exp(a_last - acs)
    e_last = jnp.exp(a_last)
    heads_per_group = H_D // N_GROUPS
    ys = []
    for grp in range(N_GROUPS):
        b0 = D_INNER + grp * D_STATE
        c0 = D_INNER + N_GROUPS * D_STATE + grp * D_STATE
        bm = xbc_ref[:, b0:b0 + D_STATE]
        cm = xbc_ref[:, c0:c0 + D_STATE]
        cb = _dot_nt(cm, bm)
        for hh in range(heads_per_group):
            h = grp * heads_per_group + hh
            c = S_DDT + h
            xs = xbc_ref[:, h * P_D:(h + 1) * P_D]
            xc = xs * dt[:, c:c + 1]
            s = state_ref[h]
            y = _dot(cb * _segment_decay(acs, acs_t, c), xc) + _dot_nt(cm * e_acs[:, c:c + 1], s)
            state_ref[h] = s * e_last[:, c:c + 1] + _dot_tn(xc, bm * e_rem[:, c:c + 1])
            ys.append(y)
    y = jnp.concatenate(ys, axis=1) + dskip_ref[...] * xbc_ref[:, 0:D_INNER]
    y = y * _silu(z_ref[...].astype(F32))
    gw = D_INNER // N_GROUPS
    for grp in range(N_GROUPS):
        yg = y[:, grp * gw:(grp + 1) * gw]
        ms = jnp.mean(yg * yg, axis=-1, keepdims=True)
        o_ref[:, grp * gw:(grp + 1) * gw] = (yg * lax.rsqrt(ms + EPS) * ng_ref[:, grp * gw:(grp + 1) * gw]).astype(o_ref.dtype)


def mamba2_ssd(xbc3, p3, small3, a_log, dt_bias, d_skip, norm_g):
    bsz, t, _ = xbc3.shape
    nchunks = t // CHUNK

    def row(vals, off):
        return jnp.zeros((1, S_COLS), F32).at[0, off:off + vals.shape[0]].set(vals.astype(F32))

    return pl.pallas_call(
        _ssd_kernel,
        grid=(bsz, nchunks),
        in_specs=[
            pl.BlockSpec((None, CHUNK, XBC_WIDTH), lambda b, n: (b, n, 0)),
            pl.BlockSpec((None, CHUNK, D_INNER), lambda b, n: (b, n, P_DZ // D_INNER)),
            pl.BlockSpec((None, CHUNK, S_COLS), lambda b, n: (b, n, 0)),
            pl.BlockSpec((1, S_COLS), lambda b, n: (0, 0)),
            pl.BlockSpec((1, S_COLS), lambda b, n: (0, 0)),
            pl.BlockSpec((1, D_INNER), lambda b, n: (0, 0)),
            pl.BlockSpec((1, D_INNER), lambda b, n: (0, 0)),
        ],
        out_specs=pl.BlockSpec((None, CHUNK, D_INNER), lambda b, n: (b, n, 0)),
        out_shape=jax.ShapeDtypeStruct((bsz, t, D_INNER), BF16),
        scratch_shapes=[pltpu.VMEM((H_D, P_D, D_STATE), F32)],
        compiler_params=_cparams(("parallel", "arbitrary")),
        name="mamba2_ssd",
    )(xbc3, p3, small3, row(a_log, S_DDT), row(dt_bias, S_DDT),
      jnp.repeat(d_skip.astype(F32), P_D).reshape(1, D_INNER), norm_g.reshape(1, D_INNER).astype(F32))


def _dsa_prep_kernel(q_ref, ckv_ref, sm_ref, wuk_ref, cg_ref, qg_ref, kg_ref, qh_ref, kk_ref, ct_ref, ki_ref):
    ckv = ckv_ref[...].astype(F32)
    c = ckv * lax.rsqrt(jnp.mean(ckv * ckv, axis=-1, keepdims=True) + EPS) * cg_ref[...]
    cb = c.astype(BF16)
    ct_ref[...] = c.T.astype(BF16)
    kk = _dot(cb, wuk_ref[...])
    kk_ref[...] = (kk * lax.rsqrt(jnp.mean(kk * kk, axis=-1, keepdims=True) + EPS) * kg_ref[...]).astype(BF16)
    for h in range(H_B):
        sl = slice(h * DH_B, (h + 1) * DH_B)
        qh = q_ref[:, sl].astype(F32)
        qh = qh * lax.rsqrt(jnp.mean(qh * qh, axis=-1, keepdims=True) + EPS) * qg_ref[...]
        qh_ref[:, sl] = (qh * DH_B ** -0.5).astype(BF16)
    ki_ref[...] = sm_ref[:, S_BKI:S_BKI + D_IDX].astype(BF16)


def dsa_prep(p3, small3, w_uk, ckv_g, q_g, k_g, tt=512):
    bsz, t, _ = p3.shape
    tt = min(tt, t)
    w = H_B * DH_B
    return pl.pallas_call(
        _dsa_prep_kernel,
        grid=(bsz, t // tt),
        in_specs=[
            pl.BlockSpec((None, tt, w), lambda b, i: (b, i, P_BQ // w)),
            pl.BlockSpec((None, tt, DC_B), lambda b, i: (b, i, P_BCKV // DC_B)),
            pl.BlockSpec((None, tt, S_COLS), lambda b, i: (b, i, 0)),
            pl.BlockSpec((DC_B, DH_B), lambda b, i: (0, 0)),
            pl.BlockSpec((1, DC_B), lambda b, i: (0, 0)),
            pl.BlockSpec((1, DH_B), lambda b, i: (0, 0)),
            pl.BlockSpec((1, DH_B), lambda b, i: (0, 0)),
        ],
        out_specs=[
            pl.BlockSpec((None, tt, w), lambda b, i: (b, i, 0)),
            pl.BlockSpec((None, tt, DH_B), lambda b, i: (b, i, 0)),
            pl.BlockSpec((None, DC_B, tt), lambda b, i: (b, 0, i)),
            pl.BlockSpec((None, tt, D_IDX), lambda b, i: (b, i, 0)),
        ],
        out_shape=[
            jax.ShapeDtypeStruct((bsz, t, w), BF16),
            jax.ShapeDtypeStruct((bsz, t, DH_B), BF16),
            jax.ShapeDtypeStruct((bsz, DC_B, t), BF16),
            jax.ShapeDtypeStruct((bsz, t, D_IDX), BF16),
        ],
        compiler_params=_cparams(("parallel", "parallel")),
        name="dsa_prep",
    )(p3, p3, small3, w_uk.astype(BF16), ckv_g.reshape(1, DC_B).astype(F32),
      q_g.reshape(1, DH_B).astype(F32), k_g.reshape(1, DH_B).astype(F32))


_T5_HALF = N_BUCKETS // 2
_T5_EXACT = _T5_HALF // 2
_T5_FAR = _T5_HALF - 1


def _relbias_kernel(rb_ref, prev_ref, diag_ref):
    h = pl.program_id(0)
    kl = lax.broadcasted_iota(I32, (Q_BLOCK, Q_BLOCK), 0)
    ql = lax.broadcasted_iota(I32, (Q_BLOCK, Q_BLOCK), 1)
    far = rb_ref[_T5_FAR, h]
    for ref, shift in ((prev_ref, -Q_BLOCK), (diag_ref, 0)):
        rel = kl - ql + shift
        n = jnp.abs(rel)
        n2 = n * n
        steps = jnp.zeros_like(n)
        for j in range(1, _T5_HALF - _T5_EXACT):
            steps = steps + (n2 >= (_T5_EXACT * _T5_EXACT) * 2 ** j).astype(I32)
        large = jnp.minimum(_T5_EXACT + steps, _T5_HALF - 1)
        bucket = jnp.where(rel > 0, _T5_HALF, 0) + jnp.where(n < _T5_EXACT, n, large)
        acc = jnp.zeros((Q_BLOCK, Q_BLOCK), F32)
        for b in range(N_BUCKETS):
            acc = jnp.where(bucket == b, rb_ref[b, h], acc)
        ref[...] = acc - far


def relbias_tables(rel_bias):
    shp = jax.ShapeDtypeStruct((H_B, Q_BLOCK, Q_BLOCK), F32)
    spec = pl.BlockSpec((None, Q_BLOCK, Q_BLOCK), lambda h: (h, 0, 0))
    return pl.pallas_call(
        _relbias_kernel,
        grid=(H_B,),
        in_specs=[pl.BlockSpec(memory_space=pltpu.SMEM)],
        out_specs=[spec, spec],
        out_shape=[shp, shp],
        compiler_params=_cparams(("arbitrary",)),
        name="relbias_tables",
    )(rel_bias.astype(F32))


_INT_MIN = -2 ** 31
_CHUNK_SHIFT = CHUNK.bit_length() - 1


def _dsa_kernel(rb_ref, qh_ref, qi_ref, sm_ref, kk_ref, ct_ref, ki_ref, dprev_ref, ddiag_ref, wuvt_ref,
                o_ref, lg_ref, *, widths, topk):
    i = pl.program_id(1)
    need = (i + 1) * Q_BLOCK
    wi_t = sm_ref[...].T[S_BWI:S_BWI + H_IDX, :] * (H_IDX ** -0.5 * D_IDX ** -0.5)

    def body(nc):
        kpos = lax.broadcasted_iota(I32, (nc, Q_BLOCK), 0)
        qpos = i * Q_BLOCK + lax.broadcasted_iota(I32, (nc, Q_BLOCK), 1)
        adm = (kpos >> _CHUNK_SHIFT) <= (qpos >> _CHUNK_SHIFT)
        ki = ki_ref[0:nc, :]
        sc = jnp.zeros((nc, Q_BLOCK), F32)
        for h in range(H_IDX):
            s = _dot_nt(ki, qi_ref[:, h * D_IDX:(h + 1) * D_IDX])
            sc = sc + wi_t[h:h + 1, :] * jnp.maximum(s, 0.0)
        sc = sc + 0.0
        bits = lax.bitcast_convert_type(sc, I32)
        key = jnp.where(bits < 0, bits ^ 0x7FFFFFFF, bits)
        key = jnp.where(adm, key, _INT_MIN)

        def count(mask):
            return jnp.sum(mask.astype(I32), axis=0, keepdims=True)

        tau = jnp.where(count(key >= 0) >= topk, 0, _INT_MIN).astype(I32)

        def vstep(it, tau):
            cand = tau | (jnp.int32(1) << (30 - it))
            return jnp.where(count(key >= cand) >= topk, cand, tau)

        tau = lax.fori_loop(0, 31, vstep, tau)
        gt = key > tau
        eq = key == tau
        room = topk - count(gt)

        def istep(it, last):
            cand = last | (jnp.int32(1) << (nbits - 1 - it))
            return jnp.where(count(eq & (kpos < cand)) < room, cand, last)

        nbits = max(1, (nc - 1).bit_length())
        last = lax.fori_loop(0, nbits, istep, jnp.zeros((1, Q_BLOCK), I32))
        sel = adm & (gt | (eq & (kpos <= last)))

        kk = kk_ref[0:nc, :]
        ct = ct_ref[:, 0:nc]
        prev_row = pl.multiple_of(jnp.maximum(i - 1, 0) * Q_BLOCK, Q_BLOCK)
        diag_row = pl.multiple_of(i * Q_BLOCK, Q_BLOCK)
        has_prev = (i > 0).astype(F32)
        outs = []
        for h in range(H_B):
            lg_ref[0:nc, :] = _dot_nt(kk, qh_ref[:, h * DH_B:(h + 1) * DH_B]) + rb_ref[_T5_FAR, h]
            lg_ref[pl.ds(prev_row, Q_BLOCK), :] += dprev_ref[h] * has_prev
            lg_ref[pl.ds(diag_row, Q_BLOCK), :] += ddiag_ref[h]
            lg = jnp.where(sel, lg_ref[0:nc, :], -jnp.inf)
            m = jnp.max(lg, axis=0, keepdims=True)
            p = jnp.exp(lg - m)
            p = p * (1.0 / jnp.sum(p, axis=0, keepdims=True))
            o_lat = _dot(ct, p.astype(BF16))
            outs.append(_dot(wuvt_ref[h], o_lat.astype(BF16)).T)
        o_ref[...] = jnp.concatenate(outs, axis=1).astype(o_ref.dtype)

    lo = 0
    for nc in widths:
        @pl.when((need > lo) & (need <= nc))
        def _(nc=nc):
            body(nc)
        lo = nc


def dsa_attention(qh3, qi_p3, small3, kk3, ct3, ki3, rel_bias, dprev, ddiag, wuv_t):
    bsz, t, w = qh3.shape
    topk = min(DSA_TOPK, t // 4)
    step = min(512, t)
    widths = tuple(range(step, t + 1, step))
    wq = H_IDX * D_IDX
    full = lambda shape: pl.BlockSpec(shape, lambda b, i: (0,) * len(shape))
    return pl.pallas_call(
        functools.partial(_dsa_kernel, widths=widths, topk=topk),
        grid=(bsz, t // Q_BLOCK),
        in_specs=[
            pl.BlockSpec(memory_space=pltpu.SMEM),
            pl.BlockSpec((None, Q_BLOCK, w), lambda b, i: (b, i, 0)),
            pl.BlockSpec((None, Q_BLOCK, wq), lambda b, i: (b, i, P_BQI // wq)),
            pl.BlockSpec((None, Q_BLOCK, S_COLS), lambda b, i: (b, i, 0)),
            pl.BlockSpec((None, t, DH_B), lambda b, i: (b, 0, 0)),
            pl.BlockSpec((None, DC_B, t), lambda b, i: (b, 0, 0)),
            pl.BlockSpec((None, t, D_IDX), lambda b, i: (b, 0, 0)),
            full((H_B, Q_BLOCK, Q_BLOCK)),
            full((H_B, Q_BLOCK, Q_BLOCK)),
            full((H_B, DH_B, DC_B)),
        ],
        out_specs=pl.BlockSpec((None, Q_BLOCK, w), lambda b, i: (b, i, 0)),
        out_shape=jax.ShapeDtypeStruct((bsz, t, w), BF16),
        scratch_shapes=[pltpu.VMEM((t, Q_BLOCK), F32)],
        compiler_params=_cparams(("parallel", "arbitrary")),
        name="dsa_attention",
    )(rel_bias.astype(F32), qh3, qi_p3, small3, kk3, ct3, ki3, dprev, ddiag, wuv_t)


def _merge_kernel(xn_ref, o0_ref, o1_ref, o2_ref, o3_ref, g0_ref, g1_ref, g2_ref, g3_ref,
                  u0_ref, u1_ref, u2_ref, u3_ref, out_ref):
    xn = xn_ref[...]
    acc = None
    for o_ref, g_ref, u_ref in ((o0_ref, g0_ref, u0_ref), (o1_ref, g1_ref, u1_ref),
                                (o2_ref, g2_ref, u2_ref), (o3_ref, g3_ref, u3_ref)):
        term = _sigmoid(_dot(xn, g_ref[...])) * _dot(o_ref[...], u_ref[...])
        acc = term if acc is None else acc + term
    out_ref[...] = acc.astype(out_ref.dtype)


def merge_branches(xn, outs, w_gate, w_up, tm=512, tn=256):
    m, d = xn.shape
    tm, tn = min(tm, m), min(tn, d)
    nj = d // tn
    in_specs = [pl.BlockSpec((tm, d), lambda j, i: (i, 0))]
    in_specs += [pl.BlockSpec((tm, BRANCH_WIDTH), lambda j, i: (i, 0)) for _ in range(N_BRANCHES)]
    in_specs += [pl.BlockSpec((d, tn), functools.partial(lambda j, i, br: (0, br * nj + j), br=br))
                 for br in range(N_BRANCHES)]
    in_specs += [pl.BlockSpec((None, BRANCH_WIDTH, tn), functools.partial(lambda j, i, br: (br, 0, j), br=br))
                 for br in range(N_BRANCHES)]
    return pl.pallas_call(
        _merge_kernel,
        grid=(nj, m // tm),
        in_specs=in_specs,
        out_specs=pl.BlockSpec((tm, tn), lambda j, i: (i, j)),
        out_shape=jax.ShapeDtypeStruct((m, d), BF16),
        compiler_params=_cparams(("parallel", "parallel")),
        name="merge_branches",
    )(xn, *outs, *([w_gate] * N_BRANCHES), *([w_up] * N_BRANCHES))


def _glu_kernel(*refs, gated, tiles_per_expert):
    if gated:
        x_ref, w1_ref, w3_ref, gate_ref, o_ref = refs
    else:
        x_ref, w1_ref, w3_ref, o_ref = refs
    x = x_ref[...]
    h = _silu(_dot(x, w1_ref[...])) * _dot(x, w3_ref[...])
    if gated:
        e = pl.program_id(0) // tiles_per_expert
        gates = gate_ref[...]
        lane = lax.broadcasted_iota(I32, gates.shape, 1)
        h = h * jnp.sum(jnp.where(lane == e, gates, 0.0), axis=-1, keepdims=True)
    o_ref[...] = h.astype(o_ref.dtype)


def glu_dense(x, w1, w3, tm=1024, tf=256):
    m, d = x.shape
    f = w1.shape[1]
    tm, tf = min(tm, m), min(tf, f)
    assert f % tf == 0
    return pl.pallas_call(
        functools.partial(_glu_kernel, gated=False, tiles_per_expert=1),
        grid=(f // tf, m // tm),
        in_specs=[pl.BlockSpec((tm, d), lambda j, i: (i, 0)),
                  pl.BlockSpec((d, tf), lambda j, i: (0, j)),
                  pl.BlockSpec((d, tf), lambda j, i: (0, j))],
        out_specs=pl.BlockSpec((tm, tf), lambda j, i: (i, j)),
        out_shape=jax.ShapeDtypeStruct((m, f), BF16),
        compiler_params=_cparams(("parallel", "parallel")),
        name="glu_dense",
    )(x, w1, w3)


def glu_experts(x, w1, w3, gates, tm=1024, tf=512):
    m, d = x.shape
    ne, _, f = w1.shape
    tm, tf = min(tm, m), min(tf, f)
    tpe = f // tf
    return pl.pallas_call(
        functools.partial(_glu_kernel, gated=True, tiles_per_expert=tpe),
        grid=(ne * tpe, m // tm),
        in_specs=[pl.BlockSpec((tm, d), lambda j, i: (i, 0)),
                  pl.BlockSpec((None, d, tf), lambda j, i: (j // tpe, 0, j % tpe)),
                  pl.BlockSpec((None, d, tf), lambda j, i: (j // tpe, 0, j % tpe)),
                  pl.BlockSpec((tm, LANES), lambda j, i: (i, 0))],
        out_specs=pl.BlockSpec((tm, tf), lambda j, i: (i, j)),
        out_shape=jax.ShapeDtypeStruct((m, ne * f), BF16),
        compiler_params=_cparams(("parallel", "parallel")),
        name="glu_experts",
    )(x, w1, w3, gates)


def _router_kernel(x_ref, g_ref, wr_ref, h_ref, gate_ref):
    x = x_ref[...]
    h = x * lax.rsqrt(jnp.mean(x * x, axis=-1, keepdims=True) + EPS) * g_ref[...]
    h_ref[...] = h.astype(h_ref.dtype)
    logits = _dot(h, wr_ref[...], HIGHEST)
    lane = lax.broadcasted_iota(I32, logits.shape, 1)
    logits = jnp.where(lane < N_EXPERTS, logits, -jnp.inf)
    m1 = jnp.max(logits, axis=-1, keepdims=True)
    i1 = jnp.min(jnp.where(logits == m1, lane, LANES), axis=-1, keepdims=True)
    rest = jnp.where(lane == i1, -jnp.inf, logits)
    m2 = jnp.max(rest, axis=-1, keepdims=True)
    i2 = jnp.min(jnp.where(rest == m2, lane, LANES), axis=-1, keepdims=True)
    e2 = jnp.exp(m2 - m1)
    inv = 1.0 / (1.0 + e2)
    gate_ref[...] = jnp.where(lane == i1, inv, 0.0) + jnp.where(lane == i2, e2 * inv, 0.0)


def moe_router(x, g, w_router, tm=256):
    m, d = x.shape
    tm = min(tm, m)
    wr = jnp.zeros((d, LANES), F32).at[:, :N_EXPERTS].set(w_router.astype(F32))
    return pl.pallas_call(
        _router_kernel,
        grid=(m // tm,),
        in_specs=[pl.BlockSpec((tm, d), lambda i: (i, 0)), pl.BlockSpec((1, d), lambda i: (0, 0)),
                  pl.BlockSpec((d, LANES), lambda i: (0, 0))],
        out_specs=[pl.BlockSpec((tm, d), lambda i: (i, 0)), pl.BlockSpec((tm, LANES), lambda i: (i, 0))],
        out_shape=[jax.ShapeDtypeStruct((m, d), BF16), jax.ShapeDtypeStruct((m, LANES), F32)],
        compiler_params=_cparams(("parallel",)),
        name="moe_router",
    )(x, g.reshape(1, d).astype(F32), wr)


def _split_in_proj(w_in_l):
    seg = lambda lo, hi: w_in_l[:, lo:hi]
    big = jnp.concatenate([
        seg(O_AQ, O_AZ), seg(O_AZ, O_AA), seg(O_BQ, O_BCKV), seg(O_CB, O_DZ),
        seg(O_DZ, O_DXBC), seg(O_DXBC, O_DDT), seg(O_BQI, O_BKI), seg(O_BCKV, O_BQI)], axis=1).astype(BF16)
    pad = jnp.zeros((w_in_l.shape[0], S_COLS - (S_DDT + H_D)), w_in_l.dtype)
    small = jnp.concatenate([
        seg(O_AA, O_AB), seg(O_AB, O_BQ), seg(O_BKI, O_BWI), seg(O_BWI, O_CB), seg(O_DDT, _O_END), pad],
        axis=1).astype(BF16)
    gate = w_in_l[:, MIX_COLS:].astype(BF16)
    return big, small, gate


def _mixer_layer(x2, bsz, t, mix_norm_g, w_in_l, conv_a_w, a_log_a, dt_bias_a, out_norm_a_g, ckv_norm_g, w_uk,
                 q_norm_b_g, k_norm_b_g, w_uv, rel_bias, bias_tables, conv_c_w, conv_d_w, conv_d_b, a_log_d,
                 dt_bias_d, d_skip, out_norm_d_g, w_up, w_out):
    m = bsz * t
    xn = rmsnorm(x2, mix_norm_g)
    w_big, w_small, w_gate = _split_in_proj(w_in_l)
    p3 = matmul(xn, w_big, out_dtype=BF16, tm=1024, tn=768, name="in_proj").reshape(bsz, t, P_COLS)
    small3 = matmul(xn, w_small, out_dtype=F32, tm=1024, tn=S_COLS, name="in_proj_small").reshape(bsz, t, S_COLS)

    qkv3 = _conv_call(functools.partial(_conv_qkv_kernel, tc=512), p3, [P_AQKV], 3 * BRANCH_WIDTH,
                      [conv_a_w.astype(F32)], F32, 512, "conv_qkv")
    o_a = gated_deltanet(qkv3, p3, small3, a_log_a, dt_bias_a, out_norm_a_g)

    qh3, kk3, ct3, ki3 = dsa_prep(p3, small3, w_uk, ckv_norm_g, q_norm_b_g, k_norm_b_g)
    dprev, ddiag = bias_tables
    o_b = dsa_attention(qh3, p3, small3, kk3, ct3, ki3, rel_bias, dprev, ddiag,
                        jnp.swapaxes(w_uv, 1, 2).astype(BF16))

    o_c = _conv_call(_conv_gated_kernel, p3, [P_C, P_C + BRANCH_WIDTH, P_C + 2 * BRANCH_WIDTH], BRANCH_WIDTH,
                     [conv_c_w.astype(F32)], BF16, 512, "conv_gated")

    xbc3 = _conv_call(_conv_xbc_kernel, p3, [P_DXBC], XBC_WIDTH,
                      [conv_d_w.astype(F32), conv_d_b.reshape(1, XBC_WIDTH).astype(F32)], F32, 512, "conv_xbc")
    o_d = mamba2_ssd(xbc3, p3, small3, a_log_d, dt_bias_d, d_skip, out_norm_d_g)

    outs = [o.reshape(m, BRANCH_WIDTH) for o in (o_a, o_b, o_c, o_d)]
    merged = merge_branches(xn, outs, w_gate, w_up.astype(BF16))
    return matmul(merged, w_out.astype(BF16), out_dtype=F32, residual=x2, tm=1024, tn=512, name="out_proj")


def _half_k(k):
    return k // 2 if k % (2 * LANES) == 0 else k


def kernel(x, mix_norm_g, w_in, conv_a_w, a_log_a, dt_bias_a, out_norm_a_g, ckv_norm_g, w_uk, q_norm_b_g, k_norm_b_g, w_uv, rel_bias, conv_c_w, conv_d_w, conv_d_b, a_log_d, dt_bias_d, d_skip, out_norm_d_g, w_up, w_out, ffn_norm_g, w1_dense, w3_dense, w2_dense, w_router, w1_moe, w3_moe, w2_moe):
    bsz, t, d = x.shape
    depth = w_in.shape[0]
    x2 = x.reshape(bsz * t, d)
    bias_tables = relbias_tables(rel_bias)
    for l in range(depth):
        x2 = _mixer_layer(x2, bsz, t, mix_norm_g[l], w_in[l], conv_a_w[l], a_log_a[l], dt_bias_a[l],
                          out_norm_a_g[l], ckv_norm_g[l], w_uk[l], q_norm_b_g[l], k_norm_b_g[l], w_uv[l],
                          rel_bias, bias_tables, conv_c_w[l], conv_d_w[l], conv_d_b[l], a_log_d[l],
                          dt_bias_d[l], d_skip[l], out_norm_d_g[l], w_up[l], w_out[l])
        j = l // 2
        if l % 2 == 0:
            h = rmsnorm(x2, ffn_norm_g[l])
            hid = glu_dense(h, w1_dense[j].astype(BF16), w3_dense[j].astype(BF16))
            x2 = matmul(hid, w2_dense[j].astype(BF16), out_dtype=F32, residual=x2,
                        tm=1024, tn=512, tk=_half_k(hid.shape[1]), name="ffn_down")
        else:
            h, gates = moe_router(x2, ffn_norm_g[l], w_router[j])
            hid = glu_experts(h, w1_moe[j].astype(BF16), w3_moe[j].astype(BF16), gates)
            w2 = w2_moe[j].astype(BF16).reshape(-1, d)
            x2 = matmul(hid, w2, out_dtype=F32, residual=x2, tm=1024, tn=512, tk=2048, name="moe_down")
    return x2.reshape(bsz, t, d)
```

```python
import functools
import math

import jax
import jax.numpy as jnp
from jax import lax
from jax.experimental import pallas as pl
from jax.experimental.pallas import tpu as pltpu

F32 = jnp.float32
BF16 = jnp.bfloat16
I32 = jnp.int32
HIGHEST = lax.Precision.HIGHEST

D_MODEL = 4096
CHUNK = 64
N_BRANCHES = 4
BRANCH_WIDTH = D_MODEL // N_BRANCHES
DK_A = 128
H_A = BRANCH_WIDTH // DK_A
CONV_A = 4
DH_B = 128
H_B = BRANCH_WIDTH // DH_B
DC_B = 256
H_IDX = 8
D_IDX = 64
DSA_TOPK = 256
Q_BLOCK = 128
N_BUCKETS = 32
T5_MAX_DISTANCE = 128
CONV_C = 3
D_INNER = BRANCH_WIDTH
P_D = 64
H_D = D_INNER // P_D
N_GROUPS = 2
D_STATE = 128
CONV_D = 4
XBC_WIDTH = D_INNER + 2 * N_GROUPS * D_STATE
N_EXPERTS = 8
TOP_K = 2
EPS = 1e-6

MIX_SPLITS = (
    BRANCH_WIDTH, BRANCH_WIDTH, BRANCH_WIDTH, BRANCH_WIDTH, H_A, H_A,
    H_B * DH_B, DC_B, H_IDX * D_IDX, D_IDX, H_IDX,
    BRANCH_WIDTH, BRANCH_WIDTH, BRANCH_WIDTH,
    D_INNER, XBC_WIDTH, H_D,
)
MIX_COLS = sum(MIX_SPLITS)
_OFF = [0]
for _w in MIX_SPLITS:
    _OFF.append(_OFF[-1] + _w)
(O_AQ, O_AK, O_AV, O_AZ, O_AA, O_AB, O_BQ, O_BCKV, O_BQI, O_BKI, O_BWI,
 O_CB, O_CC, O_CH, O_DZ, O_DXBC, O_DDT, _O_END) = _OFF

P_AQKV = 0
P_AZ = 3072
P_BQ = 4096
P_C = 5120
P_DZ = 8192
P_DXBC = 9216
P_BQI = 10752
P_BCKV = 11264
P_COLS = 11520
S_AA = 0
S_AB = 8
S_BKI = 16
S_BWI = 80
S_DDT = 88
S_COLS = 128

LANES = 128
VMEM_LIMIT_MB = 56


def _cparams(sem, vmem_mb=VMEM_LIMIT_MB):
    return pltpu.CompilerParams(dimension_semantics=sem, vmem_limit_bytes=vmem_mb * 2 ** 20)


def _softplus(x):
    return jnp.maximum(x, 0.0) + jnp.log(1.0 + jnp.exp(-jnp.abs(x)))


def _sigmoid(x):
    return 1.0 / (1.0 + jnp.exp(-x))


def _silu(x):
    return x * _sigmoid(x)


def _dot(a, b, precision=None):
    return jnp.dot(a, b, preferred_element_type=F32, precision=precision)


def _dot_nt(a, b, precision=None):
    return lax.dot_general(a, b, (((1,), (1,)), ((), ())), preferred_element_type=F32, precision=precision)


def _dot_tn(a, b, precision=None):
    return lax.dot_general(a, b, (((0,), (0,)), ((), ())), preferred_element_type=F32, precision=precision)


def _rmsnorm_kernel(x_ref, g_ref, o_ref):
    x = x_ref[...].astype(F32)
    ms = jnp.mean(x * x, axis=-1, keepdims=True)
    o_ref[...] = (x * lax.rsqrt(ms + EPS) * g_ref[...]).astype(o_ref.dtype)


def rmsnorm(x, g, out_dtype=BF16, tm=256):
    m, d = x.shape
    tm = min(tm, m)
    return pl.pallas_call(
        _rmsnorm_kernel,
        grid=(m // tm,),
        in_specs=[pl.BlockSpec((tm, d), lambda i: (i, 0)), pl.BlockSpec((1, d), lambda i: (0, 0))],
        out_specs=pl.BlockSpec((tm, d), lambda i: (i, 0)),
        out_shape=jax.ShapeDtypeStruct((m, d), out_dtype),
        compiler_params=_cparams(("parallel",)),
        name="rmsnorm",
    )(x, g.reshape(1, d).astype(F32))


def _mm_kernel(*refs, nk, has_res):
    if has_res:
        a_ref, b_ref, r_ref = refs[:3]
        rest = refs[3:]
    else:
        a_ref, b_ref = refs[:2]
        r_ref = None
        rest = refs[2:]
    o_ref = rest[0]
    if nk == 1:
        acc = _dot(a_ref[...], b_ref[...])
        if r_ref is not None:
            acc = acc + r_ref[...].astype(F32)
        o_ref[...] = acc.astype(o_ref.dtype)
        return
    acc_ref = rest[1]
    k = pl.program_id(2)

    @pl.when(k == 0)
    def _():
        acc_ref[...] = jnp.zeros_like(acc_ref)

    acc_ref[...] += _dot(a_ref[...], b_ref[...])

    @pl.when(k == nk - 1)
    def _():
        acc = acc_ref[...]
        if r_ref is not None:
            acc = acc + r_ref[...].astype(F32)
        o_ref[...] = acc.astype(o_ref.dtype)


def matmul(a, b, *, out_dtype, residual=None, tm=1024, tn=512, tk=None, name="matmul"):
    m, kdim = a.shape
    _, n = b.shape
    tm, tn = min(tm, m), min(tn, n)
    tk = kdim if tk is None else min(tk, kdim)
    nk = kdim // tk
    assert m % tm == 0 and n % tn == 0 and kdim % tk == 0, (a.shape, b.shape, tm, tn, tk)
    in_specs = [pl.BlockSpec((tm, tk), lambda i, j, k: (i, k)), pl.BlockSpec((tk, tn), lambda i, j, k: (k, j))]
    args = [a, b]
    if residual is not None:
        in_specs.append(pl.BlockSpec((tm, tn), lambda i, j, k: (i, j)))
        args.append(residual)
    scratch = [pltpu.VMEM((tm, tn), F32)] if nk > 1 else []
    return pl.pallas_call(
        functools.partial(_mm_kernel, nk=nk, has_res=residual is not None),
        grid=(m // tm, n // tn, nk),
        in_specs=in_specs,
        out_specs=pl.BlockSpec((tm, tn), lambda i, j, k: (i, j)),
        out_shape=jax.ShapeDtypeStruct((m, n), out_dtype),
        scratch_shapes=scratch,
        compiler_params=_cparams(("parallel", "parallel", "arbitrary")),
        name=name,
    )(*args)


_PAD_ROWS = 8


def _causal_conv(stage_ref, x, w_ref, ksize):
    t = x.shape[0]
    stage_ref[0:_PAD_ROWS, :] = jnp.zeros((_PAD_ROWS, x.shape[1]), F32)
    stage_ref[_PAD_ROWS:_PAD_ROWS + t, :] = x
    acc = x * w_ref[ksize - 1:ksize, :]
    for j in range(ksize - 1):
        s = ksize - 1 - j
        acc = acc + stage_ref[_PAD_ROWS - s:_PAD_ROWS - s + t, :] * w_ref[j:j + 1, :]
    return acc


def _conv_qkv_kernel(x_ref, w_ref, o_ref, stage_ref, *, tc):
    cb = pl.program_id(1)
    y = _silu(_causal_conv(stage_ref, x_ref[...].astype(F32), w_ref, CONV_A))
    col0 = cb * tc
    is_q = col0 < BRANCH_WIDTH
    is_qk = col0 < 2 * BRANCH_WIDTH
    for g in range(tc // DK_A):
        ys = y[:, g * DK_A:(g + 1) * DK_A]
        inv = lax.rsqrt(jnp.sum(ys * ys, axis=-1, keepdims=True) + EPS)
        inv = jnp.where(is_q, inv * DK_A ** -0.5, inv)
        inv = jnp.where(is_qk, inv, jnp.ones_like(inv))
        o_ref[:, g * DK_A:(g + 1) * DK_A] = ys * inv


def _conv_xbc_kernel(x_ref, w_ref, b_ref, o_ref, stage_ref):
    y = _causal_conv(stage_ref, x_ref[...].astype(F32), w_ref, CONV_D) + b_ref[...]
    o_ref[...] = _silu(y)


def _conv_gated_kernel(bg_ref, cg_ref, h_ref, w_ref, o_ref, stage_ref):
    u = cg_ref[...].astype(F32) * h_ref[...].astype(F32)
    y = _causal_conv(stage_ref, u, w_ref, CONV_C)
    o_ref[...] = (bg_ref[...].astype(F32) * y).astype(o_ref.dtype)


def _conv_call(kernel, p3, col_offsets, width, extra, out_dtype, tc, name):
    bsz, t, _ = p3.shape
    tc = min(tc, width)
    in_specs = [pl.BlockSpec((None, t, tc), functools.partial(lambda b, c, o: (b, 0, o + c), o=off // tc))
                for off in col_offsets]
    args = [p3] * len(col_offsets)
    for e in extra:
        in_specs.append(pl.BlockSpec((e.shape[0], tc), lambda b, c: (0, c)))
        args.append(e)
    return pl.pallas_call(
        kernel,
        grid=(bsz, width // tc),
        in_specs=in_specs,
        out_specs=pl.BlockSpec((None, t, tc), lambda b, c: (b, 0, c)),
        out_shape=jax.ShapeDtypeStruct((bsz, t, width), out_dtype),
        scratch_shapes=[pltpu.VMEM((t + _PAD_ROWS, tc), F32)],
        compiler_params=_cparams(("parallel", "parallel")),
        name=name,
    )(*args)


def _tril_mask(n, strict=False):
    r = lax.broadcasted_iota(I32, (n, n), 0)
    c = lax.broadcasted_iota(I32, (n, n), 1)
    return (r > c) if strict else (r >= c)


def _chunk_cumsum(x):
    tril = _tril_mask(CHUNK).astype(F32)
    cs = _dot(tril, x, HIGHEST)
    padded = jnp.concatenate([cs, jnp.zeros((LANES - CHUNK, LANES), F32)], axis=0)
    return cs, padded.T[:, :CHUNK]


def _segment_decay(cs, cs_t, c):
    tril = _tril_mask(CHUNK)
    diff = cs[:, c:c + 1] - cs_t[c:c + 1, :]
    return jnp.where(tril, jnp.exp(jnp.where(tril, diff, 0.0)), 0.0)


_INV_BLOCK = 16
_GDN_GROUP = 4
_CHUNK_SHIFT = CHUNK.bit_length() - 1


def _split_bf16(a):
    hi = a.astype(BF16)
    return hi, (a - hi.astype(F32)).astype(BF16)


def _dot3(a, b):
    a_hi, a_lo = a if isinstance(a, tuple) else _split_bf16(a)
    b_hi, b_lo = b if isinstance(b, tuple) else _split_bf16(b)
    return _dot(a_hi, b_hi) + _dot(a_hi, b_lo) + _dot(a_lo, b_hi)


def _unit_lower_inverse(low):
    n_rows = low.shape[0]
    r = lax.broadcasted_iota(I32, (n_rows, n_rows), 0)
    c = lax.broadcasted_iota(I32, (n_rows, n_rows), 1)
    shift = _INV_BLOCK.bit_length() - 1
    same = (r >> shift) == (c >> shift)
    eye = (r == c).astype(F32)
    ld = jnp.where(same, low, 0.0)
    lo = jnp.where(same, 0.0, low)
    p = eye - ld
    xs = _split_bf16(ld)
    x = _dot3(xs, xs)
    for _ in range(2):
        xs = _split_bf16(x)
        p = p + _dot3(p, xs)
        x = _dot3(xs, xs)
    p = p + _dot3(p, x)
    ps = _split_bf16(p)
    n = _dot3(ps, lo)
    ns = _split_bf16(n)
    rr = eye - n
    rr = rr + _dot3(rr, _dot3(ns, ns))
    return _dot3(rr, ps)


def _gdn_kernel(q_ref, k_ref, v_ref, z_ref, sm_ref, alog_ref, dtb_ref, ng_ref, o_ref, state_ref):
    @pl.when(pl.program_id(1) == 0)
    def _():
        state_ref[...] = jnp.zeros_like(state_ref)

    sm = sm_ref[...]
    lane = lax.broadcasted_iota(I32, (CHUNK, LANES), 1)
    g = -jnp.exp(alog_ref[...]) * _softplus(sm + dtb_ref[...])
    g = jnp.where(lane < S_AA + H_A, g, 0.0)
    beta = _sigmoid(sm)
    gc, gc_t = _chunk_cumsum(g)
    g_last = gc[CHUNK - 1:CHUNK, :]
    e_gc = jnp.exp(gc)
    e_rem = jnp.exp(g_last - gc)
    e_last = jnp.exp(g_last)
    ng = ng_ref[...]

    rows = _GDN_GROUP * CHUNK
    r = lax.broadcasted_iota(I32, (rows, rows), 0)
    c = lax.broadcasted_iota(I32, (rows, rows), 1)
    same_head = (r >> _CHUNK_SHIFT) == (c >> _CHUNK_SHIFT)
    tril = same_head & (r >= c)
    strict = same_head & (r > c)

    for grp in range(H_A // _GDN_GROUP):
        heads = range(grp * _GDN_GROUP, (grp + 1) * _GDN_GROUP)
        stack = lambda ref: jnp.concatenate([ref[:, h * DK_A:(h + 1) * DK_A] for h in heads], axis=0)
        col = lambda arr, off: jnp.concatenate([arr[:, off + h:off + h + 1] for h in heads], axis=0)
        q_st, k_st, v_st = stack(q_ref), stack(k_ref), stack(v_ref)
        beta_st, egc_st, erem_st = col(beta, S_AB), col(e_gc, S_AA), col(e_rem, S_AA)
        g_row = jnp.concatenate([gc_t[S_AA + h:S_AA + h + 1, :] for h in heads], axis=1)
        diff = col(gc, S_AA) - g_row
        decay = jnp.where(tril, jnp.exp(jnp.where(tril, diff, 0.0)), 0.0)
        kb_st = k_st * beta_st
        low = jnp.where(strict, _dot_nt(kb_st, k_st) * decay, 0.0)
        tinv = _unit_lower_inverse(low)
        uw = _dot3(tinv, jnp.concatenate([v_st * beta_st, kb_st * egc_st], axis=1))
        u, w = uw[:, :DK_A], uw[:, DK_A:]
        intra = _dot_nt(q_st, k_st) * decay
        qd_st = q_st * egc_st
        kd_st = k_st * erem_st
        v_new, q_s, states = [], [], []
        for j, h in enumerate(heads):
            rs = slice(j * CHUNK, (j + 1) * CHUNK)
            s = state_ref[h]
            wq_s = _dot(jnp.concatenate([w[rs], qd_st[rs]], axis=0), s)
            v_new.append(u[rs] - wq_s[:CHUNK])
            q_s.append(wq_s[CHUNK:])
            states.append(s)
        v_new_st = jnp.concatenate(v_new, axis=0)
        out = jnp.concatenate(q_s, axis=0) + _dot(intra, v_new_st)
        ms = jnp.mean(out * out, axis=-1, keepdims=True)
        out = out * lax.rsqrt(ms + EPS) * ng
        for j, h in enumerate(heads):
            rs = slice(j * CHUNK, (j + 1) * CHUNK)
            sl = slice(h * DK_A, (h + 1) * DK_A)
            ca = S_AA + h
            state_ref[h] = states[j] * e_last[:, ca:ca + 1] + _dot_tn(kd_st[rs], v_new[j])
            o_ref[:, sl] = (out[rs] * _silu(z_ref[:, sl].astype(F32))).astype(o_ref.dtype)


def gated_deltanet(qkv3, p3, small3, a_log, dt_bias, norm_g):
    bsz, t, _ = qkv3.shape
    nchunks = t // CHUNK

    def row(vals, off):
        return jnp.zeros((1, S_COLS), F32).at[0, off:off + vals.shape[0]].set(vals.astype(F32))

    w = BRANCH_WIDTH
    return pl.pallas_call(
        _gdn_kernel,
        grid=(bsz, nchunks),
        in_specs=[
            pl.BlockSpec((None, CHUNK, w), lambda b, n: (b, n, 0)),
            pl.BlockSpec((None, CHUNK, w), lambda b, n: (b, n, 1)),
            pl.BlockSpec((None, CHUNK, w), lambda b, n: (b, n, 2)),
            pl.BlockSpec((None, CHUNK, w), lambda b, n: (b, n, P_AZ // w)),
            pl.BlockSpec((None, CHUNK, S_COLS), lambda b, n: (b, n, 0)),
            pl.BlockSpec((1, S_COLS), lambda b, n: (0, 0)),
            pl.BlockSpec((1, S_COLS), lambda b, n: (0, 0)),
            pl.BlockSpec((1, DK_A), lambda b, n: (0, 0)),
        ],
        out_specs=pl.BlockSpec((None, CHUNK, w), lambda b, n: (b, n, 0)),
        out_shape=jax.ShapeDtypeStruct((bsz, t, w), BF16),
        scratch_shapes=[pltpu.VMEM((H_A, DK_A, DK_A), F32)],
        compiler_params=_cparams(("parallel", "arbitrary")),
        name="gated_deltanet",
    )(qkv3, qkv3, qkv3, p3, small3, row(a_log, S_AA), row(dt_bias, S_AA), norm_g.reshape(1, DK_A).astype(F32))


def _ssd_kernel(xbc_ref, z_ref, sm_ref, alog_ref, dtb_ref, dskip_ref, ng_ref, o_ref, state_ref):
    @pl.when(pl.program_id(1) == 0)
    def _():
        state_ref[...] = jnp.zeros_like(state_ref)

    sm = sm_ref[...]
    lane = lax.broadcasted_iota(I32, (CHUNK, LANES), 1)
    dt = _softplus(sm + dtb_ref[...])
    a = jnp.where((lane >= S_DDT) & (lane < S_DDT + H_D), dt * -jnp.exp(alog_ref[...]), 0.0)
    acs, acs_t = _chunk_cumsum(a)
    a_last = acs[CHUNK - 1:CHUNK, :]
    e_acs = jnp.exp(acs)
    e_rem = jnp.exp(a_last - acs)
    e_last = jnp.exp(a_last)
    heads_per_group = H_D // N_GROUPS
    ys = []
    for grp in range(N_GROUPS):
        b0 = D_INNER + grp * D_STATE
        c0 = D_INNER + N_GROUPS * D_STATE + grp * D_STATE
        bm = xbc_ref[:, b0:b0 + D_STATE]
        cm = xbc_ref[:, c0:c0 + D_STATE]
        cb = _dot_nt(cm, bm)
        for hh in range(heads_per_group):
            h = grp * heads_per_group + hh
            c = S_DDT + h
            xs = xbc_ref[:, h * P_D:(h + 1) * P_D]
            xc = xs * dt[:, c:c + 1]
            s = state_ref[h]
            y = _dot(cb * _segment_decay(acs, acs_t, c), xc) + _dot_nt(cm * e_acs[:, c:c + 1], s)
            state_ref[h] = s * e_last[:, c:c + 1] + _dot_tn(xc, bm * e_rem[:, c:c + 1])
            ys.append(y)
    y = jnp.concatenate(ys, axis=1) + dskip_ref[...] * xbc_ref[:, 0:D_INNER]
    y = y * _silu(z_ref[...].astype(F32))
    gw = D_INNER // N_GROUPS
    for grp in range(N_GROUPS):
        yg = y[:, grp * gw:(grp + 1) * gw]
        ms = jnp.mean(yg * yg, axis=-1, keepdims=True)
        o_ref[:, grp * gw:(grp + 1) * gw] = (yg * lax.rsqrt(ms + EPS) * ng_ref[:, grp * gw:(grp + 1) * gw]).astype(o_ref.dtype)


def mamba2_ssd(xbc3, p3, small3, a_log, dt_bias, d_skip, norm_g):
    bsz, t, _ = xbc3.shape
    nchunks = t // CHUNK

    def row(vals, off):
        return jnp.zeros((1, S_COLS), F32).at[0, off:off + vals.shape[0]].set(vals.astype(F32))

    return pl.pallas_call(
        _ssd_kernel,
        grid=(bsz, nchunks),
        in_specs=[
            pl.BlockSpec((None, CHUNK, XBC_WIDTH), lambda b, n: (b, n, 0)),
            pl.BlockSpec((None, CHUNK, D_INNER), lambda b, n: (b, n, P_DZ // D_INNER)),
            pl.BlockSpec((None, CHUNK, S_COLS), lambda b, n: (b, n, 0)),
            pl.BlockSpec((1, S_COLS), lambda b, n: (0, 0)),
            pl.BlockSpec((1, S_COLS), lambda b, n: (0, 0)),
            pl.BlockSpec((1, D_INNER), lambda b, n: (0, 0)),
            pl.BlockSpec((1, D_INNER), lambda b, n: (0, 0)),
        ],
        out_specs=pl.BlockSpec((None, CHUNK, D_INNER), lambda b, n: (b, n, 0)),
        out_shape=jax.ShapeDtypeStruct((bsz, t, D_INNER), BF16),
        scratch_shapes=[pltpu.VMEM((H_D, P_D, D_STATE), F32)],
        compiler_params=_cparams(("parallel", "arbitrary")),
        name="mamba2_ssd",
    )(xbc3, p3, small3, row(a_log, S_DDT), row(dt_bias, S_DDT),
      jnp.repeat(d_skip.astype(F32), P_D).reshape(1, D_INNER), norm_g.reshape(1, D_INNER).astype(F32))


def _dsa_prep_kernel(q_ref, ckv_ref, sm_ref, wuk_ref, cg_ref, qg_ref, kg_ref, qh_ref, kk_ref, ct_ref, ki_ref):
    ckv = ckv_ref[...].astype(F32)
    c = ckv * lax.rsqrt(jnp.mean(ckv * ckv, axis=-1, keepdims=True) + EPS) * cg_ref[...]
    cb = c.astype(BF16)
    ct_ref[...] = c.T.astype(BF16)
    kk = _dot(cb, wuk_ref[...])
    kk_ref[...] = (kk * lax.rsqrt(jnp.mean(kk * kk, axis=-1, keepdims=True) + EPS) * kg_ref[...]).astype(BF16)
    for h in range(H_B):
        sl = slice(h * DH_B, (h + 1) * DH_B)
        qh = q_ref[:, sl].astype(F32)
        qh = qh * lax.rsqrt(jnp.mean(qh * qh, axis=-1, keepdims=True) + EPS) * qg_ref[...]
        qh_ref[:, sl] = (qh * DH_B ** -0.5).astype(BF16)
    ki_ref[...] = sm_ref[:, S_BKI:S_BKI + D_IDX].astype(BF16)


def dsa_prep(p3, small3, w_uk, ckv_g, q_g, k_g, tt=512):
    bsz, t, _ = p3.shape
    tt = min(tt, t)
    w = H_B * DH_B
    return pl.pallas_call(
        _dsa_prep_kernel,
        grid=(bsz, t // tt),
        in_specs=[
            pl.BlockSpec((None, tt, w), lambda b, i: (b, i, P_BQ // w)),
            pl.BlockSpec((None, tt, DC_B), lambda b, i: (b, i, P_BCKV // DC_B)),
            pl.BlockSpec((None, tt, S_COLS), lambda b, i: (b, i, 0)),
            pl.BlockSpec((DC_B, DH_B), lambda b, i: (0, 0)),
            pl.BlockSpec((1, DC_B), lambda b, i: (0, 0)),
            pl.BlockSpec((1, DH_B), lambda b, i: (0, 0)),
            pl.BlockSpec((1, DH_B), lambda b, i: (0, 0)),
        ],
        out_specs=[
            pl.BlockSpec((None, tt, w), lambda b, i: (b, i, 0)),
            pl.BlockSpec((None, tt, DH_B), lambda b, i: (b, i, 0)),
            pl.BlockSpec((None, DC_B, tt), lambda b, i: (b, 0, i)),
            pl.BlockSpec((None, tt, D_IDX), lambda b, i: (b, i, 0)),
        ],
        out_shape=[
            jax.ShapeDtypeStruct((bsz, t, w), BF16),
            jax.ShapeDtypeStruct((bsz, t, DH_B), BF16),
            jax.ShapeDtypeStruct((bsz, DC_B, t), BF16),
            jax.ShapeDtypeStruct((bsz, t, D_IDX), BF16),
        ],
        compiler_params=_cparams(("parallel", "parallel")),
        name="dsa_prep",
    )(p3, p3, small3, w_uk.astype(BF16), ckv_g.reshape(1, DC_B).astype(F32),
      q_g.reshape(1, DH_B).astype(F32), k_g.reshape(1, DH_B).astype(F32))


_T5_HALF = N_BUCKETS // 2
_T5_EXACT = _T5_HALF // 2
_T5_FAR = _T5_HALF - 1


def _relbias_kernel(rb_ref, prev_ref, diag_ref):
    h = pl.program_id(0)
    kl = lax.broadcasted_iota(I32, (Q_BLOCK, Q_BLOCK), 0)
    ql = lax.broadcasted_iota(I32, (Q_BLOCK, Q_BLOCK), 1)
    far = rb_ref[_T5_FAR, h]
    for ref, shift in ((prev_ref, -Q_BLOCK), (diag_ref, 0)):
        rel = kl - ql + shift
        n = jnp.abs(rel)
        n2 = n * n
        steps = jnp.zeros_like(n)
        for j in range(1, _T5_HALF - _T5_EXACT):
            steps = steps + (n2 >= (_T5_EXACT * _T5_EXACT) * 2 ** j).astype(I32)
        large = jnp.minimum(_T5_EXACT + steps, _T5_HALF - 1)
        bucket = jnp.where(rel > 0, _T5_HALF, 0) + jnp.where(n < _T5_EXACT, n, large)
        acc = jnp.zeros((Q_BLOCK, Q_BLOCK), F32)
        for b in range(N_BUCKETS):
            acc = jnp.where(bucket == b, rb_ref[b, h], acc)
        ref[...] = acc - far


def relbias_tables(rel_bias):
    shp = jax.ShapeDtypeStruct((H_B, Q_BLOCK, Q_BLOCK), F32)
    spec = pl.BlockSpec((None, Q_BLOCK, Q_BLOCK), lambda h: (h, 0, 0))
    return pl.pallas_call(
        _relbias_kernel,
        grid=(H_B,),
        in_specs=[pl.BlockSpec(memory_space=pltpu.SMEM)],
        out_specs=[spec, spec],
        out_shape=[shp, shp],
        compiler_params=_cparams(("arbitrary",)),
        name="relbias_tables",
    )(rel_bias.astype(F32))


_INT_MIN = -2 ** 31
_SCORE_ROWS = 256
_SWEEP_ROWS = 64


def _sweep_rows(n_rows, init, step):
    def body(c, acc):
        return step(pl.multiple_of(c * _SWEEP_ROWS, _SWEEP_ROWS), acc)
    n_steps = n_rows // _SWEEP_ROWS
    return lax.fori_loop(0, n_steps, body, init, unroll=min(8, n_steps))


def _dsa_kernel(qh_ref, qi_ref, sm_ref, kk_ref, ct_ref, ki_ref, dprev_ref, ddiag_ref, wuvt_ref,
                o_ref, key_ref, selb_ref, lg_ref, p_ref, *, widths, topk):
    i = pl.program_id(1)
    need = (i + 1) * Q_BLOCK
    wi_t = sm_ref[...].T[S_BWI:S_BWI + H_IDX, :] * (H_IDX ** -0.5 * D_IDX ** -0.5)
    q_chunk = (i * Q_BLOCK + lax.broadcasted_iota(I32, (1, Q_BLOCK), 1)) >> _CHUNK_SHIFT
    row_iota = lax.broadcasted_iota(I32, (_SWEEP_ROWS, Q_BLOCK), 0)
    block = lambda ref, r0: ref[pl.ds(r0, _SWEEP_ROWS), :]

    def body(nc):
        for r0 in range(0, nc, min(_SCORE_ROWS, nc)):
            rows = slice(r0, r0 + min(_SCORE_ROWS, nc))
            ki = ki_ref[rows, :]
            sc = None
            for h in range(H_IDX):
                term = wi_t[h:h + 1, :] * jnp.maximum(_dot_nt(ki, qi_ref[:, h * D_IDX:(h + 1) * D_IDX]), 0.0)
                sc = term if sc is None else sc + term
            bits = lax.bitcast_convert_type(sc + 0.0, I32)
            key = jnp.where(bits < 0, bits ^ 0x7FFFFFFF, bits)
            kpos = r0 + lax.broadcasted_iota(I32, key.shape, 0)
            key_ref[rows, :] = jnp.where((kpos >> _CHUNK_SHIFT) <= q_chunk, key, _INT_MIN)

        def count(pred):
            acc = _sweep_rows(nc, jnp.zeros((_SWEEP_ROWS, Q_BLOCK), I32),
                              lambda r0, acc: acc + pred(r0, block(key_ref, r0)).astype(I32))
            return jnp.sum(acc, axis=0, keepdims=True)

        tau = jnp.where(count(lambda r0, k: k >= 0) >= topk, 0, _INT_MIN).astype(I32)

        def vstep(it, tau):
            cand = tau | (jnp.int32(1) << (30 - it))
            return jnp.where(count(lambda r0, k: k >= cand) >= topk, cand, tau)

        tau = lax.fori_loop(0, 31, vstep, tau)
        n_gt = count(lambda r0, k: k > tau)
        n_eq = count(lambda r0, k: k == tau)
        tied = (n_gt + n_eq > topk) & (tau > _INT_MIN)
        nbits = max(1, (nc - 1).bit_length())

        def last_tied_index():
            room = topk - n_gt

            def istep(it, last):
                cand = last | (jnp.int32(1) << (nbits - 1 - it))
                below = count(lambda r0, k: (k == tau) & (r0 + row_iota < cand))
                return jnp.where(below < room, cand, last)

            return lax.fori_loop(0, nbits, istep, jnp.zeros((1, Q_BLOCK), I32))

        last = lax.cond(jnp.max(tied.astype(I32)) > 0, last_tied_index,
                        lambda: jnp.full((1, Q_BLOCK), nc, I32))

        def write_sel(r0, carry):
            k = block(key_ref, r0)
            sel = (k > _INT_MIN) & ((k > tau) | ((k == tau) & (r0 + row_iota <= last)))
            selb_ref[pl.ds(r0, _SWEEP_ROWS), :] = jnp.where(sel, 0.0, -jnp.inf)
            return carry

        _sweep_rows(nc, 0, write_sel)

        kk = kk_ref[0:nc, :]
        ct = ct_ref[:, 0:nc]
        prev_row = pl.multiple_of(jnp.maximum(i - 1, 0) * Q_BLOCK, Q_BLOCK)
        diag_row = pl.multiple_of(i * Q_BLOCK, Q_BLOCK)
        has_prev = (i > 0).astype(F32)
        outs = []
        for h in range(H_B):
            lg_ref[0:nc, :] = _dot_nt(kk, qh_ref[:, h * DH_B:(h + 1) * DH_B]) + selb_ref[0:nc, :]
            lg_ref[pl.ds(prev_row, Q_BLOCK), :] += dprev_ref[h] * has_prev
            lg_ref[pl.ds(diag_row, Q_BLOCK), :] += ddiag_ref[h]
            m = jnp.max(_sweep_rows(nc, jnp.full((_SWEEP_ROWS, Q_BLOCK), -jnp.inf, F32),
                                    lambda r0, acc: jnp.maximum(acc, block(lg_ref, r0))), axis=0, keepdims=True)

            def exp_step(r0, acc):
                p = jnp.exp(block(lg_ref, r0) - m)
                p_ref[pl.ds(r0, _SWEEP_ROWS), :] = p.astype(BF16)
                return acc + p

            denom = jnp.sum(_sweep_rows(nc, jnp.zeros((_SWEEP_ROWS, Q_BLOCK), F32), exp_step), axis=0, keepdims=True)
            o_lat = _dot(ct, p_ref[0:nc, :]) * (1.0 / denom)
            outs.append(_dot(wuvt_ref[h], o_lat.astype(BF16)).T)
        o_ref[...] = jnp.concatenate(outs, axis=1).astype(o_ref.dtype)

    lo = 0
    for nc in widths:
        @pl.when((need > lo) & (need <= nc))
        def _(nc=nc):
            body(nc)
        lo = nc


def dsa_attention(qh3, qi_p3, small3, kk3, ct3, ki3, dprev, ddiag, wuv_t):
    bsz, t, w = qh3.shape
    topk = min(DSA_TOPK, t // 4)
    step = min(512, t)
    widths = tuple(range(step, t + 1, step))
    wq = H_IDX * D_IDX
    full = lambda shape: pl.BlockSpec(shape, lambda b, i: (0,) * len(shape))
    return pl.pallas_call(
        functools.partial(_dsa_kernel, widths=widths, topk=topk),
        grid=(bsz, t // Q_BLOCK),
        in_specs=[
            pl.BlockSpec((None, Q_BLOCK, w), lambda b, i: (b, i, 0)),
            pl.BlockSpec((None, Q_BLOCK, wq), lambda b, i: (b, i, P_BQI // wq)),
            pl.BlockSpec((None, Q_BLOCK, S_COLS), lambda b, i: (b, i, 0)),
            pl.BlockSpec((None, t, DH_B), lambda b, i: (b, 0, 0)),
            pl.BlockSpec((None, DC_B, t), lambda b, i: (b, 0, 0)),
            pl.BlockSpec((None, t, D_IDX), lambda b, i: (b, 0, 0)),
            full((H_B, Q_BLOCK, Q_BLOCK)),
            full((H_B, Q_BLOCK, Q_BLOCK)),
            full((H_B, DH_B, DC_B)),
        ],
        out_specs=pl.BlockSpec((None, Q_BLOCK, w), lambda b, i: (b, i, 0)),
        out_shape=jax.ShapeDtypeStruct((bsz, t, w), BF16),
        scratch_shapes=[pltpu.VMEM((t, Q_BLOCK), I32), pltpu.VMEM((t, Q_BLOCK), F32),
                        pltpu.VMEM((t, Q_BLOCK), F32), pltpu.VMEM((t, Q_BLOCK), BF16)],
        compiler_params=_cparams(("parallel", "arbitrary")),
        name="dsa_attention",
    )(qh3, qi_p3, small3, kk3, ct3, ki3, dprev, ddiag, wuv_t)


def _merge_kernel(xn_ref, o0_ref, o1_ref, o2_ref, o3_ref, g0_ref, g1_ref, g2_ref, g3_ref,
                  u0_ref, u1_ref, u2_ref, u3_ref, out_ref):
    xn = xn_ref[...]
    acc = None
    for o_ref, g_ref, u_ref in ((o0_ref, g0_ref, u0_ref), (o1_ref, g1_ref, u1_ref),
                                (o2_ref, g2_ref, u2_ref), (o3_ref, g3_ref, u3_ref)):
        term = _sigmoid(_dot(xn, g_ref[...])) * _dot(o_ref[...], u_ref[...])
        acc = term if acc is None else acc + term
    out_ref[...] = acc.astype(out_ref.dtype)


def merge_branches(xn, outs, w_gate, w_up, tm=512, tn=256):
    m, d = xn.shape
    tm, tn = min(tm, m), min(tn, d)
    nj = d // tn
    in_specs = [pl.BlockSpec((tm, d), lambda j, i: (i, 0))]
    in_specs += [pl.BlockSpec((tm, BRANCH_WIDTH), lambda j, i: (i, 0)) for _ in range(N_BRANCHES)]
    in_specs += [pl.BlockSpec((d, tn), functools.partial(lambda j, i, br: (0, br * nj + j), br=br))
                 for br in range(N_BRANCHES)]
    in_specs += [pl.BlockSpec((None, BRANCH_WIDTH, tn), functools.partial(lambda j, i, br: (br, 0, j), br=br))
                 for br in range(N_BRANCHES)]
    return pl.pallas_call(
        _merge_kernel,
        grid=(nj, m // tm),
        in_specs=in_specs,
        out_specs=pl.BlockSpec((tm, tn), lambda j, i: (i, j)),
        out_shape=jax.ShapeDtypeStruct((m, d), BF16),
        compiler_params=_cparams(("parallel", "parallel")),
        name="merge_branches",
    )(xn, *outs, *([w_gate] * N_BRANCHES), *([w_up] * N_BRANCHES))


def _glu_kernel(x_ref, w1_ref, w3_ref, o_ref):
    x = x_ref[...]
    o_ref[...] = (_silu(_dot(x, w1_ref[...])) * _dot(x, w3_ref[...])).astype(o_ref.dtype)


def glu_dense(x, w1, w3, tm=1024, tf=256):
    m, d = x.shape
    f = w1.shape[1]
    tm, tf = min(tm, m), min(tf, f)
    assert f % tf == 0
    return pl.pallas_call(
        _glu_kernel,
        grid=(f // tf, m // tm),
        in_specs=[pl.BlockSpec((tm, d), lambda j, i: (i, 0)),
                  pl.BlockSpec((d, tf), lambda j, i: (0, j)),
                  pl.BlockSpec((d, tf), lambda j, i: (0, j))],
        out_specs=pl.BlockSpec((tm, tf), lambda j, i: (i, j)),
        out_shape=jax.ShapeDtypeStruct((m, f), BF16),
        compiler_params=_cparams(("parallel", "parallel")),
        name="glu_dense",
    )(x, w1, w3)


MOE_TM = 512
MOE_TT = 256
SLOT_POS_A, SLOT_POS_B, SLOT_W_A, SLOT_W_B = 0, 1, 2, 3


def _router_kernel(x_ref, g_ref, wr_ref, h_ref, gate_ref, sel_ref):
    x = x_ref[...]
    h = x * lax.rsqrt(jnp.mean(x * x, axis=-1, keepdims=True) + EPS) * g_ref[...]
    h_ref[...] = h
    logits = _dot(h, wr_ref[...], HIGHEST)
    lane = lax.broadcasted_iota(I32, logits.shape, 1)
    logits = jnp.where(lane < N_EXPERTS, logits, -jnp.inf)
    m1 = jnp.max(logits, axis=-1, keepdims=True)
    i1 = jnp.min(jnp.where(logits == m1, lane, LANES), axis=-1, keepdims=True)
    rest = jnp.where(lane == i1, -jnp.inf, logits)
    m2 = jnp.max(rest, axis=-1, keepdims=True)
    i2 = jnp.min(jnp.where(rest == m2, lane, LANES), axis=-1, keepdims=True)
    e2 = jnp.exp(m2 - m1)
    inv = 1.0 / (1.0 + e2)
    gate_ref[...] = jnp.where(lane == i1, inv, 0.0) + jnp.where(lane == i2, e2 * inv, 0.0)
    sel_ref[...] = jnp.where((lane == i1) | (lane == i2), 1.0, 0.0).astype(sel_ref.dtype)


def moe_router(x, g, w_router, tm=256):
    m, d = x.shape
    tm = min(tm, m)
    wr = jnp.zeros((d, LANES), F32).at[:, :N_EXPERTS].set(w_router.astype(F32))
    row = lambda w: pl.BlockSpec((tm, w), lambda i: (i, 0))
    return pl.pallas_call(
        _router_kernel,
        grid=(m // tm,),
        in_specs=[row(d), pl.BlockSpec((1, d), lambda i: (0, 0)), pl.BlockSpec((d, LANES), lambda i: (0, 0))],
        out_specs=[row(d), row(LANES), row(LANES)],
        out_shape=[jax.ShapeDtypeStruct((m, d), F32), jax.ShapeDtypeStruct((m, LANES), F32),
                   jax.ShapeDtypeStruct((m, LANES), BF16)],
        compiler_params=_cparams(("parallel",)),
        name="moe_router",
    )(x, g.reshape(1, d).astype(F32), wr)


def _rank_kernel(sel_ref, rank_ref, cnt_ref, carry_ref):
    @pl.when(pl.program_id(0) == 0)
    def _():
        carry_ref[...] = jnp.zeros_like(carry_ref)

    sel = sel_ref[...]
    n = sel.shape[0]
    earlier = _tril_mask(n, strict=True).astype(BF16)
    rank_ref[...] = _dot(earlier, sel) + carry_ref[...]
    carry_ref[...] += jnp.sum(sel.astype(F32), axis=0, keepdims=True)
    cnt_ref[...] = carry_ref[...]


def moe_rank(sel, tr=512):
    m = sel.shape[0]
    tr = min(tr, m)
    return pl.pallas_call(
        _rank_kernel,
        grid=(m // tr,),
        in_specs=[pl.BlockSpec((tr, LANES), lambda i: (i, 0))],
        out_specs=[pl.BlockSpec((tr, LANES), lambda i: (i, 0)), pl.BlockSpec((1, LANES), lambda i: (0, 0))],
        out_shape=[jax.ShapeDtypeStruct((m, LANES), F32), jax.ShapeDtypeStruct((1, LANES), F32)],
        scratch_shapes=[pltpu.VMEM((1, LANES), F32)],
        compiler_params=_cparams(("arbitrary",)),
        name="moe_rank",
    )(sel)


def _slots_kernel(rank_ref, sel_ref, gate_ref, start_ref, out_ref):
    sel = sel_ref[...].astype(F32) > 0.0
    lane = lax.broadcasted_iota(I32, sel.shape, 1)
    dest = start_ref[...] + rank_ref[...]
    first = jnp.min(jnp.where(sel, lane, LANES), axis=-1, keepdims=True)
    second = jnp.max(jnp.where(sel, lane, -1), axis=-1, keepdims=True)
    pick = lambda arr, idx: jnp.sum(jnp.where(lane == idx, arr, 0.0), axis=-1, keepdims=True)
    gates = gate_ref[...]
    out = jnp.where(lane == SLOT_POS_A, pick(dest, first), 0.0)
    out = jnp.where(lane == SLOT_POS_B, pick(dest, second), out)
    out = jnp.where(lane == SLOT_W_A, pick(gates, first), out)
    out_ref[...] = jnp.where(lane == SLOT_W_B, pick(gates, second), out)


def moe_slots(rank, sel, gates, start_row, tr=512):
    m = sel.shape[0]
    tr = min(tr, m)
    row = pl.BlockSpec((tr, LANES), lambda i: (i, 0))
    return pl.pallas_call(
        _slots_kernel,
        grid=(m // tr,),
        in_specs=[row, row, row, pl.BlockSpec((1, LANES), lambda i: (0, 0))],
        out_specs=row,
        out_shape=jax.ShapeDtypeStruct((m, LANES), F32),
        compiler_params=_cparams(("parallel",)),
        name="moe_slots",
    )(rank, sel, gates, start_row)


def _row_copy(src_ref, src_row, dst_ref, dst_row, sem):
    return pltpu.make_async_copy(src_ref.at[pl.ds(src_row, 1), :], dst_ref.at[pl.ds(dst_row, 1), :], sem)


def _dispatch_kernel(pos_ref, tail_ref, h_ref, xg_ref, zero_ref, sem, *, tm):
    tt = h_ref.shape[0]

    @pl.when(pl.program_id(0) == 0)
    def _():
        zero_ref[...] = jnp.zeros_like(zero_ref)
        fill = lambda e: pltpu.make_async_copy(zero_ref, xg_ref.at[pl.ds(pl.multiple_of(tail_ref[e], tm), tm), :], sem)
        for e in range(tail_ref.shape[0]):
            @pl.when(tail_ref[e] >= 0)
            def _(e=e):
                fill(e).start()
        for e in range(tail_ref.shape[0]):
            @pl.when(tail_ref[e] >= 0)
            def _(e=e):
                fill(e).wait()

    def start(r, carry):
        _row_copy(h_ref, r, xg_ref, pos_ref[0, r], sem).start()
        _row_copy(h_ref, r, xg_ref, pos_ref[0, tt + r], sem).start()
        return carry

    def wait(r, carry):
        _row_copy(h_ref, r, xg_ref, pos_ref[0, r], sem).wait()
        _row_copy(h_ref, r, xg_ref, pos_ref[0, tt + r], sem).wait()
        return carry

    lax.fori_loop(0, tt, start, 0)
    lax.fori_loop(0, tt, wait, 0)


def moe_dispatch(h, pos_tiles, tails, rows, tm):
    m, d = h.shape
    tt = pos_tiles.shape[2] // 2
    return pl.pallas_call(
        functools.partial(_dispatch_kernel, tm=tm),
        grid=(m // tt,),
        in_specs=[pl.BlockSpec((None, 1, 2 * tt), lambda i: (i, 0, 0), memory_space=pltpu.SMEM),
                  pl.BlockSpec(memory_space=pltpu.SMEM),
                  pl.BlockSpec((tt, d), lambda i: (i, 0))],
        out_specs=pl.BlockSpec(memory_space=pl.ANY),
        out_shape=jax.ShapeDtypeStruct((rows, d), F32),
        scratch_shapes=[pltpu.VMEM((tm, d), F32), pltpu.SemaphoreType.DMA(())],
        compiler_params=_cparams(("arbitrary",)),
        name="moe_dispatch",
    )(pos_tiles, tails, h)


def _glu_grouped_kernel(te_ref, nv_ref, x_ref, w1_ref, w3_ref, o_ref):
    valid = pl.program_id(1) < nv_ref[0]

    @pl.when(valid)
    def _():
        x = x_ref[...].astype(BF16)
        o_ref[...] = (_silu(_dot(x, w1_ref[...])) * _dot(x, w3_ref[...])).astype(o_ref.dtype)

    @pl.when(jnp.logical_not(valid))
    def _():
        o_ref[...] = jnp.zeros_like(o_ref)


def _down_grouped_kernel(te_ref, nv_ref, h_ref, w2_ref, o_ref):
    valid = pl.program_id(1) < nv_ref[0]

    @pl.when(valid)
    def _():
        o_ref[...] = _dot(h_ref[...], w2_ref[...])

    @pl.when(jnp.logical_not(valid))
    def _():
        o_ref[...] = jnp.zeros_like(o_ref)


def _grouped_call(kernel, x, weights, tile_expert, n_valid, out_cols, tn, out_dtype, tm, name):
    rows, d = x.shape
    tile = lambda i, nv: jnp.minimum(i, nv[0] - 1)
    in_specs = [pl.BlockSpec((tm, d), lambda j, i, te, nv: (tile(i, nv), 0))]
    in_specs += [pl.BlockSpec((None, w.shape[1], tn), lambda j, i, te, nv: (te[tile(i, nv)], 0, j)) for w in weights]
    grid_spec = pltpu.PrefetchScalarGridSpec(
        num_scalar_prefetch=2,
        grid=(out_cols // tn, rows // tm),
        in_specs=in_specs,
        out_specs=pl.BlockSpec((tm, tn), lambda j, i, te, nv: (i, j)),
    )
    return pl.pallas_call(
        kernel,
        grid_spec=grid_spec,
        out_shape=jax.ShapeDtypeStruct((rows, out_cols), out_dtype),
        compiler_params=_cparams(("arbitrary", "arbitrary")),
        name=name,
    )(tile_expert, n_valid, x, *weights)


def _combine_kernel(pos_ref, x_ref, slot_ref, yg_ref, o_ref, buf_ref, sem):
    tt = x_ref.shape[0]

    def start(r, carry):
        _row_copy(yg_ref, pos_ref[0, r], buf_ref.at[0], r, sem).start()
        _row_copy(yg_ref, pos_ref[0, tt + r], buf_ref.at[1], r, sem).start()
        return carry

    def wait(r, carry):
        _row_copy(yg_ref, pos_ref[0, r], buf_ref.at[0], r, sem).wait()
        _row_copy(yg_ref, pos_ref[0, tt + r], buf_ref.at[1], r, sem).wait()
        return carry

    lax.fori_loop(0, tt, start, 0)
    lax.fori_loop(0, tt, wait, 0)
    slots = slot_ref[...]
    w_a = slots[:, SLOT_W_A:SLOT_W_A + 1]
    w_b = slots[:, SLOT_W_B:SLOT_W_B + 1]
    o_ref[...] = x_ref[...] + w_a * buf_ref[0] + w_b * buf_ref[1]


def moe_combine(x, slots, pos_tiles, yg):
    m, d = x.shape
    tt = pos_tiles.shape[2] // 2
    return pl.pallas_call(
        _combine_kernel,
        grid=(m // tt,),
        in_specs=[pl.BlockSpec((None, 1, 2 * tt), lambda i: (i, 0, 0), memory_space=pltpu.SMEM),
                  pl.BlockSpec((tt, d), lambda i: (i, 0)),
                  pl.BlockSpec((tt, LANES), lambda i: (i, 0)),
                  pl.BlockSpec(memory_space=pl.ANY)],
        out_specs=pl.BlockSpec((tt, d), lambda i: (i, 0)),
        out_shape=jax.ShapeDtypeStruct((m, d), F32),
        scratch_shapes=[pltpu.VMEM((2, tt, d), F32), pltpu.SemaphoreType.DMA(())],
        compiler_params=_cparams(("arbitrary",)),
        name="moe_combine",
    )(pos_tiles, x, slots, yg)


def moe_layer(x2, norm_g, w_router, w1, w3, w2):
    m, d = x2.shape
    tm = min(MOE_TM, m)
    tt = min(MOE_TT, m)
    n_tiles = (TOP_K * m) // tm + N_EXPERTS
    rows = n_tiles * tm
    h, gates, sel = moe_router(x2, norm_g, w_router)
    rank, counts = moe_rank(sel)

    cnt = counts[0, :N_EXPERTS].astype(I32)
    padded = ((cnt + tm - 1) // tm) * tm
    ends = jnp.cumsum(padded)
    starts = ends - padded
    start_row = jnp.zeros((1, LANES), F32).at[0, :N_EXPERTS].set(starts.astype(F32))
    tile_expert = jnp.minimum(jnp.searchsorted(ends, jnp.arange(n_tiles, dtype=I32) * tm, side="right"),
                              N_EXPERTS - 1).astype(I32)
    n_valid = (ends[-1:] // tm).astype(I32)
    unused = ends[-1] + jnp.arange(N_EXPERTS, dtype=I32) * tm
    tails = jnp.concatenate([jnp.where(padded > 0, ends - tm, -1),
                             jnp.where(unused < rows, unused, -1)]).astype(I32)

    slots = moe_slots(rank, sel, gates, start_row)
    pos = slots[:, :2].astype(I32).reshape(m // tt, tt, 2)
    pos_tiles = jnp.swapaxes(pos, 1, 2).reshape(m // tt, 1, 2 * tt)

    xg = moe_dispatch(h, pos_tiles, tails, rows, tm)
    hid = _grouped_call(_glu_grouped_kernel, xg, [w1.astype(BF16), w3.astype(BF16)], tile_expert, n_valid,
                        w1.shape[2], 512, BF16, tm, "moe_glu")
    yg = _grouped_call(_down_grouped_kernel, hid, [w2.astype(BF16)], tile_expert, n_valid,
                       d, 1024, F32, tm, "moe_down")
    return moe_combine(x2, slots, pos_tiles, yg)


def _split_in_proj(w_in_l):
    seg = lambda lo, hi: w_in_l[:, lo:hi]
    big = jnp.concatenate([
        seg(O_AQ, O_AZ), seg(O_AZ, O_AA), seg(O_BQ, O_BCKV), seg(O_CB, O_DZ),
        seg(O_DZ, O_DXBC), seg(O_DXBC, O_DDT), seg(O_BQI, O_BKI), seg(O_BCKV, O_BQI)], axis=1).astype(BF16)
    pad = jnp.zeros((w_in_l.shape[0], S_COLS - (S_DDT + H_D)), w_in_l.dtype)
    small = jnp.concatenate([
        seg(O_AA, O_AB), seg(O_AB, O_BQ), seg(O_BKI, O_BWI), seg(O_BWI, O_CB), seg(O_DDT, _O_END), pad],
        axis=1).astype(BF16)
    gate = w_in_l[:, MIX_COLS:].astype(BF16)
    return big, small, gate


def _mixer_layer(x2, bsz, t, mix_norm_g, w_in_l, conv_a_w, a_log_a, dt_bias_a, out_norm_a_g, ckv_norm_g, w_uk,
                 q_norm_b_g, k_norm_b_g, w_uv, rel_bias, bias_tables, conv_c_w, conv_d_w, conv_d_b, a_log_d,
                 dt_bias_d, d_skip, out_norm_d_g, w_up, w_out):
    m = bsz * t
    xn = rmsnorm(x2, mix_norm_g)
    w_big, w_small, w_gate = _split_in_proj(w_in_l)
    p3 = matmul(xn, w_big, out_dtype=BF16, tm=1024, tn=768, name="in_proj").reshape(bsz, t, P_COLS)
    small3 = matmul(xn, w_small, out_dtype=F32, tm=1024, tn=S_COLS, name="in_proj_small").reshape(bsz, t, S_COLS)

    qkv3 = _conv_call(functools.partial(_conv_qkv_kernel, tc=512), p3, [P_AQKV], 3 * BRANCH_WIDTH,
                      [conv_a_w.astype(F32)], F32, 512, "conv_qkv")
    o_a = gated_deltanet(qkv3, p3, small3, a_log_a, dt_bias_a, out_norm_a_g)

    qh3, kk3, ct3, ki3 = dsa_prep(p3, small3, w_uk, ckv_norm_g, q_norm_b_g, k_norm_b_g)
    dprev, ddiag = bias_tables
    o_b = dsa_attention(qh3, p3, small3, kk3, ct3, ki3, dprev, ddiag,
                        jnp.swapaxes(w_uv, 1, 2).astype(BF16))

    o_c = _conv_call(_conv_gated_kernel, p3, [P_C, P_C + BRANCH_WIDTH, P_C + 2 * BRANCH_WIDTH], BRANCH_WIDTH,
                     [conv_c_w.astype(F32)], BF16, 512, "conv_gated")

    xbc3 = _conv_call(_conv_xbc_kernel, p3, [P_DXBC], XBC_WIDTH,
                      [conv_d_w.astype(F32), conv_d_b.reshape(1, XBC_WIDTH).astype(F32)], F32, 512, "conv_xbc")
    o_d = mamba2_ssd(xbc3, p3, small3, a_log_d, dt_bias_d, d_skip, out_norm_d_g)

    outs = [o.reshape(m, BRANCH_WIDTH) for o in (o_a, o_b, o_c, o_d)]
    merged = merge_branches(xn, outs, w_gate, w_up.astype(BF16))
    return matmul(merged, w_out.astype(BF16), out_dtype=F32, residual=x2, tm=1024, tn=512, name="out_proj")


def _half_k(k):
    return k // 2 if k % (2 * LANES) == 0 else k


def kernel(x, mix_norm_g, w_in, conv_a_w, a_log_a, dt_bias_a, out_norm_a_g, ckv_norm_g, w_uk, q_norm_b_g, k_norm_b_g, w_uv, rel_bias, conv_c_w, conv_d_w, conv_d_b, a_log_d, dt_bias_d, d_skip, out_norm_d_g, w_up, w_out, ffn_norm_g, w1_dense, w3_dense, w2_dense, w_router, w1_moe, w3_moe, w2_moe):
    bsz, t, d = x.shape
    depth = w_in.shape[0]
    x2 = x.reshape(bsz * t, d)
    bias_tables = relbias_tables(rel_bias)
    for l in range(depth):
        x2 = _mixer_layer(x2, bsz, t, mix_norm_g[l], w_in[l], conv_a_w[l], a_log_a[l], dt_bias_a[l],
                          out_norm_a_g[l], ckv_norm_g[l], w_uk[l], q_norm_b_g[l], k_norm_b_g[l], w_uv[l],
                          rel_bias, bias_tables, conv_c_w[l], conv_d_w[l], conv_d_b[l], a_log_d[l],
                          dt_bias_d[l], d_skip[l], out_norm_d_g[l], w_up[l], w_out[l])
        j = l // 2
        if l % 2 == 0:
            h = rmsnorm(x2, ffn_norm_g[l])
            hid = glu_dense(h, w1_dense[j].astype(BF16), w3_dense[j].astype(BF16))
            x2 = matmul(hid, w2_dense[j].astype(BF16), out_dtype=F32, residual=x2,
                        tm=1024, tn=512, tk=_half_k(hid.shape[1]), name="ffn_down")
        else:
            x2 = moe_layer(x2, ffn_norm_g[l], w_router[j], w1_moe[j], w3_moe[j], w2_moe[j])
    return x2.reshape(bsz, t, d)
```

```python
import functools
import math

import jax
import jax.numpy as jnp
from jax import lax
from jax.experimental import pallas as pl
from jax.experimental.pallas import tpu as pltpu

F32 = jnp.float32
BF16 = jnp.bfloat16
I32 = jnp.int32
HIGHEST = lax.Precision.HIGHEST

D_MODEL = 4096
CHUNK = 64
N_BRANCHES = 4
BRANCH_WIDTH = D_MODEL // N_BRANCHES
DK_A = 128
H_A = BRANCH_WIDTH // DK_A
CONV_A = 4
DH_B = 128
H_B = BRANCH_WIDTH // DH_B
DC_B = 256
H_IDX = 8
D_IDX = 64
DSA_TOPK = 256
Q_BLOCK = 128
N_BUCKETS = 32
T5_MAX_DISTANCE = 128
CONV_C = 3
D_INNER = BRANCH_WIDTH
P_D = 64
H_D = D_INNER // P_D
N_GROUPS = 2
D_STATE = 128
CONV_D = 4
XBC_WIDTH = D_INNER + 2 * N_GROUPS * D_STATE
N_EXPERTS = 8
TOP_K = 2
EPS = 1e-6

MIX_SPLITS = (
    BRANCH_WIDTH, BRANCH_WIDTH, BRANCH_WIDTH, BRANCH_WIDTH, H_A, H_A,
    H_B * DH_B, DC_B, H_IDX * D_IDX, D_IDX, H_IDX,
    BRANCH_WIDTH, BRANCH_WIDTH, BRANCH_WIDTH,
    D_INNER, XBC_WIDTH, H_D,
)
MIX_COLS = sum(MIX_SPLITS)
_OFF = [0]
for _w in MIX_SPLITS:
    _OFF.append(_OFF[-1] + _w)
(O_AQ, O_AK, O_AV, O_AZ, O_AA, O_AB, O_BQ, O_BCKV, O_BQI, O_BKI, O_BWI,
 O_CB, O_CC, O_CH, O_DZ, O_DXBC, O_DDT, _O_END) = _OFF

P_AQKV = 0
P_AZ = 3072
P_BQ = 4096
P_C = 5120
P_DZ = 8192
P_DXBC = 9216
P_BQI = 10752
P_BCKV = 11264
P_COLS = 11520
S_AA = 0
S_AB = 8
S_BKI = 16
S_BWI = 80
S_DDT = 88
S_COLS = 128

LANES = 128
VMEM_LIMIT_MB = 56


def _cparams(sem, vmem_mb=VMEM_LIMIT_MB):
    return pltpu.CompilerParams(dimension_semantics=sem, vmem_limit_bytes=vmem_mb * 2 ** 20)


def _softplus(x):
    return jnp.maximum(x, 0.0) + jnp.log(1.0 + jnp.exp(-jnp.abs(x)))


def _sigmoid(x):
    return 1.0 / (1.0 + jnp.exp(-x))


def _silu(x):
    return x * _sigmoid(x)


def _dot(a, b, precision=None):
    return jnp.dot(a, b, preferred_element_type=F32, precision=precision)


def _dot_nt(a, b, precision=None):
    return lax.dot_general(a, b, (((1,), (1,)), ((), ())), preferred_element_type=F32, precision=precision)


def _dot_tn(a, b, precision=None):
    return lax.dot_general(a, b, (((0,), (0,)), ((), ())), preferred_element_type=F32, precision=precision)


def _rmsnorm_kernel(x_ref, g_ref, o_ref):
    x = x_ref[...].astype(F32)
    ms = jnp.mean(x * x, axis=-1, keepdims=True)
    o_ref[...] = (x * lax.rsqrt(ms + EPS) * g_ref[...]).astype(o_ref.dtype)


def rmsnorm(x, g, out_dtype=BF16, tm=256):
    m, d = x.shape
    tm = min(tm, m)
    return pl.pallas_call(
        _rmsnorm_kernel,
        grid=(m // tm,),
        in_specs=[pl.BlockSpec((tm, d), lambda i: (i, 0)), pl.BlockSpec((1, d), lambda i: (0, 0))],
        out_specs=pl.BlockSpec((tm, d), lambda i: (i, 0)),
        out_shape=jax.ShapeDtypeStruct((m, d), out_dtype),
        compiler_params=_cparams(("parallel",)),
        name="rmsnorm",
    )(x, g.reshape(1, d).astype(F32))


def _mm_kernel(*refs, nk, has_res):
    if has_res:
        a_ref, b_ref, r_ref = refs[:3]
        rest = refs[3:]
    else:
        a_ref, b_ref = refs[:2]
        r_ref = None
        rest = refs[2:]
    o_ref = rest[0]
    if nk == 1:
        acc = _dot(a_ref[...], b_ref[...].astype(BF16))
        if r_ref is not None:
            acc = acc + r_ref[...].astype(F32)
        o_ref[...] = acc.astype(o_ref.dtype)
        return
    acc_ref = rest[1]
    k = pl.program_id(2)

    @pl.when(k == 0)
    def _():
        acc_ref[...] = jnp.zeros_like(acc_ref)

    acc_ref[...] += _dot(a_ref[...], b_ref[...].astype(BF16))

    @pl.when(k == nk - 1)
    def _():
        acc = acc_ref[...]
        if r_ref is not None:
            acc = acc + r_ref[...].astype(F32)
        o_ref[...] = acc.astype(o_ref.dtype)


def matmul(a, b, *, out_dtype, residual=None, tm=1024, tn=512, tk=None, name="matmul"):
    m, kdim = a.shape
    _, n = b.shape
    tm, tn = min(tm, m), min(tn, n)
    tk = kdim if tk is None else min(tk, kdim)
    nk = kdim // tk
    assert m % tm == 0 and n % tn == 0 and kdim % tk == 0, (a.shape, b.shape, tm, tn, tk)
    in_specs = [pl.BlockSpec((tm, tk), lambda i, j, k: (i, k)), pl.BlockSpec((tk, tn), lambda i, j, k: (k, j))]
    args = [a, b]
    if residual is not None:
        in_specs.append(pl.BlockSpec((tm, tn), lambda i, j, k: (i, j)))
        args.append(residual)
    scratch = [pltpu.VMEM((tm, tn), F32)] if nk > 1 else []
    return pl.pallas_call(
        functools.partial(_mm_kernel, nk=nk, has_res=residual is not None),
        grid=(m // tm, n // tn, nk),
        in_specs=in_specs,
        out_specs=pl.BlockSpec((tm, tn), lambda i, j, k: (i, j)),
        out_shape=jax.ShapeDtypeStruct((m, n), out_dtype),
        scratch_shapes=scratch,
        compiler_params=_cparams(("parallel", "parallel", "arbitrary")),
        name=name,
    )(*args)


_PAD_ROWS = 8


def _causal_conv(stage_ref, x, w_ref, ksize):
    t = x.shape[0]
    stage_ref[0:_PAD_ROWS, :] = jnp.zeros((_PAD_ROWS, x.shape[1]), F32)
    stage_ref[_PAD_ROWS:_PAD_ROWS + t, :] = x
    acc = x * w_ref[ksize - 1:ksize, :]
    for j in range(ksize - 1):
        s = ksize - 1 - j
        acc = acc + stage_ref[_PAD_ROWS - s:_PAD_ROWS - s + t, :] * w_ref[j:j + 1, :]
    return acc


def _conv_qkv_kernel(x_ref, w_ref, o_ref, stage_ref, *, tc):
    cb = pl.program_id(1)
    y = _silu(_causal_conv(stage_ref, x_ref[...].astype(F32), w_ref, CONV_A))
    col0 = cb * tc
    is_q = col0 < BRANCH_WIDTH
    is_qk = col0 < 2 * BRANCH_WIDTH
    for g in range(tc // DK_A):
        ys = y[:, g * DK_A:(g + 1) * DK_A]
        inv = lax.rsqrt(jnp.sum(ys * ys, axis=-1, keepdims=True) + EPS)
        inv = jnp.where(is_q, inv * DK_A ** -0.5, inv)
        inv = jnp.where(is_qk, inv, jnp.ones_like(inv))
        o_ref[:, g * DK_A:(g + 1) * DK_A] = ys * inv


def _conv_xbc_kernel(x_ref, w_ref, b_ref, o_ref, stage_ref):
    y = _causal_conv(stage_ref, x_ref[...].astype(F32), w_ref, CONV_D) + b_ref[...]
    o_ref[...] = _silu(y)


def _conv_gated_kernel(bg_ref, cg_ref, h_ref, w_ref, o_ref, stage_ref):
    u = cg_ref[...].astype(F32) * h_ref[...].astype(F32)
    y = _causal_conv(stage_ref, u, w_ref, CONV_C)
    o_ref[...] = (bg_ref[...].astype(F32) * y).astype(o_ref.dtype)


def _conv_call(kernel, p3, col_offsets, width, extra, out_dtype, tc, name):
    bsz, t, _ = p3.shape
    tc = min(tc, width)
    in_specs = [pl.BlockSpec((None, t, tc), functools.partial(lambda b, c, o: (b, 0, o + c), o=off // tc))
                for off in col_offsets]
    args = [p3] * len(col_offsets)
    for e in extra:
        in_specs.append(pl.BlockSpec((e.shape[0], tc), lambda b, c: (0, c)))
        args.append(e)
    return pl.pallas_call(
        kernel,
        grid=(bsz, width // tc),
        in_specs=in_specs,
        out_specs=pl.BlockSpec((None, t, tc), lambda b, c: (b, 0, c)),
        out_shape=jax.ShapeDtypeStruct((bsz, t, width), out_dtype),
        scratch_shapes=[pltpu.VMEM((t + _PAD_ROWS, tc), F32)],
        compiler_params=_cparams(("parallel", "parallel")),
        name=name,
    )(*args)


def _tril_mask(n, strict=False):
    r = lax.broadcasted_iota(I32, (n, n), 0)
    c = lax.broadcasted_iota(I32, (n, n), 1)
    return (r > c) if strict else (r >= c)


def _chunk_cumsum(x):
    tril = _tril_mask(CHUNK).astype(F32)
    cs = _dot(tril, x, HIGHEST)
    padded = jnp.concatenate([cs, jnp.zeros((LANES - CHUNK, LANES), F32)], axis=0)
    return cs, padded.T[:, :CHUNK]


def _segment_decay(cs, cs_t, c):
    tril = _tril_mask(CHUNK)
    diff = cs[:, c:c + 1] - cs_t[c:c + 1, :]
    return jnp.where(tril, jnp.exp(jnp.where(tril, diff, 0.0)), 0.0)


_INV_BLOCK = 16
_GDN_GROUP = 2
_GDN_CHUNKS = 2
_CHUNK_SHIFT = CHUNK.bit_length() - 1


def _split_bf16(a):
    hi = a.astype(BF16)
    return hi, (a - hi.astype(F32)).astype(BF16)


def _dot3(a, b):
    a_hi, a_lo = a if isinstance(a, tuple) else _split_bf16(a)
    b_hi, b_lo = b if isinstance(b, tuple) else _split_bf16(b)
    return _dot(a_hi, b_hi) + _dot(a_hi, b_lo) + _dot(a_lo, b_hi)


def _unit_lower_inverses(lows):
    n_rows = lows[0].shape[0]
    r = lax.broadcasted_iota(I32, (n_rows, n_rows), 0)
    c = lax.broadcasted_iota(I32, (n_rows, n_rows), 1)
    shift = _INV_BLOCK.bit_length() - 1
    same = (r >> shift) == (c >> shift)
    eye = (r == c).astype(F32)
    each = lambda fn, *lists: [fn(*args) for args in zip(*lists)]
    lds = each(lambda low: jnp.where(same, low, 0.0), lows)
    los = each(lambda low: jnp.where(same, 0.0, low), lows)
    ps = each(lambda ld: eye - ld, lds)
    xss = each(_split_bf16, lds)
    xs = each(_dot3, xss, xss)
    for _ in range(2):
        xss = each(_split_bf16, xs)
        ps = each(lambda p, x: p + _dot3(p, x), ps, xss)
        xs = each(_dot3, xss, xss)
    ps = each(lambda p, x: p + _dot3(p, x), ps, xs)
    pss = each(_split_bf16, ps)
    ns = each(_dot3, pss, los)
    nss = each(_split_bf16, ns)
    n2s = each(_dot3, nss, nss)
    rrs = each(lambda n, n2: (eye - n) + _dot3(eye - n, n2), ns, n2s)
    return each(_dot3, rrs, pss)


def _gdn_kernel(q_ref, k_ref, v_ref, z_ref, sm_ref, alog_ref, dtb_ref, ng_ref, o_ref, state_ref):
    @pl.when(pl.program_id(1) == 0)
    def _():
        state_ref[...] = jnp.zeros_like(state_ref)

    ng = ng_ref[...]
    lane = lax.broadcasted_iota(I32, (CHUNK, LANES), 1)
    rows = _GDN_GROUP * CHUNK
    r = lax.broadcasted_iota(I32, (rows, rows), 0)
    c = lax.broadcasted_iota(I32, (rows, rows), 1)
    same_head = (r >> _CHUNK_SHIFT) == (c >> _CHUNK_SHIFT)
    tril = same_head & (r >= c)
    strict = same_head & (r > c)
    groups = [range(grp * _GDN_GROUP, (grp + 1) * _GDN_GROUP) for grp in range(H_A // _GDN_GROUP)]

    n_chunks = sm_ref.shape[0] // CHUNK
    e_lasts, problems = [], []
    for ci in range(n_chunks):
        ts = slice(ci * CHUNK, (ci + 1) * CHUNK)
        sm = sm_ref[ts, :]
        g = -jnp.exp(alog_ref[...]) * _softplus(sm + dtb_ref[...])
        g = jnp.where(lane < S_AA + H_A, g, 0.0)
        beta = _sigmoid(sm)
        gc, gc_t = _chunk_cumsum(g)
        g_last = gc[CHUNK - 1:CHUNK, :]
        e_gc = jnp.exp(gc)
        e_rem = jnp.exp(g_last - gc)
        e_lasts.append(jnp.exp(g_last))
        for heads in groups:
            stack = lambda ref: jnp.concatenate([ref[ts, h * DK_A:(h + 1) * DK_A] for h in heads], axis=0)
            col = lambda arr, off: jnp.concatenate([arr[:, off + h:off + h + 1] for h in heads], axis=0)
            q_st, k_st, v_st = stack(q_ref), stack(k_ref), stack(v_ref)
            beta_st, egc_st, erem_st = col(beta, S_AB), col(e_gc, S_AA), col(e_rem, S_AA)
            g_row = jnp.concatenate([gc_t[S_AA + h:S_AA + h + 1, :] for h in heads], axis=1)
            diff = col(gc, S_AA) - g_row
            decay = jnp.where(tril, jnp.exp(jnp.where(tril, diff, 0.0)), 0.0)
            kb_st = k_st * beta_st
            problems.append(dict(q=q_st, k=k_st, kb=kb_st, decay=decay, qd=q_st * egc_st, kd=k_st * erem_st,
                                 rhs=jnp.concatenate([v_st * beta_st, kb_st * egc_st], axis=1)))
    lows = [jnp.where(strict, _dot_nt(p["kb"], p["k"]) * p["decay"], 0.0) for p in problems]
    tinvs = _unit_lower_inverses(lows)
    uws = [_dot3(tinv, p["rhs"]) for tinv, p in zip(tinvs, problems)]
    intras = [_dot_nt(p["q"], p["k"]) * p["decay"] for p in problems]

    head_rows = lambda gi_j: slice(gi_j * CHUNK, (gi_j + 1) * CHUNK)
    for ci in range(n_chunks):
        ts = slice(ci * CHUNK, (ci + 1) * CHUNK)
        e_last = e_lasts[ci]
        probs = range(ci * len(groups), (ci + 1) * len(groups))
        states = [state_ref[h] for h in range(H_A)]
        wq_s = {}
        for pi, heads in zip(probs, groups):
            for j, h in enumerate(heads):
                rs = head_rows(j)
                lhs = jnp.concatenate([uws[pi][rs, DK_A:], problems[pi]["qd"][rs]], axis=0)
                wq_s[h] = _dot(lhs, states[h])
        v_new = {h: uws[pi][head_rows(j), :DK_A] - wq_s[h][:CHUNK]
                 for pi, heads in zip(probs, groups) for j, h in enumerate(heads)}
        outs = [jnp.concatenate([wq_s[h][CHUNK:] for h in heads], axis=0)
                + _dot(intras[pi], jnp.concatenate([v_new[h] for h in heads], axis=0))
                for pi, heads in zip(probs, groups)]
        for pi, heads in zip(probs, groups):
            for j, h in enumerate(heads):
                ca = S_AA + h
                state_ref[h] = states[h] * e_last[:, ca:ca + 1] + _dot_tn(problems[pi]["kd"][head_rows(j)], v_new[h])
        for out, heads in zip(outs, groups):
            ms = jnp.mean(out * out, axis=-1, keepdims=True)
            out = out * lax.rsqrt(ms + EPS) * ng
            for j, h in enumerate(heads):
                sl = slice(h * DK_A, (h + 1) * DK_A)
                o_ref[ts, sl] = (out[head_rows(j)] * _silu(z_ref[ts, sl].astype(F32))).astype(o_ref.dtype)


def gated_deltanet(qkv3, p3, small3, a_log, dt_bias, norm_g):
    bsz, t, _ = qkv3.shape
    tt = _GDN_CHUNKS * CHUNK if t % (_GDN_CHUNKS * CHUNK) == 0 else CHUNK

    def row(vals, off):
        return jnp.zeros((1, S_COLS), F32).at[0, off:off + vals.shape[0]].set(vals.astype(F32))

    w = BRANCH_WIDTH
    return pl.pallas_call(
        _gdn_kernel,
        grid=(bsz, t // tt),
        in_specs=[
            pl.BlockSpec((None, tt, w), lambda b, n: (b, n, 0)),
            pl.BlockSpec((None, tt, w), lambda b, n: (b, n, 1)),
            pl.BlockSpec((None, tt, w), lambda b, n: (b, n, 2)),
            pl.BlockSpec((None, tt, w), lambda b, n: (b, n, P_AZ // w)),
            pl.BlockSpec((None, tt, S_COLS), lambda b, n: (b, n, 0)),
            pl.BlockSpec((1, S_COLS), lambda b, n: (0, 0)),
            pl.BlockSpec((1, S_COLS), lambda b, n: (0, 0)),
            pl.BlockSpec((1, DK_A), lambda b, n: (0, 0)),
        ],
        out_specs=pl.BlockSpec((None, tt, w), lambda b, n: (b, n, 0)),
        out_shape=jax.ShapeDtypeStruct((bsz, t, w), BF16),
        scratch_shapes=[pltpu.VMEM((H_A, DK_A, DK_A), F32)],
        compiler_params=_cparams(("parallel", "arbitrary")),
        name="gated_deltanet",
    )(qkv3, qkv3, qkv3, p3, small3, row(a_log, S_AA), row(dt_bias, S_AA), norm_g.reshape(1, DK_A).astype(F32))


def _ssd_kernel(xbc_ref, z_ref, sm_ref, alog_ref, dtb_ref, dskip_ref, ng_ref, o_ref, state_ref):
    @pl.when(pl.program_id(1) == 0)
    def _():
        state_ref[...] = jnp.zeros_like(state_ref)

    sm = sm_ref[...]
    lane = lax.broadcasted_iota(I32, (CHUNK, LANES), 1)
    dt = _softplus(sm + dtb_ref[...])
    a = jnp.where((lane >= S_DDT) & (lane < S_DDT + H_D), dt * -jnp.exp(alog_ref[...]), 0.0)
    acs, acs_t = _chunk_cumsum(a)
    a_last = acs[CHUNK - 1:CHUNK, :]
    e_acs = jnp.exp(acs)
    e_rem = jnp.exp(a_last - acs)
    e_last = jnp.exp(a_last)
    heads_per_group = H_D // N_GROUPS
    ys = []
    for grp in range(N_GROUPS):
        b0 = D_INNER + grp * D_STATE
        c0 = D_INNER + N_GROUPS * D_STATE + grp * D_STATE
        bm = xbc_ref[:, b0:b0 + D_STATE]
        cm = xbc_ref[:, c0:c0 + D_STATE]
        cb = _dot_nt(cm, bm)
        for hh in range(heads_per_group):
            h = grp * heads_per_group + hh
            c = S_DDT + h
            xs = xbc_ref[:, h * P_D:(h + 1) * P_D]
            xc = xs * dt[:, c:c + 1]
            s = state_ref[h]
            y = _dot(cb * _segment_decay(acs, acs_t, c), xc) + _dot_nt(cm * e_acs[:, c:c + 1], s)
            state_ref[h] = s * e_last[:, c:c + 1] + _dot_tn(xc, bm * e_rem[:, c:c + 1])
            ys.append(y)
    y = jnp.concatenate(ys, axis=1) + dskip_ref[...] * xbc_ref[:, 0:D_INNER]
    y = y * _silu(z_ref[...].astype(F32))
    gw = D_INNER // N_GROUPS
    for grp in range(N_GROUPS):
        yg = y[:, grp * gw:(grp + 1) * gw]
        ms = jnp.mean(yg * yg, axis=-1, keepdims=True)
        o_ref[:, grp * gw:(grp + 1) * gw] = (yg * lax.rsqrt(ms + EPS) * ng_ref[:, grp * gw:(grp + 1) * gw]).astype(o_ref.dtype)


def mamba2_ssd(xbc3, p3, small3, a_log, dt_bias, d_skip, norm_g):
    bsz, t, _ = xbc3.shape
    nchunks = t // CHUNK

    def row(vals, off):
        return jnp.zeros((1, S_COLS), F32).at[0, off:off + vals.shape[0]].set(vals.astype(F32))

    return pl.pallas_call(
        _ssd_kernel,
        grid=(bsz, nchunks),
        in_specs=[
            pl.BlockSpec((None, CHUNK, XBC_WIDTH), lambda b, n: (b, n, 0)),
            pl.BlockSpec((None, CHUNK, D_INNER), lambda b, n: (b, n, P_DZ // D_INNER)),
            pl.BlockSpec((None, CHUNK, S_COLS), lambda b, n: (b, n, 0)),
            pl.BlockSpec((1, S_COLS), lambda b, n: (0, 0)),
            pl.BlockSpec((1, S_COLS), lambda b, n: (0, 0)),
            pl.BlockSpec((1, D_INNER), lambda b, n: (0, 0)),
            pl.BlockSpec((1, D_INNER), lambda b, n: (0, 0)),
        ],
        out_specs=pl.BlockSpec((None, CHUNK, D_INNER), lambda b, n: (b, n, 0)),
        out_shape=jax.ShapeDtypeStruct((bsz, t, D_INNER), BF16),
        scratch_shapes=[pltpu.VMEM((H_D, P_D, D_STATE), F32)],
        compiler_params=_cparams(("parallel", "arbitrary")),
        name="mamba2_ssd",
    )(xbc3, p3, small3, row(a_log, S_DDT), row(dt_bias, S_DDT),
      jnp.repeat(d_skip.astype(F32), P_D).reshape(1, D_INNER), norm_g.reshape(1, D_INNER).astype(F32))


def _dsa_prep_kernel(q_ref, ckv_ref, sm_ref, wuk_ref, cg_ref, qg_ref, kg_ref, qh_ref, kk_ref, ct_ref, ki_ref):
    ckv = ckv_ref[...].astype(F32)
    c = ckv * lax.rsqrt(jnp.mean(ckv * ckv, axis=-1, keepdims=True) + EPS) * cg_ref[...]
    cb = c.astype(BF16)
    ct_ref[...] = c.T.astype(BF16)
    kk = _dot(cb, wuk_ref[...])
    kk_ref[...] = (kk * lax.rsqrt(jnp.mean(kk * kk, axis=-1, keepdims=True) + EPS) * kg_ref[...]).astype(BF16)
    for h in range(H_B):
        sl = slice(h * DH_B, (h + 1) * DH_B)
        qh = q_ref[:, sl].astype(F32)
        qh = qh * lax.rsqrt(jnp.mean(qh * qh, axis=-1, keepdims=True) + EPS) * qg_ref[...]
        qh_ref[:, sl] = (qh * DH_B ** -0.5).astype(BF16)
    ki_ref[...] = sm_ref[:, S_BKI:S_BKI + D_IDX].astype(BF16)


def dsa_prep(p3, small3, w_uk, ckv_g, q_g, k_g, tt=512):
    bsz, t, _ = p3.shape
    tt = min(tt, t)
    w = H_B * DH_B
    return pl.pallas_call(
        _dsa_prep_kernel,
        grid=(bsz, t // tt),
        in_specs=[
            pl.BlockSpec((None, tt, w), lambda b, i: (b, i, P_BQ // w)),
            pl.BlockSpec((None, tt, DC_B), lambda b, i: (b, i, P_BCKV // DC_B)),
            pl.BlockSpec((None, tt, S_COLS), lambda b, i: (b, i, 0)),
            pl.BlockSpec((DC_B, DH_B), lambda b, i: (0, 0)),
            pl.BlockSpec((1, DC_B), lambda b, i: (0, 0)),
            pl.BlockSpec((1, DH_B), lambda b, i: (0, 0)),
            pl.BlockSpec((1, DH_B), lambda b, i: (0, 0)),
        ],
        out_specs=[
            pl.BlockSpec((None, tt, w), lambda b, i: (b, i, 0)),
            pl.BlockSpec((None, tt, DH_B), lambda b, i: (b, i, 0)),
            pl.BlockSpec((None, DC_B, tt), lambda b, i: (b, 0, i)),
            pl.BlockSpec((None, tt, D_IDX), lambda b, i: (b, i, 0)),
        ],
        out_shape=[
            jax.ShapeDtypeStruct((bsz, t, w), BF16),
            jax.ShapeDtypeStruct((bsz, t, DH_B), BF16),
            jax.ShapeDtypeStruct((bsz, DC_B, t), BF16),
            jax.ShapeDtypeStruct((bsz, t, D_IDX), BF16),
        ],
        compiler_params=_cparams(("parallel", "parallel")),
        name="dsa_prep",
    )(p3, p3, small3, w_uk.astype(BF16), ckv_g.reshape(1, DC_B).astype(F32),
      q_g.reshape(1, DH_B).astype(F32), k_g.reshape(1, DH_B).astype(F32))


_T5_HALF = N_BUCKETS // 2
_T5_EXACT = _T5_HALF // 2
_T5_FAR = _T5_HALF - 1


def _relbias_kernel(rb_ref, prev_ref, diag_ref):
    h = pl.program_id(0)
    kl = lax.broadcasted_iota(I32, (Q_BLOCK, Q_BLOCK), 0)
    ql = lax.broadcasted_iota(I32, (Q_BLOCK, Q_BLOCK), 1)
    far = rb_ref[_T5_FAR, h]
    for ref, shift in ((prev_ref, -Q_BLOCK), (diag_ref, 0)):
        rel = kl - ql + shift
        n = jnp.abs(rel)
        n2 = n * n
        steps = jnp.zeros_like(n)
        for j in range(1, _T5_HALF - _T5_EXACT):
            steps = steps + (n2 >= (_T5_EXACT * _T5_EXACT) * 2 ** j).astype(I32)
        large = jnp.minimum(_T5_EXACT + steps, _T5_HALF - 1)
        bucket = jnp.where(rel > 0, _T5_HALF, 0) + jnp.where(n < _T5_EXACT, n, large)
        acc = jnp.zeros((Q_BLOCK, Q_BLOCK), F32)
        for b in range(N_BUCKETS):
            acc = jnp.where(bucket == b, rb_ref[b, h], acc)
        ref[...] = acc - far


def relbias_tables(rel_bias):
    shp = jax.ShapeDtypeStruct((H_B, Q_BLOCK, Q_BLOCK), F32)
    spec = pl.BlockSpec((None, Q_BLOCK, Q_BLOCK), lambda h: (h, 0, 0))
    return pl.pallas_call(
        _relbias_kernel,
        grid=(H_B,),
        in_specs=[pl.BlockSpec(memory_space=pltpu.SMEM)],
        out_specs=[spec, spec],
        out_shape=[shp, shp],
        compiler_params=_cparams(("arbitrary",)),
        name="relbias_tables",
    )(rel_bias.astype(F32))


_INT_MIN = -2 ** 31
_SCORE_ROWS = 256
_SWEEP_ROWS = 64


def _sweep_rows(n_rows, init, step):
    def body(c, acc):
        return step(pl.multiple_of(c * _SWEEP_ROWS, _SWEEP_ROWS), acc)
    n_steps = n_rows // _SWEEP_ROWS
    return lax.fori_loop(0, n_steps, body, init, unroll=min(8, n_steps))


def _dsa_kernel(qh_ref, qi_ref, sm_ref, kk_ref, ct_ref, ki_ref, dprev_ref, ddiag_ref, wuvt_ref,
                o_ref, key_ref, selb_ref, lg_ref, p_ref, *, widths, topk):
    i = pl.program_id(1)
    need = (i + 1) * Q_BLOCK
    wi_t = sm_ref[...].T[S_BWI:S_BWI + H_IDX, :] * (H_IDX ** -0.5 * D_IDX ** -0.5)
    q_chunk = (i * Q_BLOCK + lax.broadcasted_iota(I32, (1, Q_BLOCK), 1)) >> _CHUNK_SHIFT
    row_iota = lax.broadcasted_iota(I32, (_SWEEP_ROWS, Q_BLOCK), 0)
    block = lambda ref, r0: ref[pl.ds(r0, _SWEEP_ROWS), :]

    def body(nc):
        for r0 in range(0, nc, min(_SCORE_ROWS, nc)):
            rows = slice(r0, r0 + min(_SCORE_ROWS, nc))
            ki = ki_ref[rows, :]
            sc = None
            for h in range(H_IDX):
                term = wi_t[h:h + 1, :] * jnp.maximum(_dot_nt(ki, qi_ref[:, h * D_IDX:(h + 1) * D_IDX]), 0.0)
                sc = term if sc is None else sc + term
            bits = lax.bitcast_convert_type(sc + 0.0, I32)
            key = jnp.where(bits < 0, bits ^ 0x7FFFFFFF, bits)
            kpos = r0 + lax.broadcasted_iota(I32, key.shape, 0)
            key_ref[rows, :] = jnp.where((kpos >> _CHUNK_SHIFT) <= q_chunk, key, _INT_MIN)

        def count(pred):
            acc = _sweep_rows(nc, jnp.zeros((_SWEEP_ROWS, Q_BLOCK), I32),
                              lambda r0, acc: acc + pred(r0, block(key_ref, r0)).astype(I32))
            return jnp.sum(acc, axis=0, keepdims=True)

        tau = jnp.where(count(lambda r0, k: k >= 0) >= topk, 0, _INT_MIN).astype(I32)

        def vstep(it, tau):
            cand = tau | (jnp.int32(1) << (30 - it))
            return jnp.where(count(lambda r0, k: k >= cand) >= topk, cand, tau)

        tau = lax.fori_loop(0, 31, vstep, tau)
        n_gt = count(lambda r0, k: k > tau)
        n_eq = count(lambda r0, k: k == tau)
        tied = (n_gt + n_eq > topk) & (tau > _INT_MIN)
        nbits = max(1, (nc - 1).bit_length())

        def last_tied_index():
            room = topk - n_gt

            def istep(it, last):
                cand = last | (jnp.int32(1) << (nbits - 1 - it))
                below = count(lambda r0, k: (k == tau) & (r0 + row_iota < cand))
                return jnp.where(below < room, cand, last)

            return lax.fori_loop(0, nbits, istep, jnp.zeros((1, Q_BLOCK), I32))

        last = lax.cond(jnp.max(tied.astype(I32)) > 0, last_tied_index,
                        lambda: jnp.full((1, Q_BLOCK), nc, I32))

        def write_sel(r0, carry):
            k = block(key_ref, r0)
            sel = (k > _INT_MIN) & ((k > tau) | ((k == tau) & (r0 + row_iota <= last)))
            selb_ref[pl.ds(r0, _SWEEP_ROWS), :] = jnp.where(sel, 0.0, -jnp.inf)
            return carry

        _sweep_rows(nc, 0, write_sel)

        kk = kk_ref[0:nc, :]
        ct = ct_ref[:, 0:nc]
        prev_row = pl.multiple_of(jnp.maximum(i - 1, 0) * Q_BLOCK, Q_BLOCK)
        diag_row = pl.multiple_of(i * Q_BLOCK, Q_BLOCK)
        has_prev = (i > 0).astype(F32)
        for h in range(H_B):
            lg_ref[h, 0:nc, :] = _dot_nt(kk, qh_ref[:, h * DH_B:(h + 1) * DH_B]) + selb_ref[0:nc, :]
            lg_ref[h, pl.ds(prev_row, Q_BLOCK), :] += dprev_ref[h] * has_prev
            lg_ref[h, pl.ds(diag_row, Q_BLOCK), :] += ddiag_ref[h]
        blk = min(_SCORE_ROWS, nc)
        outs = []
        for h in range(H_B):
            slot = h % 2
            m_acc = None
            for r0 in range(0, nc, blk):
                x = lg_ref[h, r0:r0 + blk, :]
                m_acc = x if m_acc is None else jnp.maximum(m_acc, x)
            m = jnp.max(m_acc, axis=0, keepdims=True)
            p_acc = None
            for r0 in range(0, nc, blk):
                p = jnp.exp(lg_ref[h, r0:r0 + blk, :] - m)
                p_ref[slot, r0:r0 + blk, :] = p.astype(BF16)
                p_acc = p if p_acc is None else p_acc + p
            denom = jnp.sum(p_acc, axis=0, keepdims=True)
            o_lat = _dot(ct, p_ref[slot, 0:nc, :]) * (1.0 / denom)
            outs.append(_dot(wuvt_ref[h], o_lat.astype(BF16)).T)
        o_ref[...] = jnp.concatenate(outs, axis=1).astype(o_ref.dtype)

    lo = 0
    for nc in widths:
        @pl.when((need > lo) & (need <= nc))
        def _(nc=nc):
            body(nc)
        lo = nc


def dsa_attention(qh3, qi_p3, small3, kk3, ct3, ki3, dprev, ddiag, wuv_t):
    bsz, t, w = qh3.shape
    topk = min(DSA_TOPK, t // 4)
    step = min(512, t)
    widths = tuple(range(step, t + 1, step))
    wq = H_IDX * D_IDX
    full = lambda shape: pl.BlockSpec(shape, lambda b, i: (0,) * len(shape))
    return pl.pallas_call(
        functools.partial(_dsa_kernel, widths=widths, topk=topk),
        grid=(bsz, t // Q_BLOCK),
        in_specs=[
            pl.BlockSpec((None, Q_BLOCK, w), lambda b, i: (b, i, 0)),
            pl.BlockSpec((None, Q_BLOCK, wq), lambda b, i: (b, i, P_BQI // wq)),
            pl.BlockSpec((None, Q_BLOCK, S_COLS), lambda b, i: (b, i, 0)),
            pl.BlockSpec((None, t, DH_B), lambda b, i: (b, 0, 0)),
            pl.BlockSpec((None, DC_B, t), lambda b, i: (b, 0, 0)),
            pl.BlockSpec((None, t, D_IDX), lambda b, i: (b, 0, 0)),
            full((H_B, Q_BLOCK, Q_BLOCK)),
            full((H_B, Q_BLOCK, Q_BLOCK)),
            full((H_B, DH_B, DC_B)),
        ],
        out_specs=pl.BlockSpec((None, Q_BLOCK, w), lambda b, i: (b, i, 0)),
        out_shape=jax.ShapeDtypeStruct((bsz, t, w), BF16),
        scratch_shapes=[pltpu.VMEM((t, Q_BLOCK), I32), pltpu.VMEM((t, Q_BLOCK), F32),
                        pltpu.VMEM((H_B, t, Q_BLOCK), F32), pltpu.VMEM((2, t, Q_BLOCK), BF16)],
        compiler_params=_cparams(("parallel", "arbitrary")),
        name="dsa_attention",
    )(qh3, qi_p3, small3, kk3, ct3, ki3, dprev, ddiag, wuv_t)


def _merge_kernel(xn_ref, o0_ref, o1_ref, o2_ref, o3_ref, g0_ref, g1_ref, g2_ref, g3_ref,
                  u0_ref, u1_ref, u2_ref, u3_ref, out_ref):
    xn = xn_ref[...]
    acc = None
    for o_ref, g_ref, u_ref in ((o0_ref, g0_ref, u0_ref), (o1_ref, g1_ref, u1_ref),
                                (o2_ref, g2_ref, u2_ref), (o3_ref, g3_ref, u3_ref)):
        term = _sigmoid(_dot(xn, g_ref[...])) * _dot(o_ref[...], u_ref[...].astype(BF16))
        acc = term if acc is None else acc + term
    out_ref[...] = acc.astype(out_ref.dtype)


def merge_branches(xn, outs, w_gate, w_up, tm=512, tn=256):
    m, d = xn.shape
    tm, tn = min(tm, m), min(tn, d)
    nj = d // tn
    in_specs = [pl.BlockSpec((tm, d), lambda j, i: (i, 0))]
    in_specs += [pl.BlockSpec((tm, BRANCH_WIDTH), lambda j, i: (i, 0)) for _ in range(N_BRANCHES)]
    in_specs += [pl.BlockSpec((d, tn), functools.partial(lambda j, i, br: (0, br * nj + j), br=br))
                 for br in range(N_BRANCHES)]
    in_specs += [pl.BlockSpec((None, BRANCH_WIDTH, tn), functools.partial(lambda j, i, br: (br, 0, j), br=br))
                 for br in range(N_BRANCHES)]
    return pl.pallas_call(
        _merge_kernel,
        grid=(nj, m // tm),
        in_specs=in_specs,
        out_specs=pl.BlockSpec((tm, tn), lambda j, i: (i, j)),
        out_shape=jax.ShapeDtypeStruct((m, d), BF16),
        compiler_params=_cparams(("parallel", "parallel")),
        name="merge_branches",
    )(xn, *outs, *([w_gate] * N_BRANCHES), *([w_up] * N_BRANCHES))


def _glu_kernel(x_ref, w1_ref, w3_ref, o_ref):
    x = x_ref[...]
    o_ref[...] = (_silu(_dot(x, w1_ref[...].astype(BF16))) * _dot(x, w3_ref[...].astype(BF16))).astype(o_ref.dtype)


def glu_dense(x, w1, w3, tm=1024, tf=256):
    m, d = x.shape
    f = w1.shape[1]
    tm, tf = min(tm, m), min(tf, f)
    assert f % tf == 0
    return pl.pallas_call(
        _glu_kernel,
        grid=(f // tf, m // tm),
        in_specs=[pl.BlockSpec((tm, d), lambda j, i: (i, 0)),
                  pl.BlockSpec((d, tf), lambda j, i: (0, j)),
                  pl.BlockSpec((d, tf), lambda j, i: (0, j))],
        out_specs=pl.BlockSpec((tm, tf), lambda j, i: (i, j)),
        out_shape=jax.ShapeDtypeStruct((m, f), BF16),
        compiler_params=_cparams(("parallel", "parallel")),
        name="glu_dense",
    )(x, w1, w3)


MOE_TM = 512
MOE_TT = 256
SLOT_POS_A, SLOT_POS_B, SLOT_W_A, SLOT_W_B = 0, 1, 2, 3


def _router_kernel(x_ref, g_ref, wr_ref, h_ref, gate_ref, sel_ref):
    x = x_ref[...]
    h = x * lax.rsqrt(jnp.mean(x * x, axis=-1, keepdims=True) + EPS) * g_ref[...]
    h_ref[...] = h
    logits = _dot(h, wr_ref[...], HIGHEST)
    lane = lax.broadcasted_iota(I32, logits.shape, 1)
    logits = jnp.where(lane < N_EXPERTS, logits, -jnp.inf)
    m1 = jnp.max(logits, axis=-1, keepdims=True)
    i1 = jnp.min(jnp.where(logits == m1, lane, LANES), axis=-1, keepdims=True)
    rest = jnp.where(lane == i1, -jnp.inf, logits)
    m2 = jnp.max(rest, axis=-1, keepdims=True)
    i2 = jnp.min(jnp.where(rest == m2, lane, LANES), axis=-1, keepdims=True)
    e2 = jnp.exp(m2 - m1)
    inv = 1.0 / (1.0 + e2)
    gate_ref[...] = jnp.where(lane == i1, inv, 0.0) + jnp.where(lane == i2, e2 * inv, 0.0)
    sel_ref[...] = jnp.where((lane == i1) | (lane == i2), 1.0, 0.0).astype(sel_ref.dtype)


def moe_router(x, g, w_router, tm=256):
    m, d = x.shape
    tm = min(tm, m)
    wr = jnp.zeros((d, LANES), F32).at[:, :N_EXPERTS].set(w_router.astype(F32))
    row = lambda w: pl.BlockSpec((tm, w), lambda i: (i, 0))
    return pl.pallas_call(
        _router_kernel,
        grid=(m // tm,),
        in_specs=[row(d), pl.BlockSpec((1, d), lambda i: (0, 0)), pl.BlockSpec((d, LANES), lambda i: (0, 0))],
        out_specs=[row(d), row(LANES), row(LANES)],
        out_shape=[jax.ShapeDtypeStruct((m, d), F32), jax.ShapeDtypeStruct((m, LANES), F32),
                   jax.ShapeDtypeStruct((m, LANES), BF16)],
        compiler_params=_cparams(("parallel",)),
        name="moe_router",
    )(x, g.reshape(1, d).astype(F32), wr)


def _rank_kernel(sel_ref, rank_ref, cnt_ref, carry_ref):
    @pl.when(pl.program_id(0) == 0)
    def _():
        carry_ref[...] = jnp.zeros_like(carry_ref)

    sel = sel_ref[...]
    n = sel.shape[0]
    earlier = _tril_mask(n, strict=True).astype(BF16)
    rank_ref[...] = _dot(earlier, sel) + carry_ref[...]
    carry_ref[...] += jnp.sum(sel.astype(F32), axis=0, keepdims=True)
    cnt_ref[...] = carry_ref[...]


def moe_rank(sel, tr=512):
    m = sel.shape[0]
    tr = min(tr, m)
    return pl.pallas_call(
        _rank_kernel,
        grid=(m // tr,),
        in_specs=[pl.BlockSpec((tr, LANES), lambda i: (i, 0))],
        out_specs=[pl.BlockSpec((tr, LANES), lambda i: (i, 0)), pl.BlockSpec((1, LANES), lambda i: (0, 0))],
        out_shape=[jax.ShapeDtypeStruct((m, LANES), F32), jax.ShapeDtypeStruct((1, LANES), F32)],
        scratch_shapes=[pltpu.VMEM((1, LANES), F32)],
        compiler_params=_cparams(("arbitrary",)),
        name="moe_rank",
    )(sel)


def _slots_kernel(rank_ref, sel_ref, gate_ref, start_ref, out_ref):
    sel = sel_ref[...].astype(F32) > 0.0
    lane = lax.broadcasted_iota(I32, sel.shape, 1)
    dest = start_ref[...] + rank_ref[...]
    first = jnp.min(jnp.where(sel, lane, LANES), axis=-1, keepdims=True)
    second = jnp.max(jnp.where(sel, lane, -1), axis=-1, keepdims=True)
    pick = lambda arr, idx: jnp.sum(jnp.where(lane == idx, arr, 0.0), axis=-1, keepdims=True)
    gates = gate_ref[...]
    out = jnp.where(lane == SLOT_POS_A, pick(dest, first), 0.0)
    out = jnp.where(lane == SLOT_POS_B, pick(dest, second), out)
    out = jnp.where(lane == SLOT_W_A, pick(gates, first), out)
    out_ref[...] = jnp.where(lane == SLOT_W_B, pick(gates, second), out)


def moe_slots(rank, sel, gates, start_row, tr=512):
    m = sel.shape[0]
    tr = min(tr, m)
    row = pl.BlockSpec((tr, LANES), lambda i: (i, 0))
    return pl.pallas_call(
        _slots_kernel,
        grid=(m // tr,),
        in_specs=[row, row, row, pl.BlockSpec((1, LANES), lambda i: (0, 0))],
        out_specs=row,
        out_shape=jax.ShapeDtypeStruct((m, LANES), F32),
        compiler_params=_cparams(("parallel",)),
        name="moe_slots",
    )(rank, sel, gates, start_row)


def _row_copy(src_ref, src_row, dst_ref, dst_row, sem):
    return pltpu.make_async_copy(src_ref.at[pl.ds(src_row, 1), :], dst_ref.at[pl.ds(dst_row, 1), :], sem)


def _dispatch_kernel(pos_ref, tail_ref, h_ref, xg_ref, zero_ref, sem, *, tm):
    tt = h_ref.shape[0]

    @pl.when(pl.program_id(0) == 0)
    def _():
        zero_ref[...] = jnp.zeros_like(zero_ref)
        fill = lambda e: pltpu.make_async_copy(zero_ref, xg_ref.at[pl.ds(pl.multiple_of(tail_ref[e], tm), tm), :], sem)
        for e in range(tail_ref.shape[0]):
            @pl.when(tail_ref[e] >= 0)
            def _(e=e):
                fill(e).start()
        for e in range(tail_ref.shape[0]):
            @pl.when(tail_ref[e] >= 0)
            def _(e=e):
                fill(e).wait()

    def start(r, carry):
        _row_copy(h_ref, r, xg_ref, pos_ref[0, r], sem).start()
        _row_copy(h_ref, r, xg_ref, pos_ref[0, tt + r], sem).start()
        return carry

    def wait(r, carry):
        _row_copy(h_ref, r, xg_ref, pos_ref[0, r], sem).wait()
        _row_copy(h_ref, r, xg_ref, pos_ref[0, tt + r], sem).wait()
        return carry

    lax.fori_loop(0, tt, start, 0)
    lax.fori_loop(0, tt, wait, 0)


def moe_dispatch(h, pos_tiles, tails, rows, tm):
    m, d = h.shape
    tt = pos_tiles.shape[2] // 2
    return pl.pallas_call(
        functools.partial(_dispatch_kernel, tm=tm),
        grid=(m // tt,),
        in_specs=[pl.BlockSpec((None, 1, 2 * tt), lambda i: (i, 0, 0), memory_space=pltpu.SMEM),
                  pl.BlockSpec(memory_space=pltpu.SMEM),
                  pl.BlockSpec((tt, d), lambda i: (i, 0))],
        out_specs=pl.BlockSpec(memory_space=pl.ANY),
        out_shape=jax.ShapeDtypeStruct((rows, d), F32),
        scratch_shapes=[pltpu.VMEM((tm, d), F32), pltpu.SemaphoreType.DMA(())],
        compiler_params=_cparams(("arbitrary",)),
        name="moe_dispatch",
    )(pos_tiles, tails, h)


def _glu_grouped_kernel(te_ref, nv_ref, x_ref, w1_ref, w3_ref, o_ref):
    valid = pl.program_id(1) < nv_ref[0]

    @pl.when(valid)
    def _():
        x = x_ref[...].astype(BF16)
        o_ref[...] = (_silu(_dot(x, w1_ref[...].astype(BF16))) * _dot(x, w3_ref[...].astype(BF16))).astype(o_ref.dtype)

    @pl.when(jnp.logical_not(valid))
    def _():
        o_ref[...] = jnp.zeros_like(o_ref)


def _down_grouped_kernel(te_ref, nv_ref, h_ref, w2_ref, o_ref):
    valid = pl.program_id(1) < nv_ref[0]

    @pl.when(valid)
    def _():
        o_ref[...] = _dot(h_ref[...], w2_ref[...].astype(BF16))

    @pl.when(jnp.logical_not(valid))
    def _():
        o_ref[...] = jnp.zeros_like(o_ref)


def _grouped_call(kernel, x, weights, tile_expert, n_valid, out_cols, tn, out_dtype, tm, name):
    rows, d = x.shape
    tile = lambda i, nv: jnp.minimum(i, nv[0] - 1)
    in_specs = [pl.BlockSpec((tm, d), lambda j, i, te, nv: (tile(i, nv), 0))]
    in_specs += [pl.BlockSpec((None, w.shape[1], tn), lambda j, i, te, nv: (te[tile(i, nv)], 0, j)) for w in weights]
    grid_spec = pltpu.PrefetchScalarGridSpec(
        num_scalar_prefetch=2,
        grid=(out_cols // tn, rows // tm),
        in_specs=in_specs,
        out_specs=pl.BlockSpec((tm, tn), lambda j, i, te, nv: (i, j)),
    )
    return pl.pallas_call(
        kernel,
        grid_spec=grid_spec,
        out_shape=jax.ShapeDtypeStruct((rows, out_cols), out_dtype),
        compiler_params=_cparams(("arbitrary", "arbitrary")),
        name=name,
    )(tile_expert, n_valid, x, *weights)


def _combine_kernel(pos_ref, x_ref, slot_ref, yg_ref, o_ref, buf_ref, sem):
    tt = x_ref.shape[0]

    def start(r, carry):
        _row_copy(yg_ref, pos_ref[0, r], buf_ref.at[0], r, sem).start()
        _row_copy(yg_ref, pos_ref[0, tt + r], buf_ref.at[1], r, sem).start()
        return carry

    def wait(r, carry):
        _row_copy(yg_ref, pos_ref[0, r], buf_ref.at[0], r, sem).wait()
        _row_copy(yg_ref, pos_ref[0, tt + r], buf_ref.at[1], r, sem).wait()
        return carry

    lax.fori_loop(0, tt, start, 0)
    lax.fori_loop(0, tt, wait, 0)
    slots = slot_ref[...]
    w_a = slots[:, SLOT_W_A:SLOT_W_A + 1]
    w_b = slots[:, SLOT_W_B:SLOT_W_B + 1]
    o_ref[...] = x_ref[...] + w_a * buf_ref[0] + w_b * buf_ref[1]


def moe_combine(x, slots, pos_tiles, yg):
    m, d = x.shape
    tt = pos_tiles.shape[2] // 2
    return pl.pallas_call(
        _combine_kernel,
        grid=(m // tt,),
        in_specs=[pl.BlockSpec((None, 1, 2 * tt), lambda i: (i, 0, 0), memory_space=pltpu.SMEM),
                  pl.BlockSpec((tt, d), lambda i: (i, 0)),
                  pl.BlockSpec((tt, LANES), lambda i: (i, 0)),
                  pl.BlockSpec(memory_space=pl.ANY)],
        out_specs=pl.BlockSpec((tt, d), lambda i: (i, 0)),
        out_shape=jax.ShapeDtypeStruct((m, d), F32),
        scratch_shapes=[pltpu.VMEM((2, tt, d), F32), pltpu.SemaphoreType.DMA(())],
        compiler_params=_cparams(("arbitrary",)),
        name="moe_combine",
    )(pos_tiles, x, slots, yg)


def moe_layer(x2, norm_g, w_router, w1, w3, w2):
    m, d = x2.shape
    tm = min(MOE_TM, m)
    tt = min(MOE_TT, m)
    n_tiles = (TOP_K * m) // tm + N_EXPERTS
    rows = n_tiles * tm
    h, gates, sel = moe_router(x2, norm_g, w_router)
    rank, counts = moe_rank(sel)

    cnt = counts[0, :N_EXPERTS].astype(I32)
    padded = ((cnt + tm - 1) // tm) * tm
    ends = jnp.cumsum(padded)
    starts = ends - padded
    start_row = jnp.zeros((1, LANES), F32).at[0, :N_EXPERTS].set(starts.astype(F32))
    tile_expert = jnp.minimum(jnp.searchsorted(ends, jnp.arange(n_tiles, dtype=I32) * tm, side="right"),
                              N_EXPERTS - 1).astype(I32)
    n_valid = (ends[-1:] // tm).astype(I32)
    unused = ends[-1] + jnp.arange(N_EXPERTS, dtype=I32) * tm
    tails = jnp.concatenate([jnp.where(padded > 0, ends - tm, -1),
                             jnp.where(unused < rows, unused, -1)]).astype(I32)

    slots = moe_slots(rank, sel, gates, start_row)
    pos = slots[:, :2].astype(I32).reshape(m // tt, tt, 2)
    pos_tiles = jnp.swapaxes(pos, 1, 2).reshape(m // tt, 1, 2 * tt)

    xg = moe_dispatch(h, pos_tiles, tails, rows, tm)
    hid = _grouped_call(_glu_grouped_kernel, xg, [w1, w3], tile_expert, n_valid,
                        w1.shape[2], 512, BF16, tm, "moe_glu")
    yg = _grouped_call(_down_grouped_kernel, hid, [w2], tile_expert, n_valid,
                       d, 1024, F32, tm, "moe_down")
    return moe_combine(x2, slots, pos_tiles, yg)


def _split_in_proj(w_in_l):
    seg = lambda lo, hi: w_in_l[:, lo:hi]
    big = jnp.concatenate([
        seg(O_AQ, O_AZ), seg(O_AZ, O_AA), seg(O_BQ, O_BCKV), seg(O_CB, O_DZ),
        seg(O_DZ, O_DXBC), seg(O_DXBC, O_DDT), seg(O_BQI, O_BKI), seg(O_BCKV, O_BQI)], axis=1).astype(BF16)
    pad = jnp.zeros((w_in_l.shape[0], S_COLS - (S_DDT + H_D)), w_in_l.dtype)
    small = jnp.concatenate([
        seg(O_AA, O_AB), seg(O_AB, O_BQ), seg(O_BKI, O_BWI), seg(O_BWI, O_CB), seg(O_DDT, _O_END), pad],
        axis=1).astype(BF16)
    gate = w_in_l[:, MIX_COLS:].astype(BF16)
    return big, small, gate


def _mixer_layer(x2, bsz, t, mix_norm_g, w_in_l, conv_a_w, a_log_a, dt_bias_a, out_norm_a_g, ckv_norm_g, w_uk,
                 q_norm_b_g, k_norm_b_g, w_uv, rel_bias, bias_tables, conv_c_w, conv_d_w, conv_d_b, a_log_d,
                 dt_bias_d, d_skip, out_norm_d_g, w_up, w_out):
    m = bsz * t
    xn = rmsnorm(x2, mix_norm_g)
    w_big, w_small, w_gate = _split_in_proj(w_in_l)
    p3 = matmul(xn, w_big, out_dtype=BF16, tm=1024, tn=768, name="in_proj").reshape(bsz, t, P_COLS)
    small3 = matmul(xn, w_small, out_dtype=F32, tm=1024, tn=S_COLS, name="in_proj_small").reshape(bsz, t, S_COLS)

    qkv3 = _conv_call(functools.partial(_conv_qkv_kernel, tc=512), p3, [P_AQKV], 3 * BRANCH_WIDTH,
                      [conv_a_w.astype(F32)], F32, 512, "conv_qkv")
    o_a = gated_deltanet(qkv3, p3, small3, a_log_a, dt_bias_a, out_norm_a_g)

    qh3, kk3, ct3, ki3 = dsa_prep(p3, small3, w_uk, ckv_norm_g, q_norm_b_g, k_norm_b_g)
    dprev, ddiag = bias_tables
    o_b = dsa_attention(qh3, p3, small3, kk3, ct3, ki3, dprev, ddiag,
                        jnp.swapaxes(w_uv, 1, 2).astype(BF16))

    o_c = _conv_call(_conv_gated_kernel, p3, [P_C, P_C + BRANCH_WIDTH, P_C + 2 * BRANCH_WIDTH], BRANCH_WIDTH,
                     [conv_c_w.astype(F32)], BF16, 512, "conv_gated")

    xbc3 = _conv_call(_conv_xbc_kernel, p3, [P_DXBC], XBC_WIDTH,
                      [conv_d_w.astype(F32), conv_d_b.reshape(1, XBC_WIDTH).astype(F32)], F32, 512, "conv_xbc")
    o_d = mamba2_ssd(xbc3, p3, small3, a_log_d, dt_bias_d, d_skip, out_norm_d_g)

    outs = [o.reshape(m, BRANCH_WIDTH) for o in (o_a, o_b, o_c, o_d)]
    merged = merge_branches(xn, outs, w_gate, w_up)
    return matmul(merged, w_out, out_dtype=F32, residual=x2, tm=1024, tn=512, name="out_proj")


def _half_k(k):
    return k // 2 if k % (2 * LANES) == 0 else k


def kernel(x, mix_norm_g, w_in, conv_a_w, a_log_a, dt_bias_a, out_norm_a_g, ckv_norm_g, w_uk, q_norm_b_g, k_norm_b_g, w_uv, rel_bias, conv_c_w, conv_d_w, conv_d_b, a_log_d, dt_bias_d, d_skip, out_norm_d_g, w_up, w_out, ffn_norm_g, w1_dense, w3_dense, w2_dense, w_router, w1_moe, w3_moe, w2_moe):
    bsz, t, d = x.shape
    depth = w_in.shape[0]
    x2 = x.reshape(bsz * t, d)
    bias_tables = relbias_tables(rel_bias)
    for l in range(depth):
        x2 = _mixer_layer(x2, bsz, t, mix_norm_g[l], w_in[l], conv_a_w[l], a_log_a[l], dt_bias_a[l],
                          out_norm_a_g[l], ckv_norm_g[l], w_uk[l], q_norm_b_g[l], k_norm_b_g[l], w_uv[l],
                          rel_bias, bias_tables, conv_c_w[l], conv_d_w[l], conv_d_b[l], a_log_d[l],
                          dt_bias_d[l], d_skip[l], out_norm_d_g[l], w_up[l], w_out[l])
        j = l // 2
        if l % 2 == 0:
            h = rmsnorm(x2, ffn_norm_g[l])
            hid = glu_dense(h, w1_dense[j], w3_dense[j])
            x2 = matmul(hid, w2_dense[j].astype(BF16), out_dtype=F32, residual=x2,
                        tm=1024, tn=512, tk=_half_k(hid.shape[1]), name="ffn_down")
        else:
            x2 = moe_layer(x2, ffn_norm_g[l], w_router[j], w1_moe[j], w3_moe[j], w2_moe[j])
    return x2.reshape(bsz, t, d)
```

```python
import functools
import math

import jax
import jax.numpy as jnp
from jax import lax
from jax.experimental import pallas as pl
from jax.experimental.pallas import tpu as pltpu

F32 = jnp.float32
BF16 = jnp.bfloat16
I32 = jnp.int32
HIGHEST = lax.Precision.HIGHEST

D_MODEL = 4096
CHUNK = 64
N_BRANCHES = 4
BRANCH_WIDTH = D_MODEL // N_BRANCHES
DK_A = 128
H_A = BRANCH_WIDTH // DK_A
CONV_A = 4
DH_B = 128
H_B = BRANCH_WIDTH // DH_B
DC_B = 256
H_IDX = 8
D_IDX = 64
DSA_TOPK = 256
Q_BLOCK = 128
N_BUCKETS = 32
T5_MAX_DISTANCE = 128
CONV_C = 3
D_INNER = BRANCH_WIDTH
P_D = 64
H_D = D_INNER // P_D
N_GROUPS = 2
D_STATE = 128
CONV_D = 4
XBC_WIDTH = D_INNER + 2 * N_GROUPS * D_STATE
N_EXPERTS = 8
TOP_K = 2
EPS = 1e-6

MIX_SPLITS = (
    BRANCH_WIDTH, BRANCH_WIDTH, BRANCH_WIDTH, BRANCH_WIDTH, H_A, H_A,
    H_B * DH_B, DC_B, H_IDX * D_IDX, D_IDX, H_IDX,
    BRANCH_WIDTH, BRANCH_WIDTH, BRANCH_WIDTH,
    D_INNER, XBC_WIDTH, H_D,
)
MIX_COLS = sum(MIX_SPLITS)
_OFF = [0]
for _w in MIX_SPLITS:
    _OFF.append(_OFF[-1] + _w)
(O_AQ, O_AK, O_AV, O_AZ, O_AA, O_AB, O_BQ, O_BCKV, O_BQI, O_BKI, O_BWI,
 O_CB, O_CC, O_CH, O_DZ, O_DXBC, O_DDT, _O_END) = _OFF

P_AQKV = 0
P_AZ = 3072
P_BQ = 4096
P_C = 5120
P_DZ = 8192
P_DXBC = 9216
P_BQI = 10752
P_BCKV = 11264
P_COLS = 11520
S_AA = 0
S_AB = 8
S_BKI = 16
S_BWI = 80
S_DDT = 88
S_COLS = 128

LANES = 128
VMEM_LIMIT_MB = 56


def _cparams(sem, vmem_mb=VMEM_LIMIT_MB):
    return pltpu.CompilerParams(dimension_semantics=sem, vmem_limit_bytes=vmem_mb * 2 ** 20)


def _softplus(x):
    return jnp.maximum(x, 0.0) + jnp.log(1.0 + jnp.exp(-jnp.abs(x)))


def _sigmoid(x):
    return 1.0 / (1.0 + jnp.exp(-x))


def _silu(x):
    return x * _sigmoid(x)


def _dot(a, b, precision=None):
    return jnp.dot(a, b, preferred_element_type=F32, precision=precision)


def _dot_nt(a, b, precision=None):
    return lax.dot_general(a, b, (((1,), (1,)), ((), ())), preferred_element_type=F32, precision=precision)


def _dot_tn(a, b, precision=None):
    return lax.dot_general(a, b, (((0,), (0,)), ((), ())), preferred_element_type=F32, precision=precision)


def _rmsnorm_kernel(x_ref, g_ref, o_ref):
    x = x_ref[...].astype(F32)
    ms = jnp.mean(x * x, axis=-1, keepdims=True)
    o_ref[...] = (x * lax.rsqrt(ms + EPS) * g_ref[...]).astype(o_ref.dtype)


def rmsnorm(x, g, out_dtype=BF16, tm=512):
    m, d = x.shape
    tm = min(tm, m)
    return pl.pallas_call(
        _rmsnorm_kernel,
        grid=(m // tm,),
        in_specs=[pl.BlockSpec((tm, d), lambda i: (i, 0)), pl.BlockSpec((1, d), lambda i: (0, 0))],
        out_specs=pl.BlockSpec((tm, d), lambda i: (i, 0)),
        out_shape=jax.ShapeDtypeStruct((m, d), out_dtype),
        compiler_params=_cparams(("parallel",)),
        name="rmsnorm",
    )(x, g.reshape(1, d).astype(F32))


def _mm_kernel(*refs, nk, has_res):
    if has_res:
        a_ref, b_ref, r_ref = refs[:3]
        rest = refs[3:]
    else:
        a_ref, b_ref = refs[:2]
        r_ref = None
        rest = refs[2:]
    o_ref = rest[0]
    if nk == 1:
        acc = _dot(a_ref[...], b_ref[...].astype(BF16))
        if r_ref is not None:
            acc = acc + r_ref[...].astype(F32)
        o_ref[...] = acc.astype(o_ref.dtype)
        return
    acc_ref = rest[1]
    k = pl.program_id(2)

    @pl.when(k == 0)
    def _():
        acc_ref[...] = jnp.zeros_like(acc_ref)

    acc_ref[...] += _dot(a_ref[...], b_ref[...].astype(BF16))

    @pl.when(k == nk - 1)
    def _():
        acc = acc_ref[...]
        if r_ref is not None:
            acc = acc + r_ref[...].astype(F32)
        o_ref[...] = acc.astype(o_ref.dtype)


def matmul(a, b, *, out_dtype, residual=None, tm=1024, tn=512, tk=None, name="matmul"):
    m, kdim = a.shape
    _, n = b.shape
    tm, tn = min(tm, m), min(tn, n)
    tk = kdim if tk is None else min(tk, kdim)
    nk = kdim // tk
    assert m % tm == 0 and n % tn == 0 and kdim % tk == 0, (a.shape, b.shape, tm, tn, tk)
    in_specs = [pl.BlockSpec((tm, tk), lambda i, j, k: (i, k)), pl.BlockSpec((tk, tn), lambda i, j, k: (k, j))]
    args = [a, b]
    if residual is not None:
        in_specs.append(pl.BlockSpec((tm, tn), lambda i, j, k: (i, j)))
        args.append(residual)
    scratch = [pltpu.VMEM((tm, tn), F32)] if nk > 1 else []
    return pl.pallas_call(
        functools.partial(_mm_kernel, nk=nk, has_res=residual is not None),
        grid=(m // tm, n // tn, nk),
        in_specs=in_specs,
        out_specs=pl.BlockSpec((tm, tn), lambda i, j, k: (i, j)),
        out_shape=jax.ShapeDtypeStruct((m, n), out_dtype),
        scratch_shapes=scratch,
        compiler_params=_cparams(("parallel", "parallel", "arbitrary")),
        name=name,
    )(*args)


_PAD_ROWS = 8


def _causal_conv(stage_ref, x, w_ref, ksize):
    t = x.shape[0]
    stage_ref[0:_PAD_ROWS, :] = jnp.zeros((_PAD_ROWS, x.shape[1]), F32)
    stage_ref[_PAD_ROWS:_PAD_ROWS + t, :] = x
    acc = x * w_ref[ksize - 1:ksize, :]
    for j in range(ksize - 1):
        s = ksize - 1 - j
        acc = acc + stage_ref[_PAD_ROWS - s:_PAD_ROWS - s + t, :] * w_ref[j:j + 1, :]
    return acc


def _conv_qkv_kernel(x_ref, w_ref, o_ref, stage_ref, *, tc):
    cb = pl.program_id(1)
    y = _silu(_causal_conv(stage_ref, x_ref[...].astype(F32), w_ref, CONV_A))
    col0 = cb * tc
    is_q = col0 < BRANCH_WIDTH
    is_qk = col0 < 2 * BRANCH_WIDTH
    for g in range(tc // DK_A):
        ys = y[:, g * DK_A:(g + 1) * DK_A]
        inv = lax.rsqrt(jnp.sum(ys * ys, axis=-1, keepdims=True) + EPS)
        inv = jnp.where(is_q, inv * DK_A ** -0.5, inv)
        inv = jnp.where(is_qk, inv, jnp.ones_like(inv))
        o_ref[:, g * DK_A:(g + 1) * DK_A] = ys * inv


def _conv_xbc_kernel(x_ref, w_ref, b_ref, o_ref, stage_ref):
    y = _causal_conv(stage_ref, x_ref[...].astype(F32), w_ref, CONV_D) + b_ref[...]
    o_ref[...] = _silu(y)


def _conv_gated_kernel(bg_ref, cg_ref, h_ref, w_ref, o_ref, stage_ref):
    u = cg_ref[...].astype(F32) * h_ref[...].astype(F32)
    y = _causal_conv(stage_ref, u, w_ref, CONV_C)
    o_ref[...] = (bg_ref[...].astype(F32) * y).astype(o_ref.dtype)


def _conv_call(kernel, p3, col_offsets, width, extra, out_dtype, tc, name):
    bsz, t, _ = p3.shape
    tc = min(tc, width)
    in_specs = [pl.BlockSpec((None, t, tc), functools.partial(lambda b, c, o: (b, 0, o + c), o=off // tc))
                for off in col_offsets]
    args = [p3] * len(col_offsets)
    for e in extra:
        in_specs.append(pl.BlockSpec((e.shape[0], tc), lambda b, c: (0, c)))
        args.append(e)
    return pl.pallas_call(
        kernel,
        grid=(bsz, width // tc),
        in_specs=in_specs,
        out_specs=pl.BlockSpec((None, t, tc), lambda b, c: (b, 0, c)),
        out_shape=jax.ShapeDtypeStruct((bsz, t, width), out_dtype),
        scratch_shapes=[pltpu.VMEM((t + _PAD_ROWS, tc), F32)],
        compiler_params=_cparams(("parallel", "parallel")),
        name=name,
    )(*args)


def _tril_mask(n, strict=False):
    r = lax.broadcasted_iota(I32, (n, n), 0)
    c = lax.broadcasted_iota(I32, (n, n), 1)
    return (r > c) if strict else (r >= c)


def _chunk_cumsum(x):
    tril = _tril_mask(CHUNK).astype(F32)
    cs = _dot(tril, x, HIGHEST)
    padded = jnp.concatenate([cs, jnp.zeros((LANES - CHUNK, LANES), F32)], axis=0)
    return cs, padded.T[:, :CHUNK]


def _segment_decay(cs, cs_t, c):
    tril = _tril_mask(CHUNK)
    diff = cs[:, c:c + 1] - cs_t[c:c + 1, :]
    return jnp.where(tril, jnp.exp(jnp.where(tril, diff, 0.0)), 0.0)


_INV_BLOCK = 16
_GDN_GROUP = 2
_GDN_CHUNKS = 2
_CHUNK_SHIFT = CHUNK.bit_length() - 1


def _split_bf16(a):
    hi = a.astype(BF16)
    return hi, (a - hi.astype(F32)).astype(BF16)


def _dot3(a, b):
    a_hi, a_lo = a if isinstance(a, tuple) else _split_bf16(a)
    b_hi, b_lo = b if isinstance(b, tuple) else _split_bf16(b)
    return _dot(a_hi, b_hi) + _dot(a_hi, b_lo) + _dot(a_lo, b_hi)


def _unit_lower_inverses(lows):
    n_rows = lows[0].shape[0]
    r = lax.broadcasted_iota(I32, (n_rows, n_rows), 0)
    c = lax.broadcasted_iota(I32, (n_rows, n_rows), 1)
    shift = _INV_BLOCK.bit_length() - 1
    same = (r >> shift) == (c >> shift)
    eye = (r == c).astype(F32)
    each = lambda fn, *lists: [fn(*args) for args in zip(*lists)]
    lds = each(lambda low: jnp.where(same, low, 0.0), lows)
    los = each(lambda low: jnp.where(same, 0.0, low), lows)
    ps = each(lambda ld: eye - ld, lds)
    xss = each(_split_bf16, lds)
    xs = each(_dot3, xss, xss)
    for _ in range(2):
        xss = each(_split_bf16, xs)
        ps = each(lambda p, x: p + _dot3(p, x), ps, xss)
        xs = each(_dot3, xss, xss)
    ps = each(lambda p, x: p + _dot3(p, x), ps, xs)
    pss = each(_split_bf16, ps)
    ns = each(_dot3, pss, los)
    nss = each(_split_bf16, ns)
    n2s = each(_dot3, nss, nss)
    rrs = each(lambda n, n2: (eye - n) + _dot3(eye - n, n2), ns, n2s)
    return each(_dot3, rrs, pss)


def _gdn_kernel(q_ref, k_ref, v_ref, z_ref, sm_ref, alog_ref, dtb_ref, ng_ref, o_ref, state_ref):
    @pl.when(pl.program_id(1) == 0)
    def _():
        state_ref[...] = jnp.zeros_like(state_ref)

    ng = ng_ref[...]
    lane = lax.broadcasted_iota(I32, (CHUNK, LANES), 1)
    rows = _GDN_GROUP * CHUNK
    r = lax.broadcasted_iota(I32, (rows, rows), 0)
    c = lax.broadcasted_iota(I32, (rows, rows), 1)
    same_head = (r >> _CHUNK_SHIFT) == (c >> _CHUNK_SHIFT)
    tril = same_head & (r >= c)
    strict = same_head & (r > c)
    groups = [range(grp * _GDN_GROUP, (grp + 1) * _GDN_GROUP) for grp in range(H_A // _GDN_GROUP)]

    n_chunks = sm_ref.shape[0] // CHUNK
    e_lasts, problems = [], []
    for ci in range(n_chunks):
        ts = slice(ci * CHUNK, (ci + 1) * CHUNK)
        sm = sm_ref[ts, :]
        g = -jnp.exp(alog_ref[...]) * _softplus(sm + dtb_ref[...])
        g = jnp.where(lane < S_AA + H_A, g, 0.0)
        beta = _sigmoid(sm)
        gc, gc_t = _chunk_cumsum(g)
        g_last = gc[CHUNK - 1:CHUNK, :]
        e_gc = jnp.exp(gc)
        e_rem = jnp.exp(g_last - gc)
        e_lasts.append(jnp.exp(g_last))
        for heads in groups:
            stack = lambda ref: jnp.concatenate([ref[ts, h * DK_A:(h + 1) * DK_A] for h in heads], axis=0)
            col = lambda arr, off: jnp.concatenate([arr[:, off + h:off + h + 1] for h in heads], axis=0)
            q_st, k_st, v_st = stack(q_ref), stack(k_ref), stack(v_ref)
            beta_st, egc_st, erem_st = col(beta, S_AB), col(e_gc, S_AA), col(e_rem, S_AA)
            g_row = jnp.concatenate([gc_t[S_AA + h:S_AA + h + 1, :] for h in heads], axis=1)
            diff = col(gc, S_AA) - g_row
            decay = jnp.where(tril, jnp.exp(jnp.where(tril, diff, 0.0)), 0.0)
            kb_st = k_st * beta_st
            problems.append(dict(q=q_st, k=k_st, kb=kb_st, decay=decay, qd=q_st * egc_st, kd=k_st * erem_st,
                                 rhs=jnp.concatenate([v_st * beta_st, kb_st * egc_st], axis=1)))
    lows = [jnp.where(strict, _dot_nt(p["kb"], p["k"]) * p["decay"], 0.0) for p in problems]
    tinvs = _unit_lower_inverses(lows)
    uws = [_dot3(tinv, p["rhs"]) for tinv, p in zip(tinvs, problems)]
    intras = [_dot_nt(p["q"], p["k"]) * p["decay"] for p in problems]

    head_rows = lambda gi_j: slice(gi_j * CHUNK, (gi_j + 1) * CHUNK)
    for ci in range(n_chunks):
        ts = slice(ci * CHUNK, (ci + 1) * CHUNK)
        e_last = e_lasts[ci]
        probs = range(ci * len(groups), (ci + 1) * len(groups))
        states = [state_ref[h] for h in range(H_A)]
        wq_s = {}
        for pi, heads in zip(probs, groups):
            for j, h in enumerate(heads):
                rs = head_rows(j)
                lhs = jnp.concatenate([uws[pi][rs, DK_A:], problems[pi]["qd"][rs]], axis=0)
                wq_s[h] = _dot(lhs, states[h])
        v_new = {h: uws[pi][head_rows(j), :DK_A] - wq_s[h][:CHUNK]
                 for pi, heads in zip(probs, groups) for j, h in enumerate(heads)}
        outs = [jnp.concatenate([wq_s[h][CHUNK:] for h in heads], axis=0)
                + _dot(intras[pi], jnp.concatenate([v_new[h] for h in heads], axis=0))
                for pi, heads in zip(probs, groups)]
        for pi, heads in zip(probs, groups):
            for j, h in enumerate(heads):
                ca = S_AA + h
                state_ref[h] = states[h] * e_last[:, ca:ca + 1] + _dot_tn(problems[pi]["kd"][head_rows(j)], v_new[h])
        for out, heads in zip(outs, groups):
            ms = jnp.mean(out * out, axis=-1, keepdims=True)
            out = out * lax.rsqrt(ms + EPS) * ng
            for j, h in enumerate(heads):
                sl = slice(h * DK_A, (h + 1) * DK_A)
                o_ref[ts, sl] = (out[head_rows(j)] * _silu(z_ref[ts, sl].astype(F32))).astype(o_ref.dtype)


def gated_deltanet(qkv3, p3, small3, a_log, dt_bias, norm_g):
    bsz, t, _ = qkv3.shape
    tt = _GDN_CHUNKS * CHUNK if t % (_GDN_CHUNKS * CHUNK) == 0 else CHUNK

    def row(vals, off):
        return jnp.zeros((1, S_COLS), F32).at[0, off:off + vals.shape[0]].set(vals.astype(F32))

    w = BRANCH_WIDTH
    return pl.pallas_call(
        _gdn_kernel,
        grid=(bsz, t // tt),
        in_specs=[
            pl.BlockSpec((None, tt, w), lambda b, n: (b, n, 0)),
            pl.BlockSpec((None, tt, w), lambda b, n: (b, n, 1)),
            pl.BlockSpec((None, tt, w), lambda b, n: (b, n, 2)),
            pl.BlockSpec((None, tt, w), lambda b, n: (b, n, P_AZ // w)),
            pl.BlockSpec((None, tt, S_COLS), lambda b, n: (b, n, 0)),
            pl.BlockSpec((1, S_COLS), lambda b, n: (0, 0)),
            pl.BlockSpec((1, S_COLS), lambda b, n: (0, 0)),
            pl.BlockSpec((1, DK_A), lambda b, n: (0, 0)),
        ],
        out_specs=pl.BlockSpec((None, tt, w), lambda b, n: (b, n, 0)),
        out_shape=jax.ShapeDtypeStruct((bsz, t, w), BF16),
        scratch_shapes=[pltpu.VMEM((H_A, DK_A, DK_A), F32)],
        compiler_params=_cparams(("parallel", "arbitrary")),
        name="gated_deltanet",
    )(qkv3, qkv3, qkv3, p3, small3, row(a_log, S_AA), row(dt_bias, S_AA), norm_g.reshape(1, DK_A).astype(F32))


def _ssd_kernel(xbc_ref, z_ref, sm_ref, alog_ref, dtb_ref, dskip_ref, ng_ref, o_ref, state_ref):
    @pl.when(pl.program_id(1) == 0)
    def _():
        state_ref[...] = jnp.zeros_like(state_ref)

    sm = sm_ref[...]
    lane = lax.broadcasted_iota(I32, (CHUNK, LANES), 1)
    dt = _softplus(sm + dtb_ref[...])
    a = jnp.where((lane >= S_DDT) & (lane < S_DDT + H_D), dt * -jnp.exp(alog_ref[...]), 0.0)
    acs, acs_t = _chunk_cumsum(a)
    a_last = acs[CHUNK - 1:CHUNK, :]
    e_acs = jnp.exp(acs)
    e_rem = jnp.exp(a_last - acs)
    e_last = jnp.exp(a_last)
    heads_per_group = H_D // N_GROUPS
    group_of = lambda h: h // heads_per_group
    bms = [xbc_ref[:, D_INNER + grp * D_STATE:D_INNER + (grp + 1) * D_STATE] for grp in range(N_GROUPS)]
    c0 = D_INNER + N_GROUPS * D_STATE
    cms = [xbc_ref[:, c0 + grp * D_STATE:c0 + (grp + 1) * D_STATE] for grp in range(N_GROUPS)]
    cbs = [_dot_nt(cm, bm) for cm, bm in zip(cms, bms)]
    xcs = [xbc_ref[:, h * P_D:(h + 1) * P_D] * dt[:, S_DDT + h:S_DDT + h + 1] for h in range(H_D)]
    states = [state_ref[h] for h in range(H_D)]
    ys = []
    for h in range(H_D):
        c = S_DDT + h
        ys.append(_dot(cbs[group_of(h)] * _segment_decay(acs, acs_t, c), xcs[h])
                  + _dot_nt(cms[group_of(h)] * e_acs[:, c:c + 1], states[h]))
    for h in range(H_D):
        c = S_DDT + h
        state_ref[h] = states[h] * e_last[:, c:c + 1] + _dot_tn(xcs[h], bms[group_of(h)] * e_rem[:, c:c + 1])
    y = jnp.concatenate(ys, axis=1) + dskip_ref[...] * xbc_ref[:, 0:D_INNER]
    y = y * _silu(z_ref[...].astype(F32))
    gw = D_INNER // N_GROUPS
    for grp in range(N_GROUPS):
        yg = y[:, grp * gw:(grp + 1) * gw]
        ms = jnp.mean(yg * yg, axis=-1, keepdims=True)
        o_ref[:, grp * gw:(grp + 1) * gw] = (yg * lax.rsqrt(ms + EPS) * ng_ref[:, grp * gw:(grp + 1) * gw]).astype(o_ref.dtype)


def mamba2_ssd(xbc3, p3, small3, a_log, dt_bias, d_skip, norm_g):
    bsz, t, _ = xbc3.shape
    nchunks = t // CHUNK

    def row(vals, off):
        return jnp.zeros((1, S_COLS), F32).at[0, off:off + vals.shape[0]].set(vals.astype(F32))

    return pl.pallas_call(
        _ssd_kernel,
        grid=(bsz, nchunks),
        in_specs=[
            pl.BlockSpec((None, CHUNK, XBC_WIDTH), lambda b, n: (b, n, 0)),
            pl.BlockSpec((None, CHUNK, D_INNER), lambda b, n: (b, n, P_DZ // D_INNER)),
            pl.BlockSpec((None, CHUNK, S_COLS), lambda b, n: (b, n, 0)),
            pl.BlockSpec((1, S_COLS), lambda b, n: (0, 0)),
            pl.BlockSpec((1, S_COLS), lambda b, n: (0, 0)),
            pl.BlockSpec((1, D_INNER), lambda b, n: (0, 0)),
            pl.BlockSpec((1, D_INNER), lambda b, n: (0, 0)),
        ],
        out_specs=pl.BlockSpec((None, CHUNK, D_INNER), lambda b, n: (b, n, 0)),
        out_shape=jax.ShapeDtypeStruct((bsz, t, D_INNER), BF16),
        scratch_shapes=[pltpu.VMEM((H_D, P_D, D_STATE), F32)],
        compiler_params=_cparams(("parallel", "arbitrary")),
        name="mamba2_ssd",
    )(xbc3, p3, small3, row(a_log, S_DDT), row(dt_bias, S_DDT),
      jnp.repeat(d_skip.astype(F32), P_D).reshape(1, D_INNER), norm_g.reshape(1, D_INNER).astype(F32))


def _dsa_prep_kernel(q_ref, ckv_ref, sm_ref, wuk_ref, cg_ref, qg_ref, kg_ref, qh_ref, kk_ref, ct_ref, ki_ref):
    ckv = ckv_ref[...].astype(F32)
    c = ckv * lax.rsqrt(jnp.mean(ckv * ckv, axis=-1, keepdims=True) + EPS) * cg_ref[...]
    cb = c.astype(BF16)
    ct_ref[...] = c.T.astype(BF16)
    kk = _dot(cb, wuk_ref[...])
    kk_ref[...] = (kk * lax.rsqrt(jnp.mean(kk * kk, axis=-1, keepdims=True) + EPS) * kg_ref[...]).astype(BF16)
    for h in range(H_B):
        sl = slice(h * DH_B, (h + 1) * DH_B)
        qh = q_ref[:, sl].astype(F32)
        qh = qh * lax.rsqrt(jnp.mean(qh * qh, axis=-1, keepdims=True) + EPS) * qg_ref[...]
        qh_ref[:, sl] = (qh * DH_B ** -0.5).astype(BF16)
    ki_ref[...] = sm_ref[:, S_BKI:S_BKI + D_IDX].astype(BF16)


def dsa_prep(p3, small3, w_uk, ckv_g, q_g, k_g, tt=512):
    bsz, t, _ = p3.shape
    tt = min(tt, t)
    w = H_B * DH_B
    return pl.pallas_call(
        _dsa_prep_kernel,
        grid=(bsz, t // tt),
        in_specs=[
            pl.BlockSpec((None, tt, w), lambda b, i: (b, i, P_BQ // w)),
            pl.BlockSpec((None, tt, DC_B), lambda b, i: (b, i, P_BCKV // DC_B)),
            pl.BlockSpec((None, tt, S_COLS), lambda b, i: (b, i, 0)),
            pl.BlockSpec((DC_B, DH_B), lambda b, i: (0, 0)),
            pl.BlockSpec((1, DC_B), lambda b, i: (0, 0)),
            pl.BlockSpec((1, DH_B), lambda b, i: (0, 0)),
            pl.BlockSpec((1, DH_B), lambda b, i: (0, 0)),
        ],
        out_specs=[
            pl.BlockSpec((None, tt, w), lambda b, i: (b, i, 0)),
            pl.BlockSpec((None, tt, DH_B), lambda b, i: (b, i, 0)),
            pl.BlockSpec((None, DC_B, tt), lambda b, i: (b, 0, i)),
            pl.BlockSpec((None, tt, D_IDX), lambda b, i: (b, i, 0)),
        ],
        out_shape=[
            jax.ShapeDtypeStruct((bsz, t, w), BF16),
            jax.ShapeDtypeStruct((bsz, t, DH_B), BF16),
            jax.ShapeDtypeStruct((bsz, DC_B, t), BF16),
            jax.ShapeDtypeStruct((bsz, t, D_IDX), BF16),
        ],
        compiler_params=_cparams(("parallel", "parallel")),
        name="dsa_prep",
    )(p3, p3, small3, w_uk.astype(BF16), ckv_g.reshape(1, DC_B).astype(F32),
      q_g.reshape(1, DH_B).astype(F32), k_g.reshape(1, DH_B).astype(F32))


_T5_HALF = N_BUCKETS // 2
_T5_EXACT = _T5_HALF // 2
_T5_FAR = _T5_HALF - 1


def _relbias_kernel(rb_ref, prev_ref, diag_ref):
    h = pl.program_id(0)
    kl = lax.broadcasted_iota(I32, (Q_BLOCK, Q_BLOCK), 0)
    ql = lax.broadcasted_iota(I32, (Q_BLOCK, Q_BLOCK), 1)
    far = rb_ref[_T5_FAR, h]
    for ref, shift in ((prev_ref, -Q_BLOCK), (diag_ref, 0)):
        rel = kl - ql + shift
        n = jnp.abs(rel)
        n2 = n * n
        steps = jnp.zeros_like(n)
        for j in range(1, _T5_HALF - _T5_EXACT):
            steps = steps + (n2 >= (_T5_EXACT * _T5_EXACT) * 2 ** j).astype(I32)
        large = jnp.minimum(_T5_EXACT + steps, _T5_HALF - 1)
        bucket = jnp.where(rel > 0, _T5_HALF, 0) + jnp.where(n < _T5_EXACT, n, large)
        acc = jnp.zeros((Q_BLOCK, Q_BLOCK), F32)
        for b in range(N_BUCKETS):
            acc = jnp.where(bucket == b, rb_ref[b, h], acc)
        ref[...] = acc - far


def relbias_tables(rel_bias):
    shp = jax.ShapeDtypeStruct((H_B, Q_BLOCK, Q_BLOCK), F32)
    spec = pl.BlockSpec((None, Q_BLOCK, Q_BLOCK), lambda h: (h, 0, 0))
    return pl.pallas_call(
        _relbias_kernel,
        grid=(H_B,),
        in_specs=[pl.BlockSpec(memory_space=pltpu.SMEM)],
        out_specs=[spec, spec],
        out_shape=[shp, shp],
        compiler_params=_cparams(("arbitrary",)),
        name="relbias_tables",
    )(rel_bias.astype(F32))


_INT_MIN = -2 ** 31
_SCORE_ROWS = 256
_SWEEP_ROWS = 64
_DSA_WIDTH_STEP = 512


def _sweep_rows(n_rows, init, step):
    def body(c, acc):
        return step(pl.multiple_of(c * _SWEEP_ROWS, _SWEEP_ROWS), acc)
    n_steps = n_rows // _SWEEP_ROWS
    return lax.fori_loop(0, n_steps, body, init, unroll=min(8, n_steps))


def _dsa_kernel(qh_ref, qi_ref, sm_ref, kk_ref, ct_ref, ki_ref, dprev_ref, ddiag_ref, wuvt_ref,
                o_ref, key_ref, selb_ref, lg_ref, p_ref, *, widths, topk):
    i = pl.program_id(1)
    need = (i + 1) * Q_BLOCK
    wi_t = sm_ref[...].T[S_BWI:S_BWI + H_IDX, :] * (H_IDX ** -0.5 * D_IDX ** -0.5)
    q_chunk = (i * Q_BLOCK + lax.broadcasted_iota(I32, (1, Q_BLOCK), 1)) >> _CHUNK_SHIFT
    row_iota = lax.broadcasted_iota(I32, (_SWEEP_ROWS, Q_BLOCK), 0)
    block = lambda ref, r0: ref[pl.ds(r0, _SWEEP_ROWS), :]

    def body(nc):
        for r0 in range(0, nc, min(_SCORE_ROWS, nc)):
            rows = slice(r0, r0 + min(_SCORE_ROWS, nc))
            ki = ki_ref[rows, :]
            sc = None
            for h in range(H_IDX):
                term = wi_t[h:h + 1, :] * jnp.maximum(_dot_nt(ki, qi_ref[:, h * D_IDX:(h + 1) * D_IDX]), 0.0)
                sc = term if sc is None else sc + term
            bits = lax.bitcast_convert_type(sc + 0.0, I32)
            key = jnp.where(bits < 0, bits ^ 0x7FFFFFFF, bits)
            kpos = r0 + lax.broadcasted_iota(I32, key.shape, 0)
            key_ref[rows, :] = jnp.where((kpos >> _CHUNK_SHIFT) <= q_chunk, key, _INT_MIN)

        def count(pred):
            acc = _sweep_rows(nc, jnp.zeros((_SWEEP_ROWS, Q_BLOCK), I32),
                              lambda r0, acc: acc + pred(r0, block(key_ref, r0)).astype(I32))
            return jnp.sum(acc, axis=0, keepdims=True)

        tau = jnp.where(count(lambda r0, k: k >= 0) >= topk, 0, _INT_MIN).astype(I32)

        def vstep(it, tau):
            cand = tau | (jnp.int32(1) << (30 - it))
            return jnp.where(count(lambda r0, k: k >= cand) >= topk, cand, tau)

        tau = lax.fori_loop(0, 31, vstep, tau)
        n_gt = count(lambda r0, k: k > tau)
        n_eq = count(lambda r0, k: k == tau)
        tied = (n_gt + n_eq > topk) & (tau > _INT_MIN)
        nbits = max(1, (nc - 1).bit_length())

        def last_tied_index():
            room = topk - n_gt

            def istep(it, last):
                cand = last | (jnp.int32(1) << (nbits - 1 - it))
                below = count(lambda r0, k: (k == tau) & (r0 + row_iota < cand))
                return jnp.where(below < room, cand, last)

            return lax.fori_loop(0, nbits, istep, jnp.zeros((1, Q_BLOCK), I32))

        last = lax.cond(jnp.max(tied.astype(I32)) > 0, last_tied_index,
                        lambda: jnp.full((1, Q_BLOCK), nc, I32))

        def write_sel(r0, carry):
            k = block(key_ref, r0)
            sel = (k > _INT_MIN) & ((k > tau) | ((k == tau) & (r0 + row_iota <= last)))
            selb_ref[pl.ds(r0, _SWEEP_ROWS), :] = jnp.where(sel, 0.0, -jnp.inf)
            return carry

        _sweep_rows(nc, 0, write_sel)

        kk = kk_ref[0:nc, :]
        ct = ct_ref[:, 0:nc]
        prev_row = pl.multiple_of(jnp.maximum(i - 1, 0) * Q_BLOCK, Q_BLOCK)
        diag_row = pl.multiple_of(i * Q_BLOCK, Q_BLOCK)
        has_prev = (i > 0).astype(F32)
        for h in range(H_B):
            lg_ref[h, 0:nc, :] = _dot_nt(kk, qh_ref[:, h * DH_B:(h + 1) * DH_B]) + selb_ref[0:nc, :]
            lg_ref[h, pl.ds(prev_row, Q_BLOCK), :] += dprev_ref[h] * has_prev
            lg_ref[h, pl.ds(diag_row, Q_BLOCK), :] += ddiag_ref[h]
        blk = min(_SCORE_ROWS, nc)
        outs = []
        for h in range(H_B):
            slot = h % 2
            m_acc = None
            for r0 in range(0, nc, blk):
                x = lg_ref[h, r0:r0 + blk, :]
                m_acc = x if m_acc is None else jnp.maximum(m_acc, x)
            m = jnp.max(m_acc, axis=0, keepdims=True)
            p_acc = None
            for r0 in range(0, nc, blk):
                p = jnp.exp(lg_ref[h, r0:r0 + blk, :] - m)
                p_ref[slot, r0:r0 + blk, :] = p.astype(BF16)
                p_acc = p if p_acc is None else p_acc + p
            denom = jnp.sum(p_acc, axis=0, keepdims=True)
            o_lat = _dot(ct, p_ref[slot, 0:nc, :]) * (1.0 / denom)
            outs.append(_dot(wuvt_ref[h], o_lat.astype(BF16)).T)
        o_ref[...] = jnp.concatenate(outs, axis=1).astype(o_ref.dtype)

    lo = 0
    for nc in widths:
        @pl.when((need > lo) & (need <= nc))
        def _(nc=nc):
            body(nc)
        lo = nc


def dsa_attention(qh3, qi_p3, small3, kk3, ct3, ki3, dprev, ddiag, wuv_t):
    bsz, t, w = qh3.shape
    topk = min(DSA_TOPK, t // 4)
    step = min(_DSA_WIDTH_STEP, t)
    widths = tuple(range(step, t + 1, step))
    wq = H_IDX * D_IDX
    full = lambda shape: pl.BlockSpec(shape, lambda b, i: (0,) * len(shape))
    return pl.pallas_call(
        functools.partial(_dsa_kernel, widths=widths, topk=topk),
        grid=(bsz, t // Q_BLOCK),
        in_specs=[
            pl.BlockSpec((None, Q_BLOCK, w), lambda b, i: (b, i, 0)),
            pl.BlockSpec((None, Q_BLOCK, wq), lambda b, i: (b, i, P_BQI // wq)),
            pl.BlockSpec((None, Q_BLOCK, S_COLS), lambda b, i: (b, i, 0)),
            pl.BlockSpec((None, t, DH_B), lambda b, i: (b, 0, 0)),
            pl.BlockSpec((None, DC_B, t), lambda b, i: (b, 0, 0)),
            pl.BlockSpec((None, t, D_IDX), lambda b, i: (b, 0, 0)),
            full((H_B, Q_BLOCK, Q_BLOCK)),
            full((H_B, Q_BLOCK, Q_BLOCK)),
            full((H_B, DH_B, DC_B)),
        ],
        out_specs=pl.BlockSpec((None, Q_BLOCK, w), lambda b, i: (b, i, 0)),
        out_shape=jax.ShapeDtypeStruct((bsz, t, w), BF16),
        scratch_shapes=[pltpu.VMEM((t, Q_BLOCK), I32), pltpu.VMEM((t, Q_BLOCK), F32),
                        pltpu.VMEM((H_B, t, Q_BLOCK), F32), pltpu.VMEM((2, t, Q_BLOCK), BF16)],
        compiler_params=_cparams(("parallel", "arbitrary")),
        name="dsa_attention",
    )(qh3, qi_p3, small3, kk3, ct3, ki3, dprev, ddiag, wuv_t)


def _merge_kernel(xn_ref, o0_ref, o1_ref, o2_ref, o3_ref, g0_ref, g1_ref, g2_ref, g3_ref,
                  u0_ref, u1_ref, u2_ref, u3_ref, out_ref):
    xn = xn_ref[...]
    acc = None
    for o_ref, g_ref, u_ref in ((o0_ref, g0_ref, u0_ref), (o1_ref, g1_ref, u1_ref),
                                (o2_ref, g2_ref, u2_ref), (o3_ref, g3_ref, u3_ref)):
        term = _sigmoid(_dot(xn, g_ref[...])) * _dot(o_ref[...], u_ref[...].astype(BF16))
        acc = term if acc is None else acc + term
    out_ref[...] = acc.astype(out_ref.dtype)


def merge_branches(xn, outs, w_gate, w_up, tm=512, tn=256):
    m, d = xn.shape
    tm, tn = min(tm, m), min(tn, d)
    nj = d // tn
    in_specs = [pl.BlockSpec((tm, d), lambda j, i: (i, 0))]
    in_specs += [pl.BlockSpec((tm, BRANCH_WIDTH), lambda j, i: (i, 0)) for _ in range(N_BRANCHES)]
    in_specs += [pl.BlockSpec((d, tn), functools.partial(lambda j, i, br: (0, br * nj + j), br=br))
                 for br in range(N_BRANCHES)]
    in_specs += [pl.BlockSpec((None, BRANCH_WIDTH, tn), functools.partial(lambda j, i, br: (br, 0, j), br=br))
                 for br in range(N_BRANCHES)]
    return pl.pallas_call(
        _merge_kernel,
        grid=(nj, m // tm),
        in_specs=in_specs,
        out_specs=pl.BlockSpec((tm, tn), lambda j, i: (i, j)),
        out_shape=jax.ShapeDtypeStruct((m, d), BF16),
        compiler_params=_cparams(("parallel", "parallel")),
        name="merge_branches",
    )(xn, *outs, *([w_gate] * N_BRANCHES), *([w_up] * N_BRANCHES))


def _glu_kernel(x_ref, w1_ref, w3_ref, o_ref):
    x = x_ref[...]
    o_ref[...] = (_silu(_dot(x, w1_ref[...].astype(BF16))) * _dot(x, w3_ref[...].astype(BF16))).astype(o_ref.dtype)


def glu_dense(x, w1, w3, tm=1024, tf=256):
    m, d = x.shape
    f = w1.shape[1]
    tm, tf = min(tm, m), min(tf, f)
    assert f % tf == 0
    return pl.pallas_call(
        _glu_kernel,
        grid=(f // tf, m // tm),
        in_specs=[pl.BlockSpec((tm, d), lambda j, i: (i, 0)),
                  pl.BlockSpec((d, tf), lambda j, i: (0, j)),
                  pl.BlockSpec((d, tf), lambda j, i: (0, j))],
        out_specs=pl.BlockSpec((tm, tf), lambda j, i: (i, j)),
        out_shape=jax.ShapeDtypeStruct((m, f), BF16),
        compiler_params=_cparams(("parallel", "parallel")),
        name="glu_dense",
    )(x, w1, w3)


MOE_TM = 512
MOE_TT = 256
SLOT_POS_A, SLOT_POS_B, SLOT_W_A, SLOT_W_B = 0, 1, 2, 3
_ROW_COPY_UNROLL = 8


def _router_kernel(x_ref, g_ref, wr_ref, h_ref, gate_ref, sel_ref):
    x = x_ref[...]
    h = x * lax.rsqrt(jnp.mean(x * x, axis=-1, keepdims=True) + EPS) * g_ref[...]
    h_ref[...] = h
    logits = _dot(h, wr_ref[...], HIGHEST)
    lane = lax.broadcasted_iota(I32, logits.shape, 1)
    logits = jnp.where(lane < N_EXPERTS, logits, -jnp.inf)
    m1 = jnp.max(logits, axis=-1, keepdims=True)
    i1 = jnp.min(jnp.where(logits == m1, lane, LANES), axis=-1, keepdims=True)
    rest = jnp.where(lane == i1, -jnp.inf, logits)
    m2 = jnp.max(rest, axis=-1, keepdims=True)
    i2 = jnp.min(jnp.where(rest == m2, lane, LANES), axis=-1, keepdims=True)
    e2 = jnp.exp(m2 - m1)
    inv = 1.0 / (1.0 + e2)
    gate_ref[...] = jnp.where(lane == i1, inv, 0.0) + jnp.where(lane == i2, e2 * inv, 0.0)
    sel_ref[...] = jnp.where((lane == i1) | (lane == i2), 1.0, 0.0).astype(sel_ref.dtype)


def moe_router(x, g, w_router, tm=512):
    m, d = x.shape
    tm = min(tm, m)
    wr = jnp.zeros((d, LANES), F32).at[:, :N_EXPERTS].set(w_router.astype(F32))
    row = lambda w: pl.BlockSpec((tm, w), lambda i: (i, 0))
    return pl.pallas_call(
        _router_kernel,
        grid=(m // tm,),
        in_specs=[row(d), pl.BlockSpec((1, d), lambda i: (0, 0)), pl.BlockSpec((d, LANES), lambda i: (0, 0))],
        out_specs=[row(d), row(LANES), row(LANES)],
        out_shape=[jax.ShapeDtypeStruct((m, d), F32), jax.ShapeDtypeStruct((m, LANES), F32),
                   jax.ShapeDtypeStruct((m, LANES), BF16)],
        compiler_params=_cparams(("parallel",)),
        name="moe_router",
    )(x, g.reshape(1, d).astype(F32), wr)


def _rank_kernel(sel_ref, rank_ref, cnt_ref, carry_ref):
    @pl.when(pl.program_id(0) == 0)
    def _():
        carry_ref[...] = jnp.zeros_like(carry_ref)

    sel = sel_ref[...]
    n = sel.shape[0]
    earlier = _tril_mask(n, strict=True).astype(BF16)
    rank_ref[...] = _dot(earlier, sel) + carry_ref[...]
    carry_ref[...] += jnp.sum(sel.astype(F32), axis=0, keepdims=True)
    cnt_ref[...] = carry_ref[...]


def moe_rank(sel, tr=512):
    m = sel.shape[0]
    tr = min(tr, m)
    return pl.pallas_call(
        _rank_kernel,
        grid=(m // tr,),
        in_specs=[pl.BlockSpec((tr, LANES), lambda i: (i, 0))],
        out_specs=[pl.BlockSpec((tr, LANES), lambda i: (i, 0)), pl.BlockSpec((1, LANES), lambda i: (0, 0))],
        out_shape=[jax.ShapeDtypeStruct((m, LANES), F32), jax.ShapeDtypeStruct((1, LANES), F32)],
        scratch_shapes=[pltpu.VMEM((1, LANES), F32)],
        compiler_params=_cparams(("arbitrary",)),
        name="moe_rank",
    )(sel)


def _slots_kernel(rank_ref, sel_ref, gate_ref, start_ref, out_ref):
    sel = sel_ref[...].astype(F32) > 0.0
    lane = lax.broadcasted_iota(I32, sel.shape, 1)
    dest = start_ref[...] + rank_ref[...]
    first = jnp.min(jnp.where(sel, lane, LANES), axis=-1, keepdims=True)
    second = jnp.max(jnp.where(sel, lane, -1), axis=-1, keepdims=True)
    pick = lambda arr, idx: jnp.sum(jnp.where(lane == idx, arr, 0.0), axis=-1, keepdims=True)
    gates = gate_ref[...]
    out = jnp.where(lane == SLOT_POS_A, pick(dest, first), 0.0)
    out = jnp.where(lane == SLOT_POS_B, pick(dest, second), out)
    out = jnp.where(lane == SLOT_W_A, pick(gates, first), out)
    out_ref[...] = jnp.where(lane == SLOT_W_B, pick(gates, second), out)


def moe_slots(rank, sel, gates, start_row, tr=512):
    m = sel.shape[0]
    tr = min(tr, m)
    row = pl.BlockSpec((tr, LANES), lambda i: (i, 0))
    return pl.pallas_call(
        _slots_kernel,
        grid=(m // tr,),
        in_specs=[row, row, row, pl.BlockSpec((1, LANES), lambda i: (0, 0))],
        out_specs=row,
        out_shape=jax.ShapeDtypeStruct((m, LANES), F32),
        compiler_params=_cparams(("parallel",)),
        name="moe_slots",
    )(rank, sel, gates, start_row)


def _row_copy(src_ref, src_row, dst_ref, dst_row, sem):
    return pltpu.make_async_copy(src_ref.at[pl.ds(src_row, 1), :], dst_ref.at[pl.ds(dst_row, 1), :], sem)


def _dispatch_kernel(pos_ref, tail_ref, h_ref, xg_ref, zero_ref, sem, *, tm):
    tt = h_ref.shape[0]

    @pl.when(pl.program_id(0) == 0)
    def _():
        zero_ref[...] = jnp.zeros_like(zero_ref)
        fill = lambda e: pltpu.make_async_copy(zero_ref, xg_ref.at[pl.ds(pl.multiple_of(tail_ref[e], tm), tm), :], sem)
        for e in range(tail_ref.shape[0]):
            @pl.when(tail_ref[e] >= 0)
            def _(e=e):
                fill(e).start()
        for e in range(tail_ref.shape[0]):
            @pl.when(tail_ref[e] >= 0)
            def _(e=e):
                fill(e).wait()

    def start(r, carry):
        _row_copy(h_ref, r, xg_ref, pos_ref[0, r], sem).start()
        _row_copy(h_ref, r, xg_ref, pos_ref[0, tt + r], sem).start()
        return carry

    def wait(r, carry):
        _row_copy(h_ref, r, xg_ref, pos_ref[0, r], sem).wait()
        _row_copy(h_ref, r, xg_ref, pos_ref[0, tt + r], sem).wait()
        return carry

    lax.fori_loop(0, tt, start, 0, unroll=_ROW_COPY_UNROLL)
    lax.fori_loop(0, tt, wait, 0, unroll=_ROW_COPY_UNROLL)


def moe_dispatch(h, pos_tiles, tails, rows, tm):
    m, d = h.shape
    tt = pos_tiles.shape[2] // 2
    return pl.pallas_call(
        functools.partial(_dispatch_kernel, tm=tm),
        grid=(m // tt,),
        in_specs=[pl.BlockSpec((None, 1, 2 * tt), lambda i: (i, 0, 0), memory_space=pltpu.SMEM),
                  pl.BlockSpec(memory_space=pltpu.SMEM),
                  pl.BlockSpec((tt, d), lambda i: (i, 0))],
        out_specs=pl.BlockSpec(memory_space=pl.ANY),
        out_shape=jax.ShapeDtypeStruct((rows, d), F32),
        scratch_shapes=[pltpu.VMEM((tm, d), F32), pltpu.SemaphoreType.DMA(())],
        compiler_params=_cparams(("arbitrary",)),
        name="moe_dispatch",
    )(pos_tiles, tails, h)


def _glu_grouped_kernel(te_ref, nv_ref, x_ref, w1_ref, w3_ref, o_ref):
    valid = pl.program_id(1) < nv_ref[0]

    @pl.when(valid)
    def _():
        x = x_ref[...].astype(BF16)
        o_ref[...] = (_silu(_dot(x, w1_ref[...].astype(BF16))) * _dot(x, w3_ref[...].astype(BF16))).astype(o_ref.dtype)

    @pl.when(jnp.logical_not(valid))
    def _():
        o_ref[...] = jnp.zeros_like(o_ref)


def _down_grouped_kernel(te_ref, nv_ref, h_ref, w2_ref, o_ref):
    valid = pl.program_id(1) < nv_ref[0]

    @pl.when(valid)
    def _():
        o_ref[...] = _dot(h_ref[...], w2_ref[...].astype(BF16))

    @pl.when(jnp.logical_not(valid))
    def _():
        o_ref[...] = jnp.zeros_like(o_ref)


def _grouped_call(kernel, x, weights, tile_expert, n_valid, out_cols, tn, out_dtype, tm, name):
    rows, d = x.shape
    tile = lambda i, nv: jnp.minimum(i, nv[0] - 1)
    in_specs = [pl.BlockSpec((tm, d), lambda j, i, te, nv: (tile(i, nv), 0))]
    in_specs += [pl.BlockSpec((None, w.shape[1], tn), lambda j, i, te, nv: (te[tile(i, nv)], 0, j)) for w in weights]
    grid_spec = pltpu.PrefetchScalarGridSpec(
        num_scalar_prefetch=2,
        grid=(out_cols // tn, rows // tm),
        in_specs=in_specs,
        out_specs=pl.BlockSpec((tm, tn), lambda j, i, te, nv: (i, j)),
    )
    return pl.pallas_call(
        kernel,
        grid_spec=grid_spec,
        out_shape=jax.ShapeDtypeStruct((rows, out_cols), out_dtype),
        compiler_params=_cparams(("arbitrary", "arbitrary")),
        name=name,
    )(tile_expert, n_valid, x, *weights)


def _combine_kernel(pos_ref, x_ref, slot_ref, yg_ref, o_ref, buf_ref, sem):
    tt = x_ref.shape[0]

    def start(r, carry):
        _row_copy(yg_ref, pos_ref[0, r], buf_ref.at[0], r, sem).start()
        _row_copy(yg_ref, pos_ref[0, tt + r], buf_ref.at[1], r, sem).start()
        return carry

    def wait(r, carry):
        _row_copy(yg_ref, pos_ref[0, r], buf_ref.at[0], r, sem).wait()
        _row_copy(yg_ref, pos_ref[0, tt + r], buf_ref.at[1], r, sem).wait()
        return carry

    lax.fori_loop(0, tt, start, 0, unroll=_ROW_COPY_UNROLL)
    lax.fori_loop(0, tt, wait, 0, unroll=_ROW_COPY_UNROLL)
    slots = slot_ref[...]
    w_a = slots[:, SLOT_W_A:SLOT_W_A + 1]
    w_b = slots[:, SLOT_W_B:SLOT_W_B + 1]
    o_ref[...] = x_ref[...] + w_a * buf_ref[0] + w_b * buf_ref[1]


def moe_combine(x, slots, pos_tiles, yg):
    m, d = x.shape
    tt = pos_tiles.shape[2] // 2
    return pl.pallas_call(
        _combine_kernel,
        grid=(m // tt,),
        in_specs=[pl.BlockSpec((None, 1, 2 * tt), lambda i: (i, 0, 0), memory_space=pltpu.SMEM),
                  pl.BlockSpec((tt, d), lambda i: (i, 0)),
                  pl.BlockSpec((tt, LANES), lambda i: (i, 0)),
                  pl.BlockSpec(memory_space=pl.ANY)],
        out_specs=pl.BlockSpec((tt, d), lambda i: (i, 0)),
        out_shape=jax.ShapeDtypeStruct((m, d), F32),
        scratch_shapes=[pltpu.VMEM((2, tt, d), F32), pltpu.SemaphoreType.DMA(())],
        compiler_params=_cparams(("arbitrary",)),
        name="moe_combine",
    )(pos_tiles, x, slots, yg)


def moe_layer(x2, norm_g, w_router, w1, w3, w2):
    m, d = x2.shape
    tm = min(MOE_TM, m)
    tt = min(MOE_TT, m)
    n_tiles = (TOP_K * m) // tm + N_EXPERTS
    rows = n_tiles * tm
    h, gates, sel = moe_router(x2, norm_g, w_router)
    rank, counts = moe_rank(sel)

    cnt = counts[0, :N_EXPERTS].astype(I32)
    padded = ((cnt + tm - 1) // tm) * tm
    ends = jnp.cumsum(padded)
    starts = ends - padded
    start_row = jnp.zeros((1, LANES), F32).at[0, :N_EXPERTS].set(starts.astype(F32))
    tile_expert = jnp.minimum(jnp.searchsorted(ends, jnp.arange(n_tiles, dtype=I32) * tm, side="right"),
                              N_EXPERTS - 1).astype(I32)
    n_valid = (ends[-1:] // tm).astype(I32)
    unused = ends[-1] + jnp.arange(N_EXPERTS, dtype=I32) * tm
    tails = jnp.concatenate([jnp.where(padded > 0, ends - tm, -1),
                             jnp.where(unused < rows, unused, -1)]).astype(I32)

    slots = moe_slots(rank, sel, gates, start_row)
    pos = slots[:, :2].astype(I32).reshape(m // tt, tt, 2)
    pos_tiles = jnp.swapaxes(pos, 1, 2).reshape(m // tt, 1, 2 * tt)

    xg = moe_dispatch(h, pos_tiles, tails, rows, tm)
    hid = _grouped_call(_glu_grouped_kernel, xg, [w1, w3], tile_expert, n_valid,
                        w1.shape[2], 512, BF16, tm, "moe_glu")
    yg = _grouped_call(_down_grouped_kernel, hid, [w2], tile_expert, n_valid,
                       d, 1024, F32, tm, "moe_down")
    return moe_combine(x2, slots, pos_tiles, yg)


def _split_in_proj(w_in_l):
    seg = lambda lo, hi: w_in_l[:, lo:hi]
    big = jnp.concatenate([
        seg(O_AQ, O_AZ), seg(O_AZ, O_AA), seg(O_BQ, O_BCKV), seg(O_CB, O_DZ),
        seg(O_DZ, O_DXBC), seg(O_DXBC, O_DDT), seg(O_BQI, O_BKI), seg(O_BCKV, O_BQI)], axis=1).astype(BF16)
    pad = jnp.zeros((w_in_l.shape[0], S_COLS - (S_DDT + H_D)), w_in_l.dtype)
    small = jnp.concatenate([
        seg(O_AA, O_AB), seg(O_AB, O_BQ), seg(O_BKI, O_BWI), seg(O_BWI, O_CB), seg(O_DDT, _O_END), pad],
        axis=1).astype(BF16)
    gate = w_in_l[:, MIX_COLS:].astype(BF16)
    return big, small, gate


def _mixer_layer(x2, bsz, t, mix_norm_g, w_in_l, conv_a_w, a_log_a, dt_bias_a, out_norm_a_g, ckv_norm_g, w_uk,
                 q_norm_b_g, k_norm_b_g, w_uv, rel_bias, bias_tables, conv_c_w, conv_d_w, conv_d_b, a_log_d,
                 dt_bias_d, d_skip, out_norm_d_g, w_up, w_out):
    m = bsz * t
    xn = rmsnorm(x2, mix_norm_g)
    w_big, w_small, w_gate = _split_in_proj(w_in_l)
    p3 = matmul(xn, w_big, out_dtype=BF16, tm=1024, tn=768, name="in_proj").reshape(bsz, t, P_COLS)
    small3 = matmul(xn, w_small, out_dtype=F32, tm=1024, tn=S_COLS, name="in_proj_small").reshape(bsz, t, S_COLS)

    qkv3 = _conv_call(functools.partial(_conv_qkv_kernel, tc=512), p3, [P_AQKV], 3 * BRANCH_WIDTH,
                      [conv_a_w.astype(F32)], F32, 512, "conv_qkv")
    o_a = gated_deltanet(qkv3, p3, small3, a_log_a, dt_bias_a, out_norm_a_g)

    qh3, kk3, ct3, ki3 = dsa_prep(p3, small3, w_uk, ckv_norm_g, q_norm_b_g, k_norm_b_g)
    dprev, ddiag = bias_tables
    o_b = dsa_attention(qh3, p3, small3, kk3, ct3, ki3, dprev, ddiag,
                        jnp.swapaxes(w_uv, 1, 2).astype(BF16))

    o_c = _conv_call(_conv_gated_kernel, p3, [P_C, P_C + BRANCH_WIDTH, P_C + 2 * BRANCH_WIDTH], BRANCH_WIDTH,
                     [conv_c_w.astype(F32)], BF16, 512, "conv_gated")

    xbc3 = _conv_call(_conv_xbc_kernel, p3, [P_DXBC], XBC_WIDTH,
                      [conv_d_w.astype(F32), conv_d_b.reshape(1, XBC_WIDTH).astype(F32)], F32, 512, "conv_xbc")
    o_d = mamba2_ssd(xbc3, p3, small3, a_log_d, dt_bias_d, d_skip, out_norm_d_g)

    outs = [o.reshape(m, BRANCH_WIDTH) for o in (o_a, o_b, o_c, o_d)]
    merged = merge_branches(xn, outs, w_gate, w_up)
    return matmul(merged, w_out, out_dtype=F32, residual=x2, tm=1024, tn=512, name="out_proj")


def _half_k(k):
    return k // 2 if k % (2 * LANES) == 0 else k


def kernel(x, mix_norm_g, w_in, conv_a_w, a_log_a, dt_bias_a, out_norm_a_g, ckv_norm_g, w_uk, q_norm_b_g, k_norm_b_g, w_uv, rel_bias, conv_c_w, conv_d_w, conv_d_b, a_log_d, dt_bias_d, d_skip, out_norm_d_g, w_up, w_out, ffn_norm_g, w1_dense, w3_dense, w2_dense, w_router, w1_moe, w3_moe, w2_moe):
    bsz, t, d = x.shape
    depth = w_in.shape[0]
    x2 = x.reshape(bsz * t, d)
    bias_tables = relbias_tables(rel_bias)
    for l in range(depth):
        x2 = _mixer_layer(x2, bsz, t, mix_norm_g[l], w_in[l], conv_a_w[l], a_log_a[l], dt_bias_a[l],
                          out_norm_a_g[l], ckv_norm_g[l], w_uk[l], q_norm_b_g[l], k_norm_b_g[l], w_uv[l],
                          rel_bias, bias_tables, conv_c_w[l], conv_d_w[l], conv_d_b[l], a_log_d[l],
                          dt_bias_d[l], d_skip[l], out_norm_d_g[l], w_up[l], w_out[l])
        j = l // 2
        if l % 2 == 0:
            h = rmsnorm(x2, ffn_norm_g[l])
            hid = glu_dense(h, w1_dense[j], w3_dense[j])
            x2 = matmul(hid, w2_dense[j].astype(BF16), out_dtype=F32, residual=x2,
                        tm=512, tn=512, name="ffn_down")
        else:
            x2 = moe_layer(x2, ffn_norm_g[l], w_router[j], w1_moe[j], w3_moe[j], w2_moe[j])
    return x2.reshape(bsz, t, d)
```

```python
import functools
import math

import jax
import jax.numpy as jnp
from jax import lax
from jax.experimental import pallas as pl
from jax.experimental.pallas import tpu as pltpu

F32 = jnp.float32
BF16 = jnp.bfloat16
I32 = jnp.int32
U32 = jnp.uint32
HIGHEST = lax.Precision.HIGHEST

D_MODEL = 4096
CHUNK = 64
N_BRANCHES = 4
BRANCH_WIDTH = D_MODEL // N_BRANCHES
DK_A = 128
H_A = BRANCH_WIDTH // DK_A
CONV_A = 4
DH_B = 128
H_B = BRANCH_WIDTH // DH_B
DC_B = 256
H_IDX = 8
D_IDX = 64
DSA_TOPK = 256
Q_BLOCK = 128
N_BUCKETS = 32
T5_MAX_DISTANCE = 128
CONV_C = 3
D_INNER = BRANCH_WIDTH
P_D = 64
H_D = D_INNER // P_D
N_GROUPS = 2
D_STATE = 128
CONV_D = 4
XBC_WIDTH = D_INNER + 2 * N_GROUPS * D_STATE
N_EXPERTS = 8
TOP_K = 2
EPS = 1e-6

MIX_SPLITS = (
    BRANCH_WIDTH, BRANCH_WIDTH, BRANCH_WIDTH, BRANCH_WIDTH, H_A, H_A,
    H_B * DH_B, DC_B, H_IDX * D_IDX, D_IDX, H_IDX,
    BRANCH_WIDTH, BRANCH_WIDTH, BRANCH_WIDTH,
    D_INNER, XBC_WIDTH, H_D,
)
MIX_COLS = sum(MIX_SPLITS)
_OFF = [0]
for _w in MIX_SPLITS:
    _OFF.append(_OFF[-1] + _w)
(O_AQ, O_AK, O_AV, O_AZ, O_AA, O_AB, O_BQ, O_BCKV, O_BQI, O_BKI, O_BWI,
 O_CB, O_CC, O_CH, O_DZ, O_DXBC, O_DDT, _O_END) = _OFF

P_AQKV = 0
P_AZ = 3072
P_BQ = 4096
P_C = 5120
P_DZ = 8192
P_DXBC = 9216
P_BQI = 10752
P_BCKV = 11264
P_COLS = 11520
S_AA = 0
S_AB = 8
S_BKI = 16
S_BWI = 80
S_DDT = 88
S_COLS = 128

LANES = 128
VMEM_LIMIT_MB = 56


def _cparams(sem, vmem_mb=VMEM_LIMIT_MB):
    return pltpu.CompilerParams(dimension_semantics=sem, vmem_limit_bytes=vmem_mb * 2 ** 20)


def _softplus(x):
    return jnp.maximum(x, 0.0) + jnp.log(1.0 + jnp.exp(-jnp.abs(x)))


def _sigmoid(x):
    return 1.0 / (1.0 + jnp.exp(-x))


def _silu(x):
    return x * _sigmoid(x)


def _dot(a, b, precision=None):
    return jnp.dot(a, b, preferred_element_type=F32, precision=precision)


def _dot_nt(a, b, precision=None):
    return lax.dot_general(a, b, (((1,), (1,)), ((), ())), preferred_element_type=F32, precision=precision)


def _dot_tn(a, b, precision=None):
    return lax.dot_general(a, b, (((0,), (0,)), ((), ())), preferred_element_type=F32, precision=precision)


def _rmsnorm_kernel(x_ref, g_ref, o_ref):
    x = x_ref[...].astype(F32)
    ms = jnp.mean(x * x, axis=-1, keepdims=True)
    o_ref[...] = (x * lax.rsqrt(ms + EPS) * g_ref[...]).astype(o_ref.dtype)


def rmsnorm(x, g, out_dtype=BF16, tm=512):
    m, d = x.shape
    tm = min(tm, m)
    return pl.pallas_call(
        _rmsnorm_kernel,
        grid=(m // tm,),
        in_specs=[pl.BlockSpec((tm, d), lambda i: (i, 0)), pl.BlockSpec((1, d), lambda i: (0, 0))],
        out_specs=pl.BlockSpec((tm, d), lambda i: (i, 0)),
        out_shape=jax.ShapeDtypeStruct((m, d), out_dtype),
        compiler_params=_cparams(("parallel",)),
        name="rmsnorm",
    )(x, g.reshape(1, d).astype(F32))


def _mm_kernel(*refs, nk, has_res):
    if has_res:
        a_ref, b_ref, r_ref = refs[:3]
        rest = refs[3:]
    else:
        a_ref, b_ref = refs[:2]
        r_ref = None
        rest = refs[2:]
    o_ref = rest[0]
    if nk == 1:
        acc = _dot(a_ref[...], b_ref[...].astype(BF16))
        if r_ref is not None:
            acc = acc + r_ref[...].astype(F32)
        o_ref[...] = acc.astype(o_ref.dtype)
        return
    acc_ref = rest[1]
    k = pl.program_id(2)

    @pl.when(k == 0)
    def _():
        acc_ref[...] = jnp.zeros_like(acc_ref)

    acc_ref[...] += _dot(a_ref[...], b_ref[...].astype(BF16))

    @pl.when(k == nk - 1)
    def _():
        acc = acc_ref[...]
        if r_ref is not None:
            acc = acc + r_ref[...].astype(F32)
        o_ref[...] = acc.astype(o_ref.dtype)


def matmul(a, b, *, out_dtype, residual=None, tm=1024, tn=512, tk=None, layer=None, name="matmul"):
    m, kdim = a.shape
    n = b.shape[-1]
    tm, tn = min(tm, m), min(tn, n)
    tk = kdim if tk is None else min(tk, kdim)
    nk = kdim // tk
    assert m % tm == 0 and n % tn == 0 and kdim % tk == 0, (a.shape, b.shape, tm, tn, tk)
    if layer is None:
        b_spec = pl.BlockSpec((tk, tn), lambda i, j, k: (k, j))
    else:
        b_spec = pl.BlockSpec((None, tk, tn), lambda i, j, k: (layer, k, j))
    in_specs = [pl.BlockSpec((tm, tk), lambda i, j, k: (i, k)), b_spec]
    args = [a, b]
    if residual is not None:
        in_specs.append(pl.BlockSpec((tm, tn), lambda i, j, k: (i, j)))
        args.append(residual)
    scratch = [pltpu.VMEM((tm, tn), F32)] if nk > 1 else []
    return pl.pallas_call(
        functools.partial(_mm_kernel, nk=nk, has_res=residual is not None),
        grid=(m // tm, n // tn, nk),
        in_specs=in_specs,
        out_specs=pl.BlockSpec((tm, tn), lambda i, j, k: (i, j)),
        out_shape=jax.ShapeDtypeStruct((m, n), out_dtype),
        scratch_shapes=scratch,
        compiler_params=_cparams(("parallel", "parallel", "arbitrary")),
        name=name,
    )(*args)


_PAD_ROWS = 8


def _causal_conv(stage_ref, x, w_ref, ksize):
    t = x.shape[0]
    stage_ref[0:_PAD_ROWS, :] = jnp.zeros((_PAD_ROWS, x.shape[1]), F32)
    stage_ref[_PAD_ROWS:_PAD_ROWS + t, :] = x
    acc = x * w_ref[ksize - 1:ksize, :]
    for j in range(ksize - 1):
        s = ksize - 1 - j
        acc = acc + stage_ref[_PAD_ROWS - s:_PAD_ROWS - s + t, :] * w_ref[j:j + 1, :]
    return acc


def _conv_qkv_kernel(x_ref, w_ref, o_ref, stage_ref, *, tc):
    cb = pl.program_id(1)
    y = _silu(_causal_conv(stage_ref, x_ref[...].astype(F32), w_ref, CONV_A))
    col0 = cb * tc
    is_q = col0 < BRANCH_WIDTH
    is_qk = col0 < 2 * BRANCH_WIDTH
    for g in range(tc // DK_A):
        ys = y[:, g * DK_A:(g + 1) * DK_A]
        inv = lax.rsqrt(jnp.sum(ys * ys, axis=-1, keepdims=True) + EPS)
        inv = jnp.where(is_q, inv * DK_A ** -0.5, inv)
        inv = jnp.where(is_qk, inv, jnp.ones_like(inv))
        o_ref[:, g * DK_A:(g + 1) * DK_A] = ys * inv


def _conv_xbc_kernel(x_ref, w_ref, b_ref, o_ref, stage_ref):
    y = _causal_conv(stage_ref, x_ref[...].astype(F32), w_ref, CONV_D) + b_ref[...]
    o_ref[...] = _silu(y)


def _conv_gated_kernel(bg_ref, cg_ref, h_ref, w_ref, o_ref, stage_ref):
    u = cg_ref[...].astype(F32) * h_ref[...].astype(F32)
    y = _causal_conv(stage_ref, u, w_ref, CONV_C)
    o_ref[...] = (bg_ref[...].astype(F32) * y).astype(o_ref.dtype)


def _conv_call(kernel, p3, col_offsets, width, extra, out_dtype, tc, name):
    bsz, t, _ = p3.shape
    tc = min(tc, width)
    in_specs = [pl.BlockSpec((None, t, tc), functools.partial(lambda b, c, o: (b, 0, o + c), o=off // tc))
                for off in col_offsets]
    args = [p3] * len(col_offsets)
    for e in extra:
        in_specs.append(pl.BlockSpec((e.shape[0], tc), lambda b, c: (0, c)))
        args.append(e)
    return pl.pallas_call(
        kernel,
        grid=(bsz, width // tc),
        in_specs=in_specs,
        out_specs=pl.BlockSpec((None, t, tc), lambda b, c: (b, 0, c)),
        out_shape=jax.ShapeDtypeStruct((bsz, t, width), out_dtype),
        scratch_shapes=[pltpu.VMEM((t + _PAD_ROWS, tc), F32)],
        compiler_params=_cparams(("parallel", "parallel")),
        name=name,
    )(*args)


def _tril_mask(n, strict=False):
    r = lax.broadcasted_iota(I32, (n, n), 0)
    c = lax.broadcasted_iota(I32, (n, n), 1)
    return (r > c) if strict else (r >= c)


def _chunk_cumsum(x):
    tril = _tril_mask(CHUNK).astype(F32)
    cs = _dot(tril, x, HIGHEST)
    padded = jnp.concatenate([cs, jnp.zeros((LANES - CHUNK, LANES), F32)], axis=0)
    return cs, padded.T[:, :CHUNK]


def _segment_decay(cs, cs_t, c):
    tril = _tril_mask(CHUNK)
    diff = cs[:, c:c + 1] - cs_t[c:c + 1, :]
    return jnp.where(tril, jnp.exp(jnp.where(tril, diff, 0.0)), 0.0)


_INV_BLOCK = 16
_GDN_GROUP = 2
_GDN_CHUNKS = 2
_CHUNK_SHIFT = CHUNK.bit_length() - 1


def _split_bf16(a):
    hi = a.astype(BF16)
    return hi, (a - hi.astype(F32)).astype(BF16)


def _dot3(a, b):
    a_hi, a_lo = a if isinstance(a, tuple) else _split_bf16(a)
    b_hi, b_lo = b if isinstance(b, tuple) else _split_bf16(b)
    return _dot(a_hi, b_hi) + _dot(a_hi, b_lo) + _dot(a_lo, b_hi)


def _unit_lower_inverses(lows):
    n_rows = lows[0].shape[0]
    r = lax.broadcasted_iota(I32, (n_rows, n_rows), 0)
    c = lax.broadcasted_iota(I32, (n_rows, n_rows), 1)
    shift = _INV_BLOCK.bit_length() - 1
    same = (r >> shift) == (c >> shift)
    eye = (r == c).astype(F32)
    each = lambda fn, *lists: [fn(*args) for args in zip(*lists)]
    lds = each(lambda low: jnp.where(same, low, 0.0), lows)
    los = each(lambda low: jnp.where(same, 0.0, low), lows)
    ps = each(lambda ld: eye - ld, lds)
    xss = each(_split_bf16, lds)
    xs = each(_dot3, xss, xss)
    for _ in range(2):
        xss = each(_split_bf16, xs)
        ps = each(lambda p, x: p + _dot3(p, x), ps, xss)
        xs = each(_dot3, xss, xss)
    ps = each(lambda p, x: p + _dot3(p, x), ps, xs)
    pss = each(_split_bf16, ps)
    ns = each(_dot3, pss, los)
    nss = each(_split_bf16, ns)
    n2s = each(_dot3, nss, nss)
    rrs = each(lambda n, n2: (eye - n) + _dot3(eye - n, n2), ns, n2s)
    return each(_dot3, rrs, pss)


def _gdn_kernel(q_ref, k_ref, v_ref, z_ref, sm_ref, alog_ref, dtb_ref, ng_ref, o_ref, state_ref):
    @pl.when(pl.program_id(1) == 0)
    def _():
        state_ref[...] = jnp.zeros_like(state_ref)

    ng = ng_ref[...]
    lane = lax.broadcasted_iota(I32, (CHUNK, LANES), 1)
    rows = _GDN_GROUP * CHUNK
    r = lax.broadcasted_iota(I32, (rows, rows), 0)
    c = lax.broadcasted_iota(I32, (rows, rows), 1)
    same_head = (r >> _CHUNK_SHIFT) == (c >> _CHUNK_SHIFT)
    tril = same_head & (r >= c)
    strict = same_head & (r > c)
    groups = [range(grp * _GDN_GROUP, (grp + 1) * _GDN_GROUP) for grp in range(H_A // _GDN_GROUP)]

    n_chunks = sm_ref.shape[0] // CHUNK
    e_lasts, problems = [], []
    for ci in range(n_chunks):
        ts = slice(ci * CHUNK, (ci + 1) * CHUNK)
        sm = sm_ref[ts, :]
        g = -jnp.exp(alog_ref[...]) * _softplus(sm + dtb_ref[...])
        g = jnp.where(lane < S_AA + H_A, g, 0.0)
        beta = _sigmoid(sm)
        gc, gc_t = _chunk_cumsum(g)
        g_last = gc[CHUNK - 1:CHUNK, :]
        e_gc = jnp.exp(gc)
        e_rem = jnp.exp(g_last - gc)
        e_lasts.append(jnp.exp(g_last))
        for heads in groups:
            stack = lambda ref: jnp.concatenate([ref[ts, h * DK_A:(h + 1) * DK_A] for h in heads], axis=0)
            col = lambda arr, off: jnp.concatenate([arr[:, off + h:off + h + 1] for h in heads], axis=0)
            q_st, k_st, v_st = stack(q_ref), stack(k_ref), stack(v_ref)
            beta_st, egc_st, erem_st = col(beta, S_AB), col(e_gc, S_AA), col(e_rem, S_AA)
            g_row = jnp.concatenate([gc_t[S_AA + h:S_AA + h + 1, :] for h in heads], axis=1)
            diff = col(gc, S_AA) - g_row
            decay = jnp.where(tril, jnp.exp(jnp.where(tril, diff, 0.0)), 0.0)
            kb_st = k_st * beta_st
            problems.append(dict(q=q_st, k=k_st, kb=kb_st, decay=decay, qd=q_st * egc_st, kd=k_st * erem_st,
                                 rhs=jnp.concatenate([v_st * beta_st, kb_st * egc_st], axis=1)))
    lows = [jnp.where(strict, _dot_nt(p["kb"], p["k"]) * p["decay"], 0.0) for p in problems]
    tinvs = _unit_lower_inverses(lows)
    uws = [_dot3(tinv, p["rhs"]) for tinv, p in zip(tinvs, problems)]
    intras = [_dot_nt(p["q"], p["k"]) * p["decay"] for p in problems]

    head_rows = lambda gi_j: slice(gi_j * CHUNK, (gi_j + 1) * CHUNK)
    for ci in range(n_chunks):
        ts = slice(ci * CHUNK, (ci + 1) * CHUNK)
        e_last = e_lasts[ci]
        probs = range(ci * len(groups), (ci + 1) * len(groups))
        states = [state_ref[h] for h in range(H_A)]
        wq_s = {}
        for pi, heads in zip(probs, groups):
            for j, h in enumerate(heads):
                rs = head_rows(j)
                lhs = jnp.concatenate([uws[pi][rs, DK_A:], problems[pi]["qd"][rs]], axis=0)
                wq_s[h] = _dot(lhs, states[h])
        v_new = {h: uws[pi][head_rows(j), :DK_A] - wq_s[h][:CHUNK]
                 for pi, heads in zip(probs, groups) for j, h in enumerate(heads)}
        outs = [jnp.concatenate([wq_s[h][CHUNK:] for h in heads], axis=0)
                + _dot(intras[pi], jnp.concatenate([v_new[h] for h in heads], axis=0))
                for pi, heads in zip(probs, groups)]
        for pi, heads in zip(probs, groups):
            for j, h in enumerate(heads):
                ca = S_AA + h
                state_ref[h] = states[h] * e_last[:, ca:ca + 1] + _dot_tn(problems[pi]["kd"][head_rows(j)], v_new[h])
        for out, heads in zip(outs, groups):
            ms = jnp.mean(out * out, axis=-1, keepdims=True)
            out = out * lax.rsqrt(ms + EPS) * ng
            for j, h in enumerate(heads):
                sl = slice(h * DK_A, (h + 1) * DK_A)
                o_ref[ts, sl] = (out[head_rows(j)] * _silu(z_ref[ts, sl].astype(F32))).astype(o_ref.dtype)


def gated_deltanet(qkv3, p3, small3, a_log, dt_bias, norm_g):
    bsz, t, _ = qkv3.shape
    tt = _GDN_CHUNKS * CHUNK if t % (_GDN_CHUNKS * CHUNK) == 0 else CHUNK

    def row(vals, off):
        return jnp.zeros((1, S_COLS), F32).at[0, off:off + vals.shape[0]].set(vals.astype(F32))

    w = BRANCH_WIDTH
    return pl.pallas_call(
        _gdn_kernel,
        grid=(bsz, t // tt),
        in_specs=[
            pl.BlockSpec((None, tt, w), lambda b, n: (b, n, 0)),
            pl.BlockSpec((None, tt, w), lambda b, n: (b, n, 1)),
            pl.BlockSpec((None, tt, w), lambda b, n: (b, n, 2)),
            pl.BlockSpec((None, tt, w), lambda b, n: (b, n, P_AZ // w)),
            pl.BlockSpec((None, tt, S_COLS), lambda b, n: (b, n, 0)),
            pl.BlockSpec((1, S_COLS), lambda b, n: (0, 0)),
            pl.BlockSpec((1, S_COLS), lambda b, n: (0, 0)),
            pl.BlockSpec((1, DK_A), lambda b, n: (0, 0)),
        ],
        out_specs=pl.BlockSpec((None, tt, w), lambda b, n: (b, n, 0)),
        out_shape=jax.ShapeDtypeStruct((bsz, t, w), BF16),
        scratch_shapes=[pltpu.VMEM((H_A, DK_A, DK_A), F32)],
        compiler_params=_cparams(("parallel", "arbitrary")),
        name="gated_deltanet",
    )(qkv3, qkv3, qkv3, p3, small3, row(a_log, S_AA), row(dt_bias, S_AA), norm_g.reshape(1, DK_A).astype(F32))


def _ssd_kernel(xbc_ref, z_ref, sm_ref, alog_ref, dtb_ref, dskip_ref, ng_ref, o_ref, state_ref):
    @pl.when(pl.program_id(1) == 0)
    def _():
        state_ref[...] = jnp.zeros_like(state_ref)

    sm = sm_ref[...]
    lane = lax.broadcasted_iota(I32, (CHUNK, LANES), 1)
    dt = _softplus(sm + dtb_ref[...])
    a = jnp.where((lane >= S_DDT) & (lane < S_DDT + H_D), dt * -jnp.exp(alog_ref[...]), 0.0)
    acs, acs_t = _chunk_cumsum(a)
    a_last = acs[CHUNK - 1:CHUNK, :]
    e_acs = jnp.exp(acs)
    e_rem = jnp.exp(a_last - acs)
    e_last = jnp.exp(a_last)
    heads_per_group = H_D // N_GROUPS
    group_of = lambda h: h // heads_per_group
    bms = [xbc_ref[:, D_INNER + grp * D_STATE:D_INNER + (grp + 1) * D_STATE] for grp in range(N_GROUPS)]
    c0 = D_INNER + N_GROUPS * D_STATE
    cms = [xbc_ref[:, c0 + grp * D_STATE:c0 + (grp + 1) * D_STATE] for grp in range(N_GROUPS)]
    cbs = [_dot_nt(cm, bm) for cm, bm in zip(cms, bms)]
    xcs = [xbc_ref[:, h * P_D:(h + 1) * P_D] * dt[:, S_DDT + h:S_DDT + h + 1] for h in range(H_D)]
    states = [state_ref[h] for h in range(H_D)]
    ys = []
    for h in range(H_D):
        c = S_DDT + h
        ys.append(_dot(cbs[group_of(h)] * _segment_decay(acs, acs_t, c), xcs[h])
                  + _dot_nt(cms[group_of(h)] * e_acs[:, c:c + 1], states[h]))
    for h in range(H_D):
        c = S_DDT + h
        state_ref[h] = states[h] * e_last[:, c:c + 1] + _dot_tn(xcs[h], bms[group_of(h)] * e_rem[:, c:c + 1])
    y = jnp.concatenate(ys, axis=1) + dskip_ref[...] * xbc_ref[:, 0:D_INNER]
    y = y * _silu(z_ref[...].astype(F32))
    gw = D_INNER // N_GROUPS
    for grp in range(N_GROUPS):
        yg = y[:, grp * gw:(grp + 1) * gw]
        ms = jnp.mean(yg * yg, axis=-1, keepdims=True)
        o_ref[:, grp * gw:(grp + 1) * gw] = (yg * lax.rsqrt(ms + EPS) * ng_ref[:, grp * gw:(grp + 1) * gw]).astype(o_ref.dtype)


def mamba2_ssd(xbc3, p3, small3, a_log, dt_bias, d_skip, norm_g):
    bsz, t, _ = xbc3.shape
    nchunks = t // CHUNK

    def row(vals, off):
        return jnp.zeros((1, S_COLS), F32).at[0, off:off + vals.shape[0]].set(vals.astype(F32))

    return pl.pallas_call(
        _ssd_kernel,
        grid=(bsz, nchunks),
        in_specs=[
            pl.BlockSpec((None, CHUNK, XBC_WIDTH), lambda b, n: (b, n, 0)),
            pl.BlockSpec((None, CHUNK, D_INNER), lambda b, n: (b, n, P_DZ // D_INNER)),
            pl.BlockSpec((None, CHUNK, S_COLS), lambda b, n: (b, n, 0)),
            pl.BlockSpec((1, S_COLS), lambda b, n: (0, 0)),
            pl.BlockSpec((1, S_COLS), lambda b, n: (0, 0)),
            pl.BlockSpec((1, D_INNER), lambda b, n: (0, 0)),
            pl.BlockSpec((1, D_INNER), lambda b, n: (0, 0)),
        ],
        out_specs=pl.BlockSpec((None, CHUNK, D_INNER), lambda b, n: (b, n, 0)),
        out_shape=jax.ShapeDtypeStruct((bsz, t, D_INNER), BF16),
        scratch_shapes=[pltpu.VMEM((H_D, P_D, D_STATE), F32)],
        compiler_params=_cparams(("parallel", "arbitrary")),
        name="mamba2_ssd",
    )(xbc3, p3, small3, row(a_log, S_DDT), row(dt_bias, S_DDT),
      jnp.repeat(d_skip.astype(F32), P_D).reshape(1, D_INNER), norm_g.reshape(1, D_INNER).astype(F32))


def _dsa_prep_kernel(q_ref, ckv_ref, sm_ref, wuk_ref, cg_ref, qg_ref, kg_ref, qh_ref, kk_ref, ct_ref, ki_ref):
    ckv = ckv_ref[...].astype(F32)
    c = ckv * lax.rsqrt(jnp.mean(ckv * ckv, axis=-1, keepdims=True) + EPS) * cg_ref[...]
    cb = c.astype(BF16)
    ct_ref[...] = c.T.astype(BF16)
    kk = _dot(cb, wuk_ref[...])
    kk_ref[...] = (kk * lax.rsqrt(jnp.mean(kk * kk, axis=-1, keepdims=True) + EPS) * kg_ref[...]).astype(BF16)
    for h in range(H_B):
        sl = slice(h * DH_B, (h + 1) * DH_B)
        qh = q_ref[:, sl].astype(F32)
        qh = qh * lax.rsqrt(jnp.mean(qh * qh, axis=-1, keepdims=True) + EPS) * qg_ref[...]
        qh_ref[:, sl] = (qh * DH_B ** -0.5).astype(BF16)
    ki_ref[...] = sm_ref[:, S_BKI:S_BKI + D_IDX].astype(BF16)


def dsa_prep(p3, small3, w_uk, ckv_g, q_g, k_g, tt=512):
    bsz, t, _ = p3.shape
    tt = min(tt, t)
    w = H_B * DH_B
    return pl.pallas_call(
        _dsa_prep_kernel,
        grid=(bsz, t // tt),
        in_specs=[
            pl.BlockSpec((None, tt, w), lambda b, i: (b, i, P_BQ // w)),
            pl.BlockSpec((None, tt, DC_B), lambda b, i: (b, i, P_BCKV // DC_B)),
            pl.BlockSpec((None, tt, S_COLS), lambda b, i: (b, i, 0)),
            pl.BlockSpec((DC_B, DH_B), lambda b, i: (0, 0)),
            pl.BlockSpec((1, DC_B), lambda b, i: (0, 0)),
            pl.BlockSpec((1, DH_B), lambda b, i: (0, 0)),
            pl.BlockSpec((1, DH_B), lambda b, i: (0, 0)),
        ],
        out_specs=[
            pl.BlockSpec((None, tt, w), lambda b, i: (b, i, 0)),
            pl.BlockSpec((None, tt, DH_B), lambda b, i: (b, i, 0)),
            pl.BlockSpec((None, DC_B, tt), lambda b, i: (b, 0, i)),
            pl.BlockSpec((None, tt, D_IDX), lambda b, i: (b, i, 0)),
        ],
        out_shape=[
            jax.ShapeDtypeStruct((bsz, t, w), BF16),
            jax.ShapeDtypeStruct((bsz, t, DH_B), BF16),
            jax.ShapeDtypeStruct((bsz, DC_B, t), BF16),
            jax.ShapeDtypeStruct((bsz, t, D_IDX), BF16),
        ],
        compiler_params=_cparams(("parallel", "parallel")),
        name="dsa_prep",
    )(p3, p3, small3, w_uk.astype(BF16), ckv_g.reshape(1, DC_B).astype(F32),
      q_g.reshape(1, DH_B).astype(F32), k_g.reshape(1, DH_B).astype(F32))


_T5_HALF = N_BUCKETS // 2
_T5_EXACT = _T5_HALF // 2
_T5_FAR = _T5_HALF - 1


def _relbias_kernel(rb_ref, prev_ref, diag_ref):
    h = pl.program_id(0)
    kl = lax.broadcasted_iota(I32, (Q_BLOCK, Q_BLOCK), 0)
    ql = lax.broadcasted_iota(I32, (Q_BLOCK, Q_BLOCK), 1)
    far = rb_ref[_T5_FAR, h]
    for ref, shift in ((prev_ref, -Q_BLOCK), (diag_ref, 0)):
        rel = kl - ql + shift
        n = jnp.abs(rel)
        n2 = n * n
        steps = jnp.zeros_like(n)
        for j in range(1, _T5_HALF - _T5_EXACT):
            steps = steps + (n2 >= (_T5_EXACT * _T5_EXACT) * 2 ** j).astype(I32)
        large = jnp.minimum(_T5_EXACT + steps, _T5_HALF - 1)
        bucket = jnp.where(rel > 0, _T5_HALF, 0) + jnp.where(n < _T5_EXACT, n, large)
        acc = jnp.zeros((Q_BLOCK, Q_BLOCK), F32)
        for b in range(N_BUCKETS):
            acc = jnp.where(bucket == b, rb_ref[b, h], acc)
        ref[...] = acc - far


def relbias_tables(rel_bias):
    shp = jax.ShapeDtypeStruct((H_B, Q_BLOCK, Q_BLOCK), F32)
    spec = pl.BlockSpec((None, Q_BLOCK, Q_BLOCK), lambda h: (h, 0, 0))
    return pl.pallas_call(
        _relbias_kernel,
        grid=(H_B,),
        in_specs=[pl.BlockSpec(memory_space=pltpu.SMEM)],
        out_specs=[spec, spec],
        out_shape=[shp, shp],
        compiler_params=_cparams(("arbitrary",)),
        name="relbias_tables",
    )(rel_bias.astype(F32))


_INT_MIN = -2 ** 31
_SCORE_ROWS = 256
_SWEEP_ROWS = 64
_DSA_WIDTH_STEP = 512


def _sweep_rows(n_rows, init, step):
    def body(c, acc):
        return step(pl.multiple_of(c * _SWEEP_ROWS, _SWEEP_ROWS), acc)
    n_steps = n_rows // _SWEEP_ROWS
    return lax.fori_loop(0, n_steps, body, init, unroll=min(8, n_steps))


def _dsa_kernel(qh_ref, qi_ref, sm_ref, kk_ref, ct_ref, ki_ref, dprev_ref, ddiag_ref, wuvt_ref,
                o_ref, key_ref, selb_ref, lg_ref, p_ref, *, widths, topk):
    i = pl.program_id(1)
    need = (i + 1) * Q_BLOCK
    wi_t = sm_ref[...].T[S_BWI:S_BWI + H_IDX, :] * (H_IDX ** -0.5 * D_IDX ** -0.5)
    qi_all = jnp.concatenate([qi_ref[:, h * D_IDX:(h + 1) * D_IDX] for h in range(H_IDX)], axis=0)
    q_chunk = (i * Q_BLOCK + lax.broadcasted_iota(I32, (1, Q_BLOCK), 1)) >> _CHUNK_SHIFT
    row_iota = lax.broadcasted_iota(I32, (_SWEEP_ROWS, Q_BLOCK), 0)
    block = lambda ref, r0: ref[pl.ds(r0, _SWEEP_ROWS), :]

    def body(nc):
        for r0 in range(0, nc, min(_SCORE_ROWS, nc)):
            rows = slice(r0, r0 + min(_SCORE_ROWS, nc))
            s_all = _dot_nt(ki_ref[rows, :], qi_all)
            sc = None
            for h in range(H_IDX):
                term = wi_t[h:h + 1, :] * jnp.maximum(s_all[:, h * Q_BLOCK:(h + 1) * Q_BLOCK], 0.0)
                sc = term if sc is None else sc + term
            bits = lax.bitcast_convert_type(sc + 0.0, I32)
            key = jnp.where(bits < 0, bits ^ 0x7FFFFFFF, bits)
            kpos = r0 + lax.broadcasted_iota(I32, key.shape, 0)
            key_ref[rows, :] = jnp.where((kpos >> _CHUNK_SHIFT) <= q_chunk, key, _INT_MIN)

        def count(pred):
            acc = _sweep_rows(nc, jnp.zeros((_SWEEP_ROWS, Q_BLOCK), I32),
                              lambda r0, acc: acc + pred(r0, block(key_ref, r0)).astype(I32))
            return jnp.sum(acc, axis=0, keepdims=True)

        tau = jnp.where(count(lambda r0, k: k >= 0) >= topk, 0, _INT_MIN).astype(I32)

        def vstep(it, tau):
            cand = tau | (jnp.int32(1) << (30 - it))
            return jnp.where(count(lambda r0, k: k >= cand) >= topk, cand, tau)

        tau = lax.fori_loop(0, 31, vstep, tau)
        n_gt = count(lambda r0, k: k > tau)
        n_eq = count(lambda r0, k: k == tau)
        tied = (n_gt + n_eq > topk) & (tau > _INT_MIN)
        nbits = max(1, (nc - 1).bit_length())

        def last_tied_index():
            room = topk - n_gt

            def istep(it, last):
                cand = last | (jnp.int32(1) << (nbits - 1 - it))
                below = count(lambda r0, k: (k == tau) & (r0 + row_iota < cand))
                return jnp.where(below < room, cand, last)

            return lax.fori_loop(0, nbits, istep, jnp.zeros((1, Q_BLOCK), I32))

        last = lax.cond(jnp.max(tied.astype(I32)) > 0, last_tied_index,
                        lambda: jnp.full((1, Q_BLOCK), nc, I32))

        def write_sel(r0, carry):
            k = block(key_ref, r0)
            sel = (k > _INT_MIN) & ((k > tau) | ((k == tau) & (r0 + row_iota <= last)))
            selb_ref[pl.ds(r0, _SWEEP_ROWS), :] = jnp.where(sel, 0.0, -jnp.inf)
            return carry

        _sweep_rows(nc, 0, write_sel)

        kk = kk_ref[0:nc, :]
        ct = ct_ref[:, 0:nc]
        prev_row = pl.multiple_of(jnp.maximum(i - 1, 0) * Q_BLOCK, Q_BLOCK)
        diag_row = pl.multiple_of(i * Q_BLOCK, Q_BLOCK)
        has_prev = (i > 0).astype(F32)
        for h0 in range(0, H_B, 2):
            q_pair = jnp.concatenate([qh_ref[:, h * DH_B:(h + 1) * DH_B] for h in (h0, h0 + 1)], axis=0)
            lg_pair = _dot_nt(kk, q_pair)
            for j, h in enumerate((h0, h0 + 1)):
                lg_ref[h, 0:nc, :] = lg_pair[:, j * Q_BLOCK:(j + 1) * Q_BLOCK] + selb_ref[0:nc, :]
                lg_ref[h, pl.ds(prev_row, Q_BLOCK), :] += dprev_ref[h] * has_prev
                lg_ref[h, pl.ds(diag_row, Q_BLOCK), :] += ddiag_ref[h]
        blk = min(_SCORE_ROWS, nc)
        outs = []
        for h in range(H_B):
            slot = h % 2
            m_acc = None
            for r0 in range(0, nc, blk):
                x = lg_ref[h, r0:r0 + blk, :]
                m_acc = x if m_acc is None else jnp.maximum(m_acc, x)
            m = jnp.max(m_acc, axis=0, keepdims=True)
            p_acc = None
            for r0 in range(0, nc, blk):
                p = jnp.exp(lg_ref[h, r0:r0 + blk, :] - m)
                p_ref[slot, r0:r0 + blk, :] = p.astype(BF16)
                p_acc = p if p_acc is None else p_acc + p
            denom = jnp.sum(p_acc, axis=0, keepdims=True)
            o_lat = _dot(ct, p_ref[slot, 0:nc, :]) * (1.0 / denom)
            outs.append(_dot(wuvt_ref[h], o_lat.astype(BF16)).T)
        o_ref[...] = jnp.concatenate(outs, axis=1).astype(o_ref.dtype)

    lo = 0
    for nc in widths:
        @pl.when((need > lo) & (need <= nc))
        def _(nc=nc):
            body(nc)
        lo = nc


def dsa_attention(qh3, qi_p3, small3, kk3, ct3, ki3, dprev, ddiag, wuv_t):
    bsz, t, w = qh3.shape
    topk = min(DSA_TOPK, t // 4)
    step = min(_DSA_WIDTH_STEP, t)
    widths = tuple(range(step, t + 1, step))
    wq = H_IDX * D_IDX
    full = lambda shape: pl.BlockSpec(shape, lambda b, i: (0,) * len(shape))
    return pl.pallas_call(
        functools.partial(_dsa_kernel, widths=widths, topk=topk),
        grid=(bsz, t // Q_BLOCK),
        in_specs=[
            pl.BlockSpec((None, Q_BLOCK, w), lambda b, i: (b, i, 0)),
            pl.BlockSpec((None, Q_BLOCK, wq), lambda b, i: (b, i, P_BQI // wq)),
            pl.BlockSpec((None, Q_BLOCK, S_COLS), lambda b, i: (b, i, 0)),
            pl.BlockSpec((None, t, DH_B), lambda b, i: (b, 0, 0)),
            pl.BlockSpec((None, DC_B, t), lambda b, i: (b, 0, 0)),
            pl.BlockSpec((None, t, D_IDX), lambda b, i: (b, 0, 0)),
            full((H_B, Q_BLOCK, Q_BLOCK)),
            full((H_B, Q_BLOCK, Q_BLOCK)),
            full((H_B, DH_B, DC_B)),
        ],
        out_specs=pl.BlockSpec((None, Q_BLOCK, w), lambda b, i: (b, i, 0)),
        out_shape=jax.ShapeDtypeStruct((bsz, t, w), BF16),
        scratch_shapes=[pltpu.VMEM((t, Q_BLOCK), I32), pltpu.VMEM((t, Q_BLOCK), F32),
                        pltpu.VMEM((H_B, t, Q_BLOCK), F32), pltpu.VMEM((2, t, Q_BLOCK), BF16)],
        compiler_params=_cparams(("parallel", "arbitrary")),
        name="dsa_attention",
    )(qh3, qi_p3, small3, kk3, ct3, ki3, dprev, ddiag, wuv_t)


def _merge_kernel(xn_ref, o0_ref, o1_ref, o2_ref, o3_ref, g0_ref, g1_ref, g2_ref, g3_ref,
                  u0_ref, u1_ref, u2_ref, u3_ref, out_ref):
    xn = xn_ref[...]
    acc = None
    for o_ref, g_ref, u_ref in ((o0_ref, g0_ref, u0_ref), (o1_ref, g1_ref, u1_ref),
                                (o2_ref, g2_ref, u2_ref), (o3_ref, g3_ref, u3_ref)):
        term = _sigmoid(_dot(xn, g_ref[...])) * _dot(o_ref[...], u_ref[...].astype(BF16))
        acc = term if acc is None else acc + term
    out_ref[...] = acc.astype(out_ref.dtype)


def merge_branches(xn, outs, w_gate, w_up, layer, tm=512, tn=256):
    m, d = xn.shape
    tm, tn = min(tm, m), min(tn, d)
    nj = d // tn
    in_specs = [pl.BlockSpec((tm, d), lambda j, i: (i, 0))]
    in_specs += [pl.BlockSpec((tm, BRANCH_WIDTH), lambda j, i: (i, 0)) for _ in range(N_BRANCHES)]
    in_specs += [pl.BlockSpec((d, tn), functools.partial(lambda j, i, br: (0, br * nj + j), br=br))
                 for br in range(N_BRANCHES)]
    in_specs += [pl.BlockSpec((None, None, BRANCH_WIDTH, tn),
                              functools.partial(lambda j, i, br: (layer, br, 0, j), br=br))
                 for br in range(N_BRANCHES)]
    return pl.pallas_call(
        _merge_kernel,
        grid=(nj, m // tm),
        in_specs=in_specs,
        out_specs=pl.BlockSpec((tm, tn), lambda j, i: (i, j)),
        out_shape=jax.ShapeDtypeStruct((m, d), BF16),
        compiler_params=_cparams(("parallel", "parallel")),
        name="merge_branches",
    )(xn, *outs, *([w_gate] * N_BRANCHES), *([w_up] * N_BRANCHES))


def _glu_kernel(x_ref, w1_ref, w3_ref, o_ref):
    x = x_ref[...]
    o_ref[...] = (_silu(_dot(x, w1_ref[...].astype(BF16))) * _dot(x, w3_ref[...].astype(BF16))).astype(o_ref.dtype)


def glu_dense(x, w1, w3, tm=1024, tf=256):
    m, d = x.shape
    f = w1.shape[1]
    tm, tf = min(tm, m), min(tf, f)
    assert f % tf == 0
    return pl.pallas_call(
        _glu_kernel,
        grid=(f // tf, m // tm),
        in_specs=[pl.BlockSpec((tm, d), lambda j, i: (i, 0)),
                  pl.BlockSpec((d, tf), lambda j, i: (0, j)),
                  pl.BlockSpec((d, tf), lambda j, i: (0, j))],
        out_specs=pl.BlockSpec((tm, tf), lambda j, i: (i, j)),
        out_shape=jax.ShapeDtypeStruct((m, f), BF16),
        compiler_params=_cparams(("parallel", "parallel")),
        name="glu_dense",
    )(x, w1, w3)


MOE_TM = 512
MOE_TT = 256
MOE_UP_TN = 512
MOE_DOWN_TN = 1024
SLOT_POS_A, SLOT_POS_B, SLOT_W_A, SLOT_W_B = 0, 1, 2, 3
_ROW_COPY_UNROLL = 8


def _pack_bf16_pair(lo, hi):
    bits = lambda v: lax.bitcast_convert_type(v.astype(BF16).astype(F32), U32)
    return bits(hi) | (bits(lo) >> 16)


def _unpack_bf16_pair(u):
    return (lax.bitcast_convert_type(u << 16, F32),
            lax.bitcast_convert_type(u & jnp.uint32(0xFFFF0000), F32))


def _router_kernel(x_ref, g_ref, wr_ref, h_ref, gate_ref, sel_ref):
    x = x_ref[...]
    h = x * lax.rsqrt(jnp.mean(x * x, axis=-1, keepdims=True) + EPS) * g_ref[...]
    half = h.shape[1] // 2
    h_ref[...] = _pack_bf16_pair(h[:, :half], h[:, half:])
    logits = _dot3(h, wr_ref[...])
    lane = lax.broadcasted_iota(I32, logits.shape, 1)
    logits = jnp.where(lane < N_EXPERTS, logits, -jnp.inf)
    m1 = jnp.max(logits, axis=-1, keepdims=True)
    i1 = jnp.min(jnp.where(logits == m1, lane, LANES), axis=-1, keepdims=True)
    rest = jnp.where(lane == i1, -jnp.inf, logits)
    m2 = jnp.max(rest, axis=-1, keepdims=True)
    i2 = jnp.min(jnp.where(rest == m2, lane, LANES), axis=-1, keepdims=True)
    e2 = jnp.exp(m2 - m1)
    inv = 1.0 / (1.0 + e2)
    gate_ref[...] = jnp.where(lane == i1, inv, 0.0) + jnp.where(lane == i2, e2 * inv, 0.0)
    sel_ref[...] = jnp.where((lane == i1) | (lane == i2), 1.0, 0.0).astype(sel_ref.dtype)


def moe_router(x, g, w_router, tm=512):
    m, d = x.shape
    tm = min(tm, m)
    wr = jnp.zeros((d, LANES), F32).at[:, :N_EXPERTS].set(w_router.astype(F32))
    row = lambda w: pl.BlockSpec((tm, w), lambda i: (i, 0))
    return pl.pallas_call(
        _router_kernel,
        grid=(m // tm,),
        in_specs=[row(d), pl.BlockSpec((1, d), lambda i: (0, 0)), pl.BlockSpec((d, LANES), lambda i: (0, 0))],
        out_specs=[row(d // 2), row(LANES), row(LANES)],
        out_shape=[jax.ShapeDtypeStruct((m, d // 2), U32), jax.ShapeDtypeStruct((m, LANES), F32),
                   jax.ShapeDtypeStruct((m, LANES), BF16)],
        compiler_params=_cparams(("parallel",)),
        name="moe_router",
    )(x, g.reshape(1, d).astype(F32), wr)


def _rank_kernel(sel_ref, rank_ref, cnt_ref, carry_ref):
    @pl.when(pl.program_id(0) == 0)
    def _():
        carry_ref[...] = jnp.zeros_like(carry_ref)

    sel = sel_ref[...]
    n = sel.shape[0]
    earlier = _tril_mask(n, strict=True).astype(BF16)
    rank_ref[...] = _dot(earlier, sel) + carry_ref[...]
    carry_ref[...] += jnp.sum(sel.astype(F32), axis=0, keepdims=True)
    cnt_ref[...] = carry_ref[...]


def moe_rank(sel, tr=512):
    m = sel.shape[0]
    tr = min(tr, m)
    return pl.pallas_call(
        _rank_kernel,
        grid=(m // tr,),
        in_specs=[pl.BlockSpec((tr, LANES), lambda i: (i, 0))],
        out_specs=[pl.BlockSpec((tr, LANES), lambda i: (i, 0)), pl.BlockSpec((1, LANES), lambda i: (0, 0))],
        out_shape=[jax.ShapeDtypeStruct((m, LANES), F32), jax.ShapeDtypeStruct((1, LANES), F32)],
        scratch_shapes=[pltpu.VMEM((1, LANES), F32)],
        compiler_params=_cparams(("arbitrary",)),
        name="moe_rank",
    )(sel)


def _slots_kernel(rank_ref, sel_ref, gate_ref, start_ref, out_ref):
    sel = sel_ref[...].astype(F32) > 0.0
    lane = lax.broadcasted_iota(I32, sel.shape, 1)
    dest = start_ref[...] + rank_ref[...]
    first = jnp.min(jnp.where(sel, lane, LANES), axis=-1, keepdims=True)
    second = jnp.max(jnp.where(sel, lane, -1), axis=-1, keepdims=True)
    pick = lambda arr, idx: jnp.sum(jnp.where(lane == idx, arr, 0.0), axis=-1, keepdims=True)
    gates = gate_ref[...]
    out = jnp.where(lane == SLOT_POS_A, pick(dest, first), 0.0)
    out = jnp.where(lane == SLOT_POS_B, pick(dest, second), out)
    out = jnp.where(lane == SLOT_W_A, pick(gates, first), out)
    out_ref[...] = jnp.where(lane == SLOT_W_B, pick(gates, second), out)


def moe_slots(rank, sel, gates, start_row, tr=512):
    m = sel.shape[0]
    tr = min(tr, m)
    row = pl.BlockSpec((tr, LANES), lambda i: (i, 0))
    return pl.pallas_call(
        _slots_kernel,
        grid=(m // tr,),
        in_specs=[row, row, row, pl.BlockSpec((1, LANES), lambda i: (0, 0))],
        out_specs=row,
        out_shape=jax.ShapeDtypeStruct((m, LANES), F32),
        compiler_params=_cparams(("parallel",)),
        name="moe_slots",
    )(rank, sel, gates, start_row)


def _row_copy(src_ref, src_row, dst_ref, dst_row, sem):
    return pltpu.make_async_copy(src_ref.at[pl.ds(src_row, 1), :], dst_ref.at[pl.ds(dst_row, 1), :], sem)


def _dispatch_kernel(pos_ref, tail_ref, h_ref, xg_ref, zero_ref, sem, *, tm):
    tt = h_ref.shape[0]

    @pl.when(pl.program_id(0) == 0)
    def _():
        zero_ref[...] = jnp.zeros_like(zero_ref)
        fill = lambda e: pltpu.make_async_copy(zero_ref, xg_ref.at[pl.ds(pl.multiple_of(tail_ref[e], tm), tm), :], sem)
        for e in range(tail_ref.shape[0]):
            @pl.when(tail_ref[e] >= 0)
            def _(e=e):
                fill(e).start()
        for e in range(tail_ref.shape[0]):
            @pl.when(tail_ref[e] >= 0)
            def _(e=e):
                fill(e).wait()

    def start(r, carry):
        _row_copy(h_ref, r, xg_ref, pos_ref[0, r], sem).start()
        _row_copy(h_ref, r, xg_ref, pos_ref[0, tt + r], sem).start()
        return carry

    def wait(r, carry):
        _row_copy(h_ref, r, xg_ref, pos_ref[0, r], sem).wait()
        _row_copy(h_ref, r, xg_ref, pos_ref[0, tt + r], sem).wait()
        return carry

    lax.fori_loop(0, tt, start, 0, unroll=_ROW_COPY_UNROLL)
    lax.fori_loop(0, tt, wait, 0, unroll=_ROW_COPY_UNROLL)


def moe_dispatch(h, pos_tiles, tails, rows, tm):
    m, d = h.shape
    tt = pos_tiles.shape[2] // 2
    return pl.pallas_call(
        functools.partial(_dispatch_kernel, tm=tm),
        grid=(m // tt,),
        in_specs=[pl.BlockSpec((None, 1, 2 * tt), lambda i: (i, 0, 0), memory_space=pltpu.SMEM),
                  pl.BlockSpec(memory_space=pltpu.SMEM),
                  pl.BlockSpec((tt, d), lambda i: (i, 0))],
        out_specs=pl.BlockSpec(memory_space=pl.ANY),
        out_shape=jax.ShapeDtypeStruct((rows, d), h.dtype),
        scratch_shapes=[pltpu.VMEM((tm, d), h.dtype), pltpu.SemaphoreType.DMA(())],
        compiler_params=_cparams(("arbitrary",)),
        name="moe_dispatch",
    )(pos_tiles, tails, h)


def _glu_grouped_kernel(te_ref, nv_ref, x_ref, w1_ref, w3_ref, o_ref):
    valid = pl.program_id(1) < nv_ref[0]

    @pl.when(valid)
    def _():
        lo, hi = (v.astype(BF16) for v in _unpack_bf16_pair(x_ref[...]))
        half = lo.shape[1]
        up = lambda w_ref: (_dot(lo, w_ref[0:half, :].astype(BF16)) + _dot(hi, w_ref[half:2 * half, :].astype(BF16)))
        o_ref[...] = (_silu(up(w1_ref)) * up(w3_ref)).astype(o_ref.dtype)

    @pl.when(jnp.logical_not(valid))
    def _():
        o_ref[...] = jnp.zeros_like(o_ref)


def _down_grouped_kernel(te_ref, nv_ref, h_ref, w2_ref, o_ref):
    valid = pl.program_id(1) < nv_ref[0]

    @pl.when(valid)
    def _():
        y = _dot(h_ref[...], w2_ref[...].astype(BF16))
        half = y.shape[1] // 2
        o_ref[...] = _pack_bf16_pair(y[:, :half], y[:, half:])

    @pl.when(jnp.logical_not(valid))
    def _():
        o_ref[...] = jnp.zeros_like(o_ref)


def _grouped_call(kernel, x, weights, tile_expert, n_valid, n_col_tiles, tn, out_tn, out_dtype, tm, name):
    rows, d = x.shape
    tile = lambda i, nv: jnp.minimum(i, nv[0] - 1)
    in_specs = [pl.BlockSpec((tm, d), lambda j, i, te, nv: (tile(i, nv), 0))]
    in_specs += [pl.BlockSpec((None, w.shape[1], tn), lambda j, i, te, nv: (te[tile(i, nv)], 0, j)) for w in weights]
    grid_spec = pltpu.PrefetchScalarGridSpec(
        num_scalar_prefetch=2,
        grid=(n_col_tiles, rows // tm),
        in_specs=in_specs,
        out_specs=pl.BlockSpec((tm, out_tn), lambda j, i, te, nv: (i, j)),
    )
    return pl.pallas_call(
        kernel,
        grid_spec=grid_spec,
        out_shape=jax.ShapeDtypeStruct((rows, n_col_tiles * out_tn), out_dtype),
        compiler_params=_cparams(("arbitrary", "arbitrary")),
        name=name,
    )(tile_expert, n_valid, x, *weights)


def _combine_kernel(pos_ref, x_ref, slot_ref, yg_ref, o_ref, buf_ref, sem):
    tt = x_ref.shape[0]

    def start(r, carry):
        _row_copy(yg_ref, pos_ref[0, r], buf_ref.at[0], r, sem).start()
        _row_copy(yg_ref, pos_ref[0, tt + r], buf_ref.at[1], r, sem).start()
        return carry

    def wait(r, carry):
        _row_copy(yg_ref, pos_ref[0, r], buf_ref.at[0], r, sem).wait()
        _row_copy(yg_ref, pos_ref[0, tt + r], buf_ref.at[1], r, sem).wait()
        return carry

    lax.fori_loop(0, tt, start, 0, unroll=_ROW_COPY_UNROLL)
    lax.fori_loop(0, tt, wait, 0, unroll=_ROW_COPY_UNROLL)
    slots = slot_ref[...]
    w_a = slots[:, SLOT_W_A:SLOT_W_A + 1]
    w_b = slots[:, SLOT_W_B:SLOT_W_B + 1]
    half = MOE_DOWN_TN // 2
    for j in range(x_ref.shape[1] // MOE_DOWN_TN):
        lo_a, hi_a = _unpack_bf16_pair(buf_ref[0, :, j * half:(j + 1) * half])
        lo_b, hi_b = _unpack_bf16_pair(buf_ref[1, :, j * half:(j + 1) * half])
        c_lo = slice(j * MOE_DOWN_TN, j * MOE_DOWN_TN + half)
        c_hi = slice(j * MOE_DOWN_TN + half, (j + 1) * MOE_DOWN_TN)
        o_ref[:, c_lo] = x_ref[:, c_lo] + w_a * lo_a + w_b * lo_b
        o_ref[:, c_hi] = x_ref[:, c_hi] + w_a * hi_a + w_b * hi_b


def moe_combine(x, slots, pos_tiles, yg):
    m, d = x.shape
    tt = pos_tiles.shape[2] // 2
    return pl.pallas_call(
        _combine_kernel,
        grid=(m // tt,),
        in_specs=[pl.BlockSpec((None, 1, 2 * tt), lambda i: (i, 0, 0), memory_space=pltpu.SMEM),
                  pl.BlockSpec((tt, d), lambda i: (i, 0)),
                  pl.BlockSpec((tt, LANES), lambda i: (i, 0)),
                  pl.BlockSpec(memory_space=pl.ANY)],
        out_specs=pl.BlockSpec((tt, d), lambda i: (i, 0)),
        out_shape=jax.ShapeDtypeStruct((m, d), F32),
        scratch_shapes=[pltpu.VMEM((2, tt, yg.shape[1]), yg.dtype), pltpu.SemaphoreType.DMA(())],
        compiler_params=_cparams(("arbitrary",)),
        name="moe_combine",
    )(pos_tiles, x, slots, yg)


def moe_layer(x2, norm_g, w_router, w1, w3, w2):
    m, d = x2.shape
    tm = min(MOE_TM, m)
    tt = min(MOE_TT, m)
    n_tiles = (TOP_K * m) // tm + N_EXPERTS
    rows = n_tiles * tm
    h, gates, sel = moe_router(x2, norm_g, w_router)
    rank, counts = moe_rank(sel)

    cnt = counts[0, :N_EXPERTS].astype(I32)
    padded = ((cnt + tm - 1) // tm) * tm
    ends = jnp.cumsum(padded)
    starts = ends - padded
    start_row = jnp.zeros((1, LANES), F32).at[0, :N_EXPERTS].set(starts.astype(F32))
    tile_expert = jnp.minimum(jnp.searchsorted(ends, jnp.arange(n_tiles, dtype=I32) * tm, side="right"),
                              N_EXPERTS - 1).astype(I32)
    n_valid = (ends[-1:] // tm).astype(I32)
    unused = ends[-1] + jnp.arange(N_EXPERTS, dtype=I32) * tm
    tails = jnp.concatenate([jnp.where(padded > 0, ends - tm, -1),
                             jnp.where(unused < rows, unused, -1)]).astype(I32)

    slots = moe_slots(rank, sel, gates, start_row)
    pos = slots[:, :2].astype(I32).reshape(m // tt, tt, 2)
    pos_tiles = jnp.swapaxes(pos, 1, 2).reshape(m // tt, 1, 2 * tt)

    xg = moe_dispatch(h, pos_tiles, tails, rows, tm)
    hid = _grouped_call(_glu_grouped_kernel, xg, [w1, w3], tile_expert, n_valid,
                        w1.shape[2] // MOE_UP_TN, MOE_UP_TN, MOE_UP_TN, BF16, tm, "moe_glu")
    yg = _grouped_call(_down_grouped_kernel, hid, [w2], tile_expert, n_valid,
                       d // MOE_DOWN_TN, MOE_DOWN_TN, MOE_DOWN_TN // 2, U32, tm, "moe_down")
    return moe_combine(x2, slots, pos_tiles, yg)


def _split_in_proj(w_in_l):
    seg = lambda lo, hi: w_in_l[:, lo:hi]
    big = jnp.concatenate([
        seg(O_AQ, O_AZ), seg(O_AZ, O_AA), seg(O_BQ, O_BCKV), seg(O_CB, O_DZ),
        seg(O_DZ, O_DXBC), seg(O_DXBC, O_DDT), seg(O_BQI, O_BKI), seg(O_BCKV, O_BQI)], axis=1).astype(BF16)
    pad = jnp.zeros((w_in_l.shape[0], S_COLS - (S_DDT + H_D)), w_in_l.dtype)
    small = jnp.concatenate([
        seg(O_AA, O_AB), seg(O_AB, O_BQ), seg(O_BKI, O_BWI), seg(O_BWI, O_CB), seg(O_DDT, _O_END), pad],
        axis=1).astype(BF16)
    gate = w_in_l[:, MIX_COLS:].astype(BF16)
    return big, small, gate


def _mixer_layer(x2, bsz, t, mix_norm_g, w_in_l, conv_a_w, a_log_a, dt_bias_a, out_norm_a_g, ckv_norm_g, w_uk,
                 q_norm_b_g, k_norm_b_g, w_uv, rel_bias, bias_tables, conv_c_w, conv_d_w, conv_d_b, a_log_d,
                 dt_bias_d, d_skip, out_norm_d_g, w_up_all, w_out_all, layer):
    m = bsz * t
    xn = rmsnorm(x2, mix_norm_g)
    w_big, w_small, w_gate = _split_in_proj(w_in_l)
    p3 = matmul(xn, w_big, out_dtype=BF16, tm=1024, tn=768, name="in_proj").reshape(bsz, t, P_COLS)
    small3 = matmul(xn, w_small, out_dtype=F32, tm=1024, tn=S_COLS, name="in_proj_small").reshape(bsz, t, S_COLS)

    qkv3 = _conv_call(functools.partial(_conv_qkv_kernel, tc=512), p3, [P_AQKV], 3 * BRANCH_WIDTH,
                      [conv_a_w.astype(F32)], F32, 512, "conv_qkv")
    o_a = gated_deltanet(qkv3, p3, small3, a_log_a, dt_bias_a, out_norm_a_g)

    qh3, kk3, ct3, ki3 = dsa_prep(p3, small3, w_uk, ckv_norm_g, q_norm_b_g, k_norm_b_g)
    dprev, ddiag = bias_tables
    o_b = dsa_attention(qh3, p3, small3, kk3, ct3, ki3, dprev, ddiag,
                        jnp.swapaxes(w_uv, 1, 2).astype(BF16))

    o_c = _conv_call(_conv_gated_kernel, p3, [P_C, P_C + BRANCH_WIDTH, P_C + 2 * BRANCH_WIDTH], BRANCH_WIDTH,
                     [conv_c_w.astype(F32)], BF16, 512, "conv_gated")

    xbc3 = _conv_call(_conv_xbc_kernel, p3, [P_DXBC], XBC_WIDTH,
                      [conv_d_w.astype(F32), conv_d_b.reshape(1, XBC_WIDTH).astype(F32)], F32, 512, "conv_xbc")
    o_d = mamba2_ssd(xbc3, p3, small3, a_log_d, dt_bias_d, d_skip, out_norm_d_g)

    outs = [o.reshape(m, BRANCH_WIDTH) for o in (o_a, o_b, o_c, o_d)]
    merged = merge_branches(xn, outs, w_gate, w_up_all, layer)
    return matmul(merged, w_out_all, out_dtype=F32, residual=x2, tm=1024, tn=512, layer=layer, name="out_proj")


def _half_k(k):
    return k // 2 if k % (2 * LANES) == 0 else k


def kernel(x, mix_norm_g, w_in, conv_a_w, a_log_a, dt_bias_a, out_norm_a_g, ckv_norm_g, w_uk, q_norm_b_g, k_norm_b_g, w_uv, rel_bias, conv_c_w, conv_d_w, conv_d_b, a_log_d, dt_bias_d, d_skip, out_norm_d_g, w_up, w_out, ffn_norm_g, w1_dense, w3_dense, w2_dense, w_router, w1_moe, w3_moe, w2_moe):
    bsz, t, d = x.shape
    depth = w_in.shape[0]
    x2 = x.reshape(bsz * t, d)
    bias_tables = relbias_tables(rel_bias)
    for l in range(depth):
        x2 = _mixer_layer(x2, bsz, t, mix_norm_g[l], w_in[l], conv_a_w[l], a_log_a[l], dt_bias_a[l],
                          out_norm_a_g[l], ckv_norm_g[l], w_uk[l], q_norm_b_g[l], k_norm_b_g[l], w_uv[l],
                          rel_bias, bias_tables, conv_c_w[l], conv_d_w[l], conv_d_b[l], a_log_d[l],
                          dt_bias_d[l], d_skip[l], out_norm_d_g[l], w_up, w_out, l)
        j = l // 2
        if l % 2 == 0:
            h = rmsnorm(x2, ffn_norm_g[l])
            hid = glu_dense(h, w1_dense[j], w3_dense[j])
            x2 = matmul(hid, w2_dense[j].astype(BF16), out_dtype=F32, residual=x2,
                        tm=512, tn=512, name="ffn_down")
        else:
            x2 = moe_layer(x2, ffn_norm_g[l], w_router[j], w1_moe[j], w3_moe[j], w2_moe[j])
    return x2.reshape(bsz, t, d)
```

```python
import functools
import math

import jax
import jax.numpy as jnp
from jax import lax
from jax.experimental import pallas as pl
from jax.experimental.pallas import tpu as pltpu

F32 = jnp.float32
BF16 = jnp.bfloat16
I32 = jnp.int32
U32 = jnp.uint32
HIGHEST = lax.Precision.HIGHEST

D_MODEL = 4096
CHUNK = 64
N_BRANCHES = 4
BRANCH_WIDTH = D_MODEL // N_BRANCHES
DK_A = 128
H_A = BRANCH_WIDTH // DK_A
CONV_A = 4
DH_B = 128
H_B = BRANCH_WIDTH // DH_B
DC_B = 256
H_IDX = 8
D_IDX = 64
DSA_TOPK = 256
Q_BLOCK = 128
N_BUCKETS = 32
T5_MAX_DISTANCE = 128
CONV_C = 3
D_INNER = BRANCH_WIDTH
P_D = 64
H_D = D_INNER // P_D
N_GROUPS = 2
D_STATE = 128
CONV_D = 4
XBC_WIDTH = D_INNER + 2 * N_GROUPS * D_STATE
N_EXPERTS = 8
TOP_K = 2
EPS = 1e-6

MIX_SPLITS = (
    BRANCH_WIDTH, BRANCH_WIDTH, BRANCH_WIDTH, BRANCH_WIDTH, H_A, H_A,
    H_B * DH_B, DC_B, H_IDX * D_IDX, D_IDX, H_IDX,
    BRANCH_WIDTH, BRANCH_WIDTH, BRANCH_WIDTH,
    D_INNER, XBC_WIDTH, H_D,
)
MIX_COLS = sum(MIX_SPLITS)
_OFF = [0]
for _w in MIX_SPLITS:
    _OFF.append(_OFF[-1] + _w)
(O_AQ, O_AK, O_AV, O_AZ, O_AA, O_AB, O_BQ, O_BCKV, O_BQI, O_BKI, O_BWI,
 O_CB, O_CC, O_CH, O_DZ, O_DXBC, O_DDT, _O_END) = _OFF

P_AQKV = 0
P_AZ = 3072
P_BQ = 4096
P_C = 5120
P_DZ = 8192
P_DXBC = 9216
P_BQI = 10752
P_BCKV = 11264
P_COLS = 11520
S_AA = 0
S_AB = 8
S_BKI = 16
S_BWI = 80
S_DDT = 88
S_COLS = 128

LANES = 128
VMEM_LIMIT_MB = 56


def _cparams(sem, vmem_mb=VMEM_LIMIT_MB):
    return pltpu.CompilerParams(dimension_semantics=sem, vmem_limit_bytes=vmem_mb * 2 ** 20)


def _softplus(x):
    return jnp.maximum(x, 0.0) + jnp.log(1.0 + jnp.exp(-jnp.abs(x)))


def _sigmoid(x):
    return 1.0 / (1.0 + jnp.exp(-x))


def _silu(x):
    return x * _sigmoid(x)


def _dot(a, b, precision=None):
    return jnp.dot(a, b, preferred_element_type=F32, precision=precision)


def _dot_nt(a, b, precision=None):
    return lax.dot_general(a, b, (((1,), (1,)), ((), ())), preferred_element_type=F32, precision=precision)


def _dot_tn(a, b, precision=None):
    return lax.dot_general(a, b, (((0,), (0,)), ((), ())), preferred_element_type=F32, precision=precision)


def _rmsnorm_kernel(x_ref, g_ref, o_ref):
    x = x_ref[...].astype(F32)
    ms = jnp.mean(x * x, axis=-1, keepdims=True)
    o_ref[...] = (x * lax.rsqrt(ms + EPS) * g_ref[...]).astype(o_ref.dtype)


def rmsnorm(x, g, out_dtype=BF16, tm=512):
    m, d = x.shape
    tm = min(tm, m)
    return pl.pallas_call(
        _rmsnorm_kernel,
        grid=(m // tm,),
        in_specs=[pl.BlockSpec((tm, d), lambda i: (i, 0)), pl.BlockSpec((1, d), lambda i: (0, 0))],
        out_specs=pl.BlockSpec((tm, d), lambda i: (i, 0)),
        out_shape=jax.ShapeDtypeStruct((m, d), out_dtype),
        compiler_params=_cparams(("parallel",)),
        name="rmsnorm",
    )(x, g.reshape(1, d).astype(F32))


def _mm_kernel(*refs, nk, has_res):
    if has_res:
        a_ref, b_ref, r_ref = refs[:3]
        rest = refs[3:]
    else:
        a_ref, b_ref = refs[:2]
        r_ref = None
        rest = refs[2:]
    o_ref = rest[0]
    if nk == 1:
        acc = _dot(a_ref[...], b_ref[...].astype(BF16))
        if r_ref is not None:
            acc = acc + r_ref[...].astype(F32)
        o_ref[...] = acc.astype(o_ref.dtype)
        return
    acc_ref = rest[1]
    k = pl.program_id(2)

    @pl.when(k == 0)
    def _():
        acc_ref[...] = jnp.zeros_like(acc_ref)

    acc_ref[...] += _dot(a_ref[...], b_ref[...].astype(BF16))

    @pl.when(k == nk - 1)
    def _():
        acc = acc_ref[...]
        if r_ref is not None:
            acc = acc + r_ref[...].astype(F32)
        o_ref[...] = acc.astype(o_ref.dtype)


def matmul(a, b, *, out_dtype, residual=None, tm=1024, tn=512, tk=None, layer=None, name="matmul"):
    m, kdim = a.shape
    n = b.shape[-1]
    tm, tn = min(tm, m), min(tn, n)
    tk = kdim if tk is None else min(tk, kdim)
    nk = kdim // tk
    assert m % tm == 0 and n % tn == 0 and kdim % tk == 0, (a.shape, b.shape, tm, tn, tk)
    if layer is None:
        b_spec = pl.BlockSpec((tk, tn), lambda i, j, k: (k, j))
    else:
        b_spec = pl.BlockSpec((None, tk, tn), lambda i, j, k: (layer, k, j))
    in_specs = [pl.BlockSpec((tm, tk), lambda i, j, k: (i, k)), b_spec]
    args = [a, b]
    if residual is not None:
        in_specs.append(pl.BlockSpec((tm, tn), lambda i, j, k: (i, j)))
        args.append(residual)
    scratch = [pltpu.VMEM((tm, tn), F32)] if nk > 1 else []
    return pl.pallas_call(
        functools.partial(_mm_kernel, nk=nk, has_res=residual is not None),
        grid=(m // tm, n // tn, nk),
        in_specs=in_specs,
        out_specs=pl.BlockSpec((tm, tn), lambda i, j, k: (i, j)),
        out_shape=jax.ShapeDtypeStruct((m, n), out_dtype),
        scratch_shapes=scratch,
        compiler_params=_cparams(("parallel", "parallel", "arbitrary")),
        name=name,
    )(*args)


_PAD_ROWS = 8


def _causal_conv(stage_ref, x, w_ref, ksize):
    t = x.shape[0]
    stage_ref[0:_PAD_ROWS, :] = jnp.zeros((_PAD_ROWS, x.shape[1]), F32)
    stage_ref[_PAD_ROWS:_PAD_ROWS + t, :] = x
    acc = x * w_ref[ksize - 1:ksize, :]
    for j in range(ksize - 1):
        s = ksize - 1 - j
        acc = acc + stage_ref[_PAD_ROWS - s:_PAD_ROWS - s + t, :] * w_ref[j:j + 1, :]
    return acc


def _conv_qkv_kernel(x_ref, w_ref, o_ref, stage_ref, *, tc):
    cb = pl.program_id(1)
    y = _silu(_causal_conv(stage_ref, x_ref[...].astype(F32), w_ref, CONV_A))
    col0 = cb * tc
    is_q = col0 < BRANCH_WIDTH
    is_qk = col0 < 2 * BRANCH_WIDTH
    for g in range(tc // DK_A):
        ys = y[:, g * DK_A:(g + 1) * DK_A]
        inv = lax.rsqrt(jnp.sum(ys * ys, axis=-1, keepdims=True) + EPS)
        inv = jnp.where(is_q, inv * DK_A ** -0.5, inv)
        inv = jnp.where(is_qk, inv, jnp.ones_like(inv))
        o_ref[:, g * DK_A:(g + 1) * DK_A] = ys * inv


def _conv_xbc_kernel(x_ref, w_ref, b_ref, o_ref, stage_ref):
    y = _causal_conv(stage_ref, x_ref[...].astype(F32), w_ref, CONV_D) + b_ref[...]
    o_ref[...] = _silu(y)


def _conv_gated_kernel(bg_ref, cg_ref, h_ref, w_ref, o_ref, stage_ref):
    u = cg_ref[...].astype(F32) * h_ref[...].astype(F32)
    y = _causal_conv(stage_ref, u, w_ref, CONV_C)
    o_ref[...] = (bg_ref[...].astype(F32) * y).astype(o_ref.dtype)


def _conv_call(kernel, p3, col_offsets, width, extra, out_dtype, tc, name):
    bsz, t, _ = p3.shape
    tc = min(tc, width)
    in_specs = [pl.BlockSpec((None, t, tc), functools.partial(lambda b, c, o: (b, 0, o + c), o=off // tc))
                for off in col_offsets]
    args = [p3] * len(col_offsets)
    for e in extra:
        in_specs.append(pl.BlockSpec((e.shape[0], tc), lambda b, c: (0, c)))
        args.append(e)
    return pl.pallas_call(
        kernel,
        grid=(bsz, width // tc),
        in_specs=in_specs,
        out_specs=pl.BlockSpec((None, t, tc), lambda b, c: (b, 0, c)),
        out_shape=jax.ShapeDtypeStruct((bsz, t, width), out_dtype),
        scratch_shapes=[pltpu.VMEM((t + _PAD_ROWS, tc), F32)],
        compiler_params=_cparams(("parallel", "parallel")),
        name=name,
    )(*args)


def _tril_mask(n, strict=False):
    r = lax.broadcasted_iota(I32, (n, n), 0)
    c = lax.broadcasted_iota(I32, (n, n), 1)
    return (r > c) if strict else (r >= c)


def _chunk_cumsum(x):
    tril = _tril_mask(CHUNK).astype(F32)
    cs = _dot(tril, x, HIGHEST)
    padded = jnp.concatenate([cs, jnp.zeros((LANES - CHUNK, LANES), F32)], axis=0)
    return cs, padded.T[:, :CHUNK]


def _segment_decay(cs, cs_t, c):
    tril = _tril_mask(CHUNK)
    diff = cs[:, c:c + 1] - cs_t[c:c + 1, :]
    return jnp.where(tril, jnp.exp(jnp.where(tril, diff, 0.0)), 0.0)


_INV_BLOCK = 16
_GDN_GROUP = 2
_GDN_CHUNKS = 4
_CHUNK_SHIFT = CHUNK.bit_length() - 1


def _split_bf16(a):
    hi = a.astype(BF16)
    return hi, (a - hi.astype(F32)).astype(BF16)


def _dot3(a, b):
    a_hi, a_lo = a if isinstance(a, tuple) else _split_bf16(a)
    b_hi, b_lo = b if isinstance(b, tuple) else _split_bf16(b)
    return _dot(a_hi, b_hi) + _dot(a_hi, b_lo) + _dot(a_lo, b_hi)


def _unit_lower_inverses(lows):
    n_rows = lows[0].shape[0]
    r = lax.broadcasted_iota(I32, (n_rows, n_rows), 0)
    c = lax.broadcasted_iota(I32, (n_rows, n_rows), 1)
    shift = _INV_BLOCK.bit_length() - 1
    same = (r >> shift) == (c >> shift)
    eye = (r == c).astype(F32)
    each = lambda fn, *lists: [fn(*args) for args in zip(*lists)]
    lds = each(lambda low: jnp.where(same, low, 0.0), lows)
    los = each(lambda low: jnp.where(same, 0.0, low), lows)
    ps = each(lambda ld: eye - ld, lds)
    xss = each(_split_bf16, lds)
    xs = each(_dot3, xss, xss)
    for _ in range(2):
        xss = each(_split_bf16, xs)
        ps = each(lambda p, x: p + _dot3(p, x), ps, xss)
        xs = each(_dot3, xss, xss)
    ps = each(lambda p, x: p + _dot3(p, x), ps, xs)
    pss = each(_split_bf16, ps)
    ns = each(_dot3, pss, los)
    nss = each(_split_bf16, ns)
    n2s = each(_dot3, nss, nss)
    rrs = each(lambda n, n2: (eye - n) + _dot3(eye - n, n2), ns, n2s)
    return each(_dot3, rrs, pss)


def _gdn_kernel(q_ref, k_ref, v_ref, z_ref, sm_ref, alog_ref, dtb_ref, ng_ref, o_ref, state_ref):
    @pl.when(pl.program_id(1) == 0)
    def _():
        state_ref[...] = jnp.zeros_like(state_ref)

    ng = ng_ref[...]
    lane = lax.broadcasted_iota(I32, (CHUNK, LANES), 1)
    rows = _GDN_GROUP * CHUNK
    r = lax.broadcasted_iota(I32, (rows, rows), 0)
    c = lax.broadcasted_iota(I32, (rows, rows), 1)
    same_head = (r >> _CHUNK_SHIFT) == (c >> _CHUNK_SHIFT)
    tril = same_head & (r >= c)
    strict = same_head & (r > c)
    groups = [range(grp * _GDN_GROUP, (grp + 1) * _GDN_GROUP) for grp in range(H_A // _GDN_GROUP)]

    n_chunks = sm_ref.shape[0] // CHUNK
    e_lasts, problems = [], []
    for ci in range(n_chunks):
        ts = slice(ci * CHUNK, (ci + 1) * CHUNK)
        sm = sm_ref[ts, :]
        g = -jnp.exp(alog_ref[...]) * _softplus(sm + dtb_ref[...])
        g = jnp.where(lane < S_AA + H_A, g, 0.0)
        beta = _sigmoid(sm)
        gc, gc_t = _chunk_cumsum(g)
        g_last = gc[CHUNK - 1:CHUNK, :]
        e_gc = jnp.exp(gc)
        e_rem = jnp.exp(g_last - gc)
        e_lasts.append(jnp.exp(g_last))
        for heads in groups:
            stack = lambda ref: jnp.concatenate([ref[ts, h * DK_A:(h + 1) * DK_A] for h in heads], axis=0)
            col = lambda arr, off: jnp.concatenate([arr[:, off + h:off + h + 1] for h in heads], axis=0)
            q_st, k_st, v_st = stack(q_ref), stack(k_ref), stack(v_ref)
            beta_st, egc_st, erem_st = col(beta, S_AB), col(e_gc, S_AA), col(e_rem, S_AA)
            g_row = jnp.concatenate([gc_t[S_AA + h:S_AA + h + 1, :] for h in heads], axis=1)
            diff = col(gc, S_AA) - g_row
            decay = jnp.where(tril, jnp.exp(jnp.where(tril, diff, 0.0)), 0.0)
            kb_st = k_st * beta_st
            problems.append(dict(q=q_st, k=k_st, kb=kb_st, decay=decay, qd=q_st * egc_st, kd=k_st * erem_st,
                                 rhs=jnp.concatenate([v_st * beta_st, kb_st * egc_st], axis=1)))
    lows = [jnp.where(strict, _dot_nt(p["kb"], p["k"]) * p["decay"], 0.0) for p in problems]
    tinvs = _unit_lower_inverses(lows)
    uws = [_dot3(tinv, p["rhs"]) for tinv, p in zip(tinvs, problems)]
    intras = [_dot_nt(p["q"], p["k"]) * p["decay"] for p in problems]

    head_rows = lambda gi_j: slice(gi_j * CHUNK, (gi_j + 1) * CHUNK)
    for ci in range(n_chunks):
        ts = slice(ci * CHUNK, (ci + 1) * CHUNK)
        e_last = e_lasts[ci]
        probs = range(ci * len(groups), (ci + 1) * len(groups))
        states = [state_ref[h] for h in range(H_A)]
        wq_s = {}
        for pi, heads in zip(probs, groups):
            for j, h in enumerate(heads):
                rs = head_rows(j)
                lhs = jnp.concatenate([uws[pi][rs, DK_A:], problems[pi]["qd"][rs]], axis=0)
                wq_s[h] = _dot(lhs, states[h])
        v_new = {h: uws[pi][head_rows(j), :DK_A] - wq_s[h][:CHUNK]
                 for pi, heads in zip(probs, groups) for j, h in enumerate(heads)}
        outs = [jnp.concatenate([wq_s[h][CHUNK:] for h in heads], axis=0)
                + _dot(intras[pi], jnp.concatenate([v_new[h] for h in heads], axis=0))
                for pi, heads in zip(probs, groups)]
        for pi, heads in zip(probs, groups):
            for j, h in enumerate(heads):
                ca = S_AA + h
                state_ref[h] = states[h] * e_last[:, ca:ca + 1] + _dot_tn(problems[pi]["kd"][head_rows(j)], v_new[h])
        for out, heads in zip(outs, groups):
            ms = jnp.mean(out * out, axis=-1, keepdims=True)
            out = out * lax.rsqrt(ms + EPS) * ng
            for j, h in enumerate(heads):
                sl = slice(h * DK_A, (h + 1) * DK_A)
                o_ref[ts, sl] = (out[head_rows(j)] * _silu(z_ref[ts, sl].astype(F32))).astype(o_ref.dtype)


def gated_deltanet(qkv3, p3, small3, a_log, dt_bias, norm_g):
    bsz, t, _ = qkv3.shape
    tt = _GDN_CHUNKS * CHUNK if t % (_GDN_CHUNKS * CHUNK) == 0 else CHUNK

    def row(vals, off):
        return jnp.zeros((1, S_COLS), F32).at[0, off:off + vals.shape[0]].set(vals.astype(F32))

    w = BRANCH_WIDTH
    return pl.pallas_call(
        _gdn_kernel,
        grid=(bsz, t // tt),
        in_specs=[
            pl.BlockSpec((None, tt, w), lambda b, n: (b, n, 0)),
            pl.BlockSpec((None, tt, w), lambda b, n: (b, n, 1)),
            pl.BlockSpec((None, tt, w), lambda b, n: (b, n, 2)),
            pl.BlockSpec((None, tt, w), lambda b, n: (b, n, P_AZ // w)),
            pl.BlockSpec((None, tt, S_COLS), lambda b, n: (b, n, 0)),
            pl.BlockSpec((1, S_COLS), lambda b, n: (0, 0)),
            pl.BlockSpec((1, S_COLS), lambda b, n: (0, 0)),
            pl.BlockSpec((1, DK_A), lambda b, n: (0, 0)),
        ],
        out_specs=pl.BlockSpec((None, tt, w), lambda b, n: (b, n, 0)),
        out_shape=jax.ShapeDtypeStruct((bsz, t, w), BF16),
        scratch_shapes=[pltpu.VMEM((H_A, DK_A, DK_A), F32)],
        compiler_params=_cparams(("parallel", "arbitrary")),
        name="gated_deltanet",
    )(qkv3, qkv3, qkv3, p3, small3, row(a_log, S_AA), row(dt_bias, S_AA), norm_g.reshape(1, DK_A).astype(F32))


def _ssd_kernel(xbc_ref, z_ref, sm_ref, alog_ref, dtb_ref, dskip_ref, ng_ref, o_ref, state_ref):
    @pl.when(pl.program_id(1) == 0)
    def _():
        state_ref[...] = jnp.zeros_like(state_ref)

    sm = sm_ref[...]
    lane = lax.broadcasted_iota(I32, (CHUNK, LANES), 1)
    dt = _softplus(sm + dtb_ref[...])
    a = jnp.where((lane >= S_DDT) & (lane < S_DDT + H_D), dt * -jnp.exp(alog_ref[...]), 0.0)
    acs, acs_t = _chunk_cumsum(a)
    a_last = acs[CHUNK - 1:CHUNK, :]
    e_acs = jnp.exp(acs)
    e_rem = jnp.exp(a_last - acs)
    e_last = jnp.exp(a_last)
    heads_per_group = H_D // N_GROUPS
    group_of = lambda h: h // heads_per_group
    bms = [xbc_ref[:, D_INNER + grp * D_STATE:D_INNER + (grp + 1) * D_STATE] for grp in range(N_GROUPS)]
    c0 = D_INNER + N_GROUPS * D_STATE
    cms = [xbc_ref[:, c0 + grp * D_STATE:c0 + (grp + 1) * D_STATE] for grp in range(N_GROUPS)]
    cbs = [_dot_nt(cm, bm) for cm, bm in zip(cms, bms)]
    xcs = [xbc_ref[:, h * P_D:(h + 1) * P_D] * dt[:, S_DDT + h:S_DDT + h + 1] for h in range(H_D)]
    states = [state_ref[h] for h in range(H_D)]
    ys = []
    for h in range(H_D):
        c = S_DDT + h
        ys.append(_dot(cbs[group_of(h)] * _segment_decay(acs, acs_t, c), xcs[h])
                  + _dot_nt(cms[group_of(h)] * e_acs[:, c:c + 1], states[h]))
    for h in range(H_D):
        c = S_DDT + h
        state_ref[h] = states[h] * e_last[:, c:c + 1] + _dot_tn(xcs[h], bms[group_of(h)] * e_rem[:, c:c + 1])
    y = jnp.concatenate(ys, axis=1) + dskip_ref[...] * xbc_ref[:, 0:D_INNER]
    y = y * _silu(z_ref[...].astype(F32))
    gw = D_INNER // N_GROUPS
    for grp in range(N_GROUPS):
        yg = y[:, grp * gw:(grp + 1) * gw]
        ms = jnp.mean(yg * yg, axis=-1, keepdims=True)
        o_ref[:, grp * gw:(grp + 1) * gw] = (yg * lax.rsqrt(ms + EPS) * ng_ref[:, grp * gw:(grp + 1) * gw]).astype(o_ref.dtype)


def mamba2_ssd(xbc3, p3, small3, a_log, dt_bias, d_skip, norm_g):
    bsz, t, _ = xbc3.shape
    nchunks = t // CHUNK

    def row(vals, off):
        return jnp.zeros((1, S_COLS), F32).at[0, off:off + vals.shape[0]].set(vals.astype(F32))

    return pl.pallas_call(
        _ssd_kernel,
        grid=(bsz, nchunks),
        in_specs=[
            pl.BlockSpec((None, CHUNK, XBC_WIDTH), lambda b, n: (b, n, 0)),
            pl.BlockSpec((None, CHUNK, D_INNER), lambda b, n: (b, n, P_DZ // D_INNER)),
            pl.BlockSpec((None, CHUNK, S_COLS), lambda b, n: (b, n, 0)),
            pl.BlockSpec((1, S_COLS), lambda b, n: (0, 0)),
            pl.BlockSpec((1, S_COLS), lambda b, n: (0, 0)),
            pl.BlockSpec((1, D_INNER), lambda b, n: (0, 0)),
            pl.BlockSpec((1, D_INNER), lambda b, n: (0, 0)),
        ],
        out_specs=pl.BlockSpec((None, CHUNK, D_INNER), lambda b, n: (b, n, 0)),
        out_shape=jax.ShapeDtypeStruct((bsz, t, D_INNER), BF16),
        scratch_shapes=[pltpu.VMEM((H_D, P_D, D_STATE), F32)],
        compiler_params=_cparams(("parallel", "arbitrary")),
        name="mamba2_ssd",
    )(xbc3, p3, small3, row(a_log, S_DDT), row(dt_bias, S_DDT),
      jnp.repeat(d_skip.astype(F32), P_D).reshape(1, D_INNER), norm_g.reshape(1, D_INNER).astype(F32))


def _dsa_prep_kernel(q_ref, ckv_ref, sm_ref, wuk_ref, cg_ref, qg_ref, kg_ref, qh_ref, kk_ref, ct_ref, ki_ref):
    ckv = ckv_ref[...].astype(F32)
    c = ckv * lax.rsqrt(jnp.mean(ckv * ckv, axis=-1, keepdims=True) + EPS) * cg_ref[...]
    cb = c.astype(BF16)
    ct_ref[...] = c.T.astype(BF16)
    kk = _dot(cb, wuk_ref[...])
    kk_ref[...] = (kk * lax.rsqrt(jnp.mean(kk * kk, axis=-1, keepdims=True) + EPS) * kg_ref[...]).astype(BF16)
    for h in range(H_B):
        sl = slice(h * DH_B, (h + 1) * DH_B)
        qh = q_ref[:, sl].astype(F32)
        qh = qh * lax.rsqrt(jnp.mean(qh * qh, axis=-1, keepdims=True) + EPS) * qg_ref[...]
        qh_ref[:, sl] = (qh * DH_B ** -0.5).astype(BF16)
    ki_ref[...] = sm_ref[:, S_BKI:S_BKI + D_IDX].astype(BF16)


def dsa_prep(p3, small3, w_uk, ckv_g, q_g, k_g, tt=512):
    bsz, t, _ = p3.shape
    tt = min(tt, t)
    w = H_B * DH_B
    return pl.pallas_call(
        _dsa_prep_kernel,
        grid=(bsz, t // tt),
        in_specs=[
            pl.BlockSpec((None, tt, w), lambda b, i: (b, i, P_BQ // w)),
            pl.BlockSpec((None, tt, DC_B), lambda b, i: (b, i, P_BCKV // DC_B)),
            pl.BlockSpec((None, tt, S_COLS), lambda b, i: (b, i, 0)),
            pl.BlockSpec((DC_B, DH_B), lambda b, i: (0, 0)),
            pl.BlockSpec((1, DC_B), lambda b, i: (0, 0)),
            pl.BlockSpec((1, DH_B), lambda b, i: (0, 0)),
            pl.BlockSpec((1, DH_B), lambda b, i: (0, 0)),
        ],
        out_specs=[
            pl.BlockSpec((None, tt, w), lambda b, i: (b, i, 0)),
            pl.BlockSpec((None, tt, DH_B), lambda b, i: (b, i, 0)),
            pl.BlockSpec((None, DC_B, tt), lambda b, i: (b, 0, i)),
            pl.BlockSpec((None, tt, D_IDX), lambda b, i: (b, i, 0)),
        ],
        out_shape=[
            jax.ShapeDtypeStruct((bsz, t, w), BF16),
            jax.ShapeDtypeStruct((bsz, t, DH_B), BF16),
            jax.ShapeDtypeStruct((bsz, DC_B, t), BF16),
            jax.ShapeDtypeStruct((bsz, t, D_IDX), BF16),
        ],
        compiler_params=_cparams(("parallel", "parallel")),
        name="dsa_prep",
    )(p3, p3, small3, w_uk.astype(BF16), ckv_g.reshape(1, DC_B).astype(F32),
      q_g.reshape(1, DH_B).astype(F32), k_g.reshape(1, DH_B).astype(F32))


_T5_HALF = N_BUCKETS // 2
_T5_EXACT = _T5_HALF // 2
_T5_FAR = _T5_HALF - 1


def _relbias_kernel(rb_ref, prev_ref, diag_ref):
    h = pl.program_id(0)
    kl = lax.broadcasted_iota(I32, (Q_BLOCK, Q_BLOCK), 0)
    ql = lax.broadcasted_iota(I32, (Q_BLOCK, Q_BLOCK), 1)
    far = rb_ref[_T5_FAR, h]
    for ref, shift in ((prev_ref, -Q_BLOCK), (diag_ref, 0)):
        rel = kl - ql + shift
        n = jnp.abs(rel)
        n2 = n * n
        steps = jnp.zeros_like(n)
        for j in range(1, _T5_HALF - _T5_EXACT):
            steps = steps + (n2 >= (_T5_EXACT * _T5_EXACT) * 2 ** j).astype(I32)
        large = jnp.minimum(_T5_EXACT + steps, _T5_HALF - 1)
        bucket = jnp.where(rel > 0, _T5_HALF, 0) + jnp.where(n < _T5_EXACT, n, large)
        acc = jnp.zeros((Q_BLOCK, Q_BLOCK), F32)
        for b in range(N_BUCKETS):
            acc = jnp.where(bucket == b, rb_ref[b, h], acc)
        ref[...] = acc - far


def relbias_tables(rel_bias):
    shp = jax.ShapeDtypeStruct((H_B, Q_BLOCK, Q_BLOCK), F32)
    spec = pl.BlockSpec((None, Q_BLOCK, Q_BLOCK), lambda h: (h, 0, 0))
    return pl.pallas_call(
        _relbias_kernel,
        grid=(H_B,),
        in_specs=[pl.BlockSpec(memory_space=pltpu.SMEM)],
        out_specs=[spec, spec],
        out_shape=[shp, shp],
        compiler_params=_cparams(("arbitrary",)),
        name="relbias_tables",
    )(rel_bias.astype(F32))


_INT_MIN = -2 ** 31
_SCORE_ROWS = 256
_SWEEP_ROWS = 64
_DSA_WIDTH_STEP = 512


def _sweep_rows(n_rows, init, step):
    def body(c, acc):
        return step(pl.multiple_of(c * _SWEEP_ROWS, _SWEEP_ROWS), acc)
    n_steps = n_rows // _SWEEP_ROWS
    return lax.fori_loop(0, n_steps, body, init, unroll=min(8, n_steps))


def _dsa_kernel(qh_ref, qi_ref, sm_ref, kk_ref, ct_ref, ki_ref, dprev_ref, ddiag_ref, wuvt_ref,
                o_ref, key_ref, selb_ref, lg_ref, p_ref, *, widths, topk):
    i = pl.program_id(1)
    need = (i + 1) * Q_BLOCK
    wi_t = sm_ref[...].T[S_BWI:S_BWI + H_IDX, :] * (H_IDX ** -0.5 * D_IDX ** -0.5)
    qi_all = jnp.concatenate([qi_ref[:, h * D_IDX:(h + 1) * D_IDX] for h in range(H_IDX)], axis=0)
    q_chunk = (i * Q_BLOCK + lax.broadcasted_iota(I32, (1, Q_BLOCK), 1)) >> _CHUNK_SHIFT
    row_iota = lax.broadcasted_iota(I32, (_SWEEP_ROWS, Q_BLOCK), 0)
    block = lambda ref, r0: ref[pl.ds(r0, _SWEEP_ROWS), :]

    def body(nc):
        for r0 in range(0, nc, min(_SCORE_ROWS, nc)):
            rows = slice(r0, r0 + min(_SCORE_ROWS, nc))
            s_all = _dot_nt(ki_ref[rows, :], qi_all)
            sc = None
            for h in range(H_IDX):
                term = wi_t[h:h + 1, :] * jnp.maximum(s_all[:, h * Q_BLOCK:(h + 1) * Q_BLOCK], 0.0)
                sc = term if sc is None else sc + term
            bits = lax.bitcast_convert_type(sc + 0.0, I32)
            key = jnp.where(bits < 0, bits ^ 0x7FFFFFFF, bits)
            kpos = r0 + lax.broadcasted_iota(I32, key.shape, 0)
            key_ref[rows, :] = jnp.where((kpos >> _CHUNK_SHIFT) <= q_chunk, key, _INT_MIN)

        def count(pred):
            acc = _sweep_rows(nc, jnp.zeros((_SWEEP_ROWS, Q_BLOCK), I32),
                              lambda r0, acc: acc + pred(r0, block(key_ref, r0)).astype(I32))
            return jnp.sum(acc, axis=0, keepdims=True)

        tau = jnp.where(count(lambda r0, k: k >= 0) >= topk, 0, _INT_MIN).astype(I32)

        def vstep(it, tau):
            cand = tau | (jnp.int32(1) << (30 - it))
            return jnp.where(count(lambda r0, k: k >= cand) >= topk, cand, tau)

        tau = lax.fori_loop(0, 31, vstep, tau)
        n_gt = count(lambda r0, k: k > tau)
        n_eq = count(lambda r0, k: k == tau)
        tied = (n_gt + n_eq > topk) & (tau > _INT_MIN)
        nbits = max(1, (nc - 1).bit_length())

        def last_tied_index():
            room = topk - n_gt

            def istep(it, last):
                cand = last | (jnp.int32(1) << (nbits - 1 - it))
                below = count(lambda r0, k: (k == tau) & (r0 + row_iota < cand))
                return jnp.where(below < room, cand, last)

            return lax.fori_loop(0, nbits, istep, jnp.zeros((1, Q_BLOCK), I32))

        last = lax.cond(jnp.max(tied.astype(I32)) > 0, last_tied_index,
                        lambda: jnp.full((1, Q_BLOCK), nc, I32))

        def write_sel(r0, carry):
            k = block(key_ref, r0)
            sel = (k > _INT_MIN) & ((k > tau) | ((k == tau) & (r0 + row_iota <= last)))
            selb_ref[pl.ds(r0, _SWEEP_ROWS), :] = jnp.where(sel, 0.0, -jnp.inf)
            return carry

        _sweep_rows(nc, 0, write_sel)

        kk = kk_ref[0:nc, :]
        ct = ct_ref[:, 0:nc]
        prev_row = pl.multiple_of(jnp.maximum(i - 1, 0) * Q_BLOCK, Q_BLOCK)
        diag_row = pl.multiple_of(i * Q_BLOCK, Q_BLOCK)
        has_prev = (i > 0).astype(F32)
        for h0 in range(0, H_B, 2):
            q_pair = jnp.concatenate([qh_ref[:, h * DH_B:(h + 1) * DH_B] for h in (h0, h0 + 1)], axis=0)
            lg_pair = _dot_nt(kk, q_pair)
            for j, h in enumerate((h0, h0 + 1)):
                lg_ref[h, 0:nc, :] = lg_pair[:, j * Q_BLOCK:(j + 1) * Q_BLOCK] + selb_ref[0:nc, :]
                lg_ref[h, pl.ds(prev_row, Q_BLOCK), :] += dprev_ref[h] * has_prev
                lg_ref[h, pl.ds(diag_row, Q_BLOCK), :] += ddiag_ref[h]
        blk = min(_SCORE_ROWS, nc)
        outs = []
        for h in range(H_B):
            slot = h % 2
            m_acc = None
            for r0 in range(0, nc, blk):
                x = lg_ref[h, r0:r0 + blk, :]
                m_acc = x if m_acc is None else jnp.maximum(m_acc, x)
            m = jnp.max(m_acc, axis=0, keepdims=True)
            p_acc = None
            for r0 in range(0, nc, blk):
                p = jnp.exp(lg_ref[h, r0:r0 + blk, :] - m)
                p_ref[slot, r0:r0 + blk, :] = p.astype(BF16)
                p_acc = p if p_acc is None else p_acc + p
            denom = jnp.sum(p_acc, axis=0, keepdims=True)
            o_lat = _dot(ct, p_ref[slot, 0:nc, :]) * (1.0 / denom)
            outs.append(_dot(wuvt_ref[h], o_lat.astype(BF16)).T)
        o_ref[...] = jnp.concatenate(outs, axis=1).astype(o_ref.dtype)

    lo = 0
    for nc in widths:
        @pl.when((need > lo) & (need <= nc))
        def _(nc=nc):
            body(nc)
        lo = nc


def dsa_attention(qh3, qi_p3, small3, kk3, ct3, ki3, dprev, ddiag, wuv_t):
    bsz, t, w = qh3.shape
    topk = min(DSA_TOPK, t // 4)
    step = min(_DSA_WIDTH_STEP, t)
    widths = tuple(range(step, t + 1, step))
    wq = H_IDX * D_IDX
    full = lambda shape: pl.BlockSpec(shape, lambda b, i: (0,) * len(shape))
    return pl.pallas_call(
        functools.partial(_dsa_kernel, widths=widths, topk=topk),
        grid=(bsz, t // Q_BLOCK),
        in_specs=[
            pl.BlockSpec((None, Q_BLOCK, w), lambda b, i: (b, i, 0)),
            pl.BlockSpec((None, Q_BLOCK, wq), lambda b, i: (b, i, P_BQI // wq)),
            pl.BlockSpec((None, Q_BLOCK, S_COLS), lambda b, i: (b, i, 0)),
            pl.BlockSpec((None, t, DH_B), lambda b, i: (b, 0, 0)),
            pl.BlockSpec((None, DC_B, t), lambda b, i: (b, 0, 0)),
            pl.BlockSpec((None, t, D_IDX), lambda b, i: (b, 0, 0)),
            full((H_B, Q_BLOCK, Q_BLOCK)),
            full((H_B, Q_BLOCK, Q_BLOCK)),
            full((H_B, DH_B, DC_B)),
        ],
        out_specs=pl.BlockSpec((None, Q_BLOCK, w), lambda b, i: (b, i, 0)),
        out_shape=jax.ShapeDtypeStruct((bsz, t, w), BF16),
        scratch_shapes=[pltpu.VMEM((t, Q_BLOCK), I32), pltpu.VMEM((t, Q_BLOCK), F32),
                        pltpu.VMEM((H_B, t, Q_BLOCK), F32), pltpu.VMEM((2, t, Q_BLOCK), BF16)],
        compiler_params=_cparams(("parallel", "arbitrary")),
        name="dsa_attention",
    )(qh3, qi_p3, small3, kk3, ct3, ki3, dprev, ddiag, wuv_t)


def _merge_kernel(xn_ref, o0_ref, o1_ref, o2_ref, o3_ref, g0_ref, g1_ref, g2_ref, g3_ref,
                  u0_ref, u1_ref, u2_ref, u3_ref, out_ref):
    xn = xn_ref[...]
    acc = None
    for o_ref, g_ref, u_ref in ((o0_ref, g0_ref, u0_ref), (o1_ref, g1_ref, u1_ref),
                                (o2_ref, g2_ref, u2_ref), (o3_ref, g3_ref, u3_ref)):
        term = _sigmoid(_dot(xn, g_ref[...])) * _dot(o_ref[...], u_ref[...].astype(BF16))
        acc = term if acc is None else acc + term
    out_ref[...] = acc.astype(out_ref.dtype)


def merge_branches(xn, outs, w_gate, w_up, layer, tm=512, tn=256):
    m, d = xn.shape
    tm, tn = min(tm, m), min(tn, d)
    nj = d // tn
    in_specs = [pl.BlockSpec((tm, d), lambda j, i: (i, 0))]
    in_specs += [pl.BlockSpec((tm, BRANCH_WIDTH), lambda j, i: (i, 0)) for _ in range(N_BRANCHES)]
    in_specs += [pl.BlockSpec((d, tn), functools.partial(lambda j, i, br: (0, br * nj + j), br=br))
                 for br in range(N_BRANCHES)]
    in_specs += [pl.BlockSpec((None, None, BRANCH_WIDTH, tn),
                              functools.partial(lambda j, i, br: (layer, br, 0, j), br=br))
                 for br in range(N_BRANCHES)]
    return pl.pallas_call(
        _merge_kernel,
        grid=(nj, m // tm),
        in_specs=in_specs,
        out_specs=pl.BlockSpec((tm, tn), lambda j, i: (i, j)),
        out_shape=jax.ShapeDtypeStruct((m, d), BF16),
        compiler_params=_cparams(("parallel", "parallel")),
        name="merge_branches",
    )(xn, *outs, *([w_gate] * N_BRANCHES), *([w_up] * N_BRANCHES))


def _glu_kernel(x_ref, w1_ref, w3_ref, o_ref):
    x = x_ref[...]
    o_ref[...] = (_silu(_dot(x, w1_ref[...].astype(BF16))) * _dot(x, w3_ref[...].astype(BF16))).astype(o_ref.dtype)


def glu_dense(x, w1, w3, tm=2048, tf=256):
    m, d = x.shape
    f = w1.shape[1]
    tm, tf = min(tm, m), min(tf, f)
    assert f % tf == 0
    return pl.pallas_call(
        _glu_kernel,
        grid=(f // tf, m // tm),
        in_specs=[pl.BlockSpec((tm, d), lambda j, i: (i, 0)),
                  pl.BlockSpec((d, tf), lambda j, i: (0, j)),
                  pl.BlockSpec((d, tf), lambda j, i: (0, j))],
        out_specs=pl.BlockSpec((tm, tf), lambda j, i: (i, j)),
        out_shape=jax.ShapeDtypeStruct((m, f), BF16),
        compiler_params=_cparams(("parallel", "parallel")),
        name="glu_dense",
    )(x, w1, w3)


MOE_TM = 512
MOE_TT = 256
MOE_UP_TN = 512
MOE_DOWN_TN = 1024
SLOT_POS_A, SLOT_POS_B, SLOT_W_A, SLOT_W_B = 0, 1, 2, 3
_ROW_COPY_UNROLL = 8


def _pack_bf16_pair(lo, hi):
    bits = lambda v: lax.bitcast_convert_type(v.astype(BF16).astype(F32), U32)
    return bits(hi) | (bits(lo) >> 16)


def _unpack_bf16_pair(u):
    return (lax.bitcast_convert_type(u << 16, F32),
            lax.bitcast_convert_type(u & jnp.uint32(0xFFFF0000), F32))


def _router_kernel(x_ref, g_ref, wr_ref, h_ref, gate_ref, sel_ref):
    x = x_ref[...]
    h = x * lax.rsqrt(jnp.mean(x * x, axis=-1, keepdims=True) + EPS) * g_ref[...]
    half = h.shape[1] // 2
    h_ref[...] = _pack_bf16_pair(h[:, :half], h[:, half:])
    logits = _dot3(h, wr_ref[...])
    lane = lax.broadcasted_iota(I32, logits.shape, 1)
    logits = jnp.where(lane < N_EXPERTS, logits, -jnp.inf)
    m1 = jnp.max(logits, axis=-1, keepdims=True)
    i1 = jnp.min(jnp.where(logits == m1, lane, LANES), axis=-1, keepdims=True)
    rest = jnp.where(lane == i1, -jnp.inf, logits)
    m2 = jnp.max(rest, axis=-1, keepdims=True)
    i2 = jnp.min(jnp.where(rest == m2, lane, LANES), axis=-1, keepdims=True)
    e2 = jnp.exp(m2 - m1)
    inv = 1.0 / (1.0 + e2)
    gate_ref[...] = jnp.where(lane == i1, inv, 0.0) + jnp.where(lane == i2, e2 * inv, 0.0)
    sel_ref[...] = jnp.where((lane == i1) | (lane == i2), 1.0, 0.0).astype(sel_ref.dtype)


def moe_router(x, g, w_router, tm=512):
    m, d = x.shape
    tm = min(tm, m)
    wr = jnp.zeros((d, LANES), F32).at[:, :N_EXPERTS].set(w_router.astype(F32))
    row = lambda w: pl.BlockSpec((tm, w), lambda i: (i, 0))
    return pl.pallas_call(
        _router_kernel,
        grid=(m // tm,),
        in_specs=[row(d), pl.BlockSpec((1, d), lambda i: (0, 0)), pl.BlockSpec((d, LANES), lambda i: (0, 0))],
        out_specs=[row(d // 2), row(LANES), row(LANES)],
        out_shape=[jax.ShapeDtypeStruct((m, d // 2), U32), jax.ShapeDtypeStruct((m, LANES), F32),
                   jax.ShapeDtypeStruct((m, LANES), BF16)],
        compiler_params=_cparams(("parallel",)),
        name="moe_router",
    )(x, g.reshape(1, d).astype(F32), wr)


def _rank_kernel(sel_ref, rank_ref, cnt_ref, carry_ref):
    @pl.when(pl.program_id(0) == 0)
    def _():
        carry_ref[...] = jnp.zeros_like(carry_ref)

    sel = sel_ref[...]
    n = sel.shape[0]
    earlier = _tril_mask(n, strict=True).astype(BF16)
    rank_ref[...] = _dot(earlier, sel) + carry_ref[...]
    carry_ref[...] += jnp.sum(sel.astype(F32), axis=0, keepdims=True)
    cnt_ref[...] = carry_ref[...]


def moe_rank(sel, tr=512):
    m = sel.shape[0]
    tr = min(tr, m)
    return pl.pallas_call(
        _rank_kernel,
        grid=(m // tr,),
        in_specs=[pl.BlockSpec((tr, LANES), lambda i: (i, 0))],
        out_specs=[pl.BlockSpec((tr, LANES), lambda i: (i, 0)), pl.BlockSpec((1, LANES), lambda i: (0, 0))],
        out_shape=[jax.ShapeDtypeStruct((m, LANES), F32), jax.ShapeDtypeStruct((1, LANES), F32)],
        scratch_shapes=[pltpu.VMEM((1, LANES), F32)],
        compiler_params=_cparams(("arbitrary",)),
        name="moe_rank",
    )(sel)


def _slots_kernel(rank_ref, sel_ref, gate_ref, start_ref, out_ref):
    sel = sel_ref[...].astype(F32) > 0.0
    lane = lax.broadcasted_iota(I32, sel.shape, 1)
    dest = start_ref[...] + rank_ref[...]
    first = jnp.min(jnp.where(sel, lane, LANES), axis=-1, keepdims=True)
    second = jnp.max(jnp.where(sel, lane, -1), axis=-1, keepdims=True)
    pick = lambda arr, idx: jnp.sum(jnp.where(lane == idx, arr, 0.0), axis=-1, keepdims=True)
    gates = gate_ref[...]
    out = jnp.where(lane == SLOT_POS_A, pick(dest, first), 0.0)
    out = jnp.where(lane == SLOT_POS_B, pick(dest, second), out)
    out = jnp.where(lane == SLOT_W_A, pick(gates, first), out)
    out_ref[...] = jnp.where(lane == SLOT_W_B, pick(gates, second), out)


def moe_slots(rank, sel, gates, start_row, tr=512):
    m = sel.shape[0]
    tr = min(tr, m)
    row = pl.BlockSpec((tr, LANES), lambda i: (i, 0))
    return pl.pallas_call(
        _slots_kernel,
        grid=(m // tr,),
        in_specs=[row, row, row, pl.BlockSpec((1, LANES), lambda i: (0, 0))],
        out_specs=row,
        out_shape=jax.ShapeDtypeStruct((m, LANES), F32),
        compiler_params=_cparams(("parallel",)),
        name="moe_slots",
    )(rank, sel, gates, start_row)


def _row_copy(src_ref, src_row, dst_ref, dst_row, sem):
    return pltpu.make_async_copy(src_ref.at[pl.ds(src_row, 1), :], dst_ref.at[pl.ds(dst_row, 1), :], sem)


def _dispatch_kernel(pos_ref, tail_ref, h_ref, xg_ref, zero_ref, sem, *, tm):
    tt = h_ref.shape[0]

    @pl.when(pl.program_id(0) == 0)
    def _():
        zero_ref[...] = jnp.zeros_like(zero_ref)
        fill = lambda e: pltpu.make_async_copy(zero_ref, xg_ref.at[pl.ds(pl.multiple_of(tail_ref[e], tm), tm), :], sem)
        for e in range(tail_ref.shape[0]):
            @pl.when(tail_ref[e] >= 0)
            def _(e=e):
                fill(e).start()
        for e in range(tail_ref.shape[0]):
            @pl.when(tail_ref[e] >= 0)
            def _(e=e):
                fill(e).wait()

    def start(r, carry):
        _row_copy(h_ref, r, xg_ref, pos_ref[0, r], sem).start(priority=0)
        _row_copy(h_ref, r, xg_ref, pos_ref[0, tt + r], sem).start(priority=1)
        return carry

    def wait(r, carry):
        _row_copy(h_ref, r, xg_ref, pos_ref[0, r], sem).wait()
        _row_copy(h_ref, r, xg_ref, pos_ref[0, tt + r], sem).wait()
        return carry

    lax.fori_loop(0, tt, start, 0, unroll=_ROW_COPY_UNROLL)
    lax.fori_loop(0, tt, wait, 0, unroll=_ROW_COPY_UNROLL)


def moe_dispatch(h, pos_tiles, tails, rows, tm):
    m, d = h.shape
    tt = pos_tiles.shape[2] // 2
    return pl.pallas_call(
        functools.partial(_dispatch_kernel, tm=tm),
        grid=(m // tt,),
        in_specs=[pl.BlockSpec((None, 1, 2 * tt), lambda i: (i, 0, 0), memory_space=pltpu.SMEM),
                  pl.BlockSpec(memory_space=pltpu.SMEM),
                  pl.BlockSpec((tt, d), lambda i: (i, 0))],
        out_specs=pl.BlockSpec(memory_space=pl.ANY),
        out_shape=jax.ShapeDtypeStruct((rows, d), h.dtype),
        scratch_shapes=[pltpu.VMEM((tm, d), h.dtype), pltpu.SemaphoreType.DMA(())],
        compiler_params=_cparams(("arbitrary",)),
        name="moe_dispatch",
    )(pos_tiles, tails, h)


def _glu_grouped_kernel(te_ref, nv_ref, x_ref, w1_ref, w3_ref, o_ref):
    valid = pl.program_id(1) < nv_ref[0]

    @pl.when(valid)
    def _():
        lo, hi = (v.astype(BF16) for v in _unpack_bf16_pair(x_ref[...]))
        half = lo.shape[1]
        up = lambda w_ref: (_dot(lo, w_ref[0:half, :].astype(BF16)) + _dot(hi, w_ref[half:2 * half, :].astype(BF16)))
        o_ref[...] = (_silu(up(w1_ref)) * up(w3_ref)).astype(o_ref.dtype)

    @pl.when(jnp.logical_not(valid))
    def _():
        o_ref[...] = jnp.zeros_like(o_ref)


def _down_grouped_kernel(te_ref, nv_ref, h_ref, w2_ref, o_ref):
    valid = pl.program_id(1) < nv_ref[0]

    @pl.when(valid)
    def _():
        y = _dot(h_ref[...], w2_ref[...].astype(BF16))
        half = y.shape[1] // 2
        o_ref[...] = _pack_bf16_pair(y[:, :half], y[:, half:])

    @pl.when(jnp.logical_not(valid))
    def _():
        o_ref[...] = jnp.zeros_like(o_ref)


def _grouped_call(kernel, x, weights, tile_expert, n_valid, n_col_tiles, tn, out_tn, out_dtype, tm, name):
    rows, d = x.shape
    tile = lambda i, nv: jnp.minimum(i, nv[0] - 1)
    in_specs = [pl.BlockSpec((tm, d), lambda j, i, te, nv: (tile(i, nv), 0))]
    in_specs += [pl.BlockSpec((None, w.shape[1], tn), lambda j, i, te, nv: (te[tile(i, nv)], 0, j)) for w in weights]
    grid_spec = pltpu.PrefetchScalarGridSpec(
        num_scalar_prefetch=2,
        grid=(n_col_tiles, rows // tm),
        in_specs=in_specs,
        out_specs=pl.BlockSpec((tm, out_tn), lambda j, i, te, nv: (i, j)),
    )
    return pl.pallas_call(
        kernel,
        grid_spec=grid_spec,
        out_shape=jax.ShapeDtypeStruct((rows, n_col_tiles * out_tn), out_dtype),
        compiler_params=_cparams(("arbitrary", "arbitrary")),
        name=name,
    )(tile_expert, n_valid, x, *weights)


def _combine_kernel(pos_ref, x_ref, slot_ref, yg_ref, o_ref, buf_ref, sem):
    tt = x_ref.shape[0]

    def start(r, carry):
        _row_copy(yg_ref, pos_ref[0, r], buf_ref.at[0], r, sem).start(priority=0)
        _row_copy(yg_ref, pos_ref[0, tt + r], buf_ref.at[1], r, sem).start(priority=1)
        return carry

    def wait(r, carry):
        _row_copy(yg_ref, pos_ref[0, r], buf_ref.at[0], r, sem).wait()
        _row_copy(yg_ref, pos_ref[0, tt + r], buf_ref.at[1], r, sem).wait()
        return carry

    lax.fori_loop(0, tt, start, 0, unroll=_ROW_COPY_UNROLL)
    lax.fori_loop(0, tt, wait, 0, unroll=_ROW_COPY_UNROLL)
    slots = slot_ref[...]
    w_a = slots[:, SLOT_W_A:SLOT_W_A + 1]
    w_b = slots[:, SLOT_W_B:SLOT_W_B + 1]
    half = MOE_DOWN_TN // 2
    for j in range(x_ref.shape[1] // MOE_DOWN_TN):
        lo_a, hi_a = _unpack_bf16_pair(buf_ref[0, :, j * half:(j + 1) * half])
        lo_b, hi_b = _unpack_bf16_pair(buf_ref[1, :, j * half:(j + 1) * half])
        c_lo = slice(j * MOE_DOWN_TN, j * MOE_DOWN_TN + half)
        c_hi = slice(j * MOE_DOWN_TN + half, (j + 1) * MOE_DOWN_TN)
        o_ref[:, c_lo] = x_ref[:, c_lo] + w_a * lo_a + w_b * lo_b
        o_ref[:, c_hi] = x_ref[:, c_hi] + w_a * hi_a + w_b * hi_b


def moe_combine(x, slots, pos_tiles, yg):
    m, d = x.shape
    tt = pos_tiles.shape[2] // 2
    return pl.pallas_call(
        _combine_kernel,
        grid=(m // tt,),
        in_specs=[pl.BlockSpec((None, 1, 2 * tt), lambda i: (i, 0, 0), memory_space=pltpu.SMEM),
                  pl.BlockSpec((tt, d), lambda i: (i, 0)),
                  pl.BlockSpec((tt, LANES), lambda i: (i, 0)),
                  pl.BlockSpec(memory_space=pl.ANY)],
        out_specs=pl.BlockSpec((tt, d), lambda i: (i, 0)),
        out_shape=jax.ShapeDtypeStruct((m, d), F32),
        scratch_shapes=[pltpu.VMEM((2, tt, yg.shape[1]), yg.dtype), pltpu.SemaphoreType.DMA(())],
        compiler_params=_cparams(("arbitrary",)),
        name="moe_combine",
    )(pos_tiles, x, slots, yg)


def moe_layer(x2, norm_g, w_router, w1, w3, w2):
    m, d = x2.shape
    tm = min(MOE_TM, m)
    tt = min(MOE_TT, m)
    n_tiles = (TOP_K * m) // tm + N_EXPERTS
    rows = n_tiles * tm
    h, gates, sel = moe_router(x2, norm_g, w_router)
    rank, counts = moe_rank(sel)

    cnt = counts[0, :N_EXPERTS].astype(I32)
    padded = ((cnt + tm - 1) // tm) * tm
    ends = jnp.cumsum(padded)
    starts = ends - padded
    start_row = jnp.zeros((1, LANES), F32).at[0, :N_EXPERTS].set(starts.astype(F32))
    tile_expert = jnp.minimum(jnp.searchsorted(ends, jnp.arange(n_tiles, dtype=I32) * tm, side="right"),
                              N_EXPERTS - 1).astype(I32)
    n_valid = (ends[-1:] // tm).astype(I32)
    unused = ends[-1] + jnp.arange(N_EXPERTS, dtype=I32) * tm
    tails = jnp.concatenate([jnp.where(padded > 0, ends - tm, -1),
                             jnp.where(unused < rows, unused, -1)]).astype(I32)

    slots = moe_slots(rank, sel, gates, start_row)
    pos = slots[:, :2].astype(I32).reshape(m // tt, tt, 2)
    pos_tiles = jnp.swapaxes(pos, 1, 2).reshape(m // tt, 1, 2 * tt)

    xg = moe_dispatch(h, pos_tiles, tails, rows, tm)
    hid = _grouped_call(_glu_grouped_kernel, xg, [w1, w3], tile_expert, n_valid,
                        w1.shape[2] // MOE_UP_TN, MOE_UP_TN, MOE_UP_TN, BF16, tm, "moe_glu")
    yg = _grouped_call(_down_grouped_kernel, hid, [w2], tile_expert, n_valid,
                       d // MOE_DOWN_TN, MOE_DOWN_TN, MOE_DOWN_TN // 2, U32, tm, "moe_down")
    return moe_combine(x2, slots, pos_tiles, yg)


def _split_in_proj(w_in_l):
    seg = lambda lo, hi: w_in_l[:, lo:hi]
    big = jnp.concatenate([
        seg(O_AQ, O_AZ), seg(O_AZ, O_AA), seg(O_BQ, O_BCKV), seg(O_CB, O_DZ),
        seg(O_DZ, O_DXBC), seg(O_DXBC, O_DDT), seg(O_BQI, O_BKI), seg(O_BCKV, O_BQI)], axis=1).astype(BF16)
    pad = jnp.zeros((w_in_l.shape[0], S_COLS - (S_DDT + H_D)), w_in_l.dtype)
    small = jnp.concatenate([
        seg(O_AA, O_AB), seg(O_AB, O_BQ), seg(O_BKI, O_BWI), seg(O_BWI, O_CB), seg(O_DDT, _O_END), pad],
        axis=1).astype(BF16)
    gate = w_in_l[:, MIX_COLS:].astype(BF16)
    return big, small, gate


def _mixer_layer(x2, bsz, t, mix_norm_g, w_in_l, conv_a_w, a_log_a, dt_bias_a, out_norm_a_g, ckv_norm_g, w_uk,
                 q_norm_b_g, k_norm_b_g, w_uv, rel_bias, bias_tables, conv_c_w, conv_d_w, conv_d_b, a_log_d,
                 dt_bias_d, d_skip, out_norm_d_g, w_up_all, w_out_all, layer):
    m = bsz * t
    xn = rmsnorm(x2, mix_norm_g)
    w_big, w_small, w_gate = _split_in_proj(w_in_l)
    p3 = matmul(xn, w_big, out_dtype=BF16, tm=1024, tn=768, name="in_proj").reshape(bsz, t, P_COLS)
    small3 = matmul(xn, w_small, out_dtype=F32, tm=1024, tn=S_COLS, name="in_proj_small").reshape(bsz, t, S_COLS)

    qkv3 = _conv_call(functools.partial(_conv_qkv_kernel, tc=512), p3, [P_AQKV], 3 * BRANCH_WIDTH,
                      [conv_a_w.astype(F32)], F32, 512, "conv_qkv")
    o_a = gated_deltanet(qkv3, p3, small3, a_log_a, dt_bias_a, out_norm_a_g)

    qh3, kk3, ct3, ki3 = dsa_prep(p3, small3, w_uk, ckv_norm_g, q_norm_b_g, k_norm_b_g)
    dprev, ddiag = bias_tables
    o_b = dsa_attention(qh3, p3, small3, kk3, ct3, ki3, dprev, ddiag,
                        jnp.swapaxes(w_uv, 1, 2).astype(BF16))

    o_c = _conv_call(_conv_gated_kernel, p3, [P_C, P_C + BRANCH_WIDTH, P_C + 2 * BRANCH_WIDTH], BRANCH_WIDTH,
                     [conv_c_w.astype(F32)], BF16, 512, "conv_gated")

    xbc3 = _conv_call(_conv_xbc_kernel, p3, [P_DXBC], XBC_WIDTH,
                      [conv_d_w.astype(F32), conv_d_b.reshape(1, XBC_WIDTH).astype(F32)], F32, 512, "conv_xbc")
    o_d = mamba2_ssd(xbc3, p3, small3, a_log_d, dt_bias_d, d_skip, out_norm_d_g)

    outs = [o.reshape(m, BRANCH_WIDTH) for o in (o_a, o_b, o_c, o_d)]
    merged = merge_branches(xn, outs, w_gate, w_up_all, layer)
    return matmul(merged, w_out_all, out_dtype=F32, residual=x2, tm=1024, tn=512, layer=layer, name="out_proj")


def _half_k(k):
    return k // 2 if k % (2 * LANES) == 0 else k


def kernel(x, mix_norm_g, w_in, conv_a_w, a_log_a, dt_bias_a, out_norm_a_g, ckv_norm_g, w_uk, q_norm_b_g, k_norm_b_g, w_uv, rel_bias, conv_c_w, conv_d_w, conv_d_b, a_log_d, dt_bias_d, d_skip, out_norm_d_g, w_up, w_out, ffn_norm_g, w1_dense, w3_dense, w2_dense, w_router, w1_moe, w3_moe, w2_moe):
    bsz, t, d = x.shape
    depth = w_in.shape[0]
    x2 = x.reshape(bsz * t, d)
    bias_tables = relbias_tables(rel_bias)
    for l in range(depth):
        x2 = _mixer_layer(x2, bsz, t, mix_norm_g[l], w_in[l], conv_a_w[l], a_log_a[l], dt_bias_a[l],
                          out_norm_a_g[l], ckv_norm_g[l], w_uk[l], q_norm_b_g[l], k_norm_b_g[l], w_uv[l],
                          rel_bias, bias_tables, conv_c_w[l], conv_d_w[l], conv_d_b[l], a_log_d[l],
                          dt_bias_d[l], d_skip[l], out_norm_d_g[l], w_up, w_out, l)
        j = l // 2
        if l % 2 == 0:
            h = rmsnorm(x2, ffn_norm_g[l])
            hid = glu_dense(h, w1_dense[j], w3_dense[j])
            x2 = matmul(hid, w2_dense[j].astype(BF16), out_dtype=F32, residual=x2,
                        tm=512, tn=512, name="ffn_down")
        else:
            x2 = moe_layer(x2, ffn_norm_g[l], w_router[j], w1_moe[j], w3_moe[j], w2_moe[j])
    return x2.reshape(bsz, t, d)
```

```python
import functools
import math

import jax
import jax.numpy as jnp
from jax import lax
from jax.experimental import pallas as pl
from jax.experimental.pallas import tpu as pltpu

F32 = jnp.float32
BF16 = jnp.bfloat16
I32 = jnp.int32
U32 = jnp.uint32
HIGHEST = lax.Precision.HIGHEST

D_MODEL = 4096
CHUNK = 64
N_BRANCHES = 4
BRANCH_WIDTH = D_MODEL // N_BRANCHES
DK_A = 128
H_A = BRANCH_WIDTH // DK_A
CONV_A = 4
DH_B = 128
H_B = BRANCH_WIDTH // DH_B
DC_B = 256
H_IDX = 8
D_IDX = 64
DSA_TOPK = 256
Q_BLOCK = 128
N_BUCKETS = 32
T5_MAX_DISTANCE = 128
CONV_C = 3
D_INNER = BRANCH_WIDTH
P_D = 64
H_D = D_INNER // P_D
N_GROUPS = 2
D_STATE = 128
CONV_D = 4
XBC_WIDTH = D_INNER + 2 * N_GROUPS * D_STATE
N_EXPERTS = 8
TOP_K = 2
EPS = 1e-6

MIX_SPLITS = (
    BRANCH_WIDTH, BRANCH_WIDTH, BRANCH_WIDTH, BRANCH_WIDTH, H_A, H_A,
    H_B * DH_B, DC_B, H_IDX * D_IDX, D_IDX, H_IDX,
    BRANCH_WIDTH, BRANCH_WIDTH, BRANCH_WIDTH,
    D_INNER, XBC_WIDTH, H_D,
)
MIX_COLS = sum(MIX_SPLITS)
_OFF = [0]
for _w in MIX_SPLITS:
    _OFF.append(_OFF[-1] + _w)
(O_AQ, O_AK, O_AV, O_AZ, O_AA, O_AB, O_BQ, O_BCKV, O_BQI, O_BKI, O_BWI,
 O_CB, O_CC, O_CH, O_DZ, O_DXBC, O_DDT, _O_END) = _OFF

P_AQKV = 0
P_AZ = 3072
P_BQ = 4096
P_C = 5120
P_DZ = 8192
P_DXBC = 9216
P_BQI = 10752
P_BCKV = 11264
P_COLS = 11520
S_AA = 0
S_AB = 8
S_BKI = 16
S_BWI = 80
S_DDT = 88
S_COLS = 128

LANES = 128
VMEM_LIMIT_MB = 56


def _cparams(sem, vmem_mb=VMEM_LIMIT_MB):
    return pltpu.CompilerParams(dimension_semantics=sem, vmem_limit_bytes=vmem_mb * 2 ** 20)


def _softplus(x):
    return jnp.maximum(x, 0.0) + jnp.log(1.0 + jnp.exp(-jnp.abs(x)))


def _sigmoid(x):
    return 1.0 / (1.0 + jnp.exp(-x))


def _silu(x):
    return x * _sigmoid(x)


def _dot(a, b, precision=None):
    return jnp.dot(a, b, preferred_element_type=F32, precision=precision)


def _dot_nt(a, b, precision=None):
    return lax.dot_general(a, b, (((1,), (1,)), ((), ())), preferred_element_type=F32, precision=precision)


def _dot_tn(a, b, precision=None):
    return lax.dot_general(a, b, (((0,), (0,)), ((), ())), preferred_element_type=F32, precision=precision)


def _rmsnorm_kernel(x_ref, g_ref, o_ref):
    x = x_ref[...].astype(F32)
    ms = jnp.mean(x * x, axis=-1, keepdims=True)
    o_ref[...] = (x * lax.rsqrt(ms + EPS) * g_ref[...]).astype(o_ref.dtype)


def rmsnorm(x, g, out_dtype=BF16, tm=512):
    m, d = x.shape
    tm = min(tm, m)
    return pl.pallas_call(
        _rmsnorm_kernel,
        grid=(m // tm,),
        in_specs=[pl.BlockSpec((tm, d), lambda i: (i, 0)), pl.BlockSpec((1, d), lambda i: (0, 0))],
        out_specs=pl.BlockSpec((tm, d), lambda i: (i, 0)),
        out_shape=jax.ShapeDtypeStruct((m, d), out_dtype),
        compiler_params=_cparams(("parallel",)),
        name="rmsnorm",
    )(x, g.reshape(1, d).astype(F32))


def _mm_kernel(*refs, nk, has_res):
    if has_res:
        a_ref, b_ref, r_ref = refs[:3]
        rest = refs[3:]
    else:
        a_ref, b_ref = refs[:2]
        r_ref = None
        rest = refs[2:]
    o_ref = rest[0]
    if nk == 1:
        acc = _dot(a_ref[...], b_ref[...].astype(BF16))
        if r_ref is not None:
            acc = acc + r_ref[...].astype(F32)
        o_ref[...] = acc.astype(o_ref.dtype)
        return
    acc_ref = rest[1]
    k = pl.program_id(2)

    @pl.when(k == 0)
    def _():
        acc_ref[...] = jnp.zeros_like(acc_ref)

    acc_ref[...] += _dot(a_ref[...], b_ref[...].astype(BF16))

    @pl.when(k == nk - 1)
    def _():
        acc = acc_ref[...]
        if r_ref is not None:
            acc = acc + r_ref[...].astype(F32)
        o_ref[...] = acc.astype(o_ref.dtype)


def matmul(a, b, *, out_dtype, residual=None, tm=1024, tn=512, tk=None, layer=None, name="matmul"):
    m, kdim = a.shape
    n = b.shape[-1]
    tm, tn = min(tm, m), min(tn, n)
    tk = kdim if tk is None else min(tk, kdim)
    nk = kdim // tk
    assert m % tm == 0 and n % tn == 0 and kdim % tk == 0, (a.shape, b.shape, tm, tn, tk)
    if layer is None:
        b_spec = pl.BlockSpec((tk, tn), lambda i, j, k: (k, j))
    else:
        b_spec = pl.BlockSpec((None, tk, tn), lambda i, j, k: (layer, k, j))
    in_specs = [pl.BlockSpec((tm, tk), lambda i, j, k: (i, k)), b_spec]
    args = [a, b]
    if residual is not None:
        in_specs.append(pl.BlockSpec((tm, tn), lambda i, j, k: (i, j)))
        args.append(residual)
    scratch = [pltpu.VMEM((tm, tn), F32)] if nk > 1 else []
    return pl.pallas_call(
        functools.partial(_mm_kernel, nk=nk, has_res=residual is not None),
        grid=(m // tm, n // tn, nk),
        in_specs=in_specs,
        out_specs=pl.BlockSpec((tm, tn), lambda i, j, k: (i, j)),
        out_shape=jax.ShapeDtypeStruct((m, n), out_dtype),
        scratch_shapes=scratch,
        compiler_params=_cparams(("parallel", "parallel", "arbitrary")),
        name=name,
    )(*args)


_PAD_ROWS = 8


def _causal_conv(stage_ref, x, w_ref, ksize):
    t = x.shape[0]
    stage_ref[0:_PAD_ROWS, :] = jnp.zeros((_PAD_ROWS, x.shape[1]), F32)
    stage_ref[_PAD_ROWS:_PAD_ROWS + t, :] = x
    acc = x * w_ref[ksize - 1:ksize, :]
    for j in range(ksize - 1):
        s = ksize - 1 - j
        acc = acc + stage_ref[_PAD_ROWS - s:_PAD_ROWS - s + t, :] * w_ref[j:j + 1, :]
    return acc


def _conv_qkv_kernel(x_ref, w_ref, o_ref, stage_ref, *, tc):
    cb = pl.program_id(1)
    y = _silu(_causal_conv(stage_ref, x_ref[...].astype(F32), w_ref, CONV_A))
    col0 = cb * tc
    is_q = col0 < BRANCH_WIDTH
    is_qk = col0 < 2 * BRANCH_WIDTH
    for g in range(tc // DK_A):
        ys = y[:, g * DK_A:(g + 1) * DK_A]
        inv = lax.rsqrt(jnp.sum(ys * ys, axis=-1, keepdims=True) + EPS)
        inv = jnp.where(is_q, inv * DK_A ** -0.5, inv)
        inv = jnp.where(is_qk, inv, jnp.ones_like(inv))
        o_ref[:, g * DK_A:(g + 1) * DK_A] = ys * inv


def _conv_xbc_kernel(x_ref, w_ref, b_ref, o_ref, stage_ref):
    y = _causal_conv(stage_ref, x_ref[...].astype(F32), w_ref, CONV_D) + b_ref[...]
    o_ref[...] = _silu(y)


def _conv_gated_kernel(bg_ref, cg_ref, h_ref, w_ref, o_ref, stage_ref):
    u = cg_ref[...].astype(F32) * h_ref[...].astype(F32)
    y = _causal_conv(stage_ref, u, w_ref, CONV_C)
    o_ref[...] = (bg_ref[...].astype(F32) * y).astype(o_ref.dtype)


def _conv_call(kernel, p3, col_offsets, width, extra, out_dtype, tc, name):
    bsz, t, _ = p3.shape
    tc = min(tc, width)
    in_specs = [pl.BlockSpec((None, t, tc), functools.partial(lambda b, c, o: (b, 0, o + c), o=off // tc))
                for off in col_offsets]
    args = [p3] * len(col_offsets)
    for e in extra:
        in_specs.append(pl.BlockSpec((e.shape[0], tc), lambda b, c: (0, c)))
        args.append(e)
    return pl.pallas_call(
        kernel,
        grid=(bsz, width // tc),
        in_specs=in_specs,
        out_specs=pl.BlockSpec((None, t, tc), lambda b, c: (b, 0, c)),
        out_shape=jax.ShapeDtypeStruct((bsz, t, width), out_dtype),
        scratch_shapes=[pltpu.VMEM((t + _PAD_ROWS, tc), F32)],
        compiler_params=_cparams(("parallel", "parallel")),
        name=name,
    )(*args)


def _tril_mask(n, strict=False):
    r = lax.broadcasted_iota(I32, (n, n), 0)
    c = lax.broadcasted_iota(I32, (n, n), 1)
    return (r > c) if strict else (r >= c)


def _chunk_cumsum(x):
    tril = _tril_mask(CHUNK).astype(F32)
    cs = _dot(tril, x, HIGHEST)
    padded = jnp.concatenate([cs, jnp.zeros((LANES - CHUNK, LANES), F32)], axis=0)
    return cs, padded.T[:, :CHUNK]


def _segment_decay(cs, cs_t, c):
    tril = _tril_mask(CHUNK)
    diff = cs[:, c:c + 1] - cs_t[c:c + 1, :]
    return jnp.where(tril, jnp.exp(jnp.where(tril, diff, 0.0)), 0.0)


_INV_BLOCK = 16
_GDN_GROUP = 2
_GDN_CHUNKS = 4
_CHUNK_SHIFT = CHUNK.bit_length() - 1


def _split_bf16(a):
    hi = a.astype(BF16)
    return hi, (a - hi.astype(F32)).astype(BF16)


def _dot3(a, b):
    a_hi, a_lo = a if isinstance(a, tuple) else _split_bf16(a)
    b_hi, b_lo = b if isinstance(b, tuple) else _split_bf16(b)
    return _dot(a_hi, b_hi) + _dot(a_hi, b_lo) + _dot(a_lo, b_hi)


def _unit_lower_inverses(lows):
    n_rows = lows[0].shape[0]
    r = lax.broadcasted_iota(I32, (n_rows, n_rows), 0)
    c = lax.broadcasted_iota(I32, (n_rows, n_rows), 1)
    shift = _INV_BLOCK.bit_length() - 1
    same = (r >> shift) == (c >> shift)
    eye = (r == c).astype(F32)
    each = lambda fn, *lists: [fn(*args) for args in zip(*lists)]
    lds = each(lambda low: jnp.where(same, low, 0.0), lows)
    los = each(lambda low: jnp.where(same, 0.0, low), lows)
    ps = each(lambda ld: eye - ld, lds)
    xss = each(_split_bf16, lds)
    xs = each(_dot3, xss, xss)
    for _ in range(2):
        xss = each(_split_bf16, xs)
        ps = each(lambda p, x: p + _dot3(p, x), ps, xss)
        xs = each(_dot3, xss, xss)
    ps = each(lambda p, x: p + _dot3(p, x), ps, xs)
    pss = each(_split_bf16, ps)
    ns = each(_dot3, pss, los)
    nss = each(_split_bf16, ns)
    n2s = each(_dot3, nss, nss)
    rrs = each(lambda n, n2: (eye - n) + _dot3(eye - n, n2), ns, n2s)
    return each(_dot3, rrs, pss)


def _gdn_kernel(q_ref, k_ref, v_ref, z_ref, sm_ref, alog_ref, dtb_ref, ng_ref, o_ref, state_ref):
    @pl.when(pl.program_id(1) == 0)
    def _():
        state_ref[...] = jnp.zeros_like(state_ref)

    ng = ng_ref[...]
    lane = lax.broadcasted_iota(I32, (CHUNK, LANES), 1)
    rows = _GDN_GROUP * CHUNK
    r = lax.broadcasted_iota(I32, (rows, rows), 0)
    c = lax.broadcasted_iota(I32, (rows, rows), 1)
    same_head = (r >> _CHUNK_SHIFT) == (c >> _CHUNK_SHIFT)
    tril = same_head & (r >= c)
    strict = same_head & (r > c)
    groups = [range(grp * _GDN_GROUP, (grp + 1) * _GDN_GROUP) for grp in range(H_A // _GDN_GROUP)]

    n_chunks = sm_ref.shape[0] // CHUNK
    e_lasts, problems = [], []
    for ci in range(n_chunks):
        ts = slice(ci * CHUNK, (ci + 1) * CHUNK)
        sm = sm_ref[ts, :]
        g = -jnp.exp(alog_ref[...]) * _softplus(sm + dtb_ref[...])
        g = jnp.where(lane < S_AA + H_A, g, 0.0)
        beta = _sigmoid(sm)
        gc, gc_t = _chunk_cumsum(g)
        g_last = gc[CHUNK - 1:CHUNK, :]
        e_gc = jnp.exp(gc)
        e_rem = jnp.exp(g_last - gc)
        e_lasts.append(jnp.exp(g_last))
        for heads in groups:
            stack = lambda ref: jnp.concatenate([ref[ts, h * DK_A:(h + 1) * DK_A] for h in heads], axis=0)
            col = lambda arr, off: jnp.concatenate([arr[:, off + h:off + h + 1] for h in heads], axis=0)
            q_st, k_st, v_st = stack(q_ref), stack(k_ref), stack(v_ref)
            beta_st, egc_st, erem_st = col(beta, S_AB), col(e_gc, S_AA), col(e_rem, S_AA)
            g_row = jnp.concatenate([gc_t[S_AA + h:S_AA + h + 1, :] for h in heads], axis=1)
            diff = col(gc, S_AA) - g_row
            decay = jnp.where(tril, jnp.exp(jnp.where(tril, diff, 0.0)), 0.0)
            kb_st = k_st * beta_st
            problems.append(dict(q=q_st, k=k_st, kb=kb_st, decay=decay, qd=q_st * egc_st, kd=k_st * erem_st,
                                 rhs=jnp.concatenate([v_st * beta_st, kb_st * egc_st], axis=1)))
    lows = [jnp.where(strict, _dot_nt(p["kb"], p["k"]) * p["decay"], 0.0) for p in problems]
    tinvs = _unit_lower_inverses(lows)
    uws = [_dot3(tinv, p["rhs"]) for tinv, p in zip(tinvs, problems)]
    intras = [_dot_nt(p["q"], p["k"]) * p["decay"] for p in problems]

    head_rows = lambda gi_j: slice(gi_j * CHUNK, (gi_j + 1) * CHUNK)
    for ci in range(n_chunks):
        ts = slice(ci * CHUNK, (ci + 1) * CHUNK)
        e_last = e_lasts[ci]
        probs = range(ci * len(groups), (ci + 1) * len(groups))
        states = [state_ref[h] for h in range(H_A)]
        wq_s = {}
        for pi, heads in zip(probs, groups):
            for j, h in enumerate(heads):
                rs = head_rows(j)
                lhs = jnp.concatenate([uws[pi][rs, DK_A:], problems[pi]["qd"][rs]], axis=0)
                wq_s[h] = _dot(lhs, states[h])
        v_new = {h: uws[pi][head_rows(j), :DK_A] - wq_s[h][:CHUNK]
                 for pi, heads in zip(probs, groups) for j, h in enumerate(heads)}
        outs = [jnp.concatenate([wq_s[h][CHUNK:] for h in heads], axis=0)
                + _dot(intras[pi], jnp.concatenate([v_new[h] for h in heads], axis=0))
                for pi, heads in zip(probs, groups)]
        for pi, heads in zip(probs, groups):
            for j, h in enumerate(heads):
                ca = S_AA + h
                state_ref[h] = states[h] * e_last[:, ca:ca + 1] + _dot_tn(problems[pi]["kd"][head_rows(j)], v_new[h])
        for out, heads in zip(outs, groups):
            ms = jnp.mean(out * out, axis=-1, keepdims=True)
            out = out * lax.rsqrt(ms + EPS) * ng
            for j, h in enumerate(heads):
                sl = slice(h * DK_A, (h + 1) * DK_A)
                o_ref[ts, sl] = (out[head_rows(j)] * _silu(z_ref[ts, sl].astype(F32))).astype(o_ref.dtype)


def gated_deltanet(qkv3, p3, small3, a_log, dt_bias, norm_g):
    bsz, t, _ = qkv3.shape
    tt = _GDN_CHUNKS * CHUNK if t % (_GDN_CHUNKS * CHUNK) == 0 else CHUNK

    def row(vals, off):
        return jnp.zeros((1, S_COLS), F32).at[0, off:off + vals.shape[0]].set(vals.astype(F32))

    w = BRANCH_WIDTH
    return pl.pallas_call(
        _gdn_kernel,
        grid=(bsz, t // tt),
        in_specs=[
            pl.BlockSpec((None, tt, w), lambda b, n: (b, n, 0)),
            pl.BlockSpec((None, tt, w), lambda b, n: (b, n, 1)),
            pl.BlockSpec((None, tt, w), lambda b, n: (b, n, 2)),
            pl.BlockSpec((None, tt, w), lambda b, n: (b, n, P_AZ // w)),
            pl.BlockSpec((None, tt, S_COLS), lambda b, n: (b, n, 0)),
            pl.BlockSpec((1, S_COLS), lambda b, n: (0, 0)),
            pl.BlockSpec((1, S_COLS), lambda b, n: (0, 0)),
            pl.BlockSpec((1, DK_A), lambda b, n: (0, 0)),
        ],
        out_specs=pl.BlockSpec((None, tt, w), lambda b, n: (b, n, 0)),
        out_shape=jax.ShapeDtypeStruct((bsz, t, w), BF16),
        scratch_shapes=[pltpu.VMEM((H_A, DK_A, DK_A), F32)],
        compiler_params=_cparams(("parallel", "arbitrary")),
        name="gated_deltanet",
    )(qkv3, qkv3, qkv3, p3, small3, row(a_log, S_AA), row(dt_bias, S_AA), norm_g.reshape(1, DK_A).astype(F32))


def _ssd_kernel(xbc_ref, z_ref, sm_ref, alog_ref, dtb_ref, dskip_ref, ng_ref, o_ref, state_ref):
    @pl.when(pl.program_id(1) == 0)
    def _():
        state_ref[...] = jnp.zeros_like(state_ref)

    sm = sm_ref[...]
    lane = lax.broadcasted_iota(I32, (CHUNK, LANES), 1)
    dt = _softplus(sm + dtb_ref[...])
    a = jnp.where((lane >= S_DDT) & (lane < S_DDT + H_D), dt * -jnp.exp(alog_ref[...]), 0.0)
    acs, acs_t = _chunk_cumsum(a)
    a_last = acs[CHUNK - 1:CHUNK, :]
    e_acs = jnp.exp(acs)
    e_rem = jnp.exp(a_last - acs)
    e_last = jnp.exp(a_last)
    heads_per_group = H_D // N_GROUPS
    group_of = lambda h: h // heads_per_group
    bms = [xbc_ref[:, D_INNER + grp * D_STATE:D_INNER + (grp + 1) * D_STATE] for grp in range(N_GROUPS)]
    c0 = D_INNER + N_GROUPS * D_STATE
    cms = [xbc_ref[:, c0 + grp * D_STATE:c0 + (grp + 1) * D_STATE] for grp in range(N_GROUPS)]
    cbs = [_dot_nt(cm, bm) for cm, bm in zip(cms, bms)]
    xcs = [xbc_ref[:, h * P_D:(h + 1) * P_D] * dt[:, S_DDT + h:S_DDT + h + 1] for h in range(H_D)]
    states = [state_ref[h] for h in range(H_D)]
    ys = []
    for h in range(H_D):
        c = S_DDT + h
        ys.append(_dot(cbs[group_of(h)] * _segment_decay(acs, acs_t, c), xcs[h])
                  + _dot_nt(cms[group_of(h)] * e_acs[:, c:c + 1], states[h]))
    for h in range(H_D):
        c = S_DDT + h
        state_ref[h] = states[h] * e_last[:, c:c + 1] + _dot_tn(xcs[h], bms[group_of(h)] * e_rem[:, c:c + 1])
    y = jnp.concatenate(ys, axis=1) + dskip_ref[...] * xbc_ref[:, 0:D_INNER]
    y = y * _silu(z_ref[...].astype(F32))
    gw = D_INNER // N_GROUPS
    for grp in range(N_GROUPS):
        yg = y[:, grp * gw:(grp + 1) * gw]
        ms = jnp.mean(yg * yg, axis=-1, keepdims=True)
        o_ref[:, grp * gw:(grp + 1) * gw] = (yg * lax.rsqrt(ms + EPS) * ng_ref[:, grp * gw:(grp + 1) * gw]).astype(o_ref.dtype)


def mamba2_ssd(xbc3, p3, small3, a_log, dt_bias, d_skip, norm_g):
    bsz, t, _ = xbc3.shape
    nchunks = t // CHUNK

    def row(vals, off):
        return jnp.zeros((1, S_COLS), F32).at[0, off:off + vals.shape[0]].set(vals.astype(F32))

    return pl.pallas_call(
        _ssd_kernel,
        grid=(bsz, nchunks),
        in_specs=[
            pl.BlockSpec((None, CHUNK, XBC_WIDTH), lambda b, n: (b, n, 0)),
            pl.BlockSpec((None, CHUNK, D_INNER), lambda b, n: (b, n, P_DZ // D_INNER)),
            pl.BlockSpec((None, CHUNK, S_COLS), lambda b, n: (b, n, 0)),
            pl.BlockSpec((1, S_COLS), lambda b, n: (0, 0)),
            pl.BlockSpec((1, S_COLS), lambda b, n: (0, 0)),
            pl.BlockSpec((1, D_INNER), lambda b, n: (0, 0)),
            pl.BlockSpec((1, D_INNER), lambda b, n: (0, 0)),
        ],
        out_specs=pl.BlockSpec((None, CHUNK, D_INNER), lambda b, n: (b, n, 0)),
        out_shape=jax.ShapeDtypeStruct((bsz, t, D_INNER), BF16),
        scratch_shapes=[pltpu.VMEM((H_D, P_D, D_STATE), F32)],
        compiler_params=_cparams(("parallel", "arbitrary")),
        name="mamba2_ssd",
    )(xbc3, p3, small3, row(a_log, S_DDT), row(dt_bias, S_DDT),
      jnp.repeat(d_skip.astype(F32), P_D).reshape(1, D_INNER), norm_g.reshape(1, D_INNER).astype(F32))


def _dsa_prep_kernel(q_ref, ckv_ref, sm_ref, wuk_ref, cg_ref, qg_ref, kg_ref, qh_ref, kk_ref, ct_ref, ki_ref):
    ckv = ckv_ref[...].astype(F32)
    c = ckv * lax.rsqrt(jnp.mean(ckv * ckv, axis=-1, keepdims=True) + EPS) * cg_ref[...]
    cb = c.astype(BF16)
    ct_ref[...] = c.T.astype(BF16)
    kk = _dot(cb, wuk_ref[...])
    kk_ref[...] = (kk * lax.rsqrt(jnp.mean(kk * kk, axis=-1, keepdims=True) + EPS) * kg_ref[...]).astype(BF16)
    for h in range(H_B):
        sl = slice(h * DH_B, (h + 1) * DH_B)
        qh = q_ref[:, sl].astype(F32)
        qh = qh * lax.rsqrt(jnp.mean(qh * qh, axis=-1, keepdims=True) + EPS) * qg_ref[...]
        qh_ref[:, sl] = (qh * DH_B ** -0.5).astype(BF16)
    ki_ref[...] = sm_ref[:, S_BKI:S_BKI + D_IDX].astype(BF16)


def dsa_prep(p3, small3, w_uk, ckv_g, q_g, k_g, tt=512):
    bsz, t, _ = p3.shape
    tt = min(tt, t)
    w = H_B * DH_B
    return pl.pallas_call(
        _dsa_prep_kernel,
        grid=(bsz, t // tt),
        in_specs=[
            pl.BlockSpec((None, tt, w), lambda b, i: (b, i, P_BQ // w)),
            pl.BlockSpec((None, tt, DC_B), lambda b, i: (b, i, P_BCKV // DC_B)),
            pl.BlockSpec((None, tt, S_COLS), lambda b, i: (b, i, 0)),
            pl.BlockSpec((DC_B, DH_B), lambda b, i: (0, 0)),
            pl.BlockSpec((1, DC_B), lambda b, i: (0, 0)),
            pl.BlockSpec((1, DH_B), lambda b, i: (0, 0)),
            pl.BlockSpec((1, DH_B), lambda b, i: (0, 0)),
        ],
        out_specs=[
            pl.BlockSpec((None, tt, w), lambda b, i: (b, i, 0)),
            pl.BlockSpec((None, tt, DH_B), lambda b, i: (b, i, 0)),
            pl.BlockSpec((None, DC_B, tt), lambda b, i: (b, 0, i)),
            pl.BlockSpec((None, tt, D_IDX), lambda b, i: (b, i, 0)),
        ],
        out_shape=[
            jax.ShapeDtypeStruct((bsz, t, w), BF16),
            jax.ShapeDtypeStruct((bsz, t, DH_B), BF16),
            jax.ShapeDtypeStruct((bsz, DC_B, t), BF16),
            jax.ShapeDtypeStruct((bsz, t, D_IDX), BF16),
        ],
        compiler_params=_cparams(("parallel", "parallel")),
        name="dsa_prep",
    )(p3, p3, small3, w_uk.astype(BF16), ckv_g.reshape(1, DC_B).astype(F32),
      q_g.reshape(1, DH_B).astype(F32), k_g.reshape(1, DH_B).astype(F32))


_T5_HALF = N_BUCKETS // 2
_T5_EXACT = _T5_HALF // 2
_T5_FAR = _T5_HALF - 1


def _relbias_kernel(rb_ref, prev_ref, diag_ref):
    h = pl.program_id(0)
    kl = lax.broadcasted_iota(I32, (Q_BLOCK, Q_BLOCK), 0)
    ql = lax.broadcasted_iota(I32, (Q_BLOCK, Q_BLOCK), 1)
    far = rb_ref[_T5_FAR, h]
    for ref, shift in ((prev_ref, -Q_BLOCK), (diag_ref, 0)):
        rel = kl - ql + shift
        n = jnp.abs(rel)
        n2 = n * n
        steps = jnp.zeros_like(n)
        for j in range(1, _T5_HALF - _T5_EXACT):
            steps = steps + (n2 >= (_T5_EXACT * _T5_EXACT) * 2 ** j).astype(I32)
        large = jnp.minimum(_T5_EXACT + steps, _T5_HALF - 1)
        bucket = jnp.where(rel > 0, _T5_HALF, 0) + jnp.where(n < _T5_EXACT, n, large)
        acc = jnp.zeros((Q_BLOCK, Q_BLOCK), F32)
        for b in range(N_BUCKETS):
            acc = jnp.where(bucket == b, rb_ref[b, h], acc)
        ref[...] = acc - far


def relbias_tables(rel_bias):
    shp = jax.ShapeDtypeStruct((H_B, Q_BLOCK, Q_BLOCK), F32)
    spec = pl.BlockSpec((None, Q_BLOCK, Q_BLOCK), lambda h: (h, 0, 0))
    return pl.pallas_call(
        _relbias_kernel,
        grid=(H_B,),
        in_specs=[pl.BlockSpec(memory_space=pltpu.SMEM)],
        out_specs=[spec, spec],
        out_shape=[shp, shp],
        compiler_params=_cparams(("arbitrary",)),
        name="relbias_tables",
    )(rel_bias.astype(F32))


_INT_MIN = -2 ** 31
_SCORE_ROWS = 256
_SWEEP_ROWS = 64
_DSA_WIDTH_STEP = 512


def _sweep_rows(n_rows, init, step):
    def body(c, acc):
        return step(pl.multiple_of(c * _SWEEP_ROWS, _SWEEP_ROWS), acc)
    n_steps = n_rows // _SWEEP_ROWS
    return lax.fori_loop(0, n_steps, body, init, unroll=min(8, n_steps))


def _dsa_kernel(qh_ref, qi_ref, sm_ref, kk_ref, ct_ref, ki_ref, dprev_ref, ddiag_ref, wuvt_ref,
                o_ref, key_ref, selb_ref, lg_ref, p_ref, *, widths, topk):
    i = pl.program_id(1)
    need = (i + 1) * Q_BLOCK
    wi_t = sm_ref[...].T[S_BWI:S_BWI + H_IDX, :] * (H_IDX ** -0.5 * D_IDX ** -0.5)
    qi_all = jnp.concatenate([qi_ref[:, h * D_IDX:(h + 1) * D_IDX] for h in range(H_IDX)], axis=0)
    q_chunk = (i * Q_BLOCK + lax.broadcasted_iota(I32, (1, Q_BLOCK), 1)) >> _CHUNK_SHIFT
    row_iota = lax.broadcasted_iota(I32, (_SWEEP_ROWS, Q_BLOCK), 0)
    block = lambda ref, r0: ref[pl.ds(r0, _SWEEP_ROWS), :]

    def body(nc):
        for r0 in range(0, nc, min(_SCORE_ROWS, nc)):
            rows = slice(r0, r0 + min(_SCORE_ROWS, nc))
            s_all = _dot_nt(ki_ref[rows, :], qi_all)
            sc = None
            for h in range(H_IDX):
                term = wi_t[h:h + 1, :] * jnp.maximum(s_all[:, h * Q_BLOCK:(h + 1) * Q_BLOCK], 0.0)
                sc = term if sc is None else sc + term
            bits = lax.bitcast_convert_type(sc + 0.0, I32)
            key = jnp.where(bits < 0, bits ^ 0x7FFFFFFF, bits)
            kpos = r0 + lax.broadcasted_iota(I32, key.shape, 0)
            key_ref[rows, :] = jnp.where((kpos >> _CHUNK_SHIFT) <= q_chunk, key, _INT_MIN)

        def count(pred):
            acc = _sweep_rows(nc, jnp.zeros((_SWEEP_ROWS, Q_BLOCK), I32),
                              lambda r0, acc: acc + pred(r0, block(key_ref, r0)).astype(I32))
            return jnp.sum(acc, axis=0, keepdims=True)

        tau = jnp.where(count(lambda r0, k: k >= 0) >= topk, 0, _INT_MIN).astype(I32)

        def vstep(it, tau):
            cand = tau | (jnp.int32(1) << (30 - it))
            return jnp.where(count(lambda r0, k: k >= cand) >= topk, cand, tau)

        tau = lax.fori_loop(0, 31, vstep, tau)
        n_gt = count(lambda r0, k: k > tau)
        n_eq = count(lambda r0, k: k == tau)
        tied = (n_gt + n_eq > topk) & (tau > _INT_MIN)
        nbits = max(1, (nc - 1).bit_length())

        def last_tied_index():
            room = topk - n_gt

            def istep(it, last):
                cand = last | (jnp.int32(1) << (nbits - 1 - it))
                below = count(lambda r0, k: (k == tau) & (r0 + row_iota < cand))
                return jnp.where(below < room, cand, last)

            return lax.fori_loop(0, nbits, istep, jnp.zeros((1, Q_BLOCK), I32))

        last = lax.cond(jnp.max(tied.astype(I32)) > 0, last_tied_index,
                        lambda: jnp.full((1, Q_BLOCK), nc, I32))

        def write_sel(r0, carry):
            k = block(key_ref, r0)
            sel = (k > _INT_MIN) & ((k > tau) | ((k == tau) & (r0 + row_iota <= last)))
            selb_ref[pl.ds(r0, _SWEEP_ROWS), :] = jnp.where(sel, 0.0, -jnp.inf)
            return carry

        _sweep_rows(nc, 0, write_sel)

        kk = kk_ref[0:nc, :]
        ct = ct_ref[:, 0:nc]
        prev_row = pl.multiple_of(jnp.maximum(i - 1, 0) * Q_BLOCK, Q_BLOCK)
        diag_row = pl.multiple_of(i * Q_BLOCK, Q_BLOCK)
        has_prev = (i > 0).astype(F32)
        for h0 in range(0, H_B, 2):
            q_pair = jnp.concatenate([qh_ref[:, h * DH_B:(h + 1) * DH_B] for h in (h0, h0 + 1)], axis=0)
            lg_pair = _dot_nt(kk, q_pair)
            for j, h in enumerate((h0, h0 + 1)):
                lg_ref[h, 0:nc, :] = lg_pair[:, j * Q_BLOCK:(j + 1) * Q_BLOCK] + selb_ref[0:nc, :]
                lg_ref[h, pl.ds(prev_row, Q_BLOCK), :] += dprev_ref[h] * has_prev
                lg_ref[h, pl.ds(diag_row, Q_BLOCK), :] += ddiag_ref[h]
        blk = min(_SCORE_ROWS, nc)
        outs = []
        for h in range(H_B):
            slot = h % 2
            m_acc = None
            for r0 in range(0, nc, blk):
                x = lg_ref[h, r0:r0 + blk, :]
                m_acc = x if m_acc is None else jnp.maximum(m_acc, x)
            m = jnp.max(m_acc, axis=0, keepdims=True)
            p_acc = None
            for r0 in range(0, nc, blk):
                p = jnp.exp(lg_ref[h, r0:r0 + blk, :] - m)
                p_ref[slot, r0:r0 + blk, :] = p.astype(BF16)
                p_acc = p if p_acc is None else p_acc + p
            denom = jnp.sum(p_acc, axis=0, keepdims=True)
            o_lat = _dot(ct, p_ref[slot, 0:nc, :]) * (1.0 / denom)
            outs.append(_dot(wuvt_ref[h], o_lat.astype(BF16)).T)
        o_ref[...] = jnp.concatenate(outs, axis=1).astype(o_ref.dtype)

    lo = 0
    for nc in widths:
        @pl.when((need > lo) & (need <= nc))
        def _(nc=nc):
            body(nc)
        lo = nc


def dsa_attention(qh3, qi_p3, small3, kk3, ct3, ki3, dprev, ddiag, wuv_t):
    bsz, t, w = qh3.shape
    topk = min(DSA_TOPK, t // 4)
    step = min(_DSA_WIDTH_STEP, t)
    widths = tuple(range(step, t + 1, step))
    wq = H_IDX * D_IDX
    full = lambda shape: pl.BlockSpec(shape, lambda b, i: (0,) * len(shape))
    return pl.pallas_call(
        functools.partial(_dsa_kernel, widths=widths, topk=topk),
        grid=(bsz, t // Q_BLOCK),
        in_specs=[
            pl.BlockSpec((None, Q_BLOCK, w), lambda b, i: (b, i, 0)),
            pl.BlockSpec((None, Q_BLOCK, wq), lambda b, i: (b, i, P_BQI // wq)),
            pl.BlockSpec((None, Q_BLOCK, S_COLS), lambda b, i: (b, i, 0)),
            pl.BlockSpec((None, t, DH_B), lambda b, i: (b, 0, 0)),
            pl.BlockSpec((None, DC_B, t), lambda b, i: (b, 0, 0)),
            pl.BlockSpec((None, t, D_IDX), lambda b, i: (b, 0, 0)),
            full((H_B, Q_BLOCK, Q_BLOCK)),
            full((H_B, Q_BLOCK, Q_BLOCK)),
            full((H_B, DH_B, DC_B)),
        ],
        out_specs=pl.BlockSpec((None, Q_BLOCK, w), lambda b, i: (b, i, 0)),
        out_shape=jax.ShapeDtypeStruct((bsz, t, w), BF16),
        scratch_shapes=[pltpu.VMEM((t, Q_BLOCK), I32), pltpu.VMEM((t, Q_BLOCK), F32),
                        pltpu.VMEM((H_B, t, Q_BLOCK), F32), pltpu.VMEM((2, t, Q_BLOCK), BF16)],
        compiler_params=_cparams(("parallel", "arbitrary")),
        name="dsa_attention",
    )(qh3, qi_p3, small3, kk3, ct3, ki3, dprev, ddiag, wuv_t)


def _merge_kernel(xn_ref, o0_ref, o1_ref, o2_ref, o3_ref, g0_ref, g1_ref, g2_ref, g3_ref,
                  u0_ref, u1_ref, u2_ref, u3_ref, out_ref):
    xn = xn_ref[...]
    acc = None
    for o_ref, g_ref, u_ref in ((o0_ref, g0_ref, u0_ref), (o1_ref, g1_ref, u1_ref),
                                (o2_ref, g2_ref, u2_ref), (o3_ref, g3_ref, u3_ref)):
        term = _sigmoid(_dot(xn, g_ref[...])) * _dot(o_ref[...], u_ref[...].astype(BF16))
        acc = term if acc is None else acc + term
    out_ref[...] = acc.astype(out_ref.dtype)


def merge_branches(xn, outs, w_gate, w_up, layer, tm=512, tn=256):
    m, d = xn.shape
    tm, tn = min(tm, m), min(tn, d)
    nj = d // tn
    in_specs = [pl.BlockSpec((tm, d), lambda j, i: (i, 0))]
    in_specs += [pl.BlockSpec((tm, BRANCH_WIDTH), lambda j, i: (i, 0)) for _ in range(N_BRANCHES)]
    in_specs += [pl.BlockSpec((d, tn), functools.partial(lambda j, i, br: (0, br * nj + j), br=br))
                 for br in range(N_BRANCHES)]
    in_specs += [pl.BlockSpec((None, None, BRANCH_WIDTH, tn),
                              functools.partial(lambda j, i, br: (layer, br, 0, j), br=br))
                 for br in range(N_BRANCHES)]
    return pl.pallas_call(
        _merge_kernel,
        grid=(nj, m // tm),
        in_specs=in_specs,
        out_specs=pl.BlockSpec((tm, tn), lambda j, i: (i, j)),
        out_shape=jax.ShapeDtypeStruct((m, d), BF16),
        compiler_params=_cparams(("parallel", "parallel")),
        name="merge_branches",
    )(xn, *outs, *([w_gate] * N_BRANCHES), *([w_up] * N_BRANCHES))


def _glu_kernel(x_ref, w1_ref, w3_ref, o_ref):
    x = x_ref[...]
    o_ref[...] = (_silu(_dot(x, w1_ref[...].astype(BF16))) * _dot(x, w3_ref[...].astype(BF16))).astype(o_ref.dtype)


def glu_dense(x, w1, w3, tm=2048, tf=256):
    m, d = x.shape
    f = w1.shape[1]
    tm, tf = min(tm, m), min(tf, f)
    assert f % tf == 0
    return pl.pallas_call(
        _glu_kernel,
        grid=(f // tf, m // tm),
        in_specs=[pl.BlockSpec((tm, d), lambda j, i: (i, 0)),
                  pl.BlockSpec((d, tf), lambda j, i: (0, j)),
                  pl.BlockSpec((d, tf), lambda j, i: (0, j))],
        out_specs=pl.BlockSpec((tm, tf), lambda j, i: (i, j)),
        out_shape=jax.ShapeDtypeStruct((m, f), BF16),
        compiler_params=_cparams(("parallel", "parallel")),
        name="glu_dense",
    )(x, w1, w3)


MOE_TM = 512
MOE_TT = 256
MOE_UP_TN = 512
MOE_DOWN_TN = 1024
SLOT_POS_A, SLOT_POS_B, SLOT_W_A, SLOT_W_B = 0, 1, 2, 3
_ROW_COPY_UNROLL = 8


def _pack_bf16_pair(lo, hi):
    bits = lambda v: lax.bitcast_convert_type(v.astype(BF16).astype(F32), U32)
    return bits(hi) | (bits(lo) >> 16)


def _unpack_bf16_pair(u):
    return (lax.bitcast_convert_type(u << 16, F32),
            lax.bitcast_convert_type(u & jnp.uint32(0xFFFF0000), F32))


def _router_kernel(x_ref, g_ref, wr_ref, h_ref, gate_ref, sel_ref):
    x = x_ref[...]
    h = x * lax.rsqrt(jnp.mean(x * x, axis=-1, keepdims=True) + EPS) * g_ref[...]
    half = h.shape[1] // 2
    h_ref[...] = _pack_bf16_pair(h[:, :half], h[:, half:])
    logits = _dot3(h, wr_ref[...])
    lane = lax.broadcasted_iota(I32, logits.shape, 1)
    logits = jnp.where(lane < N_EXPERTS, logits, -jnp.inf)
    m1 = jnp.max(logits, axis=-1, keepdims=True)
    i1 = jnp.min(jnp.where(logits == m1, lane, LANES), axis=-1, keepdims=True)
    rest = jnp.where(lane == i1, -jnp.inf, logits)
    m2 = jnp.max(rest, axis=-1, keepdims=True)
    i2 = jnp.min(jnp.where(rest == m2, lane, LANES), axis=-1, keepdims=True)
    e2 = jnp.exp(m2 - m1)
    inv = 1.0 / (1.0 + e2)
    gate_ref[...] = jnp.where(lane == i1, inv, 0.0) + jnp.where(lane == i2, e2 * inv, 0.0)
    sel_ref[...] = jnp.where((lane == i1) | (lane == i2), 1.0, 0.0).astype(sel_ref.dtype)


def moe_router(x, g, w_router, tm=512):
    m, d = x.shape
    tm = min(tm, m)
    wr = jnp.zeros((d, LANES), F32).at[:, :N_EXPERTS].set(w_router.astype(F32))
    row = lambda w: pl.BlockSpec((tm, w), lambda i: (i, 0))
    return pl.pallas_call(
        _router_kernel,
        grid=(m // tm,),
        in_specs=[row(d), pl.BlockSpec((1, d), lambda i: (0, 0)), pl.BlockSpec((d, LANES), lambda i: (0, 0))],
        out_specs=[row(d // 2), row(LANES), row(LANES)],
        out_shape=[jax.ShapeDtypeStruct((m, d // 2), U32), jax.ShapeDtypeStruct((m, LANES), F32),
                   jax.ShapeDtypeStruct((m, LANES), BF16)],
        compiler_params=_cparams(("parallel",)),
        name="moe_router",
    )(x, g.reshape(1, d).astype(F32), wr)


def _rank_kernel(sel_ref, rank_ref, cnt_ref, carry_ref):
    @pl.when(pl.program_id(0) == 0)
    def _():
        carry_ref[...] = jnp.zeros_like(carry_ref)

    sel = sel_ref[...]
    n = sel.shape[0]
    earlier = _tril_mask(n, strict=True).astype(BF16)
    rank_ref[...] = _dot(earlier, sel) + carry_ref[...]
    carry_ref[...] += jnp.sum(sel.astype(F32), axis=0, keepdims=True)
    cnt_ref[...] = carry_ref[...]


def moe_rank(sel, tr=512):
    m = sel.shape[0]
    tr = min(tr, m)
    return pl.pallas_call(
        _rank_kernel,
        grid=(m // tr,),
        in_specs=[pl.BlockSpec((tr, LANES), lambda i: (i, 0))],
        out_specs=[pl.BlockSpec((tr, LANES), lambda i: (i, 0)), pl.BlockSpec((1, LANES), lambda i: (0, 0))],
        out_shape=[jax.ShapeDtypeStruct((m, LANES), F32), jax.ShapeDtypeStruct((1, LANES), F32)],
        scratch_shapes=[pltpu.VMEM((1, LANES), F32)],
        compiler_params=_cparams(("arbitrary",)),
        name="moe_rank",
    )(sel)


def _slots_kernel(rank_ref, sel_ref, gate_ref, start_ref, out_ref):
    sel = sel_ref[...].astype(F32) > 0.0
    lane = lax.broadcasted_iota(I32, sel.shape, 1)
    dest = start_ref[...] + rank_ref[...]
    first = jnp.min(jnp.where(sel, lane, LANES), axis=-1, keepdims=True)
    second = jnp.max(jnp.where(sel, lane, -1), axis=-1, keepdims=True)
    pick = lambda arr, idx: jnp.sum(jnp.where(lane == idx, arr, 0.0), axis=-1, keepdims=True)
    gates = gate_ref[...]
    out = jnp.where(lane == SLOT_POS_A, pick(dest, first), 0.0)
    out = jnp.where(lane == SLOT_POS_B, pick(dest, second), out)
    out = jnp.where(lane == SLOT_W_A, pick(gates, first), out)
    out_ref[...] = jnp.where(lane == SLOT_W_B, pick(gates, second), out)


def moe_slots(rank, sel, gates, start_row, tr=512):
    m = sel.shape[0]
    tr = min(tr, m)
    row = pl.BlockSpec((tr, LANES), lambda i: (i, 0))
    return pl.pallas_call(
        _slots_kernel,
        grid=(m // tr,),
        in_specs=[row, row, row, pl.BlockSpec((1, LANES), lambda i: (0, 0))],
        out_specs=row,
        out_shape=jax.ShapeDtypeStruct((m, LANES), F32),
        compiler_params=_cparams(("parallel",)),
        name="moe_slots",
    )(rank, sel, gates, start_row)


def _row_copy(src_ref, src_row, dst_ref, dst_row, sem):
    return pltpu.make_async_copy(src_ref.at[pl.ds(src_row, 1), :], dst_ref.at[pl.ds(dst_row, 1), :], sem)


def _dispatch_kernel(pos_ref, tail_ref, h_ref, xg_ref, zero_ref, sem, *, tm):
    tt = h_ref.shape[0]

    @pl.when(pl.program_id(0) == 0)
    def _():
        zero_ref[...] = jnp.zeros_like(zero_ref)
        fill = lambda e: pltpu.make_async_copy(zero_ref, xg_ref.at[pl.ds(pl.multiple_of(tail_ref[e], tm), tm), :], sem)
        for e in range(tail_ref.shape[0]):
            @pl.when(tail_ref[e] >= 0)
            def _(e=e):
                fill(e).start()
        for e in range(tail_ref.shape[0]):
            @pl.when(tail_ref[e] >= 0)
            def _(e=e):
                fill(e).wait()

    def start(r, carry):
        _row_copy(h_ref, r, xg_ref, pos_ref[0, r], sem).start(priority=0)
        _row_copy(h_ref, r, xg_ref, pos_ref[0, tt + r], sem).start(priority=1)
        return carry

    def wait(r, carry):
        _row_copy(h_ref, r, xg_ref, pos_ref[0, r], sem).wait()
        _row_copy(h_ref, r, xg_ref, pos_ref[0, tt + r], sem).wait()
        return carry

    lax.fori_loop(0, tt, start, 0, unroll=_ROW_COPY_UNROLL)
    lax.fori_loop(0, tt, wait, 0, unroll=_ROW_COPY_UNROLL)


def moe_dispatch(h, pos_tiles, tails, rows, tm):
    m, d = h.shape
    tt = pos_tiles.shape[2] // 2
    return pl.pallas_call(
        functools.partial(_dispatch_kernel, tm=tm),
        grid=(m // tt,),
        in_specs=[pl.BlockSpec((None, 1, 2 * tt), lambda i: (i, 0, 0), memory_space=pltpu.SMEM),
                  pl.BlockSpec(memory_space=pltpu.SMEM),
                  pl.BlockSpec((tt, d), lambda i: (i, 0))],
        out_specs=pl.BlockSpec(memory_space=pl.ANY),
        out_shape=jax.ShapeDtypeStruct((rows, d), h.dtype),
        scratch_shapes=[pltpu.VMEM((tm, d), h.dtype), pltpu.SemaphoreType.DMA(())],
        compiler_params=_cparams(("arbitrary",)),
        name="moe_dispatch",
    )(pos_tiles, tails, h)


def _glu_grouped_kernel(te_ref, nv_ref, x_ref, w1_ref, w3_ref, o_ref):
    valid = pl.program_id(1) < nv_ref[0]

    @pl.when(valid)
    def _():
        lo, hi = (v.astype(BF16) for v in _unpack_bf16_pair(x_ref[...]))
        half = lo.shape[1]
        up = lambda w_ref: (_dot(lo, w_ref[0:half, :].astype(BF16)) + _dot(hi, w_ref[half:2 * half, :].astype(BF16)))
        o_ref[...] = (_silu(up(w1_ref)) * up(w3_ref)).astype(o_ref.dtype)

    @pl.when(jnp.logical_not(valid))
    def _():
        o_ref[...] = jnp.zeros_like(o_ref)


def _down_grouped_kernel(te_ref, nv_ref, h_ref, w2_ref, o_ref):
    valid = pl.program_id(1) < nv_ref[0]

    @pl.when(valid)
    def _():
        y = _dot(h_ref[...], w2_ref[...].astype(BF16))
        half = y.shape[1] // 2
        o_ref[...] = _pack_bf16_pair(y[:, :half], y[:, half:])

    @pl.when(jnp.logical_not(valid))
    def _():
        o_ref[...] = jnp.zeros_like(o_ref)


def _grouped_call(kernel, x, weights, tile_expert, n_valid, n_col_tiles, tn, out_tn, out_dtype, tm, name):
    rows, d = x.shape
    tile = lambda i, nv: jnp.minimum(i, nv[0] - 1)
    in_specs = [pl.BlockSpec((tm, d), lambda j, i, te, nv: (tile(i, nv), 0))]
    in_specs += [pl.BlockSpec((None, w.shape[1], tn), lambda j, i, te, nv: (te[tile(i, nv)], 0, j)) for w in weights]
    grid_spec = pltpu.PrefetchScalarGridSpec(
        num_scalar_prefetch=2,
        grid=(n_col_tiles, rows // tm),
        in_specs=in_specs,
        out_specs=pl.BlockSpec((tm, out_tn), lambda j, i, te, nv: (i, j)),
    )
    return pl.pallas_call(
        kernel,
        grid_spec=grid_spec,
        out_shape=jax.ShapeDtypeStruct((rows, n_col_tiles * out_tn), out_dtype),
        compiler_params=_cparams(("arbitrary", "arbitrary")),
        name=name,
    )(tile_expert, n_valid, x, *weights)


def _combine_kernel(pos_ref, x_ref, slot_ref, yg_ref, o_ref, buf_ref, sem):
    tt = x_ref.shape[0]

    def start(r, carry):
        _row_copy(yg_ref, pos_ref[0, r], buf_ref.at[0], r, sem).start(priority=0)
        _row_copy(yg_ref, pos_ref[0, tt + r], buf_ref.at[1], r, sem).start(priority=1)
        return carry

    def wait(r, carry):
        _row_copy(yg_ref, pos_ref[0, r], buf_ref.at[0], r, sem).wait()
        _row_copy(yg_ref, pos_ref[0, tt + r], buf_ref.at[1], r, sem).wait()
        return carry

    lax.fori_loop(0, tt, start, 0, unroll=_ROW_COPY_UNROLL)
    lax.fori_loop(0, tt, wait, 0, unroll=_ROW_COPY_UNROLL)
    slots = slot_ref[...]
    w_a = slots[:, SLOT_W_A:SLOT_W_A + 1]
    w_b = slots[:, SLOT_W_B:SLOT_W_B + 1]
    half = MOE_DOWN_TN // 2
    for j in range(x_ref.shape[1] // MOE_DOWN_TN):
        lo_a, hi_a = _unpack_bf16_pair(buf_ref[0, :, j * half:(j + 1) * half])
        lo_b, hi_b = _unpack_bf16_pair(buf_ref[1, :, j * half:(j + 1) * half])
        c_lo = slice(j * MOE_DOWN_TN, j * MOE_DOWN_TN + half)
        c_hi = slice(j * MOE_DOWN_TN + half, (j + 1) * MOE_DOWN_TN)
        o_ref[:, c_lo] = x_ref[:, c_lo] + w_a * lo_a + w_b * lo_b
        o_ref[:, c_hi] = x_ref[:, c_hi] + w_a * hi_a + w_b * hi_b


def moe_combine(x, slots, pos_tiles, yg):
    m, d = x.shape
    tt = pos_tiles.shape[2] // 2
    return pl.pallas_call(
        _combine_kernel,
        grid=(m // tt,),
        in_specs=[pl.BlockSpec((None, 1, 2 * tt), lambda i: (i, 0, 0), memory_space=pltpu.SMEM),
                  pl.BlockSpec((tt, d), lambda i: (i, 0)),
                  pl.BlockSpec((tt, LANES), lambda i: (i, 0)),
                  pl.BlockSpec(memory_space=pl.ANY)],
        out_specs=pl.BlockSpec((tt, d), lambda i: (i, 0)),
        out_shape=jax.ShapeDtypeStruct((m, d), F32),
        scratch_shapes=[pltpu.VMEM((2, tt, yg.shape[1]), yg.dtype), pltpu.SemaphoreType.DMA(())],
        compiler_params=_cparams(("arbitrary",)),
        name="moe_combine",
    )(pos_tiles, x, slots, yg)


def _glu_all_experts_kernel(x_ref, w1_ref, w3_ref, gate_ref, o_ref, *, tiles_per_expert):
    x = x_ref[...]
    h = _silu(_dot(x, w1_ref[...].astype(BF16))) * _dot(x, w3_ref[...].astype(BF16))
    e = pl.program_id(0) // tiles_per_expert
    gates = gate_ref[...]
    lane = lax.broadcasted_iota(I32, gates.shape, 1)
    h = h * jnp.sum(jnp.where(lane == e, gates, 0.0), axis=-1, keepdims=True)
    o_ref[...] = h.astype(o_ref.dtype)


def moe_layer_dense(x2, norm_g, w_router, w1, w3, w2):
    m, d = x2.shape
    ne, _, f = w1.shape
    _, gates, _ = moe_router(x2, norm_g, w_router)
    h = rmsnorm(x2, norm_g)
    tm, tf = min(MOE_TM, m), min(MOE_UP_TN, f)
    tpe = f // tf
    hid = pl.pallas_call(
        functools.partial(_glu_all_experts_kernel, tiles_per_expert=tpe),
        grid=(ne * tpe, m // tm),
        in_specs=[pl.BlockSpec((tm, d), lambda j, i: (i, 0)),
                  pl.BlockSpec((None, d, tf), lambda j, i: (j // tpe, 0, j % tpe)),
                  pl.BlockSpec((None, d, tf), lambda j, i: (j // tpe, 0, j % tpe)),
                  pl.BlockSpec((tm, LANES), lambda j, i: (i, 0))],
        out_specs=pl.BlockSpec((tm, tf), lambda j, i: (i, j)),
        out_shape=jax.ShapeDtypeStruct((m, ne * f), BF16),
        compiler_params=_cparams(("parallel", "parallel")),
        name="glu_all_experts",
    )(h, w1, w3, gates)
    return matmul(hid, w2.reshape(ne * f, d), out_dtype=F32, residual=x2, tm=1024, tn=512, tk=2048,
                  name="moe_down_all")


def moe_layer(x2, norm_g, w_router, w1, w3, w2):
    m, d = x2.shape
    tm = min(MOE_TM, m)
    tt = min(MOE_TT, m)
    n_tiles = (TOP_K * m) // tm + N_EXPERTS
    rows = n_tiles * tm
    h, gates, sel = moe_router(x2, norm_g, w_router)
    rank, counts = moe_rank(sel)

    cnt = counts[0, :N_EXPERTS].astype(I32)
    padded = ((cnt + tm - 1) // tm) * tm
    ends = jnp.cumsum(padded)
    starts = ends - padded
    start_row = jnp.zeros((1, LANES), F32).at[0, :N_EXPERTS].set(starts.astype(F32))
    tile_expert = jnp.minimum(jnp.searchsorted(ends, jnp.arange(n_tiles, dtype=I32) * tm, side="right"),
                              N_EXPERTS - 1).astype(I32)
    n_valid = (ends[-1:] // tm).astype(I32)
    unused = ends[-1] + jnp.arange(N_EXPERTS, dtype=I32) * tm
    tails = jnp.concatenate([jnp.where(padded > 0, ends - tm, -1),
                             jnp.where(unused < rows, unused, -1)]).astype(I32)

    slots = moe_slots(rank, sel, gates, start_row)
    pos = slots[:, :2].astype(I32).reshape(m // tt, tt, 2)
    pos_tiles = jnp.swapaxes(pos, 1, 2).reshape(m // tt, 1, 2 * tt)

    xg = moe_dispatch(h, pos_tiles, tails, rows, tm)
    hid = _grouped_call(_glu_grouped_kernel, xg, [w1, w3], tile_expert, n_valid,
                        w1.shape[2] // MOE_UP_TN, MOE_UP_TN, MOE_UP_TN, BF16, tm, "moe_glu")
    yg = _grouped_call(_down_grouped_kernel, hid, [w2], tile_expert, n_valid,
                       d // MOE_DOWN_TN, MOE_DOWN_TN, MOE_DOWN_TN // 2, U32, tm, "moe_down")
    return moe_combine(x2, slots, pos_tiles, yg)


def _split_in_proj(w_in_l):
    seg = lambda lo, hi: w_in_l[:, lo:hi]
    big = jnp.concatenate([
        seg(O_AQ, O_AZ), seg(O_AZ, O_AA), seg(O_BQ, O_BCKV), seg(O_CB, O_DZ),
        seg(O_DZ, O_DXBC), seg(O_DXBC, O_DDT), seg(O_BQI, O_BKI), seg(O_BCKV, O_BQI)], axis=1).astype(BF16)
    pad = jnp.zeros((w_in_l.shape[0], S_COLS - (S_DDT + H_D)), w_in_l.dtype)
    small = jnp.concatenate([
        seg(O_AA, O_AB), seg(O_AB, O_BQ), seg(O_BKI, O_BWI), seg(O_BWI, O_CB), seg(O_DDT, _O_END), pad],
        axis=1).astype(BF16)
    gate = w_in_l[:, MIX_COLS:].astype(BF16)
    return big, small, gate


def _mixer_layer(x2, bsz, t, mix_norm_g, w_in_l, conv_a_w, a_log_a, dt_bias_a, out_norm_a_g, ckv_norm_g, w_uk,
                 q_norm_b_g, k_norm_b_g, w_uv, rel_bias, bias_tables, conv_c_w, conv_d_w, conv_d_b, a_log_d,
                 dt_bias_d, d_skip, out_norm_d_g, w_up_all, w_out_all, layer):
    m = bsz * t
    xn = rmsnorm(x2, mix_norm_g)
    w_big, w_small, w_gate = _split_in_proj(w_in_l)
    p3 = matmul(xn, w_big, out_dtype=BF16, tm=1024, tn=768, name="in_proj").reshape(bsz, t, P_COLS)
    small3 = matmul(xn, w_small, out_dtype=F32, tm=1024, tn=S_COLS, name="in_proj_small").reshape(bsz, t, S_COLS)

    qkv3 = _conv_call(functools.partial(_conv_qkv_kernel, tc=512), p3, [P_AQKV], 3 * BRANCH_WIDTH,
                      [conv_a_w.astype(F32)], F32, 512, "conv_qkv")
    o_a = gated_deltanet(qkv3, p3, small3, a_log_a, dt_bias_a, out_norm_a_g)

    qh3, kk3, ct3, ki3 = dsa_prep(p3, small3, w_uk, ckv_norm_g, q_norm_b_g, k_norm_b_g)
    dprev, ddiag = bias_tables
    o_b = dsa_attention(qh3, p3, small3, kk3, ct3, ki3, dprev, ddiag,
                        jnp.swapaxes(w_uv, 1, 2).astype(BF16))

    o_c = _conv_call(_conv_gated_kernel, p3, [P_C, P_C + BRANCH_WIDTH, P_C + 2 * BRANCH_WIDTH], BRANCH_WIDTH,
                     [conv_c_w.astype(F32)], BF16, 512, "conv_gated")

    xbc3 = _conv_call(_conv_xbc_kernel, p3, [P_DXBC], XBC_WIDTH,
                      [conv_d_w.astype(F32), conv_d_b.reshape(1, XBC_WIDTH).astype(F32)], F32, 512, "conv_xbc")
    o_d = mamba2_ssd(xbc3, p3, small3, a_log_d, dt_bias_d, d_skip, out_norm_d_g)

    outs = [o.reshape(m, BRANCH_WIDTH) for o in (o_a, o_b, o_c, o_d)]
    merged = merge_branches(xn, outs, w_gate, w_up_all, layer)
    return matmul(merged, w_out_all, out_dtype=F32, residual=x2, tm=1024, tn=512, layer=layer, name="out_proj")


def _half_k(k):
    return k // 2 if k % (2 * LANES) == 0 else k


def kernel(x, mix_norm_g, w_in, conv_a_w, a_log_a, dt_bias_a, out_norm_a_g, ckv_norm_g, w_uk, q_norm_b_g, k_norm_b_g, w_uv, rel_bias, conv_c_w, conv_d_w, conv_d_b, a_log_d, dt_bias_d, d_skip, out_norm_d_g, w_up, w_out, ffn_norm_g, w1_dense, w3_dense, w2_dense, w_router, w1_moe, w3_moe, w2_moe):
    bsz, t, d = x.shape
    depth = w_in.shape[0]
    x2 = x.reshape(bsz * t, d)
    bias_tables = relbias_tables(rel_bias)
    for l in range(depth):
        x2 = _mixer_layer(x2, bsz, t, mix_norm_g[l], w_in[l], conv_a_w[l], a_log_a[l], dt_bias_a[l],
                          out_norm_a_g[l], ckv_norm_g[l], w_uk[l], q_norm_b_g[l], k_norm_b_g[l], w_uv[l],
                          rel_bias, bias_tables, conv_c_w[l], conv_d_w[l], conv_d_b[l], a_log_d[l],
                          dt_bias_d[l], d_skip[l], out_norm_d_g[l], w_up, w_out, l)
        j = l // 2
        if l % 2 == 0:
            h = rmsnorm(x2, ffn_norm_g[l])
            hid = glu_dense(h, w1_dense[j], w3_dense[j])
            x2 = matmul(hid, w2_dense[j].astype(BF16), out_dtype=F32, residual=x2,
                        tm=512, tn=512, name="ffn_down")
        else:
            x2 = moe_layer_dense(x2, ffn_norm_g[l], w_router[j], w1_moe[j], w3_moe[j], w2_moe[j])
    return x2.reshape(bsz, t, d)
```

```python
import functools
import math

import jax
import jax.numpy as jnp
from jax import lax
from jax.experimental import pallas as pl
from jax.experimental.pallas import tpu as pltpu

F32 = jnp.float32
BF16 = jnp.bfloat16
I32 = jnp.int32
U32 = jnp.uint32
HIGHEST = lax.Precision.HIGHEST

D_MODEL = 4096
CHUNK = 64
N_BRANCHES = 4
BRANCH_WIDTH = D_MODEL // N_BRANCHES
DK_A = 128
H_A = BRANCH_WIDTH // DK_A
CONV_A = 4
DH_B = 128
H_B = BRANCH_WIDTH // DH_B
DC_B = 256
H_IDX = 8
D_IDX = 64
DSA_TOPK = 256
Q_BLOCK = 128
N_BUCKETS = 32
T5_MAX_DISTANCE = 128
CONV_C = 3
D_INNER = BRANCH_WIDTH
P_D = 64
H_D = D_INNER // P_D
N_GROUPS = 2
D_STATE = 128
CONV_D = 4
XBC_WIDTH = D_INNER + 2 * N_GROUPS * D_STATE
N_EXPERTS = 8
TOP_K = 2
EPS = 1e-6

MIX_SPLITS = (
    BRANCH_WIDTH, BRANCH_WIDTH, BRANCH_WIDTH, BRANCH_WIDTH, H_A, H_A,
    H_B * DH_B, DC_B, H_IDX * D_IDX, D_IDX, H_IDX,
    BRANCH_WIDTH, BRANCH_WIDTH, BRANCH_WIDTH,
    D_INNER, XBC_WIDTH, H_D,
)
MIX_COLS = sum(MIX_SPLITS)
_OFF = [0]
for _w in MIX_SPLITS:
    _OFF.append(_OFF[-1] + _w)
(O_AQ, O_AK, O_AV, O_AZ, O_AA, O_AB, O_BQ, O_BCKV, O_BQI, O_BKI, O_BWI,
 O_CB, O_CC, O_CH, O_DZ, O_DXBC, O_DDT, _O_END) = _OFF

P_AQKV = 0
P_AZ = 3072
P_BQ = 4096
P_C = 5120
P_DZ = 8192
P_DXBC = 9216
P_BQI = 10752
P_BCKV = 11264
P_COLS = 11520
S_AA = 0
S_AB = 8
S_BKI = 16
S_BWI = 80
S_DDT = 88
S_COLS = 128

LANES = 128
VMEM_LIMIT_MB = 56


def _cparams(sem, vmem_mb=VMEM_LIMIT_MB):
    return pltpu.CompilerParams(dimension_semantics=sem, vmem_limit_bytes=vmem_mb * 2 ** 20)


def _softplus(x):
    return jnp.maximum(x, 0.0) + jnp.log(1.0 + jnp.exp(-jnp.abs(x)))


def _sigmoid(x):
    return 1.0 / (1.0 + jnp.exp(-x))


def _silu(x):
    return x * _sigmoid(x)


def _dot(a, b, precision=None):
    return jnp.dot(a, b, preferred_element_type=F32, precision=precision)


def _dot_nt(a, b, precision=None):
    return lax.dot_general(a, b, (((1,), (1,)), ((), ())), preferred_element_type=F32, precision=precision)


def _dot_tn(a, b, precision=None):
    return lax.dot_general(a, b, (((0,), (0,)), ((), ())), preferred_element_type=F32, precision=precision)


def _rmsnorm_kernel(x_ref, g_ref, o_ref):
    x = x_ref[...].astype(F32)
    ms = jnp.mean(x * x, axis=-1, keepdims=True)
    o_ref[...] = (x * lax.rsqrt(ms + EPS) * g_ref[...]).astype(o_ref.dtype)


def rmsnorm(x, g, out_dtype=BF16, tm=512):
    m, d = x.shape
    tm = min(tm, m)
    return pl.pallas_call(
        _rmsnorm_kernel,
        grid=(m // tm,),
        in_specs=[pl.BlockSpec((tm, d), lambda i: (i, 0)), pl.BlockSpec((1, d), lambda i: (0, 0))],
        out_specs=pl.BlockSpec((tm, d), lambda i: (i, 0)),
        out_shape=jax.ShapeDtypeStruct((m, d), out_dtype),
        compiler_params=_cparams(("parallel",)),
        name="rmsnorm",
    )(x, g.reshape(1, d).astype(F32))


def _mm_kernel(*refs, nk, has_res):
    if has_res:
        a_ref, b_ref, r_ref = refs[:3]
        rest = refs[3:]
    else:
        a_ref, b_ref = refs[:2]
        r_ref = None
        rest = refs[2:]
    o_ref = rest[0]
    if nk == 1:
        acc = _dot(a_ref[...], b_ref[...].astype(BF16))
        if r_ref is not None:
            acc = acc + r_ref[...].astype(F32)
        o_ref[...] = acc.astype(o_ref.dtype)
        return
    acc_ref = rest[1]
    k = pl.program_id(2)

    @pl.when(k == 0)
    def _():
        acc_ref[...] = jnp.zeros_like(acc_ref)

    acc_ref[...] += _dot(a_ref[...], b_ref[...].astype(BF16))

    @pl.when(k == nk - 1)
    def _():
        acc = acc_ref[...]
        if r_ref is not None:
            acc = acc + r_ref[...].astype(F32)
        o_ref[...] = acc.astype(o_ref.dtype)


def matmul(a, b, *, out_dtype, residual=None, tm=1024, tn=512, tk=None, layer=None, name="matmul"):
    m, kdim = a.shape
    n = b.shape[-1]
    tm, tn = min(tm, m), min(tn, n)
    tk = kdim if tk is None else min(tk, kdim)
    nk = kdim // tk
    assert m % tm == 0 and n % tn == 0 and kdim % tk == 0, (a.shape, b.shape, tm, tn, tk)
    if layer is None:
        b_spec = pl.BlockSpec((tk, tn), lambda i, j, k: (k, j))
    else:
        b_spec = pl.BlockSpec((None, tk, tn), lambda i, j, k: (layer, k, j))
    in_specs = [pl.BlockSpec((tm, tk), lambda i, j, k: (i, k)), b_spec]
    args = [a, b]
    if residual is not None:
        in_specs.append(pl.BlockSpec((tm, tn), lambda i, j, k: (i, j)))
        args.append(residual)
    scratch = [pltpu.VMEM((tm, tn), F32)] if nk > 1 else []
    return pl.pallas_call(
        functools.partial(_mm_kernel, nk=nk, has_res=residual is not None),
        grid=(m // tm, n // tn, nk),
        in_specs=in_specs,
        out_specs=pl.BlockSpec((tm, tn), lambda i, j, k: (i, j)),
        out_shape=jax.ShapeDtypeStruct((m, n), out_dtype),
        scratch_shapes=scratch,
        compiler_params=_cparams(("parallel", "parallel", "arbitrary")),
        name=name,
    )(*args)


_PAD_ROWS = 8


def _causal_conv(stage_ref, x, w_ref, ksize):
    t = x.shape[0]
    stage_ref[0:_PAD_ROWS, :] = jnp.zeros((_PAD_ROWS, x.shape[1]), F32)
    stage_ref[_PAD_ROWS:_PAD_ROWS + t, :] = x
    acc = x * w_ref[ksize - 1:ksize, :]
    for j in range(ksize - 1):
        s = ksize - 1 - j
        acc = acc + stage_ref[_PAD_ROWS - s:_PAD_ROWS - s + t, :] * w_ref[j:j + 1, :]
    return acc


def _conv_xbc_kernel(x_ref, w_ref, b_ref, o_ref, stage_ref):
    y = _causal_conv(stage_ref, x_ref[...].astype(F32), w_ref, CONV_D) + b_ref[...]
    o_ref[...] = _silu(y)


def _conv_gated_kernel(bg_ref, cg_ref, h_ref, w_ref, o_ref, stage_ref):
    u = cg_ref[...].astype(F32) * h_ref[...].astype(F32)
    y = _causal_conv(stage_ref, u, w_ref, CONV_C)
    o_ref[...] = (bg_ref[...].astype(F32) * y).astype(o_ref.dtype)


def _conv_call(kernel, p3, col_offsets, width, extra, out_dtype, tc, name):
    bsz, t, _ = p3.shape
    tc = min(tc, width)
    in_specs = [pl.BlockSpec((None, t, tc), functools.partial(lambda b, c, o: (b, 0, o + c), o=off // tc))
                for off in col_offsets]
    args = [p3] * len(col_offsets)
    for e in extra:
        in_specs.append(pl.BlockSpec((e.shape[0], tc), lambda b, c: (0, c)))
        args.append(e)
    return pl.pallas_call(
        kernel,
        grid=(bsz, width // tc),
        in_specs=in_specs,
        out_specs=pl.BlockSpec((None, t, tc), lambda b, c: (b, 0, c)),
        out_shape=jax.ShapeDtypeStruct((bsz, t, width), out_dtype),
        scratch_shapes=[pltpu.VMEM((t + _PAD_ROWS, tc), F32)],
        compiler_params=_cparams(("parallel", "parallel")),
        name=name,
    )(*args)


def _tril_mask(n, strict=False):
    r = lax.broadcasted_iota(I32, (n, n), 0)
    c = lax.broadcasted_iota(I32, (n, n), 1)
    return (r > c) if strict else (r >= c)


def _chunk_cumsum(x):
    tril = _tril_mask(CHUNK).astype(F32)
    cs = _dot(tril, x, HIGHEST)
    padded = jnp.concatenate([cs, jnp.zeros((LANES - CHUNK, LANES), F32)], axis=0)
    return cs, padded.T[:, :CHUNK]


def _segment_decay(cs, cs_t, c):
    tril = _tril_mask(CHUNK)
    diff = cs[:, c:c + 1] - cs_t[c:c + 1, :]
    return jnp.where(tril, jnp.exp(jnp.where(tril, diff, 0.0)), 0.0)


_INV_BLOCK = 16
_GDN_GROUP = 2
_GDN_CHUNKS = 4
_CONV_COLS = 512
_CHUNK_SHIFT = CHUNK.bit_length() - 1


def _split_bf16(a):
    hi = a.astype(BF16)
    return hi, (a - hi.astype(F32)).astype(BF16)


def _dot3(a, b):
    a_hi, a_lo = a if isinstance(a, tuple) else _split_bf16(a)
    b_hi, b_lo = b if isinstance(b, tuple) else _split_bf16(b)
    return _dot(a_hi, b_hi) + _dot(a_hi, b_lo) + _dot(a_lo, b_hi)


def _unit_lower_inverses(lows):
    n_rows = lows[0].shape[0]
    r = lax.broadcasted_iota(I32, (n_rows, n_rows), 0)
    c = lax.broadcasted_iota(I32, (n_rows, n_rows), 1)
    shift = _INV_BLOCK.bit_length() - 1
    same = (r >> shift) == (c >> shift)
    eye = (r == c).astype(F32)
    each = lambda fn, *lists: [fn(*args) for args in zip(*lists)]
    lds = each(lambda low: jnp.where(same, low, 0.0), lows)
    los = each(lambda low: jnp.where(same, 0.0, low), lows)
    ps = each(lambda ld: eye - ld, lds)
    xss = each(_split_bf16, lds)
    xs = each(_dot3, xss, xss)
    for _ in range(2):
        xss = each(_split_bf16, xs)
        ps = each(lambda p, x: p + _dot3(p, x), ps, xss)
        xs = each(_dot3, xss, xss)
    ps = each(lambda p, x: p + _dot3(p, x), ps, xs)
    pss = each(_split_bf16, ps)
    ns = each(_dot3, pss, los)
    nss = each(_split_bf16, ns)
    n2s = each(_dot3, nss, nss)
    rrs = each(lambda n, n2: (eye - n) + _dot3(eye - n, n2), ns, n2s)
    return each(_dot3, rrs, pss)


def _gdn_kernel(pq_ref, pk_ref, pv_ref, z_ref, sm_ref, cw_ref, alog_ref, dtb_ref, ng_ref, o_ref,
                state_ref, stage_ref, qkv_ref):
    tt = sm_ref.shape[0]

    @pl.when(pl.program_id(1) == 0)
    def _():
        state_ref[...] = jnp.zeros_like(state_ref)
        stage_ref[0:_PAD_ROWS, :] = jnp.zeros((_PAD_ROWS, stage_ref.shape[1]), F32)

    for idx, ref in enumerate((pq_ref, pk_ref, pv_ref)):
        stage_ref[_PAD_ROWS:_PAD_ROWS + tt, idx * BRANCH_WIDTH:(idx + 1) * BRANCH_WIDTH] = ref[...].astype(F32)
    for c0 in range(0, 3 * BRANCH_WIDTH, _CONV_COLS):
        cols = slice(c0, c0 + _CONV_COLS)
        acc = stage_ref[_PAD_ROWS:_PAD_ROWS + tt, cols] * cw_ref[CONV_A - 1:CONV_A, cols]
        for j in range(CONV_A - 1):
            s = CONV_A - 1 - j
            acc = acc + stage_ref[_PAD_ROWS - s:_PAD_ROWS - s + tt, cols] * cw_ref[j:j + 1, cols]
        y = _silu(acc)
        for g0 in range(0, _CONV_COLS, DK_A):
            ys = y[:, g0:g0 + DK_A]
            if c0 < 2 * BRANCH_WIDTH:
                inv = lax.rsqrt(jnp.sum(ys * ys, axis=-1, keepdims=True) + EPS)
                ys = ys * (inv * DK_A ** -0.5 if c0 < BRANCH_WIDTH else inv)
            qkv_ref[:, c0 + g0:c0 + g0 + DK_A] = ys
    stage_ref[0:_PAD_ROWS, :] = stage_ref[tt:tt + _PAD_ROWS, :]
    q_ref = qkv_ref.at[:, 0:BRANCH_WIDTH]
    k_ref = qkv_ref.at[:, BRANCH_WIDTH:2 * BRANCH_WIDTH]
    v_ref = qkv_ref.at[:, 2 * BRANCH_WIDTH:3 * BRANCH_WIDTH]

    ng = ng_ref[...]
    lane = lax.broadcasted_iota(I32, (CHUNK, LANES), 1)
    rows = _GDN_GROUP * CHUNK
    r = lax.broadcasted_iota(I32, (rows, rows), 0)
    c = lax.broadcasted_iota(I32, (rows, rows), 1)
    same_head = (r >> _CHUNK_SHIFT) == (c >> _CHUNK_SHIFT)
    tril = same_head & (r >= c)
    strict = same_head & (r > c)
    groups = [range(grp * _GDN_GROUP, (grp + 1) * _GDN_GROUP) for grp in range(H_A // _GDN_GROUP)]

    n_chunks = sm_ref.shape[0] // CHUNK
    e_lasts, problems = [], []
    for ci in range(n_chunks):
        ts = slice(ci * CHUNK, (ci + 1) * CHUNK)
        sm = sm_ref[ts, :]
        g = -jnp.exp(alog_ref[...]) * _softplus(sm + dtb_ref[...])
        g = jnp.where(lane < S_AA + H_A, g, 0.0)
        beta = _sigmoid(sm)
        gc, gc_t = _chunk_cumsum(g)
        g_last = gc[CHUNK - 1:CHUNK, :]
        e_gc = jnp.exp(gc)
        e_rem = jnp.exp(g_last - gc)
        e_lasts.append(jnp.exp(g_last))
        for heads in groups:
            stack = lambda ref: jnp.concatenate([ref[ts, h * DK_A:(h + 1) * DK_A] for h in heads], axis=0)
            col = lambda arr, off: jnp.concatenate([arr[:, off + h:off + h + 1] for h in heads], axis=0)
            q_st, k_st, v_st = stack(q_ref), stack(k_ref), stack(v_ref)
            beta_st, egc_st, erem_st = col(beta, S_AB), col(e_gc, S_AA), col(e_rem, S_AA)
            g_row = jnp.concatenate([gc_t[S_AA + h:S_AA + h + 1, :] for h in heads], axis=1)
            diff = col(gc, S_AA) - g_row
            decay = jnp.where(tril, jnp.exp(jnp.where(tril, diff, 0.0)), 0.0)
            kb_st = k_st * beta_st
            problems.append(dict(q=q_st, k=k_st, kb=kb_st, decay=decay, qd=q_st * egc_st, kd=k_st * erem_st,
                                 rhs=jnp.concatenate([v_st * beta_st, kb_st * egc_st], axis=1)))
    lows = [jnp.where(strict, _dot_nt(p["kb"], p["k"]) * p["decay"], 0.0) for p in problems]
    tinvs = _unit_lower_inverses(lows)
    uws = [_dot3(tinv, p["rhs"]) for tinv, p in zip(tinvs, problems)]
    intras = [_dot_nt(p["q"], p["k"]) * p["decay"] for p in problems]

    head_rows = lambda gi_j: slice(gi_j * CHUNK, (gi_j + 1) * CHUNK)
    for ci in range(n_chunks):
        ts = slice(ci * CHUNK, (ci + 1) * CHUNK)
        e_last = e_lasts[ci]
        probs = range(ci * len(groups), (ci + 1) * len(groups))
        states = [state_ref[h] for h in range(H_A)]
        wq_s = {}
        for pi, heads in zip(probs, groups):
            for j, h in enumerate(heads):
                rs = head_rows(j)
                lhs = jnp.concatenate([uws[pi][rs, DK_A:], problems[pi]["qd"][rs]], axis=0)
                wq_s[h] = _dot(lhs, states[h])
        v_new = {h: uws[pi][head_rows(j), :DK_A] - wq_s[h][:CHUNK]
                 for pi, heads in zip(probs, groups) for j, h in enumerate(heads)}
        outs = [jnp.concatenate([wq_s[h][CHUNK:] for h in heads], axis=0)
                + _dot(intras[pi], jnp.concatenate([v_new[h] for h in heads], axis=0))
                for pi, heads in zip(probs, groups)]
        for pi, heads in zip(probs, groups):
            for j, h in enumerate(heads):
                ca = S_AA + h
                state_ref[h] = states[h] * e_last[:, ca:ca + 1] + _dot_tn(problems[pi]["kd"][head_rows(j)], v_new[h])
        for out, heads in zip(outs, groups):
            ms = jnp.mean(out * out, axis=-1, keepdims=True)
            out = out * lax.rsqrt(ms + EPS) * ng
            for j, h in enumerate(heads):
                sl = slice(h * DK_A, (h + 1) * DK_A)
                o_ref[ts, sl] = (out[head_rows(j)] * _silu(z_ref[ts, sl].astype(F32))).astype(o_ref.dtype)


def gated_deltanet(p3, small3, conv_w, a_log, dt_bias, norm_g):
    bsz, t, _ = p3.shape
    tt = _GDN_CHUNKS * CHUNK if t % (_GDN_CHUNKS * CHUNK) == 0 else CHUNK

    def row(vals, off):
        return jnp.zeros((1, S_COLS), F32).at[0, off:off + vals.shape[0]].set(vals.astype(F32))

    w = BRANCH_WIDTH
    return pl.pallas_call(
        _gdn_kernel,
        grid=(bsz, t // tt),
        in_specs=[
            pl.BlockSpec((None, tt, w), lambda b, n: (b, n, 0)),
            pl.BlockSpec((None, tt, w), lambda b, n: (b, n, 1)),
            pl.BlockSpec((None, tt, w), lambda b, n: (b, n, 2)),
            pl.BlockSpec((None, tt, w), lambda b, n: (b, n, P_AZ // w)),
            pl.BlockSpec((None, tt, S_COLS), lambda b, n: (b, n, 0)),
            pl.BlockSpec((CONV_A, 3 * w), lambda b, n: (0, 0)),
            pl.BlockSpec((1, S_COLS), lambda b, n: (0, 0)),
            pl.BlockSpec((1, S_COLS), lambda b, n: (0, 0)),
            pl.BlockSpec((1, DK_A), lambda b, n: (0, 0)),
        ],
        out_specs=pl.BlockSpec((None, tt, w), lambda b, n: (b, n, 0)),
        out_shape=jax.ShapeDtypeStruct((bsz, t, w), BF16),
        scratch_shapes=[pltpu.VMEM((H_A, DK_A, DK_A), F32), pltpu.VMEM((tt + _PAD_ROWS, 3 * w), F32),
                        pltpu.VMEM((tt, 3 * w), F32)],
        compiler_params=_cparams(("parallel", "arbitrary")),
        name="gated_deltanet",
    )(p3, p3, p3, p3, small3, conv_w.astype(F32), row(a_log, S_AA), row(dt_bias, S_AA),
      norm_g.reshape(1, DK_A).astype(F32))


def _ssd_kernel(xbc_ref, z_ref, sm_ref, alog_ref, dtb_ref, dskip_ref, ng_ref, o_ref, state_ref):
    @pl.when(pl.program_id(1) == 0)
    def _():
        state_ref[...] = jnp.zeros_like(state_ref)

    sm = sm_ref[...]
    lane = lax.broadcasted_iota(I32, (CHUNK, LANES), 1)
    dt = _softplus(sm + dtb_ref[...])
    a = jnp.where((lane >= S_DDT) & (lane < S_DDT + H_D), dt * -jnp.exp(alog_ref[...]), 0.0)
    acs, acs_t = _chunk_cumsum(a)
    a_last = acs[CHUNK - 1:CHUNK, :]
    e_acs = jnp.exp(acs)
    e_rem = jnp.exp(a_last - acs)
    e_last = jnp.exp(a_last)
    heads_per_group = H_D // N_GROUPS
    group_of = lambda h: h // heads_per_group
    bms = [xbc_ref[:, D_INNER + grp * D_STATE:D_INNER + (grp + 1) * D_STATE] for grp in range(N_GROUPS)]
    c0 = D_INNER + N_GROUPS * D_STATE
    cms = [xbc_ref[:, c0 + grp * D_STATE:c0 + (grp + 1) * D_STATE] for grp in range(N_GROUPS)]
    cbs = [_dot_nt(cm, bm) for cm, bm in zip(cms, bms)]
    xcs = [xbc_ref[:, h * P_D:(h + 1) * P_D] * dt[:, S_DDT + h:S_DDT + h + 1] for h in range(H_D)]
    states = [state_ref[h] for h in range(H_D)]
    ys = []
    for h in range(H_D):
        c = S_DDT + h
        ys.append(_dot(cbs[group_of(h)] * _segment_decay(acs, acs_t, c), xcs[h])
                  + _dot_nt(cms[group_of(h)] * e_acs[:, c:c + 1], states[h]))
    for h in range(H_D):
        c = S_DDT + h
        state_ref[h] = states[h] * e_last[:, c:c + 1] + _dot_tn(xcs[h], bms[group_of(h)] * e_rem[:, c:c + 1])
    y = jnp.concatenate(ys, axis=1) + dskip_ref[...] * xbc_ref[:, 0:D_INNER]
    y = y * _silu(z_ref[...].astype(F32))
    gw = D_INNER // N_GROUPS
    for grp in range(N_GROUPS):
        yg = y[:, grp * gw:(grp + 1) * gw]
        ms = jnp.mean(yg * yg, axis=-1, keepdims=True)
        o_ref[:, grp * gw:(grp + 1) * gw] = (yg * lax.rsqrt(ms + EPS) * ng_ref[:, grp * gw:(grp + 1) * gw]).astype(o_ref.dtype)


def mamba2_ssd(xbc3, p3, small3, a_log, dt_bias, d_skip, norm_g):
    bsz, t, _ = xbc3.shape
    nchunks = t // CHUNK

    def row(vals, off):
        return jnp.zeros((1, S_COLS), F32).at[0, off:off + vals.shape[0]].set(vals.astype(F32))

    return pl.pallas_call(
        _ssd_kernel,
        grid=(bsz, nchunks),
        in_specs=[
            pl.BlockSpec((None, CHUNK, XBC_WIDTH), lambda b, n: (b, n, 0)),
            pl.BlockSpec((None, CHUNK, D_INNER), lambda b, n: (b, n, P_DZ // D_INNER)),
            pl.BlockSpec((None, CHUNK, S_COLS), lambda b, n: (b, n, 0)),
            pl.BlockSpec((1, S_COLS), lambda b, n: (0, 0)),
            pl.BlockSpec((1, S_COLS), lambda b, n: (0, 0)),
            pl.BlockSpec((1, D_INNER), lambda b, n: (0, 0)),
            pl.BlockSpec((1, D_INNER), lambda b, n: (0, 0)),
        ],
        out_specs=pl.BlockSpec((None, CHUNK, D_INNER), lambda b, n: (b, n, 0)),
        out_shape=jax.ShapeDtypeStruct((bsz, t, D_INNER), BF16),
        scratch_shapes=[pltpu.VMEM((H_D, P_D, D_STATE), F32)],
        compiler_params=_cparams(("parallel", "arbitrary")),
        name="mamba2_ssd",
    )(xbc3, p3, small3, row(a_log, S_DDT), row(dt_bias, S_DDT),
      jnp.repeat(d_skip.astype(F32), P_D).reshape(1, D_INNER), norm_g.reshape(1, D_INNER).astype(F32))


def _dsa_prep_kernel(q_ref, ckv_ref, sm_ref, wuk_ref, cg_ref, qg_ref, kg_ref, qh_ref, kk_ref, ct_ref, ki_ref):
    ckv = ckv_ref[...].astype(F32)
    c = ckv * lax.rsqrt(jnp.mean(ckv * ckv, axis=-1, keepdims=True) + EPS) * cg_ref[...]
    cb = c.astype(BF16)
    ct_ref[...] = c.T.astype(BF16)
    kk = _dot(cb, wuk_ref[...])
    kk_ref[...] = (kk * lax.rsqrt(jnp.mean(kk * kk, axis=-1, keepdims=True) + EPS) * kg_ref[...]).astype(BF16)
    for h in range(H_B):
        sl = slice(h * DH_B, (h + 1) * DH_B)
        qh = q_ref[:, sl].astype(F32)
        qh = qh * lax.rsqrt(jnp.mean(qh * qh, axis=-1, keepdims=True) + EPS) * qg_ref[...]
        qh_ref[:, sl] = (qh * DH_B ** -0.5).astype(BF16)
    ki_ref[...] = sm_ref[:, S_BKI:S_BKI + D_IDX].astype(BF16)


def dsa_prep(p3, small3, w_uk, ckv_g, q_g, k_g, tt=512):
    bsz, t, _ = p3.shape
    tt = min(tt, t)
    w = H_B * DH_B
    return pl.pallas_call(
        _dsa_prep_kernel,
        grid=(bsz, t // tt),
        in_specs=[
            pl.BlockSpec((None, tt, w), lambda b, i: (b, i, P_BQ // w)),
            pl.BlockSpec((None, tt, DC_B), lambda b, i: (b, i, P_BCKV // DC_B)),
            pl.BlockSpec((None, tt, S_COLS), lambda b, i: (b, i, 0)),
            pl.BlockSpec((DC_B, DH_B), lambda b, i: (0, 0)),
            pl.BlockSpec((1, DC_B), lambda b, i: (0, 0)),
            pl.BlockSpec((1, DH_B), lambda b, i: (0, 0)),
            pl.BlockSpec((1, DH_B), lambda b, i: (0, 0)),
        ],
        out_specs=[
            pl.BlockSpec((None, tt, w), lambda b, i: (b, i, 0)),
            pl.BlockSpec((None, tt, DH_B), lambda b, i: (b, i, 0)),
            pl.BlockSpec((None, DC_B, tt), lambda b, i: (b, 0, i)),
            pl.BlockSpec((None, tt, D_IDX), lambda b, i: (b, i, 0)),
        ],
        out_shape=[
            jax.ShapeDtypeStruct((bsz, t, w), BF16),
            jax.ShapeDtypeStruct((bsz, t, DH_B), BF16),
            jax.ShapeDtypeStruct((bsz, DC_B, t), BF16),
            jax.ShapeDtypeStruct((bsz, t, D_IDX), BF16),
        ],
        compiler_params=_cparams(("parallel", "parallel")),
        name="dsa_prep",
    )(p3, p3, small3, w_uk.astype(BF16), ckv_g.reshape(1, DC_B).astype(F32),
      q_g.reshape(1, DH_B).astype(F32), k_g.reshape(1, DH_B).astype(F32))


_T5_HALF = N_BUCKETS // 2
_T5_EXACT = _T5_HALF // 2
_T5_FAR = _T5_HALF - 1


def _relbias_kernel(rb_ref, prev_ref, diag_ref):
    h = pl.program_id(0)
    kl = lax.broadcasted_iota(I32, (Q_BLOCK, Q_BLOCK), 0)
    ql = lax.broadcasted_iota(I32, (Q_BLOCK, Q_BLOCK), 1)
    far = rb_ref[_T5_FAR, h]
    for ref, shift in ((prev_ref, -Q_BLOCK), (diag_ref, 0)):
        rel = kl - ql + shift
        n = jnp.abs(rel)
        n2 = n * n
        steps = jnp.zeros_like(n)
        for j in range(1, _T5_HALF - _T5_EXACT):
            steps = steps + (n2 >= (_T5_EXACT * _T5_EXACT) * 2 ** j).astype(I32)
        large = jnp.minimum(_T5_EXACT + steps, _T5_HALF - 1)
        bucket = jnp.where(rel > 0, _T5_HALF, 0) + jnp.where(n < _T5_EXACT, n, large)
        acc = jnp.zeros((Q_BLOCK, Q_BLOCK), F32)
        for b in range(N_BUCKETS):
            acc = jnp.where(bucket == b, rb_ref[b, h], acc)
        ref[...] = acc - far


def relbias_tables(rel_bias):
    shp = jax.ShapeDtypeStruct((H_B, Q_BLOCK, Q_BLOCK), F32)
    spec = pl.BlockSpec((None, Q_BLOCK, Q_BLOCK), lambda h: (h, 0, 0))
    return pl.pallas_call(
        _relbias_kernel,
        grid=(H_B,),
        in_specs=[pl.BlockSpec(memory_space=pltpu.SMEM)],
        out_specs=[spec, spec],
        out_shape=[shp, shp],
        compiler_params=_cparams(("arbitrary",)),
        name="relbias_tables",
    )(rel_bias.astype(F32))


_INT_MIN = -2 ** 31
_SCORE_ROWS = 256
_SWEEP_ROWS = 64
_DSA_WIDTH_STEP = 512


def _sweep_rows(n_rows, init, step):
    def body(c, acc):
        return step(pl.multiple_of(c * _SWEEP_ROWS, _SWEEP_ROWS), acc)
    n_steps = n_rows // _SWEEP_ROWS
    return lax.fori_loop(0, n_steps, body, init, unroll=min(8, n_steps))


def _dsa_kernel(qh_ref, qi_ref, sm_ref, kk_ref, ct_ref, ki_ref, dprev_ref, ddiag_ref, wuvt_ref,
                o_ref, key_ref, selb_ref, lg_ref, p_ref, *, widths, topk):
    i = pl.program_id(1)
    need = (i + 1) * Q_BLOCK
    wi_t = sm_ref[...].T[S_BWI:S_BWI + H_IDX, :] * (H_IDX ** -0.5 * D_IDX ** -0.5)
    qi_all = jnp.concatenate([qi_ref[:, h * D_IDX:(h + 1) * D_IDX] for h in range(H_IDX)], axis=0)
    q_chunk = (i * Q_BLOCK + lax.broadcasted_iota(I32, (1, Q_BLOCK), 1)) >> _CHUNK_SHIFT
    row_iota = lax.broadcasted_iota(I32, (_SWEEP_ROWS, Q_BLOCK), 0)
    block = lambda ref, r0: ref[pl.ds(r0, _SWEEP_ROWS), :]

    def body(nc):
        for r0 in range(0, nc, min(_SCORE_ROWS, nc)):
            rows = slice(r0, r0 + min(_SCORE_ROWS, nc))
            s_all = _dot_nt(ki_ref[rows, :], qi_all)
            sc = None
            for h in range(H_IDX):
                term = wi_t[h:h + 1, :] * jnp.maximum(s_all[:, h * Q_BLOCK:(h + 1) * Q_BLOCK], 0.0)
                sc = term if sc is None else sc + term
            bits = lax.bitcast_convert_type(sc + 0.0, I32)
            key = jnp.where(bits < 0, bits ^ 0x7FFFFFFF, bits)
            kpos = r0 + lax.broadcasted_iota(I32, key.shape, 0)
            key_ref[rows, :] = jnp.where((kpos >> _CHUNK_SHIFT) <= q_chunk, key, _INT_MIN)

        def count(pred):
            acc = _sweep_rows(nc, jnp.zeros((_SWEEP_ROWS, Q_BLOCK), I32),
                              lambda r0, acc: acc + pred(r0, block(key_ref, r0)).astype(I32))
            return jnp.sum(acc, axis=0, keepdims=True)

        tau = jnp.where(count(lambda r0, k: k >= 0) >= topk, 0, _INT_MIN).astype(I32)

        def vstep(it, tau):
            cand = tau | (jnp.int32(1) << (30 - it))
            return jnp.where(count(lambda r0, k: k >= cand) >= topk, cand, tau)

        tau = lax.fori_loop(0, 31, vstep, tau)
        n_gt = count(lambda r0, k: k > tau)
        n_eq = count(lambda r0, k: k == tau)
        tied = (n_gt + n_eq > topk) & (tau > _INT_MIN)
        nbits = max(1, (nc - 1).bit_length())

        def last_tied_index():
            room = topk - n_gt

            def istep(it, last):
                cand = last | (jnp.int32(1) << (nbits - 1 - it))
                below = count(lambda r0, k: (k == tau) & (r0 + row_iota < cand))
                return jnp.where(below < room, cand, last)

            return lax.fori_loop(0, nbits, istep, jnp.zeros((1, Q_BLOCK), I32))

        last = lax.cond(jnp.max(tied.astype(I32)) > 0, last_tied_index,
                        lambda: jnp.full((1, Q_BLOCK), nc, I32))

        def write_sel(r0, carry):
            k = block(key_ref, r0)
            sel = (k > _INT_MIN) & ((k > tau) | ((k == tau) & (r0 + row_iota <= last)))
            selb_ref[pl.ds(r0, _SWEEP_ROWS), :] = jnp.where(sel, 0.0, -jnp.inf)
            return carry

        _sweep_rows(nc, 0, write_sel)

        kk = kk_ref[0:nc, :]
        ct = ct_ref[:, 0:nc]
        prev_row = pl.multiple_of(jnp.maximum(i - 1, 0) * Q_BLOCK, Q_BLOCK)
        diag_row = pl.multiple_of(i * Q_BLOCK, Q_BLOCK)
        has_prev = (i > 0).astype(F32)
        for h0 in range(0, H_B, 2):
            q_pair = jnp.concatenate([qh_ref[:, h * DH_B:(h + 1) * DH_B] for h in (h0, h0 + 1)], axis=0)
            lg_pair = _dot_nt(kk, q_pair)
            for j, h in enumerate((h0, h0 + 1)):
                lg_ref[h, 0:nc, :] = lg_pair[:, j * Q_BLOCK:(j + 1) * Q_BLOCK] + selb_ref[0:nc, :]
                lg_ref[h, pl.ds(prev_row, Q_BLOCK), :] += dprev_ref[h] * has_prev
                lg_ref[h, pl.ds(diag_row, Q_BLOCK), :] += ddiag_ref[h]
        blk = min(_SCORE_ROWS, nc)
        outs = []
        for h in range(H_B):
            slot = h % 2
            m_acc = None
            for r0 in range(0, nc, blk):
                x = lg_ref[h, r0:r0 + blk, :]
                m_acc = x if m_acc is None else jnp.maximum(m_acc, x)
            m = jnp.max(m_acc, axis=0, keepdims=True)
            p_acc = None
            for r0 in range(0, nc, blk):
                p = jnp.exp(lg_ref[h, r0:r0 + blk, :] - m)
                p_ref[slot, r0:r0 + blk, :] = p.astype(BF16)
                p_acc = p if p_acc is None else p_acc + p
            denom = jnp.sum(p_acc, axis=0, keepdims=True)
            o_lat = _dot(ct, p_ref[slot, 0:nc, :]) * (1.0 / denom)
            outs.append(_dot(wuvt_ref[h], o_lat.astype(BF16)).T)
        o_ref[...] = jnp.concatenate(outs, axis=1).astype(o_ref.dtype)

    lo = 0
    for nc in widths:
        @pl.when((need > lo) & (need <= nc))
        def _(nc=nc):
            body(nc)
        lo = nc


def dsa_attention(qh3, qi_p3, small3, kk3, ct3, ki3, dprev, ddiag, wuv_t):
    bsz, t, w = qh3.shape
    topk = min(DSA_TOPK, t // 4)
    step = min(_DSA_WIDTH_STEP, t)
    widths = tuple(range(step, t + 1, step))
    wq = H_IDX * D_IDX
    full = lambda shape: pl.BlockSpec(shape, lambda b, i: (0,) * len(shape))
    return pl.pallas_call(
        functools.partial(_dsa_kernel, widths=widths, topk=topk),
        grid=(bsz, t // Q_BLOCK),
        in_specs=[
            pl.BlockSpec((None, Q_BLOCK, w), lambda b, i: (b, i, 0)),
            pl.BlockSpec((None, Q_BLOCK, wq), lambda b, i: (b, i, P_BQI // wq)),
            pl.BlockSpec((None, Q_BLOCK, S_COLS), lambda b, i: (b, i, 0)),
            pl.BlockSpec((None, t, DH_B), lambda b, i: (b, 0, 0)),
            pl.BlockSpec((None, DC_B, t), lambda b, i: (b, 0, 0)),
            pl.BlockSpec((None, t, D_IDX), lambda b, i: (b, 0, 0)),
            full((H_B, Q_BLOCK, Q_BLOCK)),
            full((H_B, Q_BLOCK, Q_BLOCK)),
            full((H_B, DH_B, DC_B)),
        ],
        out_specs=pl.BlockSpec((None, Q_BLOCK, w), lambda b, i: (b, i, 0)),
        out_shape=jax.ShapeDtypeStruct((bsz, t, w), BF16),
        scratch_shapes=[pltpu.VMEM((t, Q_BLOCK), I32), pltpu.VMEM((t, Q_BLOCK), F32),
                        pltpu.VMEM((H_B, t, Q_BLOCK), F32), pltpu.VMEM((2, t, Q_BLOCK), BF16)],
        compiler_params=_cparams(("parallel", "arbitrary")),
        name="dsa_attention",
    )(qh3, qi_p3, small3, kk3, ct3, ki3, dprev, ddiag, wuv_t)


def _merge_kernel(xn_ref, o0_ref, o1_ref, o2_ref, o3_ref, g0_ref, g1_ref, g2_ref, g3_ref,
                  u0_ref, u1_ref, u2_ref, u3_ref, out_ref):
    xn = xn_ref[...]
    acc = None
    for o_ref, g_ref, u_ref in ((o0_ref, g0_ref, u0_ref), (o1_ref, g1_ref, u1_ref),
                                (o2_ref, g2_ref, u2_ref), (o3_ref, g3_ref, u3_ref)):
        term = _sigmoid(_dot(xn, g_ref[...])) * _dot(o_ref[...], u_ref[...].astype(BF16))
        acc = term if acc is None else acc + term
    out_ref[...] = acc.astype(out_ref.dtype)


def merge_branches(xn, outs, w_gate, w_up, layer, tm=512, tn=256):
    m, d = xn.shape
    tm, tn = min(tm, m), min(tn, d)
    nj = d // tn
    in_specs = [pl.BlockSpec((tm, d), lambda j, i: (i, 0))]
    in_specs += [pl.BlockSpec((tm, BRANCH_WIDTH), lambda j, i: (i, 0)) for _ in range(N_BRANCHES)]
    in_specs += [pl.BlockSpec((d, tn), functools.partial(lambda j, i, br: (0, br * nj + j), br=br))
                 for br in range(N_BRANCHES)]
    in_specs += [pl.BlockSpec((None, None, BRANCH_WIDTH, tn),
                              functools.partial(lambda j, i, br: (layer, br, 0, j), br=br))
                 for br in range(N_BRANCHES)]
    return pl.pallas_call(
        _merge_kernel,
        grid=(nj, m // tm),
        in_specs=in_specs,
        out_specs=pl.BlockSpec((tm, tn), lambda j, i: (i, j)),
        out_shape=jax.ShapeDtypeStruct((m, d), BF16),
        compiler_params=_cparams(("parallel", "parallel")),
        name="merge_branches",
    )(xn, *outs, *([w_gate] * N_BRANCHES), *([w_up] * N_BRANCHES))


def _glu_kernel(x_ref, w1_ref, w3_ref, o_ref):
    x = x_ref[...]
    o_ref[...] = (_silu(_dot(x, w1_ref[...].astype(BF16))) * _dot(x, w3_ref[...].astype(BF16))).astype(o_ref.dtype)


def glu_dense(x, w1, w3, tm=2048, tf=256):
    m, d = x.shape
    f = w1.shape[1]
    tm, tf = min(tm, m), min(tf, f)
    assert f % tf == 0
    return pl.pallas_call(
        _glu_kernel,
        grid=(f // tf, m // tm),
        in_specs=[pl.BlockSpec((tm, d), lambda j, i: (i, 0)),
                  pl.BlockSpec((d, tf), lambda j, i: (0, j)),
                  pl.BlockSpec((d, tf), lambda j, i: (0, j))],
        out_specs=pl.BlockSpec((tm, tf), lambda j, i: (i, j)),
        out_shape=jax.ShapeDtypeStruct((m, f), BF16),
        compiler_params=_cparams(("parallel", "parallel")),
        name="glu_dense",
    )(x, w1, w3)


MOE_TM = 512
MOE_TT = 256
MOE_UP_TN = 512
MOE_DOWN_TN = 1024
SLOT_POS_A, SLOT_POS_B, SLOT_W_A, SLOT_W_B = 0, 1, 2, 3
_ROW_COPY_UNROLL = 8


def _pack_bf16_pair(lo, hi):
    bits = lambda v: lax.bitcast_convert_type(v.astype(BF16).astype(F32), U32)
    return bits(hi) | (bits(lo) >> 16)


def _unpack_bf16_pair(u):
    return (lax.bitcast_convert_type(u << 16, F32),
            lax.bitcast_convert_type(u & jnp.uint32(0xFFFF0000), F32))


def _router_kernel(x_ref, g_ref, wr_ref, h_ref, gate_ref, sel_ref):
    x = x_ref[...]
    h = x * lax.rsqrt(jnp.mean(x * x, axis=-1, keepdims=True) + EPS) * g_ref[...]
    half = h.shape[1] // 2
    h_ref[...] = _pack_bf16_pair(h[:, :half], h[:, half:])
    logits = _dot3(h, wr_ref[...])
    lane = lax.broadcasted_iota(I32, logits.shape, 1)
    logits = jnp.where(lane < N_EXPERTS, logits, -jnp.inf)
    m1 = jnp.max(logits, axis=-1, keepdims=True)
    i1 = jnp.min(jnp.where(logits == m1, lane, LANES), axis=-1, keepdims=True)
    rest = jnp.where(lane == i1, -jnp.inf, logits)
    m2 = jnp.max(rest, axis=-1, keepdims=True)
    i2 = jnp.min(jnp.where(rest == m2, lane, LANES), axis=-1, keepdims=True)
    e2 = jnp.exp(m2 - m1)
    inv = 1.0 / (1.0 + e2)
    gate_ref[...] = jnp.where(lane == i1, inv, 0.0) + jnp.where(lane == i2, e2 * inv, 0.0)
    sel_ref[...] = jnp.where((lane == i1) | (lane == i2), 1.0, 0.0).astype(sel_ref.dtype)


def moe_router(x, g, w_router, tm=512):
    m, d = x.shape
    tm = min(tm, m)
    wr = jnp.zeros((d, LANES), F32).at[:, :N_EXPERTS].set(w_router.astype(F32))
    row = lambda w: pl.BlockSpec((tm, w), lambda i: (i, 0))
    return pl.pallas_call(
        _router_kernel,
        grid=(m // tm,),
        in_specs=[row(d), pl.BlockSpec((1, d), lambda i: (0, 0)), pl.BlockSpec((d, LANES), lambda i: (0, 0))],
        out_specs=[row(d // 2), row(LANES), row(LANES)],
        out_shape=[jax.ShapeDtypeStruct((m, d // 2), U32), jax.ShapeDtypeStruct((m, LANES), F32),
                   jax.ShapeDtypeStruct((m, LANES), BF16)],
        compiler_params=_cparams(("parallel",)),
        name="moe_router",
    )(x, g.reshape(1, d).astype(F32), wr)


def _rank_kernel(sel_ref, rank_ref, cnt_ref, carry_ref):
    @pl.when(pl.program_id(0) == 0)
    def _():
        carry_ref[...] = jnp.zeros_like(carry_ref)

    sel = sel_ref[...]
    n = sel.shape[0]
    earlier = _tril_mask(n, strict=True).astype(BF16)
    rank_ref[...] = _dot(earlier, sel) + carry_ref[...]
    carry_ref[...] += jnp.sum(sel.astype(F32), axis=0, keepdims=True)
    cnt_ref[...] = carry_ref[...]


def moe_rank(sel, tr=512):
    m = sel.shape[0]
    tr = min(tr, m)
    return pl.pallas_call(
        _rank_kernel,
        grid=(m // tr,),
        in_specs=[pl.BlockSpec((tr, LANES), lambda i: (i, 0))],
        out_specs=[pl.BlockSpec((tr, LANES), lambda i: (i, 0)), pl.BlockSpec((1, LANES), lambda i: (0, 0))],
        out_shape=[jax.ShapeDtypeStruct((m, LANES), F32), jax.ShapeDtypeStruct((1, LANES), F32)],
        scratch_shapes=[pltpu.VMEM((1, LANES), F32)],
        compiler_params=_cparams(("arbitrary",)),
        name="moe_rank",
    )(sel)


def _slots_kernel(rank_ref, sel_ref, gate_ref, start_ref, out_ref):
    sel = sel_ref[...].astype(F32) > 0.0
    lane = lax.broadcasted_iota(I32, sel.shape, 1)
    dest = start_ref[...] + rank_ref[...]
    first = jnp.min(jnp.where(sel, lane, LANES), axis=-1, keepdims=True)
    second = jnp.max(jnp.where(sel, lane, -1), axis=-1, keepdims=True)
    pick = lambda arr, idx: jnp.sum(jnp.where(lane == idx, arr, 0.0), axis=-1, keepdims=True)
    gates = gate_ref[...]
    out = jnp.where(lane == SLOT_POS_A, pick(dest, first), 0.0)
    out = jnp.where(lane == SLOT_POS_B, pick(dest, second), out)
    out = jnp.where(lane == SLOT_W_A, pick(gates, first), out)
    out_ref[...] = jnp.where(lane == SLOT_W_B, pick(gates, second), out)


def moe_slots(rank, sel, gates, start_row, tr=512):
    m = sel.shape[0]
    tr = min(tr, m)
    row = pl.BlockSpec((tr, LANES), lambda i: (i, 0))
    return pl.pallas_call(
        _slots_kernel,
        grid=(m // tr,),
        in_specs=[row, row, row, pl.BlockSpec((1, LANES), lambda i: (0, 0))],
        out_specs=row,
        out_shape=jax.ShapeDtypeStruct((m, LANES), F32),
        compiler_params=_cparams(("parallel",)),
        name="moe_slots",
    )(rank, sel, gates, start_row)


def _row_copy(src_ref, src_row, dst_ref, dst_row, sem):
    return pltpu.make_async_copy(src_ref.at[pl.ds(src_row, 1), :], dst_ref.at[pl.ds(dst_row, 1), :], sem)


def _dispatch_kernel(pos_ref, tail_ref, h_ref, xg_ref, zero_ref, sem, *, tm):
    tt = h_ref.shape[0]

    @pl.when(pl.program_id(0) == 0)
    def _():
        zero_ref[...] = jnp.zeros_like(zero_ref)
        fill = lambda e: pltpu.make_async_copy(zero_ref, xg_ref.at[pl.ds(pl.multiple_of(tail_ref[e], tm), tm), :], sem)
        for e in range(tail_ref.shape[0]):
            @pl.when(tail_ref[e] >= 0)
            def _(e=e):
                fill(e).start()
        for e in range(tail_ref.shape[0]):
            @pl.when(tail_ref[e] >= 0)
            def _(e=e):
                fill(e).wait()

    def start(r, carry):
        _row_copy(h_ref, r, xg_ref, pos_ref[0, r], sem).start(priority=0)
        _row_copy(h_ref, r, xg_ref, pos_ref[0, tt + r], sem).start(priority=1)
        return carry

    def wait(r, carry):
        _row_copy(h_ref, r, xg_ref, pos_ref[0, r], sem).wait()
        _row_copy(h_ref, r, xg_ref, pos_ref[0, tt + r], sem).wait()
        return carry

    lax.fori_loop(0, tt, start, 0, unroll=_ROW_COPY_UNROLL)
    lax.fori_loop(0, tt, wait, 0, unroll=_ROW_COPY_UNROLL)


def moe_dispatch(h, pos_tiles, tails, rows, tm):
    m, d = h.shape
    tt = pos_tiles.shape[2] // 2
    return pl.pallas_call(
        functools.partial(_dispatch_kernel, tm=tm),
        grid=(m // tt,),
        in_specs=[pl.BlockSpec((None, 1, 2 * tt), lambda i: (i, 0, 0), memory_space=pltpu.SMEM),
                  pl.BlockSpec(memory_space=pltpu.SMEM),
                  pl.BlockSpec((tt, d), lambda i: (i, 0))],
        out_specs=pl.BlockSpec(memory_space=pl.ANY),
        out_shape=jax.ShapeDtypeStruct((rows, d), h.dtype),
        scratch_shapes=[pltpu.VMEM((tm, d), h.dtype), pltpu.SemaphoreType.DMA(())],
        compiler_params=_cparams(("arbitrary",)),
        name="moe_dispatch",
    )(pos_tiles, tails, h)


def _glu_grouped_kernel(te_ref, nv_ref, x_ref, w1_ref, w3_ref, o_ref):
    valid = pl.program_id(1) < nv_ref[0]

    @pl.when(valid)
    def _():
        lo, hi = (v.astype(BF16) for v in _unpack_bf16_pair(x_ref[...]))
        half = lo.shape[1]
        up = lambda w_ref: (_dot(lo, w_ref[0:half, :].astype(BF16)) + _dot(hi, w_ref[half:2 * half, :].astype(BF16)))
        o_ref[...] = (_silu(up(w1_ref)) * up(w3_ref)).astype(o_ref.dtype)

    @pl.when(jnp.logical_not(valid))
    def _():
        o_ref[...] = jnp.zeros_like(o_ref)


def _down_grouped_kernel(te_ref, nv_ref, h_ref, w2_ref, o_ref):
    valid = pl.program_id(1) < nv_ref[0]

    @pl.when(valid)
    def _():
        y = _dot(h_ref[...], w2_ref[...].astype(BF16))
        half = y.shape[1] // 2
        o_ref[...] = _pack_bf16_pair(y[:, :half], y[:, half:])

    @pl.when(jnp.logical_not(valid))
    def _():
        o_ref[...] = jnp.zeros_like(o_ref)


def _grouped_call(kernel, x, weights, tile_expert, n_valid, n_col_tiles, tn, out_tn, out_dtype, tm, name):
    rows, d = x.shape
    tile = lambda i, nv: jnp.minimum(i, nv[0] - 1)
    in_specs = [pl.BlockSpec((tm, d), lambda j, i, te, nv: (tile(i, nv), 0))]
    in_specs += [pl.BlockSpec((None, w.shape[1], tn), lambda j, i, te, nv: (te[tile(i, nv)], 0, j)) for w in weights]
    grid_spec = pltpu.PrefetchScalarGridSpec(
        num_scalar_prefetch=2,
        grid=(n_col_tiles, rows // tm),
        in_specs=in_specs,
        out_specs=pl.BlockSpec((tm, out_tn), lambda j, i, te, nv: (i, j)),
    )
    return pl.pallas_call(
        kernel,
        grid_spec=grid_spec,
        out_shape=jax.ShapeDtypeStruct((rows, n_col_tiles * out_tn), out_dtype),
        compiler_params=_cparams(("arbitrary", "arbitrary")),
        name=name,
    )(tile_expert, n_valid, x, *weights)


def _combine_kernel(pos_ref, next_pos_ref, x_ref, slot_ref, yg_ref, o_ref, buf_ref, sems):
    tt = x_ref.shape[0]
    i = pl.program_id(0)
    cur = i % 2

    def gather(p_ref, slot):
        def copies(r):
            return (_row_copy(yg_ref, p_ref[0, r], buf_ref.at[slot, 0], r, sems.at[slot]),
                    _row_copy(yg_ref, p_ref[0, tt + r], buf_ref.at[slot, 1], r, sems.at[slot]))
        return copies

    def start_all(copies):
        def body(r, carry):
            a, b = copies(r)
            a.start(priority=0)
            b.start(priority=1)
            return carry
        lax.fori_loop(0, tt, body, 0, unroll=_ROW_COPY_UNROLL)

    def wait_all(copies):
        def body(r, carry):
            a, b = copies(r)
            a.wait()
            b.wait()
            return carry
        lax.fori_loop(0, tt, body, 0, unroll=_ROW_COPY_UNROLL)

    @pl.when(i == 0)
    def _():
        start_all(gather(pos_ref, cur))

    @pl.when(i + 1 < pl.num_programs(0))
    def _():
        start_all(gather(next_pos_ref, 1 - cur))

    wait_all(gather(pos_ref, cur))
    rows_ref = buf_ref.at[cur]
    slots = slot_ref[...]
    w_a = slots[:, SLOT_W_A:SLOT_W_A + 1]
    w_b = slots[:, SLOT_W_B:SLOT_W_B + 1]
    half = MOE_DOWN_TN // 2
    for j in range(x_ref.shape[1] // MOE_DOWN_TN):
        lo_a, hi_a = _unpack_bf16_pair(rows_ref[0, :, j * half:(j + 1) * half])
        lo_b, hi_b = _unpack_bf16_pair(rows_ref[1, :, j * half:(j + 1) * half])
        c_lo = slice(j * MOE_DOWN_TN, j * MOE_DOWN_TN + half)
        c_hi = slice(j * MOE_DOWN_TN + half, (j + 1) * MOE_DOWN_TN)
        o_ref[:, c_lo] = x_ref[:, c_lo] + w_a * lo_a + w_b * lo_b
        o_ref[:, c_hi] = x_ref[:, c_hi] + w_a * hi_a + w_b * hi_b


def moe_combine(x, slots, pos_tiles, yg):
    m, d = x.shape
    tt = pos_tiles.shape[2] // 2
    n_steps = m // tt
    pos_spec = lambda step: pl.BlockSpec((None, 1, 2 * tt), lambda i: (step(i), 0, 0), memory_space=pltpu.SMEM)
    return pl.pallas_call(
        _combine_kernel,
        grid=(n_steps,),
        in_specs=[pos_spec(lambda i: i),
                  pos_spec(lambda i: jnp.minimum(i + 1, n_steps - 1)),
                  pl.BlockSpec((tt, d), lambda i: (i, 0)),
                  pl.BlockSpec((tt, LANES), lambda i: (i, 0)),
                  pl.BlockSpec(memory_space=pl.ANY)],
        out_specs=pl.BlockSpec((tt, d), lambda i: (i, 0)),
        out_shape=jax.ShapeDtypeStruct((m, d), F32),
        scratch_shapes=[pltpu.VMEM((2, 2, tt, yg.shape[1]), yg.dtype), pltpu.SemaphoreType.DMA((2,))],
        compiler_params=_cparams(("arbitrary",)),
        name="moe_combine",
    )(pos_tiles, pos_tiles, x, slots, yg)


def moe_layer(x2, norm_g, w_router, w1, w3, w2):
    m, d = x2.shape
    tm = min(MOE_TM, m)
    tt = min(MOE_TT, m)
    n_tiles = (TOP_K * m) // tm + N_EXPERTS
    rows = n_tiles * tm
    h, gates, sel = moe_router(x2, norm_g, w_router)
    rank, counts = moe_rank(sel)

    cnt = counts[0, :N_EXPERTS].astype(I32)
    padded = ((cnt + tm - 1) // tm) * tm
    ends = jnp.cumsum(padded)
    starts = ends - padded
    start_row = jnp.zeros((1, LANES), F32).at[0, :N_EXPERTS].set(starts.astype(F32))
    tile_expert = jnp.minimum(jnp.searchsorted(ends, jnp.arange(n_tiles, dtype=I32) * tm, side="right"),
                              N_EXPERTS - 1).astype(I32)
    n_valid = (ends[-1:] // tm).astype(I32)
    unused = ends[-1] + jnp.arange(N_EXPERTS, dtype=I32) * tm
    tails = jnp.concatenate([jnp.where(padded > 0, ends - tm, -1),
                             jnp.where(unused < rows, unused, -1)]).astype(I32)

    slots = moe_slots(rank, sel, gates, start_row)
    pos = slots[:, :2].astype(I32).reshape(m // tt, tt, 2)
    pos_tiles = jnp.swapaxes(pos, 1, 2).reshape(m // tt, 1, 2 * tt)

    xg = moe_dispatch(h, pos_tiles, tails, rows, tm)
    hid = _grouped_call(_glu_grouped_kernel, xg, [w1, w3], tile_expert, n_valid,
                        w1.shape[2] // MOE_UP_TN, MOE_UP_TN, MOE_UP_TN, BF16, tm, "moe_glu")
    yg = _grouped_call(_down_grouped_kernel, hid, [w2], tile_expert, n_valid,
                       d // MOE_DOWN_TN, MOE_DOWN_TN, MOE_DOWN_TN // 2, U32, tm, "moe_down")
    return moe_combine(x2, slots, pos_tiles, yg)


def _split_in_proj(w_in_l):
    seg = lambda lo, hi: w_in_l[:, lo:hi]
    big = jnp.concatenate([
        seg(O_AQ, O_AZ), seg(O_AZ, O_AA), seg(O_BQ, O_BCKV), seg(O_CB, O_DZ),
        seg(O_DZ, O_DXBC), seg(O_DXBC, O_DDT), seg(O_BQI, O_BKI), seg(O_BCKV, O_BQI)], axis=1).astype(BF16)
    pad = jnp.zeros((w_in_l.shape[0], S_COLS - (S_DDT + H_D)), w_in_l.dtype)
    small = jnp.concatenate([
        seg(O_AA, O_AB), seg(O_AB, O_BQ), seg(O_BKI, O_BWI), seg(O_BWI, O_CB), seg(O_DDT, _O_END), pad],
        axis=1).astype(BF16)
    gate = w_in_l[:, MIX_COLS:].astype(BF16)
    return big, small, gate


def _mixer_layer(x2, bsz, t, mix_norm_g, w_in_l, conv_a_w, a_log_a, dt_bias_a, out_norm_a_g, ckv_norm_g, w_uk,
                 q_norm_b_g, k_norm_b_g, w_uv, rel_bias, bias_tables, conv_c_w, conv_d_w, conv_d_b, a_log_d,
                 dt_bias_d, d_skip, out_norm_d_g, w_up_all, w_out_all, layer):
    m = bsz * t
    xn = rmsnorm(x2, mix_norm_g)
    w_big, w_small, w_gate = _split_in_proj(w_in_l)
    p3 = matmul(xn, w_big, out_dtype=BF16, tm=1024, tn=768, name="in_proj").reshape(bsz, t, P_COLS)
    small3 = matmul(xn, w_small, out_dtype=F32, tm=1024, tn=S_COLS, name="in_proj_small").reshape(bsz, t, S_COLS)

    o_a = gated_deltanet(p3, small3, conv_a_w, a_log_a, dt_bias_a, out_norm_a_g)

    qh3, kk3, ct3, ki3 = dsa_prep(p3, small3, w_uk, ckv_norm_g, q_norm_b_g, k_norm_b_g)
    dprev, ddiag = bias_tables
    o_b = dsa_attention(qh3, p3, small3, kk3, ct3, ki3, dprev, ddiag,
                        jnp.swapaxes(w_uv, 1, 2).astype(BF16))

    o_c = _conv_call(_conv_gated_kernel, p3, [P_C, P_C + BRANCH_WIDTH, P_C + 2 * BRANCH_WIDTH], BRANCH_WIDTH,
                     [conv_c_w.astype(F32)], BF16, 512, "conv_gated")

    xbc3 = _conv_call(_conv_xbc_kernel, p3, [P_DXBC], XBC_WIDTH,
                      [conv_d_w.astype(F32), conv_d_b.reshape(1, XBC_WIDTH).astype(F32)], F32, 512, "conv_xbc")
    o_d = mamba2_ssd(xbc3, p3, small3, a_log_d, dt_bias_d, d_skip, out_norm_d_g)

    outs = [o.reshape(m, BRANCH_WIDTH) for o in (o_a, o_b, o_c, o_d)]
    merged = merge_branches(xn, outs, w_gate, w_up_all, layer)
    return matmul(merged, w_out_all, out_dtype=F32, residual=x2, tm=1024, tn=512, layer=layer, name="out_proj")


def _half_k(k):
    return k // 2 if k % (2 * LANES) == 0 else k


def kernel(x, mix_norm_g, w_in, conv_a_w, a_log_a, dt_bias_a, out_norm_a_g, ckv_norm_g, w_uk, q_norm_b_g, k_norm_b_g, w_uv, rel_bias, conv_c_w, conv_d_w, conv_d_b, a_log_d, dt_bias_d, d_skip, out_norm_d_g, w_up, w_out, ffn_norm_g, w1_dense, w3_dense, w2_dense, w_router, w1_moe, w3_moe, w2_moe):
    bsz, t, d = x.shape
    depth = w_in.shape[0]
    x2 = x.reshape(bsz * t, d)
    bias_tables = relbias_tables(rel_bias)
    for l in range(depth):
        x2 = _mixer_layer(x2, bsz, t, mix_norm_g[l], w_in[l], conv_a_w[l], a_log_a[l], dt_bias_a[l],
                          out_norm_a_g[l], ckv_norm_g[l], w_uk[l], q_norm_b_g[l], k_norm_b_g[l], w_uv[l],
                          rel_bias, bias_tables, conv_c_w[l], conv_d_w[l], conv_d_b[l], a_log_d[l],
                          dt_bias_d[l], d_skip[l], out_norm_d_g[l], w_up, w_out, l)
        j = l // 2
        if l % 2 == 0:
            h = rmsnorm(x2, ffn_norm_g[l])
            hid = glu_dense(h, w1_dense[j], w3_dense[j])
            x2 = matmul(hid, w2_dense[j].astype(BF16), out_dtype=F32, residual=x2,
                        tm=512, tn=512, name="ffn_down")
        else:
            x2 = moe_layer(x2, ffn_norm_g[l], w_router[j], w1_moe[j], w3_moe[j], w2_moe[j])
    return x2.reshape(bsz, t, d)
```

```python
import functools

import jax
import jax.numpy as jnp
from jax import lax
from jax.experimental import pallas as pl
from jax.experimental.pallas import tpu as pltpu

F32 = jnp.float32
BF16 = jnp.bfloat16
I32 = jnp.int32
U32 = jnp.uint32
HIGHEST = lax.Precision.HIGHEST

D_MODEL = 4096
CHUNK = 64
N_BRANCHES = 4
BRANCH_WIDTH = D_MODEL // N_BRANCHES
DK_A = 128
H_A = BRANCH_WIDTH // DK_A
CONV_A = 4
DH_B = 128
H_B = BRANCH_WIDTH // DH_B
DC_B = 256
H_IDX = 8
D_IDX = 64
DSA_TOPK = 256
Q_BLOCK = 128
N_BUCKETS = 32
T5_MAX_DISTANCE = 128
CONV_C = 3
D_INNER = BRANCH_WIDTH
P_D = 64
H_D = D_INNER // P_D
N_GROUPS = 2
D_STATE = 128
CONV_D = 4
XBC_WIDTH = D_INNER + 2 * N_GROUPS * D_STATE
N_EXPERTS = 8
TOP_K = 2
EPS = 1e-6

MIX_SPLITS = (
    BRANCH_WIDTH, BRANCH_WIDTH, BRANCH_WIDTH, BRANCH_WIDTH, H_A, H_A,
    H_B * DH_B, DC_B, H_IDX * D_IDX, D_IDX, H_IDX,
    BRANCH_WIDTH, BRANCH_WIDTH, BRANCH_WIDTH,
    D_INNER, XBC_WIDTH, H_D,
)
MIX_COLS = sum(MIX_SPLITS)
_OFF = [0]
for _w in MIX_SPLITS:
    _OFF.append(_OFF[-1] + _w)
(O_AQ, O_AK, O_AV, O_AZ, O_AA, O_AB, O_BQ, O_BCKV, O_BQI, O_BKI, O_BWI,
 O_CB, O_CC, O_CH, O_DZ, O_DXBC, O_DDT, _O_END) = _OFF

P_AQKV = 0
P_AZ = 3072
P_BQ = 4096
P_C = 5120
P_DZ = 8192
P_DXBC = 9216
P_BQI = 10752
P_BCKV = 11264
P_COLS = 11520
S_AA = 0
S_AB = 8
S_BKI = 16
S_BWI = 80
S_DDT = 88
S_COLS = 128

LANES = 128
VMEM_LIMIT_MB = 56


def _cparams(sem, vmem_mb=VMEM_LIMIT_MB):
    return pltpu.CompilerParams(dimension_semantics=sem, vmem_limit_bytes=vmem_mb * 2 ** 20)


def _softplus(x):
    return jnp.maximum(x, 0.0) + jnp.log(1.0 + jnp.exp(-jnp.abs(x)))


def _sigmoid(x):
    return 1.0 / (1.0 + jnp.exp(-x))


def _silu(x):
    return x * _sigmoid(x)


def _dot(a, b, precision=None):
    return jnp.dot(a, b, preferred_element_type=F32, precision=precision)


def _dot_nt(a, b, precision=None):
    return lax.dot_general(a, b, (((1,), (1,)), ((), ())), preferred_element_type=F32, precision=precision)


def _dot_tn(a, b, precision=None):
    return lax.dot_general(a, b, (((0,), (0,)), ((), ())), preferred_element_type=F32, precision=precision)


def _rmsnorm_kernel(x_ref, g_ref, o_ref):
    x = x_ref[...].astype(F32)
    ms = jnp.mean(x * x, axis=-1, keepdims=True)
    o_ref[...] = (x * lax.rsqrt(ms + EPS) * g_ref[...]).astype(o_ref.dtype)


def rmsnorm(x, g, out_dtype=BF16, tm=512):
    m, d = x.shape
    tm = min(tm, m)
    return pl.pallas_call(
        _rmsnorm_kernel,
        grid=(m // tm,),
        in_specs=[pl.BlockSpec((tm, d), lambda i: (i, 0)), pl.BlockSpec((1, d), lambda i: (0, 0))],
        out_specs=pl.BlockSpec((tm, d), lambda i: (i, 0)),
        out_shape=jax.ShapeDtypeStruct((m, d), out_dtype),
        compiler_params=_cparams(("parallel",)),
        name="rmsnorm",
    )(x, g.reshape(1, d).astype(F32))


def _mm_kernel(*refs, nk, has_res):
    if has_res:
        a_ref, b_ref, r_ref = refs[:3]
        rest = refs[3:]
    else:
        a_ref, b_ref = refs[:2]
        r_ref = None
        rest = refs[2:]
    o_ref = rest[0]
    if nk == 1:
        acc = _dot(a_ref[...], b_ref[...].astype(BF16))
        if r_ref is not None:
            acc = acc + r_ref[...].astype(F32)
        o_ref[...] = acc.astype(o_ref.dtype)
        return
    acc_ref = rest[1]
    k = pl.program_id(2)

    @pl.when(k == 0)
    def _():
        acc_ref[...] = jnp.zeros_like(acc_ref)

    acc_ref[...] += _dot(a_ref[...], b_ref[...].astype(BF16))

    @pl.when(k == nk - 1)
    def _():
        acc = acc_ref[...]
        if r_ref is not None:
            acc = acc + r_ref[...].astype(F32)
        o_ref[...] = acc.astype(o_ref.dtype)


def matmul(a, b, *, out_dtype, residual=None, tm=1024, tn=512, tk=None, layer=None, name="matmul"):
    m, kdim = a.shape
    n = b.shape[-1]
    tm, tn = min(tm, m), min(tn, n)
    tk = kdim if tk is None else min(tk, kdim)
    nk = kdim // tk
    assert m % tm == 0 and n % tn == 0 and kdim % tk == 0, (a.shape, b.shape, tm, tn, tk)
    if layer is None:
        b_spec = pl.BlockSpec((tk, tn), lambda i, j, k: (k, j))
    else:
        b_spec = pl.BlockSpec((None, tk, tn), lambda i, j, k: (layer, k, j))
    in_specs = [pl.BlockSpec((tm, tk), lambda i, j, k: (i, k)), b_spec]
    args = [a, b]
    if residual is not None:
        in_specs.append(pl.BlockSpec((tm, tn), lambda i, j, k: (i, j)))
        args.append(residual)
    scratch = [pltpu.VMEM((tm, tn), F32)] if nk > 1 else []
    return pl.pallas_call(
        functools.partial(_mm_kernel, nk=nk, has_res=residual is not None),
        grid=(m // tm, n // tn, nk),
        in_specs=in_specs,
        out_specs=pl.BlockSpec((tm, tn), lambda i, j, k: (i, j)),
        out_shape=jax.ShapeDtypeStruct((m, n), out_dtype),
        scratch_shapes=scratch,
        compiler_params=_cparams(("parallel", "parallel", "arbitrary")),
        name=name,
    )(*args)


_PAD_ROWS = 8


def _causal_conv(stage_ref, x, w_ref, ksize):
    t = x.shape[0]
    stage_ref[0:_PAD_ROWS, :] = jnp.zeros((_PAD_ROWS, x.shape[1]), F32)
    stage_ref[_PAD_ROWS:_PAD_ROWS + t, :] = x
    acc = x * w_ref[ksize - 1:ksize, :]
    for j in range(ksize - 1):
        s = ksize - 1 - j
        acc = acc + stage_ref[_PAD_ROWS - s:_PAD_ROWS - s + t, :] * w_ref[j:j + 1, :]
    return acc


def _conv_xbc_kernel(x_ref, w_ref, b_ref, o_ref, stage_ref):
    y = _causal_conv(stage_ref, x_ref[...].astype(F32), w_ref, CONV_D) + b_ref[...]
    o_ref[...] = _silu(y)


def _conv_gated_kernel(bg_ref, cg_ref, h_ref, w_ref, o_ref, stage_ref):
    u = cg_ref[...].astype(F32) * h_ref[...].astype(F32)
    y = _causal_conv(stage_ref, u, w_ref, CONV_C)
    o_ref[...] = (bg_ref[...].astype(F32) * y).astype(o_ref.dtype)


def _conv_call(kernel, p3, col_offsets, width, extra, out_dtype, tc, name):
    bsz, t, _ = p3.shape
    tc = min(tc, width)
    in_specs = [pl.BlockSpec((None, t, tc), functools.partial(lambda b, c, o: (b, 0, o + c), o=off // tc))
                for off in col_offsets]
    args = [p3] * len(col_offsets)
    for e in extra:
        in_specs.append(pl.BlockSpec((e.shape[0], tc), lambda b, c: (0, c)))
        args.append(e)
    return pl.pallas_call(
        kernel,
        grid=(bsz, width // tc),
        in_specs=in_specs,
        out_specs=pl.BlockSpec((None, t, tc), lambda b, c: (b, 0, c)),
        out_shape=jax.ShapeDtypeStruct((bsz, t, width), out_dtype),
        scratch_shapes=[pltpu.VMEM((t + _PAD_ROWS, tc), F32)],
        compiler_params=_cparams(("parallel", "parallel")),
        name=name,
    )(*args)


def _tril_mask(n, strict=False):
    r = lax.broadcasted_iota(I32, (n, n), 0)
    c = lax.broadcasted_iota(I32, (n, n), 1)
    return (r > c) if strict else (r >= c)


def _chunk_cumsum(x):
    tril = _tril_mask(CHUNK).astype(F32)
    cs = _dot(tril, x, HIGHEST)
    padded = jnp.concatenate([cs, jnp.zeros((LANES - CHUNK, LANES), F32)], axis=0)
    return cs, padded.T[:, :CHUNK]


def _segment_decay(cs, cs_t, c):
    tril = _tril_mask(CHUNK)
    diff = cs[:, c:c + 1] - cs_t[c:c + 1, :]
    return jnp.where(tril, jnp.exp(jnp.where(tril, diff, 0.0)), 0.0)


_INV_BLOCK = 16
_GDN_GROUP = 2
_GDN_CHUNKS = 4
_CONV_COLS = 512
_CHUNK_SHIFT = CHUNK.bit_length() - 1


def _split_bf16(a):
    hi = a.astype(BF16)
    return hi, (a - hi.astype(F32)).astype(BF16)


def _dot3(a, b):
    a_hi, a_lo = a if isinstance(a, tuple) else _split_bf16(a)
    b_hi, b_lo = b if isinstance(b, tuple) else _split_bf16(b)
    return _dot(a_hi, b_hi) + _dot(a_hi, b_lo) + _dot(a_lo, b_hi)


def _unit_lower_inverses(lows):
    n_rows = lows[0].shape[0]
    r = lax.broadcasted_iota(I32, (n_rows, n_rows), 0)
    c = lax.broadcasted_iota(I32, (n_rows, n_rows), 1)
    shift = _INV_BLOCK.bit_length() - 1
    same = (r >> shift) == (c >> shift)
    eye = (r == c).astype(F32)
    each = lambda fn, *lists: [fn(*args) for args in zip(*lists)]
    lds = each(lambda low: jnp.where(same, low, 0.0), lows)
    los = each(lambda low: jnp.where(same, 0.0, low), lows)
    ps = each(lambda ld: eye - ld, lds)
    xss = each(_split_bf16, lds)
    xs = each(_dot3, xss, xss)
    for _ in range(2):
        xss = each(_split_bf16, xs)
        ps = each(lambda p, x: p + _dot3(p, x), ps, xss)
        xs = each(_dot3, xss, xss)
    ps = each(lambda p, x: p + _dot3(p, x), ps, xs)
    pss = each(_split_bf16, ps)
    ns = each(_dot3, pss, los)
    nss = each(_split_bf16, ns)
    n2s = each(_dot3, nss, nss)
    rrs = each(lambda n, n2: (eye - n) + _dot3(eye - n, n2), ns, n2s)
    return each(_dot3, rrs, pss)


def _gdn_kernel(pq_ref, pk_ref, pv_ref, z_ref, sm_ref, cw_ref, alog_ref, dtb_ref, ng_ref, o_ref,
                state_ref, stage_ref, qkv_ref):
    tt = sm_ref.shape[0]

    @pl.when(pl.program_id(1) == 0)
    def _():
        state_ref[...] = jnp.zeros_like(state_ref)
        stage_ref[0:_PAD_ROWS, :] = jnp.zeros((_PAD_ROWS, stage_ref.shape[1]), F32)

    for idx, ref in enumerate((pq_ref, pk_ref, pv_ref)):
        stage_ref[_PAD_ROWS:_PAD_ROWS + tt, idx * BRANCH_WIDTH:(idx + 1) * BRANCH_WIDTH] = ref[...].astype(F32)
    for c0 in range(0, 3 * BRANCH_WIDTH, _CONV_COLS):
        cols = slice(c0, c0 + _CONV_COLS)
        acc = stage_ref[_PAD_ROWS:_PAD_ROWS + tt, cols] * cw_ref[CONV_A - 1:CONV_A, cols]
        for j in range(CONV_A - 1):
            s = CONV_A - 1 - j
            acc = acc + stage_ref[_PAD_ROWS - s:_PAD_ROWS - s + tt, cols] * cw_ref[j:j + 1, cols]
        y = _silu(acc)
        for g0 in range(0, _CONV_COLS, DK_A):
            ys = y[:, g0:g0 + DK_A]
            if c0 < 2 * BRANCH_WIDTH:
                inv = lax.rsqrt(jnp.sum(ys * ys, axis=-1, keepdims=True) + EPS)
                ys = ys * (inv * DK_A ** -0.5 if c0 < BRANCH_WIDTH else inv)
            qkv_ref[:, c0 + g0:c0 + g0 + DK_A] = ys
    stage_ref[0:_PAD_ROWS, :] = stage_ref[tt:tt + _PAD_ROWS, :]
    q_ref = qkv_ref.at[:, 0:BRANCH_WIDTH]
    k_ref = qkv_ref.at[:, BRANCH_WIDTH:2 * BRANCH_WIDTH]
    v_ref = qkv_ref.at[:, 2 * BRANCH_WIDTH:3 * BRANCH_WIDTH]

    ng = ng_ref[...]
    lane = lax.broadcasted_iota(I32, (CHUNK, LANES), 1)
    rows = _GDN_GROUP * CHUNK
    r = lax.broadcasted_iota(I32, (rows, rows), 0)
    c = lax.broadcasted_iota(I32, (rows, rows), 1)
    same_head = (r >> _CHUNK_SHIFT) == (c >> _CHUNK_SHIFT)
    tril = same_head & (r >= c)
    strict = same_head & (r > c)
    groups = [range(grp * _GDN_GROUP, (grp + 1) * _GDN_GROUP) for grp in range(H_A // _GDN_GROUP)]

    n_chunks = sm_ref.shape[0] // CHUNK
    e_lasts, problems = [], []
    for ci in range(n_chunks):
        ts = slice(ci * CHUNK, (ci + 1) * CHUNK)
        sm = sm_ref[ts, :]
        g = -jnp.exp(alog_ref[...]) * _softplus(sm + dtb_ref[...])
        g = jnp.where(lane < S_AA + H_A, g, 0.0)
        beta = _sigmoid(sm)
        gc, gc_t = _chunk_cumsum(g)
        g_last = gc[CHUNK - 1:CHUNK, :]
        e_gc = jnp.exp(gc)
        e_rem = jnp.exp(g_last - gc)
        e_lasts.append(jnp.exp(g_last))
        for heads in groups:
            stack = lambda ref: jnp.concatenate([ref[ts, h * DK_A:(h + 1) * DK_A] for h in heads], axis=0)
            col = lambda arr, off: jnp.concatenate([arr[:, off + h:off + h + 1] for h in heads], axis=0)
            q_st, k_st, v_st = stack(q_ref), stack(k_ref), stack(v_ref)
            beta_st, egc_st, erem_st = col(beta, S_AB), col(e_gc, S_AA), col(e_rem, S_AA)
            g_row = jnp.concatenate([gc_t[S_AA + h:S_AA + h + 1, :] for h in heads], axis=1)
            diff = col(gc, S_AA) - g_row
            decay = jnp.where(tril, jnp.exp(jnp.where(tril, diff, 0.0)), 0.0)
            kb_st = k_st * beta_st
            problems.append(dict(q=q_st, k=k_st, kb=kb_st, decay=decay, qd=q_st * egc_st, kd=k_st * erem_st,
                                 rhs=jnp.concatenate([v_st * beta_st, kb_st * egc_st], axis=1)))
    lows = [jnp.where(strict, _dot_nt(p["kb"], p["k"]) * p["decay"], 0.0) for p in problems]
    tinvs = _unit_lower_inverses(lows)
    uws = [_dot3(tinv, p["rhs"]) for tinv, p in zip(tinvs, problems)]
    intras = [_dot_nt(p["q"], p["k"]) * p["decay"] for p in problems]

    head_rows = lambda gi_j: slice(gi_j * CHUNK, (gi_j + 1) * CHUNK)
    for ci in range(n_chunks):
        ts = slice(ci * CHUNK, (ci + 1) * CHUNK)
        e_last = e_lasts[ci]
        probs = range(ci * len(groups), (ci + 1) * len(groups))
        states = [state_ref[h] for h in range(H_A)]
        wq_s = {}
        for pi, heads in zip(probs, groups):
            for j, h in enumerate(heads):
                rs = head_rows(j)
                lhs = jnp.concatenate([uws[pi][rs, DK_A:], problems[pi]["qd"][rs]], axis=0)
                wq_s[h] = _dot(lhs, states[h])
        v_new = {h: uws[pi][head_rows(j), :DK_A] - wq_s[h][:CHUNK]
                 for pi, heads in zip(probs, groups) for j, h in enumerate(heads)}
        outs = [jnp.concatenate([wq_s[h][CHUNK:] for h in heads], axis=0)
                + _dot(intras[pi], jnp.concatenate([v_new[h] for h in heads], axis=0))
                for pi, heads in zip(probs, groups)]
        for pi, heads in zip(probs, groups):
            for j, h in enumerate(heads):
                ca = S_AA + h
                state_ref[h] = states[h] * e_last[:, ca:ca + 1] + _dot_tn(problems[pi]["kd"][head_rows(j)], v_new[h])
        for out, heads in zip(outs, groups):
            ms = jnp.mean(out * out, axis=-1, keepdims=True)
            out = out * lax.rsqrt(ms + EPS) * ng
            for j, h in enumerate(heads):
                sl = slice(h * DK_A, (h + 1) * DK_A)
                o_ref[ts, sl] = (out[head_rows(j)] * _silu(z_ref[ts, sl].astype(F32))).astype(o_ref.dtype)


def gated_deltanet(p3, small3, conv_w, a_log, dt_bias, norm_g):
    bsz, t, _ = p3.shape
    tt = _GDN_CHUNKS * CHUNK if t % (_GDN_CHUNKS * CHUNK) == 0 else CHUNK

    def row(vals, off):
        return jnp.zeros((1, S_COLS), F32).at[0, off:off + vals.shape[0]].set(vals.astype(F32))

    w = BRANCH_WIDTH
    return pl.pallas_call(
        _gdn_kernel,
        grid=(bsz, t // tt),
        in_specs=[
            pl.BlockSpec((None, tt, w), lambda b, n: (b, n, 0)),
            pl.BlockSpec((None, tt, w), lambda b, n: (b, n, 1)),
            pl.BlockSpec((None, tt, w), lambda b, n: (b, n, 2)),
            pl.BlockSpec((None, tt, w), lambda b, n: (b, n, P_AZ // w)),
            pl.BlockSpec((None, tt, S_COLS), lambda b, n: (b, n, 0)),
            pl.BlockSpec((CONV_A, 3 * w), lambda b, n: (0, 0)),
            pl.BlockSpec((1, S_COLS), lambda b, n: (0, 0)),
            pl.BlockSpec((1, S_COLS), lambda b, n: (0, 0)),
            pl.BlockSpec((1, DK_A), lambda b, n: (0, 0)),
        ],
        out_specs=pl.BlockSpec((None, tt, w), lambda b, n: (b, n, 0)),
        out_shape=jax.ShapeDtypeStruct((bsz, t, w), BF16),
        scratch_shapes=[pltpu.VMEM((H_A, DK_A, DK_A), F32), pltpu.VMEM((tt + _PAD_ROWS, 3 * w), F32),
                        pltpu.VMEM((tt, 3 * w), F32)],
        compiler_params=_cparams(("parallel", "arbitrary")),
        name="gated_deltanet",
    )(p3, p3, p3, p3, small3, conv_w.astype(F32), row(a_log, S_AA), row(dt_bias, S_AA),
      norm_g.reshape(1, DK_A).astype(F32))


def _ssd_kernel(xbc_ref, z_ref, sm_ref, alog_ref, dtb_ref, dskip_ref, ng_ref, o_ref, state_ref):
    @pl.when(pl.program_id(1) == 0)
    def _():
        state_ref[...] = jnp.zeros_like(state_ref)

    sm = sm_ref[...]
    lane = lax.broadcasted_iota(I32, (CHUNK, LANES), 1)
    dt = _softplus(sm + dtb_ref[...])
    a = jnp.where((lane >= S_DDT) & (lane < S_DDT + H_D), dt * -jnp.exp(alog_ref[...]), 0.0)
    acs, acs_t = _chunk_cumsum(a)
    a_last = acs[CHUNK - 1:CHUNK, :]
    e_acs = jnp.exp(acs)
    e_rem = jnp.exp(a_last - acs)
    e_last = jnp.exp(a_last)
    heads_per_group = H_D // N_GROUPS
    group_of = lambda h: h // heads_per_group
    bms = [xbc_ref[:, D_INNER + grp * D_STATE:D_INNER + (grp + 1) * D_STATE] for grp in range(N_GROUPS)]
    c0 = D_INNER + N_GROUPS * D_STATE
    cms = [xbc_ref[:, c0 + grp * D_STATE:c0 + (grp + 1) * D_STATE] for grp in range(N_GROUPS)]
    cbs = [_dot_nt(cm, bm) for cm, bm in zip(cms, bms)]
    xcs = [xbc_ref[:, h * P_D:(h + 1) * P_D] * dt[:, S_DDT + h:S_DDT + h + 1] for h in range(H_D)]
    states = [state_ref[h] for h in range(H_D)]
    ys = []
    for h in range(H_D):
        c = S_DDT + h
        ys.append(_dot(cbs[group_of(h)] * _segment_decay(acs, acs_t, c), xcs[h])
                  + _dot_nt(cms[group_of(h)] * e_acs[:, c:c + 1], states[h]))
    for h in range(H_D):
        c = S_DDT + h
        state_ref[h] = states[h] * e_last[:, c:c + 1] + _dot_tn(xcs[h], bms[group_of(h)] * e_rem[:, c:c + 1])
    y = jnp.concatenate(ys, axis=1) + dskip_ref[...] * xbc_ref[:, 0:D_INNER]
    y = y * _silu(z_ref[...].astype(F32))
    gw = D_INNER // N_GROUPS
    for grp in range(N_GROUPS):
        yg = y[:, grp * gw:(grp + 1) * gw]
        ms = jnp.mean(yg * yg, axis=-1, keepdims=True)
        o_ref[:, grp * gw:(grp + 1) * gw] = (yg * lax.rsqrt(ms + EPS) * ng_ref[:, grp * gw:(grp + 1) * gw]).astype(o_ref.dtype)


def mamba2_ssd(xbc3, p3, small3, a_log, dt_bias, d_skip, norm_g):
    bsz, t, _ = xbc3.shape
    nchunks = t // CHUNK

    def row(vals, off):
        return jnp.zeros((1, S_COLS), F32).at[0, off:off + vals.shape[0]].set(vals.astype(F32))

    return pl.pallas_call(
        _ssd_kernel,
        grid=(bsz, nchunks),
        in_specs=[
            pl.BlockSpec((None, CHUNK, XBC_WIDTH), lambda b, n: (b, n, 0)),
            pl.BlockSpec((None, CHUNK, D_INNER), lambda b, n: (b, n, P_DZ // D_INNER)),
            pl.BlockSpec((None, CHUNK, S_COLS), lambda b, n: (b, n, 0)),
            pl.BlockSpec((1, S_COLS), lambda b, n: (0, 0)),
            pl.BlockSpec((1, S_COLS), lambda b, n: (0, 0)),
            pl.BlockSpec((1, D_INNER), lambda b, n: (0, 0)),
            pl.BlockSpec((1, D_INNER), lambda b, n: (0, 0)),
        ],
        out_specs=pl.BlockSpec((None, CHUNK, D_INNER), lambda b, n: (b, n, 0)),
        out_shape=jax.ShapeDtypeStruct((bsz, t, D_INNER), BF16),
        scratch_shapes=[pltpu.VMEM((H_D, P_D, D_STATE), F32)],
        compiler_params=_cparams(("parallel", "arbitrary")),
        name="mamba2_ssd",
    )(xbc3, p3, small3, row(a_log, S_DDT), row(dt_bias, S_DDT),
      jnp.repeat(d_skip.astype(F32), P_D).reshape(1, D_INNER), norm_g.reshape(1, D_INNER).astype(F32))


def _dsa_prep_kernel(q_ref, ckv_ref, sm_ref, wuk_ref, cg_ref, qg_ref, kg_ref, qh_ref, kk_ref, ct_ref, ki_ref):
    ckv = ckv_ref[...].astype(F32)
    c = ckv * lax.rsqrt(jnp.mean(ckv * ckv, axis=-1, keepdims=True) + EPS) * cg_ref[...]
    cb = c.astype(BF16)
    ct_ref[...] = c.T.astype(BF16)
    kk = _dot(cb, wuk_ref[...])
    kk_ref[...] = (kk * lax.rsqrt(jnp.mean(kk * kk, axis=-1, keepdims=True) + EPS) * kg_ref[...]).astype(BF16)
    for h in range(H_B):
        sl = slice(h * DH_B, (h + 1) * DH_B)
        qh = q_ref[:, sl].astype(F32)
        qh = qh * lax.rsqrt(jnp.mean(qh * qh, axis=-1, keepdims=True) + EPS) * qg_ref[...]
        qh_ref[:, sl] = (qh * DH_B ** -0.5).astype(BF16)
    ki_ref[...] = sm_ref[:, S_BKI:S_BKI + D_IDX].astype(BF16)


def dsa_prep(p3, small3, w_uk, ckv_g, q_g, k_g, tt=512):
    bsz, t, _ = p3.shape
    tt = min(tt, t)
    w = H_B * DH_B
    return pl.pallas_call(
        _dsa_prep_kernel,
        grid=(bsz, t // tt),
        in_specs=[
            pl.BlockSpec((None, tt, w), lambda b, i: (b, i, P_BQ // w)),
            pl.BlockSpec((None, tt, DC_B), lambda b, i: (b, i, P_BCKV // DC_B)),
            pl.BlockSpec((None, tt, S_COLS), lambda b, i: (b, i, 0)),
            pl.BlockSpec((DC_B, DH_B), lambda b, i: (0, 0)),
            pl.BlockSpec((1, DC_B), lambda b, i: (0, 0)),
            pl.BlockSpec((1, DH_B), lambda b, i: (0, 0)),
            pl.BlockSpec((1, DH_B), lambda b, i: (0, 0)),
        ],
        out_specs=[
            pl.BlockSpec((None, tt, w), lambda b, i: (b, i, 0)),
            pl.BlockSpec((None, tt, DH_B), lambda b, i: (b, i, 0)),
            pl.BlockSpec((None, DC_B, tt), lambda b, i: (b, 0, i)),
            pl.BlockSpec((None, tt, D_IDX), lambda b, i: (b, i, 0)),
        ],
        out_shape=[
            jax.ShapeDtypeStruct((bsz, t, w), BF16),
            jax.ShapeDtypeStruct((bsz, t, DH_B), BF16),
            jax.ShapeDtypeStruct((bsz, DC_B, t), BF16),
            jax.ShapeDtypeStruct((bsz, t, D_IDX), BF16),
        ],
        compiler_params=_cparams(("parallel", "parallel")),
        name="dsa_prep",
    )(p3, p3, small3, w_uk.astype(BF16), ckv_g.reshape(1, DC_B).astype(F32),
      q_g.reshape(1, DH_B).astype(F32), k_g.reshape(1, DH_B).astype(F32))


_T5_HALF = N_BUCKETS // 2
_T5_EXACT = _T5_HALF // 2
_T5_FAR = _T5_HALF - 1


def _relbias_kernel(rb_ref, prev_ref, diag_ref):
    h = pl.program_id(0)
    kl = lax.broadcasted_iota(I32, (Q_BLOCK, Q_BLOCK), 0)
    ql = lax.broadcasted_iota(I32, (Q_BLOCK, Q_BLOCK), 1)
    far = rb_ref[_T5_FAR, h]
    for ref, shift in ((prev_ref, -Q_BLOCK), (diag_ref, 0)):
        rel = kl - ql + shift
        n = jnp.abs(rel)
        n2 = n * n
        steps = jnp.zeros_like(n)
        for j in range(1, _T5_HALF - _T5_EXACT):
            steps = steps + (n2 >= (_T5_EXACT * _T5_EXACT) * 2 ** j).astype(I32)
        large = jnp.minimum(_T5_EXACT + steps, _T5_HALF - 1)
        bucket = jnp.where(rel > 0, _T5_HALF, 0) + jnp.where(n < _T5_EXACT, n, large)
        acc = jnp.zeros((Q_BLOCK, Q_BLOCK), F32)
        for b in range(N_BUCKETS):
            acc = jnp.where(bucket == b, rb_ref[b, h], acc)
        ref[...] = acc - far


def relbias_tables(rel_bias):
    shp = jax.ShapeDtypeStruct((H_B, Q_BLOCK, Q_BLOCK), F32)
    spec = pl.BlockSpec((None, Q_BLOCK, Q_BLOCK), lambda h: (h, 0, 0))
    return pl.pallas_call(
        _relbias_kernel,
        grid=(H_B,),
        in_specs=[pl.BlockSpec(memory_space=pltpu.SMEM)],
        out_specs=[spec, spec],
        out_shape=[shp, shp],
        compiler_params=_cparams(("arbitrary",)),
        name="relbias_tables",
    )(rel_bias.astype(F32))


_INT_MIN = -2 ** 31
_SCORE_ROWS = 256
_SWEEP_ROWS = 64
_DSA_WIDTH_STEP = 512


def _sweep_rows(n_rows, init, step):
    def body(c, acc):
        return step(pl.multiple_of(c * _SWEEP_ROWS, _SWEEP_ROWS), acc)
    n_steps = n_rows // _SWEEP_ROWS
    return lax.fori_loop(0, n_steps, body, init, unroll=min(8, n_steps))


def _dsa_kernel(qh_ref, qi_ref, sm_ref, kk_ref, ct_ref, ki_ref, dprev_ref, ddiag_ref, wuvt_ref,
                o_ref, key_ref, selb_ref, lg_ref, p_ref, *, widths, topk):
    i = pl.program_id(1)
    need = (i + 1) * Q_BLOCK
    wi_t = sm_ref[...].T[S_BWI:S_BWI + H_IDX, :] * (H_IDX ** -0.5 * D_IDX ** -0.5)
    qi_all = jnp.concatenate([qi_ref[:, h * D_IDX:(h + 1) * D_IDX] for h in range(H_IDX)], axis=0)
    q_chunk = (i * Q_BLOCK + lax.broadcasted_iota(I32, (1, Q_BLOCK), 1)) >> _CHUNK_SHIFT
    row_iota = lax.broadcasted_iota(I32, (_SWEEP_ROWS, Q_BLOCK), 0)
    block = lambda ref, r0: ref[pl.ds(r0, _SWEEP_ROWS), :]

    def body(nc):
        for r0 in range(0, nc, min(_SCORE_ROWS, nc)):
            rows = slice(r0, r0 + min(_SCORE_ROWS, nc))
            s_all = _dot_nt(ki_ref[rows, :], qi_all)
            sc = None
            for h in range(H_IDX):
                term = wi_t[h:h + 1, :] * jnp.maximum(s_all[:, h * Q_BLOCK:(h + 1) * Q_BLOCK], 0.0)
                sc = term if sc is None else sc + term
            bits = lax.bitcast_convert_type(sc + 0.0, I32)
            key = jnp.where(bits < 0, bits ^ 0x7FFFFFFF, bits)
            kpos = r0 + lax.broadcasted_iota(I32, key.shape, 0)
            key_ref[rows, :] = jnp.where((kpos >> _CHUNK_SHIFT) <= q_chunk, key, _INT_MIN)

        def count(pred):
            acc = _sweep_rows(nc, jnp.zeros((_SWEEP_ROWS, Q_BLOCK), I32),
                              lambda r0, acc: acc + pred(r0, block(key_ref, r0)).astype(I32))
            return jnp.sum(acc, axis=0, keepdims=True)

        tau = jnp.where(count(lambda r0, k: k >= 0) >= topk, 0, _INT_MIN).astype(I32)

        def vstep(it, tau):
            cand = tau | (jnp.int32(1) << (30 - it))
            return jnp.where(count(lambda r0, k: k >= cand) >= topk, cand, tau)

        tau = lax.fori_loop(0, 31, vstep, tau)
        zeros = jnp.zeros((_SWEEP_ROWS, Q_BLOCK), I32)
        acc_gt, acc_eq = _sweep_rows(
            nc, (zeros, zeros),
            lambda r0, acc: (acc[0] + (block(key_ref, r0) > tau).astype(I32),
                             acc[1] + (block(key_ref, r0) == tau).astype(I32)))
        n_gt = jnp.sum(acc_gt, axis=0, keepdims=True)
        n_eq = jnp.sum(acc_eq, axis=0, keepdims=True)
        tied = (n_gt + n_eq > topk) & (tau > _INT_MIN)
        nbits = max(1, (nc - 1).bit_length())

        def last_tied_index():
            room = topk - n_gt

            def istep(it, last):
                cand = last | (jnp.int32(1) << (nbits - 1 - it))
                below = count(lambda r0, k: (k == tau) & (r0 + row_iota < cand))
                return jnp.where(below < room, cand, last)

            return lax.fori_loop(0, nbits, istep, jnp.zeros((1, Q_BLOCK), I32))

        last = lax.cond(jnp.max(tied.astype(I32)) > 0, last_tied_index,
                        lambda: jnp.full((1, Q_BLOCK), nc, I32))

        def write_sel(r0, carry):
            k = block(key_ref, r0)
            sel = (k > _INT_MIN) & ((k > tau) | ((k == tau) & (r0 + row_iota <= last)))
            selb_ref[pl.ds(r0, _SWEEP_ROWS), :] = jnp.where(sel, 0.0, -jnp.inf)
            return carry

        _sweep_rows(nc, 0, write_sel)

        kk = kk_ref[0:nc, :]
        ct = ct_ref[:, 0:nc]
        prev_row = pl.multiple_of(jnp.maximum(i - 1, 0) * Q_BLOCK, Q_BLOCK)
        diag_row = pl.multiple_of(i * Q_BLOCK, Q_BLOCK)
        has_prev = (i > 0).astype(F32)
        for h0 in range(0, H_B, 2):
            q_pair = jnp.concatenate([qh_ref[:, h * DH_B:(h + 1) * DH_B] for h in (h0, h0 + 1)], axis=0)
            lg_pair = _dot_nt(kk, q_pair)
            for j, h in enumerate((h0, h0 + 1)):
                lg_ref[h, 0:nc, :] = lg_pair[:, j * Q_BLOCK:(j + 1) * Q_BLOCK] + selb_ref[0:nc, :]
                lg_ref[h, pl.ds(prev_row, Q_BLOCK), :] += dprev_ref[h] * has_prev
                lg_ref[h, pl.ds(diag_row, Q_BLOCK), :] += ddiag_ref[h]
        blk = min(_SCORE_ROWS, nc)
        outs = []
        for h in range(H_B):
            slot = h % 2
            m_acc = None
            for r0 in range(0, nc, blk):
                x = lg_ref[h, r0:r0 + blk, :]
                m_acc = x if m_acc is None else jnp.maximum(m_acc, x)
            m = jnp.max(m_acc, axis=0, keepdims=True)
            p_acc = None
            for r0 in range(0, nc, blk):
                p = jnp.exp(lg_ref[h, r0:r0 + blk, :] - m)
                p_ref[slot, r0:r0 + blk, :] = p.astype(BF16)
                p_acc = p if p_acc is None else p_acc + p
            denom = jnp.sum(p_acc, axis=0, keepdims=True)
            o_lat = _dot(ct, p_ref[slot, 0:nc, :]) * (1.0 / denom)
            outs.append(_dot(wuvt_ref[h], o_lat.astype(BF16)).T)
        o_ref[...] = jnp.concatenate(outs, axis=1).astype(o_ref.dtype)

    lo = 0
    for nc in widths:
        @pl.when((need > lo) & (need <= nc))
        def _(nc=nc):
            body(nc)
        lo = nc


def dsa_attention(qh3, qi_p3, small3, kk3, ct3, ki3, dprev, ddiag, wuv_t):
    bsz, t, w = qh3.shape
    topk = min(DSA_TOPK, t // 4)
    step = min(_DSA_WIDTH_STEP, t)
    widths = tuple(range(step, t + 1, step))
    wq = H_IDX * D_IDX
    full = lambda shape: pl.BlockSpec(shape, lambda b, i: (0,) * len(shape))
    return pl.pallas_call(
        functools.partial(_dsa_kernel, widths=widths, topk=topk),
        grid=(bsz, t // Q_BLOCK),
        in_specs=[
            pl.BlockSpec((None, Q_BLOCK, w), lambda b, i: (b, i, 0)),
            pl.BlockSpec((None, Q_BLOCK, wq), lambda b, i: (b, i, P_BQI // wq)),
            pl.BlockSpec((None, Q_BLOCK, S_COLS), lambda b, i: (b, i, 0)),
            pl.BlockSpec((None, t, DH_B), lambda b, i: (b, 0, 0)),
            pl.BlockSpec((None, DC_B, t), lambda b, i: (b, 0, 0)),
            pl.BlockSpec((None, t, D_IDX), lambda b, i: (b, 0, 0)),
            full((H_B, Q_BLOCK, Q_BLOCK)),
            full((H_B, Q_BLOCK, Q_BLOCK)),
            full((H_B, DH_B, DC_B)),
        ],
        out_specs=pl.BlockSpec((None, Q_BLOCK, w), lambda b, i: (b, i, 0)),
        out_shape=jax.ShapeDtypeStruct((bsz, t, w), BF16),
        scratch_shapes=[pltpu.VMEM((t, Q_BLOCK), I32), pltpu.VMEM((t, Q_BLOCK), F32),
                        pltpu.VMEM((H_B, t, Q_BLOCK), F32), pltpu.VMEM((2, t, Q_BLOCK), BF16)],
        compiler_params=_cparams(("parallel", "arbitrary")),
        name="dsa_attention",
    )(qh3, qi_p3, small3, kk3, ct3, ki3, dprev, ddiag, wuv_t)


def _merge_kernel(xn_ref, o0_ref, o1_ref, o2_ref, o3_ref, g0_ref, g1_ref, g2_ref, g3_ref,
                  u0_ref, u1_ref, u2_ref, u3_ref, out_ref):
    xn = xn_ref[...]
    acc = None
    for o_ref, g_ref, u_ref in ((o0_ref, g0_ref, u0_ref), (o1_ref, g1_ref, u1_ref),
                                (o2_ref, g2_ref, u2_ref), (o3_ref, g3_ref, u3_ref)):
        term = _sigmoid(_dot(xn, g_ref[...])) * _dot(o_ref[...], u_ref[...].astype(BF16))
        acc = term if acc is None else acc + term
    out_ref[...] = acc.astype(out_ref.dtype)


def merge_branches(xn, outs, w_gate, w_up, layer, tm=512, tn=256):
    m, d = xn.shape
    tm, tn = min(tm, m), min(tn, d)
    nj = d // tn
    in_specs = [pl.BlockSpec((tm, d), lambda j, i: (i, 0))]
    in_specs += [pl.BlockSpec((tm, BRANCH_WIDTH), lambda j, i: (i, 0)) for _ in range(N_BRANCHES)]
    in_specs += [pl.BlockSpec((d, tn), functools.partial(lambda j, i, br: (0, br * nj + j), br=br))
                 for br in range(N_BRANCHES)]
    in_specs += [pl.BlockSpec((None, None, BRANCH_WIDTH, tn),
                              functools.partial(lambda j, i, br: (layer, br, 0, j), br=br))
                 for br in range(N_BRANCHES)]
    return pl.pallas_call(
        _merge_kernel,
        grid=(nj, m // tm),
        in_specs=in_specs,
        out_specs=pl.BlockSpec((tm, tn), lambda j, i: (i, j)),
        out_shape=jax.ShapeDtypeStruct((m, d), BF16),
        compiler_params=_cparams(("parallel", "parallel")),
        name="merge_branches",
    )(xn, *outs, *([w_gate] * N_BRANCHES), *([w_up] * N_BRANCHES))


def _glu_kernel(x_ref, w1_ref, w3_ref, o_ref):
    x = x_ref[...]
    o_ref[...] = (_silu(_dot(x, w1_ref[...].astype(BF16))) * _dot(x, w3_ref[...].astype(BF16))).astype(o_ref.dtype)


def glu_dense(x, w1, w3, tm=2048, tf=256):
    m, d = x.shape
    f = w1.shape[1]
    tm, tf = min(tm, m), min(tf, f)
    assert f % tf == 0
    return pl.pallas_call(
        _glu_kernel,
        grid=(f // tf, m // tm),
        in_specs=[pl.BlockSpec((tm, d), lambda j, i: (i, 0)),
                  pl.BlockSpec((d, tf), lambda j, i: (0, j)),
                  pl.BlockSpec((d, tf), lambda j, i: (0, j))],
        out_specs=pl.BlockSpec((tm, tf), lambda j, i: (i, j)),
        out_shape=jax.ShapeDtypeStruct((m, f), BF16),
        compiler_params=_cparams(("parallel", "parallel")),
        name="glu_dense",
    )(x, w1, w3)


MOE_TM = 512
MOE_TT = 256
MOE_UP_TN = 512
MOE_DOWN_TN = 1024
SLOT_POS_A, SLOT_POS_B, SLOT_W_A, SLOT_W_B = 0, 1, 2, 3
_ROW_COPY_UNROLL = 8


def _pack_bf16_pair(lo, hi):
    bits = lambda v: lax.bitcast_convert_type(v.astype(BF16).astype(F32), U32)
    return bits(hi) | (bits(lo) >> 16)


def _unpack_bf16_pair(u):
    return (lax.bitcast_convert_type(u << 16, F32),
            lax.bitcast_convert_type(u & jnp.uint32(0xFFFF0000), F32))


def _router_kernel(x_ref, g_ref, wr_ref, h_ref, gate_ref, sel_ref):
    x = x_ref[...]
    h = x * lax.rsqrt(jnp.mean(x * x, axis=-1, keepdims=True) + EPS) * g_ref[...]
    half = h.shape[1] // 2
    h_ref[...] = _pack_bf16_pair(h[:, :half], h[:, half:])
    logits = _dot3(h, wr_ref[...])
    lane = lax.broadcasted_iota(I32, logits.shape, 1)
    logits = jnp.where(lane < N_EXPERTS, logits, -jnp.inf)
    m1 = jnp.max(logits, axis=-1, keepdims=True)
    i1 = jnp.min(jnp.where(logits == m1, lane, LANES), axis=-1, keepdims=True)
    rest = jnp.where(lane == i1, -jnp.inf, logits)
    m2 = jnp.max(rest, axis=-1, keepdims=True)
    i2 = jnp.min(jnp.where(rest == m2, lane, LANES), axis=-1, keepdims=True)
    e2 = jnp.exp(m2 - m1)
    inv = 1.0 / (1.0 + e2)
    gate_ref[...] = jnp.where(lane == i1, inv, 0.0) + jnp.where(lane == i2, e2 * inv, 0.0)
    sel_ref[...] = jnp.where((lane == i1) | (lane == i2), 1.0, 0.0).astype(sel_ref.dtype)


def moe_router(x, g, w_router, tm=512):
    m, d = x.shape
    tm = min(tm, m)
    wr = jnp.zeros((d, LANES), F32).at[:, :N_EXPERTS].set(w_router.astype(F32))
    row = lambda w: pl.BlockSpec((tm, w), lambda i: (i, 0))
    return pl.pallas_call(
        _router_kernel,
        grid=(m // tm,),
        in_specs=[row(d), pl.BlockSpec((1, d), lambda i: (0, 0)), pl.BlockSpec((d, LANES), lambda i: (0, 0))],
        out_specs=[row(d // 2), row(LANES), row(LANES)],
        out_shape=[jax.ShapeDtypeStruct((m, d // 2), U32), jax.ShapeDtypeStruct((m, LANES), F32),
                   jax.ShapeDtypeStruct((m, LANES), BF16)],
        compiler_params=_cparams(("parallel",)),
        name="moe_router",
    )(x, g.reshape(1, d).astype(F32), wr)


def _rank_kernel(sel_ref, rank_ref, cnt_ref, carry_ref):
    @pl.when(pl.program_id(0) == 0)
    def _():
        carry_ref[...] = jnp.zeros_like(carry_ref)

    sel = sel_ref[...]
    n = sel.shape[0]
    earlier = _tril_mask(n, strict=True).astype(BF16)
    rank_ref[...] = _dot(earlier, sel) + carry_ref[...]
    carry_ref[...] += jnp.sum(sel.astype(F32), axis=0, keepdims=True)
    cnt_ref[...] = carry_ref[...]


def moe_rank(sel, tr=512):
    m = sel.shape[0]
    tr = min(tr, m)
    return pl.pallas_call(
        _rank_kernel,
        grid=(m // tr,),
        in_specs=[pl.BlockSpec((tr, LANES), lambda i: (i, 0))],
        out_specs=[pl.BlockSpec((tr, LANES), lambda i: (i, 0)), pl.BlockSpec((1, LANES), lambda i: (0, 0))],
        out_shape=[jax.ShapeDtypeStruct((m, LANES), F32), jax.ShapeDtypeStruct((1, LANES), F32)],
        scratch_shapes=[pltpu.VMEM((1, LANES), F32)],
        compiler_params=_cparams(("arbitrary",)),
        name="moe_rank",
    )(sel)


def _slots_kernel(rank_ref, sel_ref, gate_ref, start_ref, out_ref):
    sel = sel_ref[...].astype(F32) > 0.0
    lane = lax.broadcasted_iota(I32, sel.shape, 1)
    dest = start_ref[...] + rank_ref[...]
    first = jnp.min(jnp.where(sel, lane, LANES), axis=-1, keepdims=True)
    second = jnp.max(jnp.where(sel, lane, -1), axis=-1, keepdims=True)
    pick = lambda arr, idx: jnp.sum(jnp.where(lane == idx, arr, 0.0), axis=-1, keepdims=True)
    gates = gate_ref[...]
    out = jnp.where(lane == SLOT_POS_A, pick(dest, first), 0.0)
    out = jnp.where(lane == SLOT_POS_B, pick(dest, second), out)
    out = jnp.where(lane == SLOT_W_A, pick(gates, first), out)
    out_ref[...] = jnp.where(lane == SLOT_W_B, pick(gates, second), out)


def moe_slots(rank, sel, gates, start_row, tr=512):
    m = sel.shape[0]
    tr = min(tr, m)
    row = pl.BlockSpec((tr, LANES), lambda i: (i, 0))
    return pl.pallas_call(
        _slots_kernel,
        grid=(m // tr,),
        in_specs=[row, row, row, pl.BlockSpec((1, LANES), lambda i: (0, 0))],
        out_specs=row,
        out_shape=jax.ShapeDtypeStruct((m, LANES), F32),
        compiler_params=_cparams(("parallel",)),
        name="moe_slots",
    )(rank, sel, gates, start_row)


def _row_copy(src_ref, src_row, dst_ref, dst_row, sem):
    return pltpu.make_async_copy(src_ref.at[pl.ds(src_row, 1), :], dst_ref.at[pl.ds(dst_row, 1), :], sem)


def _dispatch_kernel(pos_ref, tail_ref, h_ref, xg_ref, zero_ref, sem, *, tm):
    tt = h_ref.shape[0]

    @pl.when(pl.program_id(0) == 0)
    def _():
        zero_ref[...] = jnp.zeros_like(zero_ref)
        fill = lambda e: pltpu.make_async_copy(zero_ref, xg_ref.at[pl.ds(pl.multiple_of(tail_ref[e], tm), tm), :], sem)
        for e in range(tail_ref.shape[0]):
            @pl.when(tail_ref[e] >= 0)
            def _(e=e):
                fill(e).start()
        for e in range(tail_ref.shape[0]):
            @pl.when(tail_ref[e] >= 0)
            def _(e=e):
                fill(e).wait()

    def start(r, carry):
        _row_copy(h_ref, r, xg_ref, pos_ref[0, r], sem).start(priority=0)
        _row_copy(h_ref, r, xg_ref, pos_ref[0, tt + r], sem).start(priority=1)
        return carry

    def wait(r, carry):
        _row_copy(h_ref, r, xg_ref, pos_ref[0, r], sem).wait()
        _row_copy(h_ref, r, xg_ref, pos_ref[0, tt + r], sem).wait()
        return carry

    lax.fori_loop(0, tt, start, 0, unroll=_ROW_COPY_UNROLL)
    lax.fori_loop(0, tt, wait, 0, unroll=_ROW_COPY_UNROLL)


def moe_dispatch(h, pos_tiles, tails, rows, tm):
    m, d = h.shape
    tt = pos_tiles.shape[2] // 2
    return pl.pallas_call(
        functools.partial(_dispatch_kernel, tm=tm),
        grid=(m // tt,),
        in_specs=[pl.BlockSpec((None, 1, 2 * tt), lambda i: (i, 0, 0), memory_space=pltpu.SMEM),
                  pl.BlockSpec(memory_space=pltpu.SMEM),
                  pl.BlockSpec((tt, d), lambda i: (i, 0))],
        out_specs=pl.BlockSpec(memory_space=pl.ANY),
        out_shape=jax.ShapeDtypeStruct((rows, d), h.dtype),
        scratch_shapes=[pltpu.VMEM((tm, d), h.dtype), pltpu.SemaphoreType.DMA(())],
        compiler_params=_cparams(("arbitrary",)),
        name="moe_dispatch",
    )(pos_tiles, tails, h)


def _glu_grouped_kernel(te_ref, nv_ref, x_ref, w1_ref, w3_ref, o_ref):
    valid = pl.program_id(1) < nv_ref[0]

    @pl.when(valid)
    def _():
        lo, hi = (v.astype(BF16) for v in _unpack_bf16_pair(x_ref[...]))
        half = lo.shape[1]
        up = lambda w_ref: (_dot(lo, w_ref[0:half, :].astype(BF16)) + _dot(hi, w_ref[half:2 * half, :].astype(BF16)))
        o_ref[...] = (_silu(up(w1_ref)) * up(w3_ref)).astype(o_ref.dtype)

    @pl.when(jnp.logical_not(valid))
    def _():
        o_ref[...] = jnp.zeros_like(o_ref)


def _down_grouped_kernel(te_ref, nv_ref, h_ref, w2_ref, o_ref):
    valid = pl.program_id(1) < nv_ref[0]

    @pl.when(valid)
    def _():
        y = _dot(h_ref[...], w2_ref[...].astype(BF16))
        half = y.shape[1] // 2
        o_ref[...] = _pack_bf16_pair(y[:, :half], y[:, half:])

    @pl.when(jnp.logical_not(valid))
    def _():
        o_ref[...] = jnp.zeros_like(o_ref)


def _grouped_call(kernel, x, weights, tile_expert, n_valid, n_col_tiles, tn, out_tn, out_dtype, tm, name):
    rows, d = x.shape
    tile = lambda i, nv: jnp.minimum(i, nv[0] - 1)
    in_specs = [pl.BlockSpec((tm, d), lambda j, i, te, nv: (tile(i, nv), 0))]
    in_specs += [pl.BlockSpec((None, w.shape[1], tn), lambda j, i, te, nv: (te[tile(i, nv)], 0, j)) for w in weights]
    grid_spec = pltpu.PrefetchScalarGridSpec(
        num_scalar_prefetch=2,
        grid=(n_col_tiles, rows // tm),
        in_specs=in_specs,
        out_specs=pl.BlockSpec((tm, out_tn), lambda j, i, te, nv: (i, j)),
    )
    return pl.pallas_call(
        kernel,
        grid_spec=grid_spec,
        out_shape=jax.ShapeDtypeStruct((rows, n_col_tiles * out_tn), out_dtype),
        compiler_params=_cparams(("arbitrary", "arbitrary")),
        name=name,
    )(tile_expert, n_valid, x, *weights)


def _combine_kernel(pos_ref, next_pos_ref, x_ref, slot_ref, yg_ref, o_ref, buf_ref, sems):
    tt = x_ref.shape[0]
    i = pl.program_id(0)
    cur = i % 2

    def gather(p_ref, slot):
        def copies(r):
            return (_row_copy(yg_ref, p_ref[0, r], buf_ref.at[slot, 0], r, sems.at[slot]),
                    _row_copy(yg_ref, p_ref[0, tt + r], buf_ref.at[slot, 1], r, sems.at[slot]))
        return copies

    def start_all(copies):
        def body(r, carry):
            a, b = copies(r)
            a.start(priority=0)
            b.start(priority=1)
            return carry
        lax.fori_loop(0, tt, body, 0, unroll=_ROW_COPY_UNROLL)

    def wait_all(copies):
        def body(r, carry):
            a, b = copies(r)
            a.wait()
            b.wait()
            return carry
        lax.fori_loop(0, tt, body, 0, unroll=_ROW_COPY_UNROLL)

    @pl.when(i == 0)
    def _():
        start_all(gather(pos_ref, cur))

    @pl.when(i + 1 < pl.num_programs(0))
    def _():
        start_all(gather(next_pos_ref, 1 - cur))

    wait_all(gather(pos_ref, cur))
    rows_ref = buf_ref.at[cur]
    slots = slot_ref[...]
    w_a = slots[:, SLOT_W_A:SLOT_W_A + 1]
    w_b = slots[:, SLOT_W_B:SLOT_W_B + 1]
    half = MOE_DOWN_TN // 2
    for j in range(x_ref.shape[1] // MOE_DOWN_TN):
        lo_a, hi_a = _unpack_bf16_pair(rows_ref[0, :, j * half:(j + 1) * half])
        lo_b, hi_b = _unpack_bf16_pair(rows_ref[1, :, j * half:(j + 1) * half])
        c_lo = slice(j * MOE_DOWN_TN, j * MOE_DOWN_TN + half)
        c_hi = slice(j * MOE_DOWN_TN + half, (j + 1) * MOE_DOWN_TN)
        o_ref[:, c_lo] = x_ref[:, c_lo] + w_a * lo_a + w_b * lo_b
        o_ref[:, c_hi] = x_ref[:, c_hi] + w_a * hi_a + w_b * hi_b


def moe_combine(x, slots, pos_tiles, yg):
    m, d = x.shape
    tt = pos_tiles.shape[2] // 2
    n_steps = m // tt
    pos_spec = lambda step: pl.BlockSpec((None, 1, 2 * tt), lambda i: (step(i), 0, 0), memory_space=pltpu.SMEM)
    return pl.pallas_call(
        _combine_kernel,
        grid=(n_steps,),
        in_specs=[pos_spec(lambda i: i),
                  pos_spec(lambda i: jnp.minimum(i + 1, n_steps - 1)),
                  pl.BlockSpec((tt, d), lambda i: (i, 0)),
                  pl.BlockSpec((tt, LANES), lambda i: (i, 0)),
                  pl.BlockSpec(memory_space=pl.ANY)],
        out_specs=pl.BlockSpec((tt, d), lambda i: (i, 0)),
        out_shape=jax.ShapeDtypeStruct((m, d), F32),
        scratch_shapes=[pltpu.VMEM((2, 2, tt, yg.shape[1]), yg.dtype), pltpu.SemaphoreType.DMA((2,))],
        compiler_params=_cparams(("arbitrary",)),
        name="moe_combine",
    )(pos_tiles, pos_tiles, x, slots, yg)


def moe_layer(x2, norm_g, w_router, w1, w3, w2):
    m, d = x2.shape
    tm = min(MOE_TM, m)
    tt = min(MOE_TT, m)
    n_tiles = (TOP_K * m) // tm + N_EXPERTS
    rows = n_tiles * tm
    h, gates, sel = moe_router(x2, norm_g, w_router)
    rank, counts = moe_rank(sel)

    cnt = counts[0, :N_EXPERTS].astype(I32)
    padded = ((cnt + tm - 1) // tm) * tm
    ends = jnp.cumsum(padded)
    starts = ends - padded
    start_row = jnp.zeros((1, LANES), F32).at[0, :N_EXPERTS].set(starts.astype(F32))
    tile_expert = jnp.minimum(jnp.searchsorted(ends, jnp.arange(n_tiles, dtype=I32) * tm, side="right"),
                              N_EXPERTS - 1).astype(I32)
    n_valid = (ends[-1:] // tm).astype(I32)
    unused = ends[-1] + jnp.arange(N_EXPERTS, dtype=I32) * tm
    tails = jnp.concatenate([jnp.where(padded > 0, ends - tm, -1),
                             jnp.where(unused < rows, unused, -1)]).astype(I32)

    slots = moe_slots(rank, sel, gates, start_row)
    pos = slots[:, :2].astype(I32).reshape(m // tt, tt, 2)
    pos_tiles = jnp.swapaxes(pos, 1, 2).reshape(m // tt, 1, 2 * tt)

    xg = moe_dispatch(h, pos_tiles, tails, rows, tm)
    hid = _grouped_call(_glu_grouped_kernel, xg, [w1, w3], tile_expert, n_valid,
                        w1.shape[2] // MOE_UP_TN, MOE_UP_TN, MOE_UP_TN, BF16, tm, "moe_glu")
    yg = _grouped_call(_down_grouped_kernel, hid, [w2], tile_expert, n_valid,
                       d // MOE_DOWN_TN, MOE_DOWN_TN, MOE_DOWN_TN // 2, U32, tm, "moe_down")
    return moe_combine(x2, slots, pos_tiles, yg)


def _split_in_proj(w_in_l):
    seg = lambda lo, hi: w_in_l[:, lo:hi]
    big = jnp.concatenate([
        seg(O_AQ, O_AZ), seg(O_AZ, O_AA), seg(O_BQ, O_BCKV), seg(O_CB, O_DZ),
        seg(O_DZ, O_DXBC), seg(O_DXBC, O_DDT), seg(O_BQI, O_BKI), seg(O_BCKV, O_BQI)], axis=1).astype(BF16)
    pad = jnp.zeros((w_in_l.shape[0], S_COLS - (S_DDT + H_D)), w_in_l.dtype)
    small = jnp.concatenate([
        seg(O_AA, O_AB), seg(O_AB, O_BQ), seg(O_BKI, O_BWI), seg(O_BWI, O_CB), seg(O_DDT, _O_END), pad],
        axis=1).astype(BF16)
    gate = w_in_l[:, MIX_COLS:].astype(BF16)
    return big, small, gate


def _mixer_layer(x2, bsz, t, mix_norm_g, w_in_l, conv_a_w, a_log_a, dt_bias_a, out_norm_a_g, ckv_norm_g, w_uk,
                 q_norm_b_g, k_norm_b_g, w_uv, bias_tables, conv_c_w, conv_d_w, conv_d_b, a_log_d,
                 dt_bias_d, d_skip, out_norm_d_g, w_up_all, w_out_all, layer):
    m = bsz * t
    xn = rmsnorm(x2, mix_norm_g)
    w_big, w_small, w_gate = _split_in_proj(w_in_l)
    p3 = matmul(xn, w_big, out_dtype=BF16, tm=1024, tn=768, name="in_proj").reshape(bsz, t, P_COLS)
    small3 = matmul(xn, w_small, out_dtype=F32, tm=1024, tn=S_COLS, name="in_proj_small").reshape(bsz, t, S_COLS)

    o_a = gated_deltanet(p3, small3, conv_a_w, a_log_a, dt_bias_a, out_norm_a_g)

    qh3, kk3, ct3, ki3 = dsa_prep(p3, small3, w_uk, ckv_norm_g, q_norm_b_g, k_norm_b_g)
    dprev, ddiag = bias_tables
    o_b = dsa_attention(qh3, p3, small3, kk3, ct3, ki3, dprev, ddiag,
                        jnp.swapaxes(w_uv, 1, 2).astype(BF16))

    o_c = _conv_call(_conv_gated_kernel, p3, [P_C, P_C + BRANCH_WIDTH, P_C + 2 * BRANCH_WIDTH], BRANCH_WIDTH,
                     [conv_c_w.astype(F32)], BF16, 512, "conv_gated")

    xbc3 = _conv_call(_conv_xbc_kernel, p3, [P_DXBC], XBC_WIDTH,
                      [conv_d_w.astype(F32), conv_d_b.reshape(1, XBC_WIDTH).astype(F32)], F32, 512, "conv_xbc")
    o_d = mamba2_ssd(xbc3, p3, small3, a_log_d, dt_bias_d, d_skip, out_norm_d_g)

    outs = [o.reshape(m, BRANCH_WIDTH) for o in (o_a, o_b, o_c, o_d)]
    merged = merge_branches(xn, outs, w_gate, w_up_all, layer)
    return matmul(merged, w_out_all, out_dtype=F32, residual=x2, tm=1024, tn=512, layer=layer, name="out_proj")


def kernel(x, mix_norm_g, w_in, conv_a_w, a_log_a, dt_bias_a, out_norm_a_g, ckv_norm_g, w_uk, q_norm_b_g, k_norm_b_g, w_uv, rel_bias, conv_c_w, conv_d_w, conv_d_b, a_log_d, dt_bias_d, d_skip, out_norm_d_g, w_up, w_out, ffn_norm_g, w1_dense, w3_dense, w2_dense, w_router, w1_moe, w3_moe, w2_moe):
    bsz, t, d = x.shape
    depth = w_in.shape[0]
    x2 = x.reshape(bsz * t, d)
    bias_tables = relbias_tables(rel_bias)
    for l in range(depth):
        x2 = _mixer_layer(x2, bsz, t, mix_norm_g[l], w_in[l], conv_a_w[l], a_log_a[l], dt_bias_a[l],
                          out_norm_a_g[l], ckv_norm_g[l], w_uk[l], q_norm_b_g[l], k_norm_b_g[l], w_uv[l],
                          bias_tables, conv_c_w[l], conv_d_w[l], conv_d_b[l], a_log_d[l],
                          dt_bias_d[l], d_skip[l], out_norm_d_g[l], w_up, w_out, l)
        j = l // 2
        if l % 2 == 0:
            h = rmsnorm(x2, ffn_norm_g[l])
            hid = glu_dense(h, w1_dense[j], w3_dense[j])
            x2 = matmul(hid, w2_dense[j].astype(BF16), out_dtype=F32, residual=x2,
                        tm=512, tn=512, name="ffn_down")
        else:
            x2 = moe_layer(x2, ffn_norm_g[l], w_router[j], w1_moe[j], w3_moe[j], w2_moe[j])
    return x2.reshape(bsz, t, d)
```

```python
import functools

import jax
import jax.numpy as jnp
from jax import lax
from jax.experimental import pallas as pl
from jax.experimental.pallas import tpu as pltpu

F32 = jnp.float32
BF16 = jnp.bfloat16
I32 = jnp.int32
U32 = jnp.uint32
HIGHEST = lax.Precision.HIGHEST

D_MODEL = 4096
CHUNK = 64
N_BRANCHES = 4
BRANCH_WIDTH = D_MODEL // N_BRANCHES
DK_A = 128
H_A = BRANCH_WIDTH // DK_A
CONV_A = 4
DH_B = 128
H_B = BRANCH_WIDTH // DH_B
DC_B = 256
H_IDX = 8
D_IDX = 64
DSA_TOPK = 256
Q_BLOCK = 128
N_BUCKETS = 32
T5_MAX_DISTANCE = 128
CONV_C = 3
D_INNER = BRANCH_WIDTH
P_D = 64
H_D = D_INNER // P_D
N_GROUPS = 2
D_STATE = 128
CONV_D = 4
XBC_WIDTH = D_INNER + 2 * N_GROUPS * D_STATE
N_EXPERTS = 8
TOP_K = 2
EPS = 1e-6

MIX_SPLITS = (
    BRANCH_WIDTH, BRANCH_WIDTH, BRANCH_WIDTH, BRANCH_WIDTH, H_A, H_A,
    H_B * DH_B, DC_B, H_IDX * D_IDX, D_IDX, H_IDX,
    BRANCH_WIDTH, BRANCH_WIDTH, BRANCH_WIDTH,
    D_INNER, XBC_WIDTH, H_D,
)
MIX_COLS = sum(MIX_SPLITS)
_OFF = [0]
for _w in MIX_SPLITS:
    _OFF.append(_OFF[-1] + _w)
(O_AQ, O_AK, O_AV, O_AZ, O_AA, O_AB, O_BQ, O_BCKV, O_BQI, O_BKI, O_BWI,
 O_CB, O_CC, O_CH, O_DZ, O_DXBC, O_DDT, _O_END) = _OFF

P_AQKV = 0
P_AZ = 3072
P_BQ = 4096
P_C = 5120
P_DZ = 8192
P_DXBC = 9216
P_BQI = 10752
P_BCKV = 11264
P_COLS = 11520
S_AA = 0
S_AB = 8
S_BKI = 16
S_BWI = 80
S_DDT = 88
S_COLS = 128

LANES = 128
VMEM_LIMIT_MB = 56


def _cparams(sem, vmem_mb=VMEM_LIMIT_MB):
    return pltpu.CompilerParams(dimension_semantics=sem, vmem_limit_bytes=vmem_mb * 2 ** 20)


def _softplus(x):
    return jnp.maximum(x, 0.0) + jnp.log(1.0 + jnp.exp(-jnp.abs(x)))


def _sigmoid(x):
    return 1.0 / (1.0 + jnp.exp(-x))


def _silu(x):
    return x * _sigmoid(x)


def _dot(a, b, precision=None):
    return jnp.dot(a, b, preferred_element_type=F32, precision=precision)


def _dot_nt(a, b, precision=None):
    return lax.dot_general(a, b, (((1,), (1,)), ((), ())), preferred_element_type=F32, precision=precision)


def _dot_tn(a, b, precision=None):
    return lax.dot_general(a, b, (((0,), (0,)), ((), ())), preferred_element_type=F32, precision=precision)


def _rmsnorm_kernel(x_ref, g_ref, o_ref):
    x = x_ref[...].astype(F32)
    ms = jnp.mean(x * x, axis=-1, keepdims=True)
    o_ref[...] = (x * lax.rsqrt(ms + EPS) * g_ref[...]).astype(o_ref.dtype)


def rmsnorm(x, g, out_dtype=BF16, tm=512):
    m, d = x.shape
    tm = min(tm, m)
    return pl.pallas_call(
        _rmsnorm_kernel,
        grid=(m // tm,),
        in_specs=[pl.BlockSpec((tm, d), lambda i: (i, 0)), pl.BlockSpec((1, d), lambda i: (0, 0))],
        out_specs=pl.BlockSpec((tm, d), lambda i: (i, 0)),
        out_shape=jax.ShapeDtypeStruct((m, d), out_dtype),
        compiler_params=_cparams(("parallel",)),
        name="rmsnorm",
    )(x, g.reshape(1, d).astype(F32))


def _mm_kernel(*refs, nk, has_res):
    if has_res:
        a_ref, b_ref, r_ref = refs[:3]
        rest = refs[3:]
    else:
        a_ref, b_ref = refs[:2]
        r_ref = None
        rest = refs[2:]
    o_ref = rest[0]
    if nk == 1:
        acc = _dot(a_ref[...], b_ref[...].astype(BF16))
        if r_ref is not None:
            acc = acc + r_ref[...].astype(F32)
        o_ref[...] = acc.astype(o_ref.dtype)
        return
    acc_ref = rest[1]
    k = pl.program_id(2)

    @pl.when(k == 0)
    def _():
        acc_ref[...] = jnp.zeros_like(acc_ref)

    acc_ref[...] += _dot(a_ref[...], b_ref[...].astype(BF16))

    @pl.when(k == nk - 1)
    def _():
        acc = acc_ref[...]
        if r_ref is not None:
            acc = acc + r_ref[...].astype(F32)
        o_ref[...] = acc.astype(o_ref.dtype)


def matmul(a, b, *, out_dtype, residual=None, tm=1024, tn=512, tk=None, layer=None, name="matmul"):
    m, kdim = a.shape
    n = b.shape[-1]
    tm, tn = min(tm, m), min(tn, n)
    tk = kdim if tk is None else min(tk, kdim)
    nk = kdim // tk
    assert m % tm == 0 and n % tn == 0 and kdim % tk == 0, (a.shape, b.shape, tm, tn, tk)
    if layer is None:
        b_spec = pl.BlockSpec((tk, tn), lambda i, j, k: (k, j))
    else:
        b_spec = pl.BlockSpec((None, tk, tn), lambda i, j, k: (layer, k, j))
    in_specs = [pl.BlockSpec((tm, tk), lambda i, j, k: (i, k)), b_spec]
    args = [a, b]
    if residual is not None:
        in_specs.append(pl.BlockSpec((tm, tn), lambda i, j, k: (i, j)))
        args.append(residual)
    scratch = [pltpu.VMEM((tm, tn), F32)] if nk > 1 else []
    return pl.pallas_call(
        functools.partial(_mm_kernel, nk=nk, has_res=residual is not None),
        grid=(m // tm, n // tn, nk),
        in_specs=in_specs,
        out_specs=pl.BlockSpec((tm, tn), lambda i, j, k: (i, j)),
        out_shape=jax.ShapeDtypeStruct((m, n), out_dtype),
        scratch_shapes=scratch,
        compiler_params=_cparams(("parallel", "parallel", "arbitrary")),
        name=name,
    )(*args)


_PAD_ROWS = 8


def _causal_conv(stage_ref, x, w_ref, ksize):
    t = x.shape[0]
    stage_ref[0:_PAD_ROWS, :] = jnp.zeros((_PAD_ROWS, x.shape[1]), F32)
    stage_ref[_PAD_ROWS:_PAD_ROWS + t, :] = x
    acc = x * w_ref[ksize - 1:ksize, :]
    for j in range(ksize - 1):
        s = ksize - 1 - j
        acc = acc + stage_ref[_PAD_ROWS - s:_PAD_ROWS - s + t, :] * w_ref[j:j + 1, :]
    return acc


def _conv_xbc_kernel(x_ref, w_ref, b_ref, o_ref, stage_ref):
    y = _causal_conv(stage_ref, x_ref[...].astype(F32), w_ref, CONV_D) + b_ref[...]
    o_ref[...] = _silu(y)


def _conv_gated_kernel(bg_ref, cg_ref, h_ref, w_ref, o_ref, stage_ref):
    u = cg_ref[...].astype(F32) * h_ref[...].astype(F32)
    y = _causal_conv(stage_ref, u, w_ref, CONV_C)
    o_ref[...] = (bg_ref[...].astype(F32) * y).astype(o_ref.dtype)


def _conv_call(kernel, p3, col_offsets, width, extra, out_dtype, tc, name):
    bsz, t, _ = p3.shape
    tc = min(tc, width)
    in_specs = [pl.BlockSpec((None, t, tc), functools.partial(lambda b, c, o: (b, 0, o + c), o=off // tc))
                for off in col_offsets]
    args = [p3] * len(col_offsets)
    for e in extra:
        in_specs.append(pl.BlockSpec((e.shape[0], tc), lambda b, c: (0, c)))
        args.append(e)
    return pl.pallas_call(
        kernel,
        grid=(bsz, width // tc),
        in_specs=in_specs,
        out_specs=pl.BlockSpec((None, t, tc), lambda b, c: (b, 0, c)),
        out_shape=jax.ShapeDtypeStruct((bsz, t, width), out_dtype),
        scratch_shapes=[pltpu.VMEM((t + _PAD_ROWS, tc), F32)],
        compiler_params=_cparams(("parallel", "parallel")),
        name=name,
    )(*args)


def _tril_mask(n, strict=False):
    r = lax.broadcasted_iota(I32, (n, n), 0)
    c = lax.broadcasted_iota(I32, (n, n), 1)
    return (r > c) if strict else (r >= c)


def _chunk_cumsum(x):
    tril = _tril_mask(CHUNK).astype(F32)
    cs = _dot(tril, x, HIGHEST)
    padded = jnp.concatenate([cs, jnp.zeros((LANES - CHUNK, LANES), F32)], axis=0)
    return cs, padded.T[:, :CHUNK]


def _segment_decay(cs, cs_t, c):
    tril = _tril_mask(CHUNK)
    diff = cs[:, c:c + 1] - cs_t[c:c + 1, :]
    return jnp.where(tril, jnp.exp(jnp.where(tril, diff, 0.0)), 0.0)


_INV_BLOCK = 16
_GDN_GROUP = 2
_GDN_CHUNKS = 4
_CONV_COLS = 512
_CHUNK_SHIFT = CHUNK.bit_length() - 1


def _split_bf16(a):
    hi = a.astype(BF16)
    return hi, (a - hi.astype(F32)).astype(BF16)


def _dot3(a, b):
    a_hi, a_lo = a if isinstance(a, tuple) else _split_bf16(a)
    b_hi, b_lo = b if isinstance(b, tuple) else _split_bf16(b)
    return _dot(a_hi, b_hi) + _dot(a_hi, b_lo) + _dot(a_lo, b_hi)


def _unit_lower_inverses(lows):
    n_rows = lows[0].shape[0]
    r = lax.broadcasted_iota(I32, (n_rows, n_rows), 0)
    c = lax.broadcasted_iota(I32, (n_rows, n_rows), 1)
    shift = _INV_BLOCK.bit_length() - 1
    same = (r >> shift) == (c >> shift)
    eye = (r == c).astype(F32)
    each = lambda fn, *lists: [fn(*args) for args in zip(*lists)]
    lds = each(lambda low: jnp.where(same, low, 0.0), lows)
    los = each(lambda low: jnp.where(same, 0.0, low), lows)
    ps = each(lambda ld: eye - ld, lds)
    xss = each(_split_bf16, lds)
    xs = each(_dot3, xss, xss)
    for _ in range(2):
        xss = each(_split_bf16, xs)
        ps = each(lambda p, x: p + _dot3(p, x), ps, xss)
        xs = each(_dot3, xss, xss)
    ps = each(lambda p, x: p + _dot3(p, x), ps, xs)
    pss = each(_split_bf16, ps)
    ns = each(_dot3, pss, los)
    nss = each(_split_bf16, ns)
    n2s = each(_dot3, nss, nss)
    rrs = each(lambda n, n2: (eye - n) + _dot3(eye - n, n2), ns, n2s)
    return each(_dot3, rrs, pss)


def _gdn_kernel(pq_ref, pk_ref, pv_ref, z_ref, sm_ref, cw_ref, alog_ref, dtb_ref, ng_ref, o_ref,
                state_ref, stage_ref, qkv_ref):
    tt = sm_ref.shape[0]

    @pl.when(pl.program_id(1) == 0)
    def _():
        state_ref[...] = jnp.zeros_like(state_ref)
        stage_ref[0:_PAD_ROWS, :] = jnp.zeros((_PAD_ROWS, stage_ref.shape[1]), F32)

    for idx, ref in enumerate((pq_ref, pk_ref, pv_ref)):
        stage_ref[_PAD_ROWS:_PAD_ROWS + tt, idx * BRANCH_WIDTH:(idx + 1) * BRANCH_WIDTH] = ref[...].astype(F32)
    for c0 in range(0, 3 * BRANCH_WIDTH, _CONV_COLS):
        cols = slice(c0, c0 + _CONV_COLS)
        acc = stage_ref[_PAD_ROWS:_PAD_ROWS + tt, cols] * cw_ref[CONV_A - 1:CONV_A, cols]
        for j in range(CONV_A - 1):
            s = CONV_A - 1 - j
            acc = acc + stage_ref[_PAD_ROWS - s:_PAD_ROWS - s + tt, cols] * cw_ref[j:j + 1, cols]
        y = _silu(acc)
        for g0 in range(0, _CONV_COLS, DK_A):
            ys = y[:, g0:g0 + DK_A]
            if c0 < 2 * BRANCH_WIDTH:
                inv = lax.rsqrt(jnp.sum(ys * ys, axis=-1, keepdims=True) + EPS)
                ys = ys * (inv * DK_A ** -0.5 if c0 < BRANCH_WIDTH else inv)
            qkv_ref[:, c0 + g0:c0 + g0 + DK_A] = ys
    stage_ref[0:_PAD_ROWS, :] = stage_ref[tt:tt + _PAD_ROWS, :]
    q_ref = qkv_ref.at[:, 0:BRANCH_WIDTH]
    k_ref = qkv_ref.at[:, BRANCH_WIDTH:2 * BRANCH_WIDTH]
    v_ref = qkv_ref.at[:, 2 * BRANCH_WIDTH:3 * BRANCH_WIDTH]

    ng = ng_ref[...]
    lane = lax.broadcasted_iota(I32, (CHUNK, LANES), 1)
    rows = _GDN_GROUP * CHUNK
    r = lax.broadcasted_iota(I32, (rows, rows), 0)
    c = lax.broadcasted_iota(I32, (rows, rows), 1)
    same_head = (r >> _CHUNK_SHIFT) == (c >> _CHUNK_SHIFT)
    tril = same_head & (r >= c)
    strict = same_head & (r > c)
    groups = [range(grp * _GDN_GROUP, (grp + 1) * _GDN_GROUP) for grp in range(H_A // _GDN_GROUP)]

    n_chunks = sm_ref.shape[0] // CHUNK
    e_lasts, problems = [], []
    for ci in range(n_chunks):
        ts = slice(ci * CHUNK, (ci + 1) * CHUNK)
        sm = sm_ref[ts, :]
        g = -jnp.exp(alog_ref[...]) * _softplus(sm + dtb_ref[...])
        g = jnp.where(lane < S_AA + H_A, g, 0.0)
        beta = _sigmoid(sm)
        gc, gc_t = _chunk_cumsum(g)
        g_last = gc[CHUNK - 1:CHUNK, :]
        e_gc = jnp.exp(gc)
        e_rem = jnp.exp(g_last - gc)
        e_lasts.append(jnp.exp(g_last))
        for heads in groups:
            stack = lambda ref: jnp.concatenate([ref[ts, h * DK_A:(h + 1) * DK_A] for h in heads], axis=0)
            col = lambda arr, off: jnp.concatenate([arr[:, off + h:off + h + 1] for h in heads], axis=0)
            q_st, k_st, v_st = stack(q_ref), stack(k_ref), stack(v_ref)
            beta_st, egc_st, erem_st = col(beta, S_AB), col(e_gc, S_AA), col(e_rem, S_AA)
            g_row = jnp.concatenate([gc_t[S_AA + h:S_AA + h + 1, :] for h in heads], axis=1)
            diff = col(gc, S_AA) - g_row
            decay = jnp.where(tril, jnp.exp(jnp.where(tril, diff, 0.0)), 0.0)
            kb_st = k_st * beta_st
            problems.append(dict(q=q_st, k=k_st, kb=kb_st, decay=decay, qd=q_st * egc_st, kd=k_st * erem_st,
                                 rhs=jnp.concatenate([v_st * beta_st, kb_st * egc_st], axis=1)))
    lows = [jnp.where(strict, _dot_nt(p["kb"], p["k"]) * p["decay"], 0.0) for p in problems]
    tinvs = _unit_lower_inverses(lows)
    uws = [_dot3(tinv, p["rhs"]) for tinv, p in zip(tinvs, problems)]
    intras = [_dot_nt(p["q"], p["k"]) * p["decay"] for p in problems]

    head_rows = lambda gi_j: slice(gi_j * CHUNK, (gi_j + 1) * CHUNK)
    for ci in range(n_chunks):
        ts = slice(ci * CHUNK, (ci + 1) * CHUNK)
        e_last = e_lasts[ci]
        probs = range(ci * len(groups), (ci + 1) * len(groups))
        states = [state_ref[h] for h in range(H_A)]
        wq_s = {}
        for pi, heads in zip(probs, groups):
            for j, h in enumerate(heads):
                rs = head_rows(j)
                lhs = jnp.concatenate([uws[pi][rs, DK_A:], problems[pi]["qd"][rs]], axis=0)
                wq_s[h] = _dot(lhs, states[h])
        v_new = {h: uws[pi][head_rows(j), :DK_A] - wq_s[h][:CHUNK]
                 for pi, heads in zip(probs, groups) for j, h in enumerate(heads)}
        outs = [jnp.concatenate([wq_s[h][CHUNK:] for h in heads], axis=0)
                + _dot(intras[pi], jnp.concatenate([v_new[h] for h in heads], axis=0))
                for pi, heads in zip(probs, groups)]
        for pi, heads in zip(probs, groups):
            for j, h in enumerate(heads):
                ca = S_AA + h
                state_ref[h] = states[h] * e_last[:, ca:ca + 1] + _dot_tn(problems[pi]["kd"][head_rows(j)], v_new[h])
        for out, heads in zip(outs, groups):
            ms = jnp.mean(out * out, axis=-1, keepdims=True)
            out = out * lax.rsqrt(ms + EPS) * ng
            for j, h in enumerate(heads):
                sl = slice(h * DK_A, (h + 1) * DK_A)
                o_ref[ts, sl] = (out[head_rows(j)] * _silu(z_ref[ts, sl].astype(F32))).astype(o_ref.dtype)


def gated_deltanet(p3, small3, conv_w, a_log, dt_bias, norm_g):
    bsz, t, _ = p3.shape
    tt = _GDN_CHUNKS * CHUNK if t % (_GDN_CHUNKS * CHUNK) == 0 else CHUNK

    def row(vals, off):
        return jnp.zeros((1, S_COLS), F32).at[0, off:off + vals.shape[0]].set(vals.astype(F32))

    w = BRANCH_WIDTH
    return pl.pallas_call(
        _gdn_kernel,
        grid=(bsz, t // tt),
        in_specs=[
            pl.BlockSpec((None, tt, w), lambda b, n: (b, n, 0)),
            pl.BlockSpec((None, tt, w), lambda b, n: (b, n, 1)),
            pl.BlockSpec((None, tt, w), lambda b, n: (b, n, 2)),
            pl.BlockSpec((None, tt, w), lambda b, n: (b, n, P_AZ // w)),
            pl.BlockSpec((None, tt, S_COLS), lambda b, n: (b, n, 0)),
            pl.BlockSpec((CONV_A, 3 * w), lambda b, n: (0, 0)),
            pl.BlockSpec((1, S_COLS), lambda b, n: (0, 0)),
            pl.BlockSpec((1, S_COLS), lambda b, n: (0, 0)),
            pl.BlockSpec((1, DK_A), lambda b, n: (0, 0)),
        ],
        out_specs=pl.BlockSpec((None, tt, w), lambda b, n: (b, n, 0)),
        out_shape=jax.ShapeDtypeStruct((bsz, t, w), BF16),
        scratch_shapes=[pltpu.VMEM((H_A, DK_A, DK_A), F32), pltpu.VMEM((tt + _PAD_ROWS, 3 * w), F32),
                        pltpu.VMEM((tt, 3 * w), F32)],
        compiler_params=_cparams(("parallel", "arbitrary")),
        name="gated_deltanet",
    )(p3, p3, p3, p3, small3, conv_w.astype(F32), row(a_log, S_AA), row(dt_bias, S_AA),
      norm_g.reshape(1, DK_A).astype(F32))


def _ssd_kernel(xbc_ref, z_ref, sm_ref, alog_ref, dtb_ref, dskip_ref, ng_ref, o_ref, state_ref):
    @pl.when(pl.program_id(1) == 0)
    def _():
        state_ref[...] = jnp.zeros_like(state_ref)

    sm = sm_ref[...]
    lane = lax.broadcasted_iota(I32, (CHUNK, LANES), 1)
    dt = _softplus(sm + dtb_ref[...])
    a = jnp.where((lane >= S_DDT) & (lane < S_DDT + H_D), dt * -jnp.exp(alog_ref[...]), 0.0)
    acs, acs_t = _chunk_cumsum(a)
    a_last = acs[CHUNK - 1:CHUNK, :]
    e_acs = jnp.exp(acs)
    e_rem = jnp.exp(a_last - acs)
    e_last = jnp.exp(a_last)
    heads_per_group = H_D // N_GROUPS
    group_of = lambda h: h // heads_per_group
    bms = [xbc_ref[:, D_INNER + grp * D_STATE:D_INNER + (grp + 1) * D_STATE] for grp in range(N_GROUPS)]
    c0 = D_INNER + N_GROUPS * D_STATE
    cms = [xbc_ref[:, c0 + grp * D_STATE:c0 + (grp + 1) * D_STATE] for grp in range(N_GROUPS)]
    cbs = [_dot_nt(cm, bm) for cm, bm in zip(cms, bms)]
    xcs = [xbc_ref[:, h * P_D:(h + 1) * P_D] * dt[:, S_DDT + h:S_DDT + h + 1] for h in range(H_D)]
    states = [state_ref[h] for h in range(H_D)]
    ys = []
    for h in range(H_D):
        c = S_DDT + h
        ys.append(_dot(cbs[group_of(h)] * _segment_decay(acs, acs_t, c), xcs[h])
                  + _dot_nt(cms[group_of(h)] * e_acs[:, c:c + 1], states[h]))
    for h in range(H_D):
        c = S_DDT + h
        state_ref[h] = states[h] * e_last[:, c:c + 1] + _dot_tn(xcs[h], bms[group_of(h)] * e_rem[:, c:c + 1])
    y = jnp.concatenate(ys, axis=1) + dskip_ref[...] * xbc_ref[:, 0:D_INNER]
    y = y * _silu(z_ref[...].astype(F32))
    gw = D_INNER // N_GROUPS
    for grp in range(N_GROUPS):
        yg = y[:, grp * gw:(grp + 1) * gw]
        ms = jnp.mean(yg * yg, axis=-1, keepdims=True)
        o_ref[:, grp * gw:(grp + 1) * gw] = (yg * lax.rsqrt(ms + EPS) * ng_ref[:, grp * gw:(grp + 1) * gw]).astype(o_ref.dtype)


def mamba2_ssd(xbc3, p3, small3, a_log, dt_bias, d_skip, norm_g):
    bsz, t, _ = xbc3.shape
    nchunks = t // CHUNK

    def row(vals, off):
        return jnp.zeros((1, S_COLS), F32).at[0, off:off + vals.shape[0]].set(vals.astype(F32))

    return pl.pallas_call(
        _ssd_kernel,
        grid=(bsz, nchunks),
        in_specs=[
            pl.BlockSpec((None, CHUNK, XBC_WIDTH), lambda b, n: (b, n, 0)),
            pl.BlockSpec((None, CHUNK, D_INNER), lambda b, n: (b, n, P_DZ // D_INNER)),
            pl.BlockSpec((None, CHUNK, S_COLS), lambda b, n: (b, n, 0)),
            pl.BlockSpec((1, S_COLS), lambda b, n: (0, 0)),
            pl.BlockSpec((1, S_COLS), lambda b, n: (0, 0)),
            pl.BlockSpec((1, D_INNER), lambda b, n: (0, 0)),
            pl.BlockSpec((1, D_INNER), lambda b, n: (0, 0)),
        ],
        out_specs=pl.BlockSpec((None, CHUNK, D_INNER), lambda b, n: (b, n, 0)),
        out_shape=jax.ShapeDtypeStruct((bsz, t, D_INNER), BF16),
        scratch_shapes=[pltpu.VMEM((H_D, P_D, D_STATE), F32)],
        compiler_params=_cparams(("parallel", "arbitrary")),
        name="mamba2_ssd",
    )(xbc3, p3, small3, row(a_log, S_DDT), row(dt_bias, S_DDT),
      jnp.repeat(d_skip.astype(F32), P_D).reshape(1, D_INNER), norm_g.reshape(1, D_INNER).astype(F32))


def _dsa_prep_kernel(q_ref, ckv_ref, sm_ref, wuk_ref, cg_ref, qg_ref, kg_ref, qh_ref, kk_ref, ct_ref, ki_ref):
    ckv = ckv_ref[...].astype(F32)
    c = ckv * lax.rsqrt(jnp.mean(ckv * ckv, axis=-1, keepdims=True) + EPS) * cg_ref[...]
    cb = c.astype(BF16)
    ct_ref[...] = c.T.astype(BF16)
    kk = _dot(cb, wuk_ref[...])
    kk_ref[...] = (kk * lax.rsqrt(jnp.mean(kk * kk, axis=-1, keepdims=True) + EPS) * kg_ref[...]).astype(BF16)
    for h in range(H_B):
        sl = slice(h * DH_B, (h + 1) * DH_B)
        qh = q_ref[:, sl].astype(F32)
        qh = qh * lax.rsqrt(jnp.mean(qh * qh, axis=-1, keepdims=True) + EPS) * qg_ref[...]
        qh_ref[:, sl] = (qh * DH_B ** -0.5).astype(BF16)
    ki_ref[...] = sm_ref[:, S_BKI:S_BKI + D_IDX].astype(BF16)


def dsa_prep(p3, small3, w_uk, ckv_g, q_g, k_g, tt=512):
    bsz, t, _ = p3.shape
    tt = min(tt, t)
    w = H_B * DH_B
    return pl.pallas_call(
        _dsa_prep_kernel,
        grid=(bsz, t // tt),
        in_specs=[
            pl.BlockSpec((None, tt, w), lambda b, i: (b, i, P_BQ // w)),
            pl.BlockSpec((None, tt, DC_B), lambda b, i: (b, i, P_BCKV // DC_B)),
            pl.BlockSpec((None, tt, S_COLS), lambda b, i: (b, i, 0)),
            pl.BlockSpec((DC_B, DH_B), lambda b, i: (0, 0)),
            pl.BlockSpec((1, DC_B), lambda b, i: (0, 0)),
            pl.BlockSpec((1, DH_B), lambda b, i: (0, 0)),
            pl.BlockSpec((1, DH_B), lambda b, i: (0, 0)),
        ],
        out_specs=[
            pl.BlockSpec((None, tt, w), lambda b, i: (b, i, 0)),
            pl.BlockSpec((None, tt, DH_B), lambda b, i: (b, i, 0)),
            pl.BlockSpec((None, DC_B, tt), lambda b, i: (b, 0, i)),
            pl.BlockSpec((None, tt, D_IDX), lambda b, i: (b, i, 0)),
        ],
        out_shape=[
            jax.ShapeDtypeStruct((bsz, t, w), BF16),
            jax.ShapeDtypeStruct((bsz, t, DH_B), BF16),
            jax.ShapeDtypeStruct((bsz, DC_B, t), BF16),
            jax.ShapeDtypeStruct((bsz, t, D_IDX), BF16),
        ],
        compiler_params=_cparams(("parallel", "parallel")),
        name="dsa_prep",
    )(p3, p3, small3, w_uk.astype(BF16), ckv_g.reshape(1, DC_B).astype(F32),
      q_g.reshape(1, DH_B).astype(F32), k_g.reshape(1, DH_B).astype(F32))


_T5_HALF = N_BUCKETS // 2
_T5_EXACT = _T5_HALF // 2
_T5_FAR = _T5_HALF - 1


def _relbias_kernel(rb_ref, prev_ref, diag_ref):
    h = pl.program_id(0)
    kl = lax.broadcasted_iota(I32, (Q_BLOCK, Q_BLOCK), 0)
    ql = lax.broadcasted_iota(I32, (Q_BLOCK, Q_BLOCK), 1)
    far = rb_ref[_T5_FAR, h]
    for ref, shift in ((prev_ref, -Q_BLOCK), (diag_ref, 0)):
        rel = kl - ql + shift
        n = jnp.abs(rel)
        n2 = n * n
        steps = jnp.zeros_like(n)
        for j in range(1, _T5_HALF - _T5_EXACT):
            steps = steps + (n2 >= (_T5_EXACT * _T5_EXACT) * 2 ** j).astype(I32)
        large = jnp.minimum(_T5_EXACT + steps, _T5_HALF - 1)
        bucket = jnp.where(rel > 0, _T5_HALF, 0) + jnp.where(n < _T5_EXACT, n, large)
        acc = jnp.zeros((Q_BLOCK, Q_BLOCK), F32)
        for b in range(N_BUCKETS):
            acc = jnp.where(bucket == b, rb_ref[b, h], acc)
        ref[...] = acc - far


def relbias_tables(rel_bias):
    shp = jax.ShapeDtypeStruct((H_B, Q_BLOCK, Q_BLOCK), F32)
    spec = pl.BlockSpec((None, Q_BLOCK, Q_BLOCK), lambda h: (h, 0, 0))
    return pl.pallas_call(
        _relbias_kernel,
        grid=(H_B,),
        in_specs=[pl.BlockSpec(memory_space=pltpu.SMEM)],
        out_specs=[spec, spec],
        out_shape=[shp, shp],
        compiler_params=_cparams(("arbitrary",)),
        name="relbias_tables",
    )(rel_bias.astype(F32))


_INT_MIN = -2 ** 31
_SCORE_ROWS = 256
_SWEEP_ROWS = 64
_DSA_WIDTH_STEP = 512


def _sweep_rows(n_rows, init, step):
    def body(c, acc):
        return step(pl.multiple_of(c * _SWEEP_ROWS, _SWEEP_ROWS), acc)
    n_steps = n_rows // _SWEEP_ROWS
    return lax.fori_loop(0, n_steps, body, init, unroll=min(8, n_steps))


def _dsa_kernel(qh_ref, qi_ref, sm_ref, kk_ref, ct_ref, ki_ref, dprev_ref, ddiag_ref, wuvt_ref,
                o_ref, key_ref, selb_ref, lg_ref, p_ref, *, widths, topk):
    i = pl.program_id(1)
    need = (i + 1) * Q_BLOCK
    wi_t = sm_ref[...].T[S_BWI:S_BWI + H_IDX, :] * (H_IDX ** -0.5 * D_IDX ** -0.5)
    qi_all = jnp.concatenate([qi_ref[:, h * D_IDX:(h + 1) * D_IDX] for h in range(H_IDX)], axis=0)
    q_chunk = (i * Q_BLOCK + lax.broadcasted_iota(I32, (1, Q_BLOCK), 1)) >> _CHUNK_SHIFT
    row_iota = lax.broadcasted_iota(I32, (_SWEEP_ROWS, Q_BLOCK), 0)
    block = lambda ref, r0: ref[pl.ds(r0, _SWEEP_ROWS), :]

    def body(nc):
        for r0 in range(0, nc, min(_SCORE_ROWS, nc)):
            rows = slice(r0, r0 + min(_SCORE_ROWS, nc))
            s_all = _dot_nt(ki_ref[rows, :], qi_all)
            sc = None
            for h in range(H_IDX):
                term = wi_t[h:h + 1, :] * jnp.maximum(s_all[:, h * Q_BLOCK:(h + 1) * Q_BLOCK], 0.0)
                sc = term if sc is None else sc + term
            bits = lax.bitcast_convert_type(sc + 0.0, I32)
            key = jnp.where(bits < 0, bits ^ 0x7FFFFFFF, bits)
            kpos = r0 + lax.broadcasted_iota(I32, key.shape, 0)
            key_ref[rows, :] = jnp.where((kpos >> _CHUNK_SHIFT) <= q_chunk, key, _INT_MIN)

        def count(pred):
            acc = _sweep_rows(nc, jnp.zeros((_SWEEP_ROWS, Q_BLOCK), I32),
                              lambda r0, acc: acc + pred(r0, block(key_ref, r0)).astype(I32))
            return jnp.sum(acc, axis=0, keepdims=True)

        tau = jnp.where(count(lambda r0, k: k >= 0) >= topk, 0, _INT_MIN).astype(I32)

        def vstep(it, tau):
            cand = tau | (jnp.int32(1) << (30 - it))
            return jnp.where(count(lambda r0, k: k >= cand) >= topk, cand, tau)

        tau = lax.fori_loop(0, 31, vstep, tau)
        zeros = jnp.zeros((_SWEEP_ROWS, Q_BLOCK), I32)
        acc_gt, acc_eq = _sweep_rows(
            nc, (zeros, zeros),
            lambda r0, acc: (acc[0] + (block(key_ref, r0) > tau).astype(I32),
                             acc[1] + (block(key_ref, r0) == tau).astype(I32)))
        n_gt = jnp.sum(acc_gt, axis=0, keepdims=True)
        n_eq = jnp.sum(acc_eq, axis=0, keepdims=True)
        tied = (n_gt + n_eq > topk) & (tau > _INT_MIN)
        nbits = max(1, (nc - 1).bit_length())

        def last_tied_index():
            room = topk - n_gt

            def istep(it, last):
                cand = last | (jnp.int32(1) << (nbits - 1 - it))
                below = count(lambda r0, k: (k == tau) & (r0 + row_iota < cand))
                return jnp.where(below < room, cand, last)

            return lax.fori_loop(0, nbits, istep, jnp.zeros((1, Q_BLOCK), I32))

        last = lax.cond(jnp.max(tied.astype(I32)) > 0, last_tied_index,
                        lambda: jnp.full((1, Q_BLOCK), nc, I32))

        def write_sel(r0, carry):
            k = block(key_ref, r0)
            sel = (k > _INT_MIN) & ((k > tau) | ((k == tau) & (r0 + row_iota <= last)))
            selb_ref[pl.ds(r0, _SWEEP_ROWS), :] = jnp.where(sel, 0.0, -jnp.inf)
            return carry

        _sweep_rows(nc, 0, write_sel)

        kk = kk_ref[0:nc, :]
        ct = ct_ref[:, 0:nc]
        prev_row = pl.multiple_of(jnp.maximum(i - 1, 0) * Q_BLOCK, Q_BLOCK)
        diag_row = pl.multiple_of(i * Q_BLOCK, Q_BLOCK)
        has_prev = (i > 0).astype(F32)
        for h0 in range(0, H_B, 2):
            q_pair = jnp.concatenate([qh_ref[:, h * DH_B:(h + 1) * DH_B] for h in (h0, h0 + 1)], axis=0)
            lg_pair = _dot_nt(kk, q_pair)
            for j, h in enumerate((h0, h0 + 1)):
                lg_ref[h, 0:nc, :] = lg_pair[:, j * Q_BLOCK:(j + 1) * Q_BLOCK] + selb_ref[0:nc, :]
                lg_ref[h, pl.ds(prev_row, Q_BLOCK), :] += dprev_ref[h] * has_prev
                lg_ref[h, pl.ds(diag_row, Q_BLOCK), :] += ddiag_ref[h]
        blk = min(_SCORE_ROWS, nc)
        outs = []
        for h in range(H_B):
            slot = h % 2
            m_acc = None
            for r0 in range(0, nc, blk):
                x = lg_ref[h, r0:r0 + blk, :]
                m_acc = x if m_acc is None else jnp.maximum(m_acc, x)
            m = jnp.max(m_acc, axis=0, keepdims=True)
            p_acc = None
            for r0 in range(0, nc, blk):
                p = jnp.exp(lg_ref[h, r0:r0 + blk, :] - m)
                p_ref[slot, r0:r0 + blk, :] = p.astype(BF16)
                p_acc = p if p_acc is None else p_acc + p
            denom = jnp.sum(p_acc, axis=0, keepdims=True)
            o_lat = _dot(ct, p_ref[slot, 0:nc, :]) * (1.0 / denom)
            outs.append(_dot(wuvt_ref[h], o_lat.astype(BF16)).T)
        o_ref[...] = jnp.concatenate(outs, axis=1).astype(o_ref.dtype)

    lo = 0
    for nc in widths:
        @pl.when((need > lo) & (need <= nc))
        def _(nc=nc):
            body(nc)
        lo = nc


def dsa_attention(qh3, qi_p3, small3, kk3, ct3, ki3, dprev, ddiag, wuv_t):
    bsz, t, w = qh3.shape
    topk = min(DSA_TOPK, t // 4)
    step = min(_DSA_WIDTH_STEP, t)
    widths = tuple(range(step, t + 1, step))
    wq = H_IDX * D_IDX
    full = lambda shape: pl.BlockSpec(shape, lambda b, i: (0,) * len(shape))
    return pl.pallas_call(
        functools.partial(_dsa_kernel, widths=widths, topk=topk),
        grid=(bsz, t // Q_BLOCK),
        in_specs=[
            pl.BlockSpec((None, Q_BLOCK, w), lambda b, i: (b, i, 0)),
            pl.BlockSpec((None, Q_BLOCK, wq), lambda b, i: (b, i, P_BQI // wq)),
            pl.BlockSpec((None, Q_BLOCK, S_COLS), lambda b, i: (b, i, 0)),
            pl.BlockSpec((None, t, DH_B), lambda b, i: (b, 0, 0)),
            pl.BlockSpec((None, DC_B, t), lambda b, i: (b, 0, 0)),
            pl.BlockSpec((None, t, D_IDX), lambda b, i: (b, 0, 0)),
            full((H_B, Q_BLOCK, Q_BLOCK)),
            full((H_B, Q_BLOCK, Q_BLOCK)),
            full((H_B, DH_B, DC_B)),
        ],
        out_specs=pl.BlockSpec((None, Q_BLOCK, w), lambda b, i: (b, i, 0)),
        out_shape=jax.ShapeDtypeStruct((bsz, t, w), BF16),
        scratch_shapes=[pltpu.VMEM((t, Q_BLOCK), I32), pltpu.VMEM((t, Q_BLOCK), F32),
                        pltpu.VMEM((H_B, t, Q_BLOCK), F32), pltpu.VMEM((2, t, Q_BLOCK), BF16)],
        compiler_params=_cparams(("parallel", "arbitrary")),
        name="dsa_attention",
    )(qh3, qi_p3, small3, kk3, ct3, ki3, dprev, ddiag, wuv_t)


def _merge_kernel(xn_ref, o0_ref, o1_ref, o2_ref, o3_ref, g0_ref, g1_ref, g2_ref, g3_ref,
                  u0_ref, u1_ref, u2_ref, u3_ref, out_ref):
    xn = xn_ref[...]
    acc = None
    for o_ref, g_ref, u_ref in ((o0_ref, g0_ref, u0_ref), (o1_ref, g1_ref, u1_ref),
                                (o2_ref, g2_ref, u2_ref), (o3_ref, g3_ref, u3_ref)):
        term = _sigmoid(_dot(xn, g_ref[...])) * _dot(o_ref[...], u_ref[...].astype(BF16))
        acc = term if acc is None else acc + term
    out_ref[...] = acc.astype(out_ref.dtype)


def merge_branches(xn, outs, w_gate, w_up, layer, tm=512, tn=256):
    m, d = xn.shape
    tm, tn = min(tm, m), min(tn, d)
    nj = d // tn
    in_specs = [pl.BlockSpec((tm, d), lambda j, i: (i, 0))]
    in_specs += [pl.BlockSpec((tm, BRANCH_WIDTH), lambda j, i: (i, 0)) for _ in range(N_BRANCHES)]
    in_specs += [pl.BlockSpec((d, tn), functools.partial(lambda j, i, br: (0, br * nj + j), br=br))
                 for br in range(N_BRANCHES)]
    in_specs += [pl.BlockSpec((None, None, BRANCH_WIDTH, tn),
                              functools.partial(lambda j, i, br: (layer, br, 0, j), br=br))
                 for br in range(N_BRANCHES)]
    return pl.pallas_call(
        _merge_kernel,
        grid=(nj, m // tm),
        in_specs=in_specs,
        out_specs=pl.BlockSpec((tm, tn), lambda j, i: (i, j)),
        out_shape=jax.ShapeDtypeStruct((m, d), BF16),
        compiler_params=_cparams(("parallel", "parallel")),
        name="merge_branches",
    )(xn, *outs, *([w_gate] * N_BRANCHES), *([w_up] * N_BRANCHES))


def _glu_kernel(x_ref, w1_ref, w3_ref, o_ref):
    x = x_ref[...]
    o_ref[...] = (_silu(_dot(x, w1_ref[...].astype(BF16))) * _dot(x, w3_ref[...].astype(BF16))).astype(o_ref.dtype)


def glu_dense(x, w1, w3, tm=2048, tf=256):
    m, d = x.shape
    f = w1.shape[1]
    tm, tf = min(tm, m), min(tf, f)
    assert f % tf == 0
    return pl.pallas_call(
        _glu_kernel,
        grid=(f // tf, m // tm),
        in_specs=[pl.BlockSpec((tm, d), lambda j, i: (i, 0)),
                  pl.BlockSpec((d, tf), lambda j, i: (0, j)),
                  pl.BlockSpec((d, tf), lambda j, i: (0, j))],
        out_specs=pl.BlockSpec((tm, tf), lambda j, i: (i, j)),
        out_shape=jax.ShapeDtypeStruct((m, f), BF16),
        compiler_params=_cparams(("parallel", "parallel")),
        name="glu_dense",
    )(x, w1, w3)


MOE_TM = 512
MOE_TT = 256
MOE_UP_TN = 512
MOE_DOWN_TN = 1024
SLOT_POS_A, SLOT_POS_B, SLOT_W_A, SLOT_W_B = 0, 1, 2, 3
_ROW_COPY_UNROLL = 8


def _pack_bf16_pair(lo, hi):
    bits = lambda v: lax.bitcast_convert_type(v.astype(BF16).astype(F32), U32)
    return bits(hi) | (bits(lo) >> 16)


def _unpack_bf16_pair(u):
    return (lax.bitcast_convert_type(u << 16, F32),
            lax.bitcast_convert_type(u & jnp.uint32(0xFFFF0000), F32))


def _router_kernel(x_ref, g_ref, wr_ref, h_ref, gate_ref, sel_ref):
    x = x_ref[...]
    h = x * lax.rsqrt(jnp.mean(x * x, axis=-1, keepdims=True) + EPS) * g_ref[...]
    half = h.shape[1] // 2
    h_ref[...] = _pack_bf16_pair(h[:, :half], h[:, half:])
    logits = _dot3(h, wr_ref[...])
    lane = lax.broadcasted_iota(I32, logits.shape, 1)
    logits = jnp.where(lane < N_EXPERTS, logits, -jnp.inf)
    m1 = jnp.max(logits, axis=-1, keepdims=True)
    i1 = jnp.min(jnp.where(logits == m1, lane, LANES), axis=-1, keepdims=True)
    rest = jnp.where(lane == i1, -jnp.inf, logits)
    m2 = jnp.max(rest, axis=-1, keepdims=True)
    i2 = jnp.min(jnp.where(rest == m2, lane, LANES), axis=-1, keepdims=True)
    e2 = jnp.exp(m2 - m1)
    inv = 1.0 / (1.0 + e2)
    gate_ref[...] = jnp.where(lane == i1, inv, 0.0) + jnp.where(lane == i2, e2 * inv, 0.0)
    sel_ref[...] = jnp.where((lane == i1) | (lane == i2), 1.0, 0.0).astype(sel_ref.dtype)


def moe_router(x, g, w_router, tm=512):
    m, d = x.shape
    tm = min(tm, m)
    wr = jnp.zeros((d, LANES), F32).at[:, :N_EXPERTS].set(w_router.astype(F32))
    row = lambda w: pl.BlockSpec((tm, w), lambda i: (i, 0))
    return pl.pallas_call(
        _router_kernel,
        grid=(m // tm,),
        in_specs=[row(d), pl.BlockSpec((1, d), lambda i: (0, 0)), pl.BlockSpec((d, LANES), lambda i: (0, 0))],
        out_specs=[row(d // 2), row(LANES), row(LANES)],
        out_shape=[jax.ShapeDtypeStruct((m, d // 2), U32), jax.ShapeDtypeStruct((m, LANES), F32),
                   jax.ShapeDtypeStruct((m, LANES), BF16)],
        compiler_params=_cparams(("parallel",)),
        name="moe_router",
    )(x, g.reshape(1, d).astype(F32), wr)


def _rank_kernel(sel_ref, rank_ref, cnt_ref, carry_ref):
    @pl.when(pl.program_id(0) == 0)
    def _():
        carry_ref[...] = jnp.zeros_like(carry_ref)

    sel = sel_ref[...]
    n = sel.shape[0]
    earlier = _tril_mask(n, strict=True).astype(BF16)
    rank_ref[...] = _dot(earlier, sel) + carry_ref[...]
    carry_ref[...] += jnp.sum(sel.astype(F32), axis=0, keepdims=True)
    cnt_ref[...] = carry_ref[...]


def moe_rank(sel, tr=512):
    m = sel.shape[0]
    tr = min(tr, m)
    return pl.pallas_call(
        _rank_kernel,
        grid=(m // tr,),
        in_specs=[pl.BlockSpec((tr, LANES), lambda i: (i, 0))],
        out_specs=[pl.BlockSpec((tr, LANES), lambda i: (i, 0)), pl.BlockSpec((1, LANES), lambda i: (0, 0))],
        out_shape=[jax.ShapeDtypeStruct((m, LANES), F32), jax.ShapeDtypeStruct((1, LANES), F32)],
        scratch_shapes=[pltpu.VMEM((1, LANES), F32)],
        compiler_params=_cparams(("arbitrary",)),
        name="moe_rank",
    )(sel)


def _slots_kernel(rank_ref, sel_ref, gate_ref, start_ref, out_ref):
    sel = sel_ref[...].astype(F32) > 0.0
    lane = lax.broadcasted_iota(I32, sel.shape, 1)
    dest = start_ref[...] + rank_ref[...]
    first = jnp.min(jnp.where(sel, lane, LANES), axis=-1, keepdims=True)
    second = jnp.max(jnp.where(sel, lane, -1), axis=-1, keepdims=True)
    pick = lambda arr, idx: jnp.sum(jnp.where(lane == idx, arr, 0.0), axis=-1, keepdims=True)
    gates = gate_ref[...]
    out = jnp.where(lane == SLOT_POS_A, pick(dest, first), 0.0)
    out = jnp.where(lane == SLOT_POS_B, pick(dest, second), out)
    out = jnp.where(lane == SLOT_W_A, pick(gates, first), out)
    out_ref[...] = jnp.where(lane == SLOT_W_B, pick(gates, second), out)


def moe_slots(rank, sel, gates, start_row, tr=512):
    m = sel.shape[0]
    tr = min(tr, m)
    row = pl.BlockSpec((tr, LANES), lambda i: (i, 0))
    return pl.pallas_call(
        _slots_kernel,
        grid=(m // tr,),
        in_specs=[row, row, row, pl.BlockSpec((1, LANES), lambda i: (0, 0))],
        out_specs=row,
        out_shape=jax.ShapeDtypeStruct((m, LANES), F32),
        compiler_params=_cparams(("parallel",)),
        name="moe_slots",
    )(rank, sel, gates, start_row)


def _row_copy(src_ref, src_row, dst_ref, dst_row, sem):
    return pltpu.make_async_copy(src_ref.at[pl.ds(src_row, 1), :], dst_ref.at[pl.ds(dst_row, 1), :], sem)


def _dispatch_kernel(pos_ref, tail_ref, h_ref, xg_ref, zero_ref, sem, *, tm):
    tt = h_ref.shape[0]

    @pl.when(pl.program_id(0) == 0)
    def _():
        zero_ref[...] = jnp.zeros_like(zero_ref)
        fill = lambda e: pltpu.make_async_copy(zero_ref, xg_ref.at[pl.ds(pl.multiple_of(tail_ref[e], tm), tm), :], sem)
        for e in range(tail_ref.shape[0]):
            @pl.when(tail_ref[e] >= 0)
            def _(e=e):
                fill(e).start()
        for e in range(tail_ref.shape[0]):
            @pl.when(tail_ref[e] >= 0)
            def _(e=e):
                fill(e).wait()

    def start(r, carry):
        _row_copy(h_ref, r, xg_ref, pos_ref[0, r], sem).start(priority=0)
        _row_copy(h_ref, r, xg_ref, pos_ref[0, tt + r], sem).start(priority=1)
        return carry

    def wait(r, carry):
        _row_copy(h_ref, r, xg_ref, pos_ref[0, r], sem).wait()
        _row_copy(h_ref, r, xg_ref, pos_ref[0, tt + r], sem).wait()
        return carry

    lax.fori_loop(0, tt, start, 0, unroll=_ROW_COPY_UNROLL)
    lax.fori_loop(0, tt, wait, 0, unroll=_ROW_COPY_UNROLL)


def moe_dispatch(h, pos_tiles, tails, rows, tm):
    m, d = h.shape
    tt = pos_tiles.shape[2] // 2
    return pl.pallas_call(
        functools.partial(_dispatch_kernel, tm=tm),
        grid=(m // tt,),
        in_specs=[pl.BlockSpec((None, 1, 2 * tt), lambda i: (i, 0, 0), memory_space=pltpu.SMEM),
                  pl.BlockSpec(memory_space=pltpu.SMEM),
                  pl.BlockSpec((tt, d), lambda i: (i, 0))],
        out_specs=pl.BlockSpec(memory_space=pl.ANY),
        out_shape=jax.ShapeDtypeStruct((rows, d), h.dtype),
        scratch_shapes=[pltpu.VMEM((tm, d), h.dtype), pltpu.SemaphoreType.DMA(())],
        compiler_params=_cparams(("arbitrary",)),
        name="moe_dispatch",
    )(pos_tiles, tails, h)


def _glu_grouped_kernel(te_ref, nv_ref, x_ref, w1_ref, w3_ref, o_ref):
    valid = pl.program_id(1) < nv_ref[0]

    @pl.when(valid)
    def _():
        lo, hi = (v.astype(BF16) for v in _unpack_bf16_pair(x_ref[...]))
        half = lo.shape[1]
        up = lambda w_ref: (_dot(lo, w_ref[0:half, :].astype(BF16)) + _dot(hi, w_ref[half:2 * half, :].astype(BF16)))
        o_ref[...] = (_silu(up(w1_ref)) * up(w3_ref)).astype(o_ref.dtype)

    @pl.when(jnp.logical_not(valid))
    def _():
        o_ref[...] = jnp.zeros_like(o_ref)


def _down_grouped_kernel(te_ref, nv_ref, h_ref, w2_ref, o_ref):
    valid = pl.program_id(1) < nv_ref[0]

    @pl.when(valid)
    def _():
        y = _dot(h_ref[...], w2_ref[...].astype(BF16))
        half = y.shape[1] // 2
        o_ref[...] = _pack_bf16_pair(y[:, :half], y[:, half:])

    @pl.when(jnp.logical_not(valid))
    def _():
        o_ref[...] = jnp.zeros_like(o_ref)


def _grouped_call(kernel, x, weights, tile_expert, n_valid, n_col_tiles, tn, out_tn, out_dtype, tm, name):
    rows, d = x.shape
    tile = lambda i, nv: jnp.minimum(i, nv[0] - 1)
    in_specs = [pl.BlockSpec((tm, d), lambda j, i, te, nv: (tile(i, nv), 0))]
    in_specs += [pl.BlockSpec((None, w.shape[1], tn), lambda j, i, te, nv: (te[tile(i, nv)], 0, j)) for w in weights]
    grid_spec = pltpu.PrefetchScalarGridSpec(
        num_scalar_prefetch=2,
        grid=(n_col_tiles, rows // tm),
        in_specs=in_specs,
        out_specs=pl.BlockSpec((tm, out_tn), lambda j, i, te, nv: (i, j)),
    )
    return pl.pallas_call(
        kernel,
        grid_spec=grid_spec,
        out_shape=jax.ShapeDtypeStruct((rows, n_col_tiles * out_tn), out_dtype),
        compiler_params=_cparams(("arbitrary", "arbitrary")),
        name=name,
    )(tile_expert, n_valid, x, *weights)


def _combine_kernel(pos_ref, next_pos_ref, x_ref, slot_ref, yg_ref, o_ref, buf_ref, sems):
    tt = x_ref.shape[0]
    i = pl.program_id(0)
    cur = i % 2

    def gather(p_ref, slot):
        def copies(r):
            return (_row_copy(yg_ref, p_ref[0, r], buf_ref.at[slot, 0], r, sems.at[slot]),
                    _row_copy(yg_ref, p_ref[0, tt + r], buf_ref.at[slot, 1], r, sems.at[slot]))
        return copies

    def start_all(copies):
        def body(r, carry):
            a, b = copies(r)
            a.start(priority=0)
            b.start(priority=1)
            return carry
        lax.fori_loop(0, tt, body, 0, unroll=_ROW_COPY_UNROLL)

    def wait_all(copies):
        def body(r, carry):
            a, b = copies(r)
            a.wait()
            b.wait()
            return carry
        lax.fori_loop(0, tt, body, 0, unroll=_ROW_COPY_UNROLL)

    @pl.when(i == 0)
    def _():
        start_all(gather(pos_ref, cur))

    @pl.when(i + 1 < pl.num_programs(0))
    def _():
        start_all(gather(next_pos_ref, 1 - cur))

    wait_all(gather(pos_ref, cur))
    rows_ref = buf_ref.at[cur]
    slots = slot_ref[...]
    w_a = slots[:, SLOT_W_A:SLOT_W_A + 1]
    w_b = slots[:, SLOT_W_B:SLOT_W_B + 1]
    half = MOE_DOWN_TN // 2
    for j in range(x_ref.shape[1] // MOE_DOWN_TN):
        lo_a, hi_a = _unpack_bf16_pair(rows_ref[0, :, j * half:(j + 1) * half])
        lo_b, hi_b = _unpack_bf16_pair(rows_ref[1, :, j * half:(j + 1) * half])
        c_lo = slice(j * MOE_DOWN_TN, j * MOE_DOWN_TN + half)
        c_hi = slice(j * MOE_DOWN_TN + half, (j + 1) * MOE_DOWN_TN)
        o_ref[:, c_lo] = x_ref[:, c_lo] + w_a * lo_a + w_b * lo_b
        o_ref[:, c_hi] = x_ref[:, c_hi] + w_a * hi_a + w_b * hi_b


def moe_combine(x, slots, pos_tiles, yg):
    m, d = x.shape
    tt = pos_tiles.shape[2] // 2
    n_steps = m // tt
    pos_spec = lambda step: pl.BlockSpec((None, 1, 2 * tt), lambda i: (step(i), 0, 0), memory_space=pltpu.SMEM)
    return pl.pallas_call(
        _combine_kernel,
        grid=(n_steps,),
        in_specs=[pos_spec(lambda i: i),
                  pos_spec(lambda i: jnp.minimum(i + 1, n_steps - 1)),
                  pl.BlockSpec((tt, d), lambda i: (i, 0)),
                  pl.BlockSpec((tt, LANES), lambda i: (i, 0)),
                  pl.BlockSpec(memory_space=pl.ANY)],
        out_specs=pl.BlockSpec((tt, d), lambda i: (i, 0)),
        out_shape=jax.ShapeDtypeStruct((m, d), F32),
        scratch_shapes=[pltpu.VMEM((2, 2, tt, yg.shape[1]), yg.dtype), pltpu.SemaphoreType.DMA((2,))],
        compiler_params=_cparams(("arbitrary",)),
        name="moe_combine",
    )(pos_tiles, pos_tiles, x, slots, yg)


def moe_layer(x2, norm_g, w_router, w1, w3, w2):
    m, d = x2.shape
    tm = min(MOE_TM, m)
    tt = min(MOE_TT, m)
    n_tiles = (TOP_K * m) // tm + N_EXPERTS
    rows = n_tiles * tm
    h, gates, sel = moe_router(x2, norm_g, w_router)
    rank, counts = moe_rank(sel)

    cnt = counts[0, :N_EXPERTS].astype(I32)
    padded = ((cnt + tm - 1) // tm) * tm
    ends = jnp.cumsum(padded)
    starts = ends - padded
    start_row = jnp.zeros((1, LANES), F32).at[0, :N_EXPERTS].set(starts.astype(F32))
    tile_expert = jnp.minimum(jnp.searchsorted(ends, jnp.arange(n_tiles, dtype=I32) * tm, side="right"),
                              N_EXPERTS - 1).astype(I32)
    n_valid = (ends[-1:] // tm).astype(I32)
    unused = ends[-1] + jnp.arange(N_EXPERTS, dtype=I32) * tm
    tails = jnp.concatenate([jnp.where(padded > 0, ends - tm, -1),
                             jnp.where(unused < rows, unused, -1)]).astype(I32)

    slots = moe_slots(rank, sel, gates, start_row)
    pos = slots[:, :2].astype(I32).reshape(m // tt, tt, 2)
    pos_tiles = jnp.swapaxes(pos, 1, 2).reshape(m // tt, 1, 2 * tt)

    xg = moe_dispatch(h, pos_tiles, tails, rows, tm)
    hid = _grouped_call(_glu_grouped_kernel, xg, [w1, w3], tile_expert, n_valid,
                        w1.shape[2] // MOE_UP_TN, MOE_UP_TN, MOE_UP_TN, BF16, tm, "moe_glu")
    yg = _grouped_call(_down_grouped_kernel, hid, [w2], tile_expert, n_valid,
                       d // MOE_DOWN_TN, MOE_DOWN_TN, MOE_DOWN_TN // 2, U32, tm, "moe_down")
    return moe_combine(x2, slots, pos_tiles, yg)


def _split_in_proj(w_in_l):
    seg = lambda lo, hi: w_in_l[:, lo:hi]
    big = jnp.concatenate([
        seg(O_AQ, O_AZ), seg(O_AZ, O_AA), seg(O_BQ, O_BCKV), seg(O_CB, O_DZ),
        seg(O_DZ, O_DXBC), seg(O_DXBC, O_DDT), seg(O_BQI, O_BKI), seg(O_BCKV, O_BQI)], axis=1).astype(BF16)
    pad = jnp.zeros((w_in_l.shape[0], S_COLS - (S_DDT + H_D)), w_in_l.dtype)
    small = jnp.concatenate([
        seg(O_AA, O_AB), seg(O_AB, O_BQ), seg(O_BKI, O_BWI), seg(O_BWI, O_CB), seg(O_DDT, _O_END), pad],
        axis=1).astype(BF16)
    gate = w_in_l[:, MIX_COLS:].astype(BF16)
    return big, small, gate


def _mixer_layer(x2, bsz, t, mix_norm_g, w_in_l, conv_a_w, a_log_a, dt_bias_a, out_norm_a_g, ckv_norm_g, w_uk,
                 q_norm_b_g, k_norm_b_g, w_uv, bias_tables, conv_c_w, conv_d_w, conv_d_b, a_log_d,
                 dt_bias_d, d_skip, out_norm_d_g, w_up_all, w_out_all, layer):
    m = bsz * t
    xn = rmsnorm(x2, mix_norm_g)
    w_big, w_small, w_gate = _split_in_proj(w_in_l)
    p3 = matmul(xn, w_big, out_dtype=BF16, tm=1024, tn=768, name="in_proj").reshape(bsz, t, P_COLS)
    small3 = matmul(xn, w_small, out_dtype=F32, tm=1024, tn=S_COLS, name="in_proj_small").reshape(bsz, t, S_COLS)

    o_a = gated_deltanet(p3, small3, conv_a_w, a_log_a, dt_bias_a, out_norm_a_g)

    qh3, kk3, ct3, ki3 = dsa_prep(p3, small3, w_uk, ckv_norm_g, q_norm_b_g, k_norm_b_g)
    dprev, ddiag = bias_tables
    o_b = dsa_attention(qh3, p3, small3, kk3, ct3, ki3, dprev, ddiag,
                        jnp.swapaxes(w_uv, 1, 2).astype(BF16))

    o_c = _conv_call(_conv_gated_kernel, p3, [P_C, P_C + BRANCH_WIDTH, P_C + 2 * BRANCH_WIDTH], BRANCH_WIDTH,
                     [conv_c_w.astype(F32)], BF16, 512, "conv_gated")

    xbc3 = _conv_call(_conv_xbc_kernel, p3, [P_DXBC], XBC_WIDTH,
                      [conv_d_w.astype(F32), conv_d_b.reshape(1, XBC_WIDTH).astype(F32)], F32, 512, "conv_xbc")
    o_d = mamba2_ssd(xbc3, p3, small3, a_log_d, dt_bias_d, d_skip, out_norm_d_g)

    outs = [o.reshape(m, BRANCH_WIDTH) for o in (o_a, o_b, o_c, o_d)]
    merged = merge_branches(xn, outs, w_gate, w_up_all, layer)
    return matmul_streamed(merged, w_out_all, x2, layer)


def matmul_streamed(a, b_all, residual, layer, tm=1024, tn=512):
    m, kdim = a.shape
    n = b_all.shape[-1]
    tm, tn = min(tm, m), min(tn, n)
    deep = pl.Buffered(3)

    def body(a_ref, b_ref, r_ref, o_ref):
        o_ref[...] = _dot(a_ref[...], b_ref[...].astype(BF16)) + r_ref[...]

    def outer(a_hbm, b_hbm, r_hbm, o_hbm):
        pltpu.emit_pipeline(
            body,
            grid=(m // tm, n // tn),
            in_specs=[pl.BlockSpec((tm, kdim), lambda i, j: (i, 0)),
                      pl.BlockSpec((None, kdim, tn), lambda i, j: (layer, 0, j), pipeline_mode=deep),
                      pl.BlockSpec((tm, tn), lambda i, j: (i, j), pipeline_mode=deep)],
            out_specs=pl.BlockSpec((tm, tn), lambda i, j: (i, j)),
        )(a_hbm, b_hbm, r_hbm, o_hbm)

    any_spec = pl.BlockSpec(memory_space=pl.ANY)
    return pl.pallas_call(
        outer,
        in_specs=[any_spec, any_spec, any_spec],
        out_specs=any_spec,
        out_shape=jax.ShapeDtypeStruct((m, n), F32),
        compiler_params=pltpu.CompilerParams(vmem_limit_bytes=VMEM_LIMIT_MB * 2 ** 20),
        name="out_proj",
    )(a, b_all, residual)


def kernel(x, mix_norm_g, w_in, conv_a_w, a_log_a, dt_bias_a, out_norm_a_g, ckv_norm_g, w_uk, q_norm_b_g, k_norm_b_g, w_uv, rel_bias, conv_c_w, conv_d_w, conv_d_b, a_log_d, dt_bias_d, d_skip, out_norm_d_g, w_up, w_out, ffn_norm_g, w1_dense, w3_dense, w2_dense, w_router, w1_moe, w3_moe, w2_moe):
    bsz, t, d = x.shape
    depth = w_in.shape[0]
    x2 = x.reshape(bsz * t, d)
    bias_tables = relbias_tables(rel_bias)
    for l in range(depth):
        x2 = _mixer_layer(x2, bsz, t, mix_norm_g[l], w_in[l], conv_a_w[l], a_log_a[l], dt_bias_a[l],
                          out_norm_a_g[l], ckv_norm_g[l], w_uk[l], q_norm_b_g[l], k_norm_b_g[l], w_uv[l],
                          bias_tables, conv_c_w[l], conv_d_w[l], conv_d_b[l], a_log_d[l],
                          dt_bias_d[l], d_skip[l], out_norm_d_g[l], w_up, w_out, l)
        j = l // 2
        if l % 2 == 0:
            h = rmsnorm(x2, ffn_norm_g[l])
            hid = glu_dense(h, w1_dense[j], w3_dense[j])
            x2 = matmul(hid, w2_dense[j].astype(BF16), out_dtype=F32, residual=x2,
                        tm=512, tn=512, name="ffn_down")
        else:
            x2 = moe_layer(x2, ffn_norm_g[l], w_router[j], w1_moe[j], w3_moe[j], w2_moe[j])
    return x2.reshape(bsz, t, d)
```

```python
import functools

import jax
import jax.numpy as jnp
from jax import lax
from jax.experimental import pallas as pl
from jax.experimental.pallas import tpu as pltpu

F32 = jnp.float32
BF16 = jnp.bfloat16
I32 = jnp.int32
U32 = jnp.uint32
HIGHEST = lax.Precision.HIGHEST

D_MODEL = 4096
CHUNK = 64
N_BRANCHES = 4
BRANCH_WIDTH = D_MODEL // N_BRANCHES
DK_A = 128
H_A = BRANCH_WIDTH // DK_A
CONV_A = 4
DH_B = 128
H_B = BRANCH_WIDTH // DH_B
DC_B = 256
H_IDX = 8
D_IDX = 64
DSA_TOPK = 256
Q_BLOCK = 128
N_BUCKETS = 32
T5_MAX_DISTANCE = 128
CONV_C = 3
D_INNER = BRANCH_WIDTH
P_D = 64
H_D = D_INNER // P_D
N_GROUPS = 2
D_STATE = 128
CONV_D = 4
XBC_WIDTH = D_INNER + 2 * N_GROUPS * D_STATE
N_EXPERTS = 8
TOP_K = 2
EPS = 1e-6

MIX_SPLITS = (
    BRANCH_WIDTH, BRANCH_WIDTH, BRANCH_WIDTH, BRANCH_WIDTH, H_A, H_A,
    H_B * DH_B, DC_B, H_IDX * D_IDX, D_IDX, H_IDX,
    BRANCH_WIDTH, BRANCH_WIDTH, BRANCH_WIDTH,
    D_INNER, XBC_WIDTH, H_D,
)
MIX_COLS = sum(MIX_SPLITS)
_OFF = [0]
for _w in MIX_SPLITS:
    _OFF.append(_OFF[-1] + _w)
(O_AQ, O_AK, O_AV, O_AZ, O_AA, O_AB, O_BQ, O_BCKV, O_BQI, O_BKI, O_BWI,
 O_CB, O_CC, O_CH, O_DZ, O_DXBC, O_DDT, _O_END) = _OFF

P_AQKV = 0
P_AZ = 3072
P_BQ = 4096
P_C = 5120
P_DZ = 8192
P_DXBC = 9216
P_BQI = 10752
P_BCKV = 11264
P_COLS = 11520
S_AA = 0
S_AB = 8
S_BKI = 16
S_BWI = 80
S_DDT = 88
S_COLS = 128

LANES = 128
VMEM_LIMIT_MB = 56


def _cparams(sem, vmem_mb=VMEM_LIMIT_MB):
    return pltpu.CompilerParams(dimension_semantics=sem, vmem_limit_bytes=vmem_mb * 2 ** 20)


def _softplus(x):
    return jnp.maximum(x, 0.0) + jnp.log(1.0 + jnp.exp(-jnp.abs(x)))


def _sigmoid(x):
    return 1.0 / (1.0 + jnp.exp(-x))


def _silu(x):
    return x * _sigmoid(x)


def _dot(a, b, precision=None):
    return jnp.dot(a, b, preferred_element_type=F32, precision=precision)


def _dot_nt(a, b, precision=None):
    return lax.dot_general(a, b, (((1,), (1,)), ((), ())), preferred_element_type=F32, precision=precision)


def _dot_tn(a, b, precision=None):
    return lax.dot_general(a, b, (((0,), (0,)), ((), ())), preferred_element_type=F32, precision=precision)


def _rmsnorm_kernel(x_ref, g_ref, o_ref):
    x = x_ref[...].astype(F32)
    ms = jnp.mean(x * x, axis=-1, keepdims=True)
    o_ref[...] = (x * lax.rsqrt(ms + EPS) * g_ref[...]).astype(o_ref.dtype)


def rmsnorm(x, g, out_dtype=BF16, tm=512):
    m, d = x.shape
    tm = min(tm, m)
    return pl.pallas_call(
        _rmsnorm_kernel,
        grid=(m // tm,),
        in_specs=[pl.BlockSpec((tm, d), lambda i: (i, 0)), pl.BlockSpec((1, d), lambda i: (0, 0))],
        out_specs=pl.BlockSpec((tm, d), lambda i: (i, 0)),
        out_shape=jax.ShapeDtypeStruct((m, d), out_dtype),
        compiler_params=_cparams(("parallel",)),
        name="rmsnorm",
    )(x, g.reshape(1, d).astype(F32))


def _mm_kernel(*refs, nk, has_res):
    if has_res:
        a_ref, b_ref, r_ref = refs[:3]
        rest = refs[3:]
    else:
        a_ref, b_ref = refs[:2]
        r_ref = None
        rest = refs[2:]
    o_ref = rest[0]
    if nk == 1:
        acc = _dot(a_ref[...], b_ref[...].astype(BF16))
        if r_ref is not None:
            acc = acc + r_ref[...].astype(F32)
        o_ref[...] = acc.astype(o_ref.dtype)
        return
    acc_ref = rest[1]
    k = pl.program_id(2)

    @pl.when(k == 0)
    def _():
        acc_ref[...] = jnp.zeros_like(acc_ref)

    acc_ref[...] += _dot(a_ref[...], b_ref[...].astype(BF16))

    @pl.when(k == nk - 1)
    def _():
        acc = acc_ref[...]
        if r_ref is not None:
            acc = acc + r_ref[...].astype(F32)
        o_ref[...] = acc.astype(o_ref.dtype)


def matmul(a, b, *, out_dtype, residual=None, tm=1024, tn=512, tk=None, layer=None, name="matmul"):
    m, kdim = a.shape
    n = b.shape[-1]
    tm, tn = min(tm, m), min(tn, n)
    tk = kdim if tk is None else min(tk, kdim)
    nk = kdim // tk
    assert m % tm == 0 and n % tn == 0 and kdim % tk == 0, (a.shape, b.shape, tm, tn, tk)
    if layer is None:
        b_spec = pl.BlockSpec((tk, tn), lambda i, j, k: (k, j))
    else:
        b_spec = pl.BlockSpec((None, tk, tn), lambda i, j, k: (layer, k, j))
    in_specs = [pl.BlockSpec((tm, tk), lambda i, j, k: (i, k)), b_spec]
    args = [a, b]
    if residual is not None:
        in_specs.append(pl.BlockSpec((tm, tn), lambda i, j, k: (i, j)))
        args.append(residual)
    scratch = [pltpu.VMEM((tm, tn), F32)] if nk > 1 else []
    return pl.pallas_call(
        functools.partial(_mm_kernel, nk=nk, has_res=residual is not None),
        grid=(m // tm, n // tn, nk),
        in_specs=in_specs,
        out_specs=pl.BlockSpec((tm, tn), lambda i, j, k: (i, j)),
        out_shape=jax.ShapeDtypeStruct((m, n), out_dtype),
        scratch_shapes=scratch,
        compiler_params=_cparams(("parallel", "parallel", "arbitrary")),
        name=name,
    )(*args)


_PAD_ROWS = 8


def _causal_conv(stage_ref, x, w_ref, ksize):
    t = x.shape[0]
    stage_ref[0:_PAD_ROWS, :] = jnp.zeros((_PAD_ROWS, x.shape[1]), F32)
    stage_ref[_PAD_ROWS:_PAD_ROWS + t, :] = x
    acc = x * w_ref[ksize - 1:ksize, :]
    for j in range(ksize - 1):
        s = ksize - 1 - j
        acc = acc + stage_ref[_PAD_ROWS - s:_PAD_ROWS - s + t, :] * w_ref[j:j + 1, :]
    return acc


def _conv_xbc_kernel(x_ref, w_ref, b_ref, o_ref, stage_ref):
    y = _causal_conv(stage_ref, x_ref[...].astype(F32), w_ref, CONV_D) + b_ref[...]
    o_ref[...] = _silu(y)


def _conv_gated_kernel(bg_ref, cg_ref, h_ref, w_ref, o_ref, stage_ref):
    u = cg_ref[...].astype(F32) * h_ref[...].astype(F32)
    y = _causal_conv(stage_ref, u, w_ref, CONV_C)
    o_ref[...] = (bg_ref[...].astype(F32) * y).astype(o_ref.dtype)


def _conv_call(kernel, p3, col_offsets, width, extra, out_dtype, tc, name):
    bsz, t, _ = p3.shape
    tc = min(tc, width)
    in_specs = [pl.BlockSpec((None, t, tc), functools.partial(lambda b, c, o: (b, 0, o + c), o=off // tc))
                for off in col_offsets]
    args = [p3] * len(col_offsets)
    for e in extra:
        in_specs.append(pl.BlockSpec((e.shape[0], tc), lambda b, c: (0, c)))
        args.append(e)
    return pl.pallas_call(
        kernel,
        grid=(bsz, width // tc),
        in_specs=in_specs,
        out_specs=pl.BlockSpec((None, t, tc), lambda b, c: (b, 0, c)),
        out_shape=jax.ShapeDtypeStruct((bsz, t, width), out_dtype),
        scratch_shapes=[pltpu.VMEM((t + _PAD_ROWS, tc), F32)],
        compiler_params=_cparams(("parallel", "parallel")),
        name=name,
    )(*args)


def _tril_mask(n, strict=False):
    r = lax.broadcasted_iota(I32, (n, n), 0)
    c = lax.broadcasted_iota(I32, (n, n), 1)
    return (r > c) if strict else (r >= c)


def _chunk_cumsum(x):
    tril = _tril_mask(CHUNK).astype(F32)
    cs = _dot(tril, x, HIGHEST)
    padded = jnp.concatenate([cs, jnp.zeros((LANES - CHUNK, LANES), F32)], axis=0)
    return cs, padded.T[:, :CHUNK]


def _segment_decay(cs, cs_t, c):
    tril = _tril_mask(CHUNK)
    diff = cs[:, c:c + 1] - cs_t[c:c + 1, :]
    return jnp.where(tril, jnp.exp(jnp.where(tril, diff, 0.0)), 0.0)


_INV_BLOCK = 16
_GDN_GROUP = 2
_GDN_CHUNKS = 4
_CONV_COLS = 512
_CHUNK_SHIFT = CHUNK.bit_length() - 1


def _split_bf16(a):
    hi = a.astype(BF16)
    return hi, (a - hi.astype(F32)).astype(BF16)


def _dot3(a, b):
    a_hi, a_lo = a if isinstance(a, tuple) else _split_bf16(a)
    b_hi, b_lo = b if isinstance(b, tuple) else _split_bf16(b)
    return _dot(a_hi, b_hi) + _dot(a_hi, b_lo) + _dot(a_lo, b_hi)


def _unit_lower_inverses(lows):
    n_rows = lows[0].shape[0]
    r = lax.broadcasted_iota(I32, (n_rows, n_rows), 0)
    c = lax.broadcasted_iota(I32, (n_rows, n_rows), 1)
    shift = _INV_BLOCK.bit_length() - 1
    same = (r >> shift) == (c >> shift)
    eye = (r == c).astype(F32)
    each = lambda fn, *lists: [fn(*args) for args in zip(*lists)]
    lds = each(lambda low: jnp.where(same, low, 0.0), lows)
    los = each(lambda low: jnp.where(same, 0.0, low), lows)
    ps = each(lambda ld: eye - ld, lds)
    xss = each(_split_bf16, lds)
    xs = each(_dot3, xss, xss)
    for _ in range(2):
        xss = each(_split_bf16, xs)
        ps = each(lambda p, x: p + _dot3(p, x), ps, xss)
        xs = each(_dot3, xss, xss)
    ps = each(lambda p, x: p + _dot3(p, x), ps, xs)
    pss = each(_split_bf16, ps)
    ns = each(_dot3, pss, los)
    nss = each(_split_bf16, ns)
    n2s = each(_dot3, nss, nss)
    rrs = each(lambda n, n2: (eye - n) + _dot3(eye - n, n2), ns, n2s)
    return each(_dot3, rrs, pss)


def _gdn_kernel(pq_ref, pk_ref, pv_ref, z_ref, sm_ref, cw_ref, alog_ref, dtb_ref, ng_ref, o_ref,
                state_ref, stage_ref, qkv_ref):
    tt = sm_ref.shape[0]

    @pl.when(pl.program_id(1) == 0)
    def _():
        state_ref[...] = jnp.zeros_like(state_ref)
        stage_ref[0:_PAD_ROWS, :] = jnp.zeros((_PAD_ROWS, stage_ref.shape[1]), F32)

    for idx, ref in enumerate((pq_ref, pk_ref, pv_ref)):
        stage_ref[_PAD_ROWS:_PAD_ROWS + tt, idx * BRANCH_WIDTH:(idx + 1) * BRANCH_WIDTH] = ref[...].astype(F32)
    for c0 in range(0, 3 * BRANCH_WIDTH, _CONV_COLS):
        cols = slice(c0, c0 + _CONV_COLS)
        acc = stage_ref[_PAD_ROWS:_PAD_ROWS + tt, cols] * cw_ref[CONV_A - 1:CONV_A, cols]
        for j in range(CONV_A - 1):
            s = CONV_A - 1 - j
            acc = acc + stage_ref[_PAD_ROWS - s:_PAD_ROWS - s + tt, cols] * cw_ref[j:j + 1, cols]
        y = _silu(acc)
        for g0 in range(0, _CONV_COLS, DK_A):
            ys = y[:, g0:g0 + DK_A]
            if c0 < 2 * BRANCH_WIDTH:
                inv = lax.rsqrt(jnp.sum(ys * ys, axis=-1, keepdims=True) + EPS)
                ys = ys * (inv * DK_A ** -0.5 if c0 < BRANCH_WIDTH else inv)
            qkv_ref[:, c0 + g0:c0 + g0 + DK_A] = ys
    stage_ref[0:_PAD_ROWS, :] = stage_ref[tt:tt + _PAD_ROWS, :]
    q_ref = qkv_ref.at[:, 0:BRANCH_WIDTH]
    k_ref = qkv_ref.at[:, BRANCH_WIDTH:2 * BRANCH_WIDTH]
    v_ref = qkv_ref.at[:, 2 * BRANCH_WIDTH:3 * BRANCH_WIDTH]

    ng = ng_ref[...]
    lane = lax.broadcasted_iota(I32, (CHUNK, LANES), 1)
    rows = _GDN_GROUP * CHUNK
    r = lax.broadcasted_iota(I32, (rows, rows), 0)
    c = lax.broadcasted_iota(I32, (rows, rows), 1)
    same_head = (r >> _CHUNK_SHIFT) == (c >> _CHUNK_SHIFT)
    tril = same_head & (r >= c)
    strict = same_head & (r > c)
    groups = [range(grp * _GDN_GROUP, (grp + 1) * _GDN_GROUP) for grp in range(H_A // _GDN_GROUP)]

    n_chunks = sm_ref.shape[0] // CHUNK
    e_lasts, problems = [], []
    for ci in range(n_chunks):
        ts = slice(ci * CHUNK, (ci + 1) * CHUNK)
        sm = sm_ref[ts, :]
        g = -jnp.exp(alog_ref[...]) * _softplus(sm + dtb_ref[...])
        g = jnp.where(lane < S_AA + H_A, g, 0.0)
        beta = _sigmoid(sm)
        gc, gc_t = _chunk_cumsum(g)
        g_last = gc[CHUNK - 1:CHUNK, :]
        e_gc = jnp.exp(gc)
        e_rem = jnp.exp(g_last - gc)
        e_lasts.append(jnp.exp(g_last))
        for heads in groups:
            stack = lambda ref: jnp.concatenate([ref[ts, h * DK_A:(h + 1) * DK_A] for h in heads], axis=0)
            col = lambda arr, off: jnp.concatenate([arr[:, off + h:off + h + 1] for h in heads], axis=0)
            q_st, k_st, v_st = stack(q_ref), stack(k_ref), stack(v_ref)
            beta_st, egc_st, erem_st = col(beta, S_AB), col(e_gc, S_AA), col(e_rem, S_AA)
            g_row = jnp.concatenate([gc_t[S_AA + h:S_AA + h + 1, :] for h in heads], axis=1)
            diff = col(gc, S_AA) - g_row
            decay = jnp.where(tril, jnp.exp(jnp.where(tril, diff, 0.0)), 0.0)
            kb_st = k_st * beta_st
            problems.append(dict(q=q_st, k=k_st, kb=kb_st, decay=decay, qd=q_st * egc_st, kd=k_st * erem_st,
                                 rhs=jnp.concatenate([v_st * beta_st, kb_st * egc_st], axis=1)))
    lows = [jnp.where(strict, _dot_nt(p["kb"], p["k"]) * p["decay"], 0.0) for p in problems]
    tinvs = _unit_lower_inverses(lows)
    uws = [_dot3(tinv, p["rhs"]) for tinv, p in zip(tinvs, problems)]
    intras = [_dot_nt(p["q"], p["k"]) * p["decay"] for p in problems]

    head_rows = lambda gi_j: slice(gi_j * CHUNK, (gi_j + 1) * CHUNK)
    for ci in range(n_chunks):
        ts = slice(ci * CHUNK, (ci + 1) * CHUNK)
        e_last = e_lasts[ci]
        probs = range(ci * len(groups), (ci + 1) * len(groups))
        states = [state_ref[h] for h in range(H_A)]
        wq_s = {}
        for pi, heads in zip(probs, groups):
            for j, h in enumerate(heads):
                rs = head_rows(j)
                lhs = jnp.concatenate([uws[pi][rs, DK_A:], problems[pi]["qd"][rs]], axis=0)
                wq_s[h] = _dot(lhs, states[h])
        v_new = {h: uws[pi][head_rows(j), :DK_A] - wq_s[h][:CHUNK]
                 for pi, heads in zip(probs, groups) for j, h in enumerate(heads)}
        outs = [jnp.concatenate([wq_s[h][CHUNK:] for h in heads], axis=0)
                + _dot(intras[pi], jnp.concatenate([v_new[h] for h in heads], axis=0))
                for pi, heads in zip(probs, groups)]
        for pi, heads in zip(probs, groups):
            for j, h in enumerate(heads):
                ca = S_AA + h
                state_ref[h] = states[h] * e_last[:, ca:ca + 1] + _dot_tn(problems[pi]["kd"][head_rows(j)], v_new[h])
        for out, heads in zip(outs, groups):
            ms = jnp.mean(out * out, axis=-1, keepdims=True)
            out = out * lax.rsqrt(ms + EPS) * ng
            for j, h in enumerate(heads):
                sl = slice(h * DK_A, (h + 1) * DK_A)
                o_ref[ts, sl] = (out[head_rows(j)] * _silu(z_ref[ts, sl].astype(F32))).astype(o_ref.dtype)


def gated_deltanet(p3, small3, conv_w, a_log, dt_bias, norm_g):
    bsz, t, _ = p3.shape
    tt = _GDN_CHUNKS * CHUNK if t % (_GDN_CHUNKS * CHUNK) == 0 else CHUNK

    def row(vals, off):
        return jnp.zeros((1, S_COLS), F32).at[0, off:off + vals.shape[0]].set(vals.astype(F32))

    w = BRANCH_WIDTH
    return pl.pallas_call(
        _gdn_kernel,
        grid=(bsz, t // tt),
        in_specs=[
            pl.BlockSpec((None, tt, w), lambda b, n: (b, n, 0)),
            pl.BlockSpec((None, tt, w), lambda b, n: (b, n, 1)),
            pl.BlockSpec((None, tt, w), lambda b, n: (b, n, 2)),
            pl.BlockSpec((None, tt, w), lambda b, n: (b, n, P_AZ // w)),
            pl.BlockSpec((None, tt, S_COLS), lambda b, n: (b, n, 0)),
            pl.BlockSpec((CONV_A, 3 * w), lambda b, n: (0, 0)),
            pl.BlockSpec((1, S_COLS), lambda b, n: (0, 0)),
            pl.BlockSpec((1, S_COLS), lambda b, n: (0, 0)),
            pl.BlockSpec((1, DK_A), lambda b, n: (0, 0)),
        ],
        out_specs=pl.BlockSpec((None, tt, w), lambda b, n: (b, n, 0)),
        out_shape=jax.ShapeDtypeStruct((bsz, t, w), BF16),
        scratch_shapes=[pltpu.VMEM((H_A, DK_A, DK_A), F32), pltpu.VMEM((tt + _PAD_ROWS, 3 * w), F32),
                        pltpu.VMEM((tt, 3 * w), F32)],
        compiler_params=_cparams(("parallel", "arbitrary")),
        name="gated_deltanet",
    )(p3, p3, p3, p3, small3, conv_w.astype(F32), row(a_log, S_AA), row(dt_bias, S_AA),
      norm_g.reshape(1, DK_A).astype(F32))


def _ssd_kernel(xbc_ref, z_ref, sm_ref, alog_ref, dtb_ref, dskip_ref, ng_ref, o_ref, state_ref):
    @pl.when(pl.program_id(1) == 0)
    def _():
        state_ref[...] = jnp.zeros_like(state_ref)

    sm = sm_ref[...]
    lane = lax.broadcasted_iota(I32, (CHUNK, LANES), 1)
    dt = _softplus(sm + dtb_ref[...])
    a = jnp.where((lane >= S_DDT) & (lane < S_DDT + H_D), dt * -jnp.exp(alog_ref[...]), 0.0)
    acs, acs_t = _chunk_cumsum(a)
    a_last = acs[CHUNK - 1:CHUNK, :]
    e_acs = jnp.exp(acs)
    e_rem = jnp.exp(a_last - acs)
    e_last = jnp.exp(a_last)
    heads_per_group = H_D // N_GROUPS
    group_of = lambda h: h // heads_per_group
    bms = [xbc_ref[:, D_INNER + grp * D_STATE:D_INNER + (grp + 1) * D_STATE] for grp in range(N_GROUPS)]
    c0 = D_INNER + N_GROUPS * D_STATE
    cms = [xbc_ref[:, c0 + grp * D_STATE:c0 + (grp + 1) * D_STATE] for grp in range(N_GROUPS)]
    cbs = [_dot_nt(cm, bm) for cm, bm in zip(cms, bms)]
    xcs = [xbc_ref[:, h * P_D:(h + 1) * P_D] * dt[:, S_DDT + h:S_DDT + h + 1] for h in range(H_D)]
    states = [state_ref[h] for h in range(H_D)]
    ys = []
    for h in range(H_D):
        c = S_DDT + h
        ys.append(_dot(cbs[group_of(h)] * _segment_decay(acs, acs_t, c), xcs[h])
                  + _dot_nt(cms[group_of(h)] * e_acs[:, c:c + 1], states[h]))
    for h in range(H_D):
        c = S_DDT + h
        state_ref[h] = states[h] * e_last[:, c:c + 1] + _dot_tn(xcs[h], bms[group_of(h)] * e_rem[:, c:c + 1])
    y = jnp.concatenate(ys, axis=1) + dskip_ref[...] * xbc_ref[:, 0:D_INNER]
    y = y * _silu(z_ref[...].astype(F32))
    gw = D_INNER // N_GROUPS
    for grp in range(N_GROUPS):
        yg = y[:, grp * gw:(grp + 1) * gw]
        ms = jnp.mean(yg * yg, axis=-1, keepdims=True)
        o_ref[:, grp * gw:(grp + 1) * gw] = (yg * lax.rsqrt(ms + EPS) * ng_ref[:, grp * gw:(grp + 1) * gw]).astype(o_ref.dtype)


def mamba2_ssd(xbc3, p3, small3, a_log, dt_bias, d_skip, norm_g):
    bsz, t, _ = xbc3.shape
    nchunks = t // CHUNK

    def row(vals, off):
        return jnp.zeros((1, S_COLS), F32).at[0, off:off + vals.shape[0]].set(vals.astype(F32))

    return pl.pallas_call(
        _ssd_kernel,
        grid=(bsz, nchunks),
        in_specs=[
            pl.BlockSpec((None, CHUNK, XBC_WIDTH), lambda b, n: (b, n, 0)),
            pl.BlockSpec((None, CHUNK, D_INNER), lambda b, n: (b, n, P_DZ // D_INNER)),
            pl.BlockSpec((None, CHUNK, S_COLS), lambda b, n: (b, n, 0)),
            pl.BlockSpec((1, S_COLS), lambda b, n: (0, 0)),
            pl.BlockSpec((1, S_COLS), lambda b, n: (0, 0)),
            pl.BlockSpec((1, D_INNER), lambda b, n: (0, 0)),
            pl.BlockSpec((1, D_INNER), lambda b, n: (0, 0)),
        ],
        out_specs=pl.BlockSpec((None, CHUNK, D_INNER), lambda b, n: (b, n, 0)),
        out_shape=jax.ShapeDtypeStruct((bsz, t, D_INNER), BF16),
        scratch_shapes=[pltpu.VMEM((H_D, P_D, D_STATE), F32)],
        compiler_params=_cparams(("parallel", "arbitrary")),
        name="mamba2_ssd",
    )(xbc3, p3, small3, row(a_log, S_DDT), row(dt_bias, S_DDT),
      jnp.repeat(d_skip.astype(F32), P_D).reshape(1, D_INNER), norm_g.reshape(1, D_INNER).astype(F32))


def _dsa_prep_kernel(q_ref, ckv_ref, sm_ref, wuk_ref, cg_ref, qg_ref, kg_ref, qh_ref, kk_ref, ct_ref, ki_ref):
    ckv = ckv_ref[...].astype(F32)
    c = ckv * lax.rsqrt(jnp.mean(ckv * ckv, axis=-1, keepdims=True) + EPS) * cg_ref[...]
    cb = c.astype(BF16)
    ct_ref[...] = c.T.astype(BF16)
    kk = _dot(cb, wuk_ref[...])
    kk_ref[...] = (kk * lax.rsqrt(jnp.mean(kk * kk, axis=-1, keepdims=True) + EPS) * kg_ref[...]).astype(BF16)
    for h in range(H_B):
        sl = slice(h * DH_B, (h + 1) * DH_B)
        qh = q_ref[:, sl].astype(F32)
        qh = qh * lax.rsqrt(jnp.mean(qh * qh, axis=-1, keepdims=True) + EPS) * qg_ref[...]
        qh_ref[:, sl] = (qh * DH_B ** -0.5).astype(BF16)
    ki_ref[...] = sm_ref[:, S_BKI:S_BKI + D_IDX].astype(BF16)


def dsa_prep(p3, small3, w_uk, ckv_g, q_g, k_g, tt=512):
    bsz, t, _ = p3.shape
    tt = min(tt, t)
    w = H_B * DH_B
    return pl.pallas_call(
        _dsa_prep_kernel,
        grid=(bsz, t // tt),
        in_specs=[
            pl.BlockSpec((None, tt, w), lambda b, i: (b, i, P_BQ // w)),
            pl.BlockSpec((None, tt, DC_B), lambda b, i: (b, i, P_BCKV // DC_B)),
            pl.BlockSpec((None, tt, S_COLS), lambda b, i: (b, i, 0)),
            pl.BlockSpec((DC_B, DH_B), lambda b, i: (0, 0)),
            pl.BlockSpec((1, DC_B), lambda b, i: (0, 0)),
            pl.BlockSpec((1, DH_B), lambda b, i: (0, 0)),
            pl.BlockSpec((1, DH_B), lambda b, i: (0, 0)),
        ],
        out_specs=[
            pl.BlockSpec((None, tt, w), lambda b, i: (b, i, 0)),
            pl.BlockSpec((None, tt, DH_B), lambda b, i: (b, i, 0)),
            pl.BlockSpec((None, DC_B, tt), lambda b, i: (b, 0, i)),
            pl.BlockSpec((None, tt, D_IDX), lambda b, i: (b, i, 0)),
        ],
        out_shape=[
            jax.ShapeDtypeStruct((bsz, t, w), BF16),
            jax.ShapeDtypeStruct((bsz, t, DH_B), BF16),
            jax.ShapeDtypeStruct((bsz, DC_B, t), BF16),
            jax.ShapeDtypeStruct((bsz, t, D_IDX), BF16),
        ],
        compiler_params=_cparams(("parallel", "parallel")),
        name="dsa_prep",
    )(p3, p3, small3, w_uk.astype(BF16), ckv_g.reshape(1, DC_B).astype(F32),
      q_g.reshape(1, DH_B).astype(F32), k_g.reshape(1, DH_B).astype(F32))


_T5_HALF = N_BUCKETS // 2
_T5_EXACT = _T5_HALF // 2
_T5_FAR = _T5_HALF - 1


def _relbias_kernel(rb_ref, prev_ref, diag_ref):
    h = pl.program_id(0)
    kl = lax.broadcasted_iota(I32, (Q_BLOCK, Q_BLOCK), 0)
    ql = lax.broadcasted_iota(I32, (Q_BLOCK, Q_BLOCK), 1)
    far = rb_ref[_T5_FAR, h]
    for ref, shift in ((prev_ref, -Q_BLOCK), (diag_ref, 0)):
        rel = kl - ql + shift
        n = jnp.abs(rel)
        n2 = n * n
        steps = jnp.zeros_like(n)
        for j in range(1, _T5_HALF - _T5_EXACT):
            steps = steps + (n2 >= (_T5_EXACT * _T5_EXACT) * 2 ** j).astype(I32)
        large = jnp.minimum(_T5_EXACT + steps, _T5_HALF - 1)
        bucket = jnp.where(rel > 0, _T5_HALF, 0) + jnp.where(n < _T5_EXACT, n, large)
        acc = jnp.zeros((Q_BLOCK, Q_BLOCK), F32)
        for b in range(N_BUCKETS):
            acc = jnp.where(bucket == b, rb_ref[b, h], acc)
        ref[...] = acc - far


def relbias_tables(rel_bias):
    shp = jax.ShapeDtypeStruct((H_B, Q_BLOCK, Q_BLOCK), F32)
    spec = pl.BlockSpec((None, Q_BLOCK, Q_BLOCK), lambda h: (h, 0, 0))
    return pl.pallas_call(
        _relbias_kernel,
        grid=(H_B,),
        in_specs=[pl.BlockSpec(memory_space=pltpu.SMEM)],
        out_specs=[spec, spec],
        out_shape=[shp, shp],
        compiler_params=_cparams(("arbitrary",)),
        name="relbias_tables",
    )(rel_bias.astype(F32))


_INT_MIN = -2 ** 31
_SCORE_ROWS = 256
_SWEEP_ROWS = 64
_DSA_WIDTH_STEP = 512


def _sweep_rows(n_rows, init, step):
    def body(c, acc):
        return step(pl.multiple_of(c * _SWEEP_ROWS, _SWEEP_ROWS), acc)
    n_steps = n_rows // _SWEEP_ROWS
    return lax.fori_loop(0, n_steps, body, init, unroll=min(8, n_steps))


def _dsa_kernel(qh_ref, qi_ref, sm_ref, kk_ref, ct_ref, ki_ref, dprev_ref, ddiag_ref, wuvt_ref,
                o_ref, key_ref, selb_ref, lg_ref, p_ref, *, widths, topk):
    i = pl.program_id(1)
    need = (i + 1) * Q_BLOCK
    wi_t = sm_ref[...].T[S_BWI:S_BWI + H_IDX, :] * (H_IDX ** -0.5 * D_IDX ** -0.5)
    qi_all = jnp.concatenate([qi_ref[:, h * D_IDX:(h + 1) * D_IDX] for h in range(H_IDX)], axis=0)
    q_chunk = (i * Q_BLOCK + lax.broadcasted_iota(I32, (1, Q_BLOCK), 1)) >> _CHUNK_SHIFT
    row_iota = lax.broadcasted_iota(I32, (_SWEEP_ROWS, Q_BLOCK), 0)
    block = lambda ref, r0: ref[pl.ds(r0, _SWEEP_ROWS), :]

    def body(nc):
        for r0 in range(0, nc, min(_SCORE_ROWS, nc)):
            rows = slice(r0, r0 + min(_SCORE_ROWS, nc))
            s_all = _dot_nt(ki_ref[rows, :], qi_all)
            sc = None
            for h in range(H_IDX):
                term = wi_t[h:h + 1, :] * jnp.maximum(s_all[:, h * Q_BLOCK:(h + 1) * Q_BLOCK], 0.0)
                sc = term if sc is None else sc + term
            bits = lax.bitcast_convert_type(sc + 0.0, I32)
            key = jnp.where(bits < 0, bits ^ 0x7FFFFFFF, bits)
            kpos = r0 + lax.broadcasted_iota(I32, key.shape, 0)
            key_ref[rows, :] = jnp.where((kpos >> _CHUNK_SHIFT) <= q_chunk, key, _INT_MIN)

        def count(pred):
            acc = _sweep_rows(nc, jnp.zeros((_SWEEP_ROWS, Q_BLOCK), I32),
                              lambda r0, acc: acc + pred(r0, block(key_ref, r0)).astype(I32))
            return jnp.sum(acc, axis=0, keepdims=True)

        tau = jnp.where(count(lambda r0, k: k >= 0) >= topk, 0, _INT_MIN).astype(I32)

        def vstep(it, tau):
            cand = tau | (jnp.int32(1) << (30 - it))
            return jnp.where(count(lambda r0, k: k >= cand) >= topk, cand, tau)

        tau = lax.fori_loop(0, 31, vstep, tau)
        zeros = jnp.zeros((_SWEEP_ROWS, Q_BLOCK), I32)
        acc_gt, acc_eq = _sweep_rows(
            nc, (zeros, zeros),
            lambda r0, acc: (acc[0] + (block(key_ref, r0) > tau).astype(I32),
                             acc[1] + (block(key_ref, r0) == tau).astype(I32)))
        n_gt = jnp.sum(acc_gt, axis=0, keepdims=True)
        n_eq = jnp.sum(acc_eq, axis=0, keepdims=True)
        tied = (n_gt + n_eq > topk) & (tau > _INT_MIN)
        nbits = max(1, (nc - 1).bit_length())

        def last_tied_index():
            room = topk - n_gt

            def istep(it, last):
                cand = last | (jnp.int32(1) << (nbits - 1 - it))
                below = count(lambda r0, k: (k == tau) & (r0 + row_iota < cand))
                return jnp.where(below < room, cand, last)

            return lax.fori_loop(0, nbits, istep, jnp.zeros((1, Q_BLOCK), I32))

        last = lax.cond(jnp.max(tied.astype(I32)) > 0, last_tied_index,
                        lambda: jnp.full((1, Q_BLOCK), nc, I32))

        def write_sel(r0, carry):
            k = block(key_ref, r0)
            sel = (k > _INT_MIN) & ((k > tau) | ((k == tau) & (r0 + row_iota <= last)))
            selb_ref[pl.ds(r0, _SWEEP_ROWS), :] = jnp.where(sel, 0.0, -jnp.inf)
            return carry

        _sweep_rows(nc, 0, write_sel)

        kk = kk_ref[0:nc, :]
        ct = ct_ref[:, 0:nc]
        prev_row = pl.multiple_of(jnp.maximum(i - 1, 0) * Q_BLOCK, Q_BLOCK)
        diag_row = pl.multiple_of(i * Q_BLOCK, Q_BLOCK)
        has_prev = (i > 0).astype(F32)
        for h0 in range(0, H_B, 2):
            q_pair = jnp.concatenate([qh_ref[:, h * DH_B:(h + 1) * DH_B] for h in (h0, h0 + 1)], axis=0)
            lg_pair = _dot_nt(kk, q_pair)
            for j, h in enumerate((h0, h0 + 1)):
                lg_ref[h, 0:nc, :] = lg_pair[:, j * Q_BLOCK:(j + 1) * Q_BLOCK] + selb_ref[0:nc, :]
                lg_ref[h, pl.ds(prev_row, Q_BLOCK), :] += dprev_ref[h] * has_prev
                lg_ref[h, pl.ds(diag_row, Q_BLOCK), :] += ddiag_ref[h]
        blk = min(_SCORE_ROWS, nc)
        outs = []
        for h in range(H_B):
            slot = h % 2
            m_acc = None
            for r0 in range(0, nc, blk):
                x = lg_ref[h, r0:r0 + blk, :]
                m_acc = x if m_acc is None else jnp.maximum(m_acc, x)
            m = jnp.max(m_acc, axis=0, keepdims=True)
            p_acc = None
            for r0 in range(0, nc, blk):
                p = jnp.exp(lg_ref[h, r0:r0 + blk, :] - m)
                p_ref[slot, r0:r0 + blk, :] = p.astype(BF16)
                p_acc = p if p_acc is None else p_acc + p
            denom = jnp.sum(p_acc, axis=0, keepdims=True)
            o_lat = _dot(ct, p_ref[slot, 0:nc, :]) * (1.0 / denom)
            outs.append(_dot(wuvt_ref[h], o_lat.astype(BF16)).T)
        o_ref[...] = jnp.concatenate(outs, axis=1).astype(o_ref.dtype)

    lo = 0
    for nc in widths:
        @pl.when((need > lo) & (need <= nc))
        def _(nc=nc):
            body(nc)
        lo = nc


def dsa_attention(qh3, qi_p3, small3, kk3, ct3, ki3, dprev, ddiag, wuv_t):
    bsz, t, w = qh3.shape
    topk = min(DSA_TOPK, t // 4)
    step = min(_DSA_WIDTH_STEP, t)
    widths = tuple(range(step, t + 1, step))
    wq = H_IDX * D_IDX
    full = lambda shape: pl.BlockSpec(shape, lambda b, i: (0,) * len(shape))
    return pl.pallas_call(
        functools.partial(_dsa_kernel, widths=widths, topk=topk),
        grid=(bsz, t // Q_BLOCK),
        in_specs=[
            pl.BlockSpec((None, Q_BLOCK, w), lambda b, i: (b, i, 0)),
            pl.BlockSpec((None, Q_BLOCK, wq), lambda b, i: (b, i, P_BQI // wq)),
            pl.BlockSpec((None, Q_BLOCK, S_COLS), lambda b, i: (b, i, 0)),
            pl.BlockSpec((None, t, DH_B), lambda b, i: (b, 0, 0)),
            pl.BlockSpec((None, DC_B, t), lambda b, i: (b, 0, 0)),
            pl.BlockSpec((None, t, D_IDX), lambda b, i: (b, 0, 0)),
            full((H_B, Q_BLOCK, Q_BLOCK)),
            full((H_B, Q_BLOCK, Q_BLOCK)),
            full((H_B, DH_B, DC_B)),
        ],
        out_specs=pl.BlockSpec((None, Q_BLOCK, w), lambda b, i: (b, i, 0)),
        out_shape=jax.ShapeDtypeStruct((bsz, t, w), BF16),
        scratch_shapes=[pltpu.VMEM((t, Q_BLOCK), I32), pltpu.VMEM((t, Q_BLOCK), F32),
                        pltpu.VMEM((H_B, t, Q_BLOCK), F32), pltpu.VMEM((2, t, Q_BLOCK), BF16)],
        compiler_params=_cparams(("parallel", "arbitrary")),
        name="dsa_attention",
    )(qh3, qi_p3, small3, kk3, ct3, ki3, dprev, ddiag, wuv_t)


def _merge_kernel(xn_ref, o0_ref, o1_ref, o2_ref, o3_ref, g0_ref, g1_ref, g2_ref, g3_ref,
                  u0_ref, u1_ref, u2_ref, u3_ref, out_ref):
    xn = xn_ref[...]
    acc = None
    for o_ref, g_ref, u_ref in ((o0_ref, g0_ref, u0_ref), (o1_ref, g1_ref, u1_ref),
                                (o2_ref, g2_ref, u2_ref), (o3_ref, g3_ref, u3_ref)):
        term = _sigmoid(_dot(xn, g_ref[...])) * _dot(o_ref[...], u_ref[...].astype(BF16))
        acc = term if acc is None else acc + term
    out_ref[...] = acc.astype(out_ref.dtype)


def merge_branches(xn, outs, w_gate, w_up, layer, tm=512, tn=256):
    m, d = xn.shape
    tm, tn = min(tm, m), min(tn, d)
    nj = d // tn
    deep = pl.Buffered(3)
    in_specs = [pl.BlockSpec((tm, d), lambda j, i: (i, 0), pipeline_mode=deep)]
    in_specs += [pl.BlockSpec((tm, BRANCH_WIDTH), lambda j, i: (i, 0), pipeline_mode=deep) for _ in range(N_BRANCHES)]
    in_specs += [pl.BlockSpec((d, tn), functools.partial(lambda j, i, br: (0, br * nj + j), br=br))
                 for br in range(N_BRANCHES)]
    in_specs += [pl.BlockSpec((None, None, BRANCH_WIDTH, tn),
                              functools.partial(lambda j, i, br: (layer, br, 0, j), br=br))
                 for br in range(N_BRANCHES)]

    def outer(*refs):
        pltpu.emit_pipeline(
            _merge_kernel,
            grid=(nj, m // tm),
            in_specs=in_specs,
            out_specs=pl.BlockSpec((tm, tn), lambda j, i: (i, j)),
        )(*refs)

    any_spec = pl.BlockSpec(memory_space=pl.ANY)
    return pl.pallas_call(
        outer,
        in_specs=[any_spec] * (1 + 3 * N_BRANCHES),
        out_specs=any_spec,
        out_shape=jax.ShapeDtypeStruct((m, d), BF16),
        compiler_params=pltpu.CompilerParams(vmem_limit_bytes=VMEM_LIMIT_MB * 2 ** 20),
        name="merge_branches",
    )(xn, *outs, *([w_gate] * N_BRANCHES), *([w_up] * N_BRANCHES))


def _glu_kernel(x_ref, w1_ref, w3_ref, o_ref):
    x = x_ref[...]
    o_ref[...] = (_silu(_dot(x, w1_ref[...].astype(BF16))) * _dot(x, w3_ref[...].astype(BF16))).astype(o_ref.dtype)


def glu_dense(x, w1, w3, tm=2048, tf=256):
    m, d = x.shape
    f = w1.shape[1]
    tm, tf = min(tm, m), min(tf, f)
    assert f % tf == 0
    return pl.pallas_call(
        _glu_kernel,
        grid=(f // tf, m // tm),
        in_specs=[pl.BlockSpec((tm, d), lambda j, i: (i, 0)),
                  pl.BlockSpec((d, tf), lambda j, i: (0, j)),
                  pl.BlockSpec((d, tf), lambda j, i: (0, j))],
        out_specs=pl.BlockSpec((tm, tf), lambda j, i: (i, j)),
        out_shape=jax.ShapeDtypeStruct((m, f), BF16),
        compiler_params=_cparams(("parallel", "parallel")),
        name="glu_dense",
    )(x, w1, w3)


MOE_TM = 512
MOE_TT = 256
MOE_UP_TN = 512
MOE_DOWN_TN = 1024
SLOT_POS_A, SLOT_POS_B, SLOT_W_A, SLOT_W_B = 0, 1, 2, 3
_ROW_COPY_UNROLL = 8


def _pack_bf16_pair(lo, hi):
    bits = lambda v: lax.bitcast_convert_type(v.astype(BF16).astype(F32), U32)
    return bits(hi) | (bits(lo) >> 16)


def _unpack_bf16_pair(u):
    return (lax.bitcast_convert_type(u << 16, F32),
            lax.bitcast_convert_type(u & jnp.uint32(0xFFFF0000), F32))


def _router_kernel(x_ref, g_ref, wr_ref, h_ref, gate_ref, sel_ref):
    x = x_ref[...]
    h = x * lax.rsqrt(jnp.mean(x * x, axis=-1, keepdims=True) + EPS) * g_ref[...]
    half = h.shape[1] // 2
    h_ref[...] = _pack_bf16_pair(h[:, :half], h[:, half:])
    logits = _dot3(h, wr_ref[...])
    lane = lax.broadcasted_iota(I32, logits.shape, 1)
    logits = jnp.where(lane < N_EXPERTS, logits, -jnp.inf)
    m1 = jnp.max(logits, axis=-1, keepdims=True)
    i1 = jnp.min(jnp.where(logits == m1, lane, LANES), axis=-1, keepdims=True)
    rest = jnp.where(lane == i1, -jnp.inf, logits)
    m2 = jnp.max(rest, axis=-1, keepdims=True)
    i2 = jnp.min(jnp.where(rest == m2, lane, LANES), axis=-1, keepdims=True)
    e2 = jnp.exp(m2 - m1)
    inv = 1.0 / (1.0 + e2)
    gate_ref[...] = jnp.where(lane == i1, inv, 0.0) + jnp.where(lane == i2, e2 * inv, 0.0)
    sel_ref[...] = jnp.where((lane == i1) | (lane == i2), 1.0, 0.0).astype(sel_ref.dtype)


def moe_router(x, g, w_router, tm=512):
    m, d = x.shape
    tm = min(tm, m)
    wr = jnp.zeros((d, LANES), F32).at[:, :N_EXPERTS].set(w_router.astype(F32))
    row = lambda w: pl.BlockSpec((tm, w), lambda i: (i, 0))
    return pl.pallas_call(
        _router_kernel,
        grid=(m // tm,),
        in_specs=[row(d), pl.BlockSpec((1, d), lambda i: (0, 0)), pl.BlockSpec((d, LANES), lambda i: (0, 0))],
        out_specs=[row(d // 2), row(LANES), row(LANES)],
        out_shape=[jax.ShapeDtypeStruct((m, d // 2), U32), jax.ShapeDtypeStruct((m, LANES), F32),
                   jax.ShapeDtypeStruct((m, LANES), BF16)],
        compiler_params=_cparams(("parallel",)),
        name="moe_router",
    )(x, g.reshape(1, d).astype(F32), wr)


def _rank_kernel(sel_ref, rank_ref, cnt_ref, carry_ref):
    @pl.when(pl.program_id(0) == 0)
    def _():
        carry_ref[...] = jnp.zeros_like(carry_ref)

    sel = sel_ref[...]
    n = sel.shape[0]
    earlier = _tril_mask(n, strict=True).astype(BF16)
    rank_ref[...] = _dot(earlier, sel) + carry_ref[...]
    carry_ref[...] += jnp.sum(sel.astype(F32), axis=0, keepdims=True)
    cnt_ref[...] = carry_ref[...]


def moe_rank(sel, tr=512):
    m = sel.shape[0]
    tr = min(tr, m)
    return pl.pallas_call(
        _rank_kernel,
        grid=(m // tr,),
        in_specs=[pl.BlockSpec((tr, LANES), lambda i: (i, 0))],
        out_specs=[pl.BlockSpec((tr, LANES), lambda i: (i, 0)), pl.BlockSpec((1, LANES), lambda i: (0, 0))],
        out_shape=[jax.ShapeDtypeStruct((m, LANES), F32), jax.ShapeDtypeStruct((1, LANES), F32)],
        scratch_shapes=[pltpu.VMEM((1, LANES), F32)],
        compiler_params=_cparams(("arbitrary",)),
        name="moe_rank",
    )(sel)


def _slots_kernel(rank_ref, sel_ref, gate_ref, start_ref, out_ref):
    sel = sel_ref[...].astype(F32) > 0.0
    lane = lax.broadcasted_iota(I32, sel.shape, 1)
    dest = start_ref[...] + rank_ref[...]
    first = jnp.min(jnp.where(sel, lane, LANES), axis=-1, keepdims=True)
    second = jnp.max(jnp.where(sel, lane, -1), axis=-1, keepdims=True)
    pick = lambda arr, idx: jnp.sum(jnp.where(lane == idx, arr, 0.0), axis=-1, keepdims=True)
    gates = gate_ref[...]
    out = jnp.where(lane == SLOT_POS_A, pick(dest, first), 0.0)
    out = jnp.where(lane == SLOT_POS_B, pick(dest, second), out)
    out = jnp.where(lane == SLOT_W_A, pick(gates, first), out)
    out_ref[...] = jnp.where(lane == SLOT_W_B, pick(gates, second), out)


def moe_slots(rank, sel, gates, start_row, tr=512):
    m = sel.shape[0]
    tr = min(tr, m)
    row = pl.BlockSpec((tr, LANES), lambda i: (i, 0))
    return pl.pallas_call(
        _slots_kernel,
        grid=(m // tr,),
        in_specs=[row, row, row, pl.BlockSpec((1, LANES), lambda i: (0, 0))],
        out_specs=row,
        out_shape=jax.ShapeDtypeStruct((m, LANES), F32),
        compiler_params=_cparams(("parallel",)),
        name="moe_slots",
    )(rank, sel, gates, start_row)


def _row_copy(src_ref, src_row, dst_ref, dst_row, sem):
    return pltpu.make_async_copy(src_ref.at[pl.ds(src_row, 1), :], dst_ref.at[pl.ds(dst_row, 1), :], sem)


def _dispatch_kernel(pos_ref, tail_ref, h_ref, xg_ref, zero_ref, sem, *, tm):
    tt = h_ref.shape[0]

    @pl.when(pl.program_id(0) == 0)
    def _():
        zero_ref[...] = jnp.zeros_like(zero_ref)
        fill = lambda e: pltpu.make_async_copy(zero_ref, xg_ref.at[pl.ds(pl.multiple_of(tail_ref[e], tm), tm), :], sem)
        for e in range(tail_ref.shape[0]):
            @pl.when(tail_ref[e] >= 0)
            def _(e=e):
                fill(e).start()
        for e in range(tail_ref.shape[0]):
            @pl.when(tail_ref[e] >= 0)
            def _(e=e):
                fill(e).wait()

    def start(r, carry):
        _row_copy(h_ref, r, xg_ref, pos_ref[0, r], sem).start(priority=0)
        _row_copy(h_ref, r, xg_ref, pos_ref[0, tt + r], sem).start(priority=1)
        return carry

    def wait(r, carry):
        _row_copy(h_ref, r, xg_ref, pos_ref[0, r], sem).wait()
        _row_copy(h_ref, r, xg_ref, pos_ref[0, tt + r], sem).wait()
        return carry

    lax.fori_loop(0, tt, start, 0, unroll=_ROW_COPY_UNROLL)
    lax.fori_loop(0, tt, wait, 0, unroll=_ROW_COPY_UNROLL)


def moe_dispatch(h, pos_tiles, tails, rows, tm):
    m, d = h.shape
    tt = pos_tiles.shape[2] // 2
    return pl.pallas_call(
        functools.partial(_dispatch_kernel, tm=tm),
        grid=(m // tt,),
        in_specs=[pl.BlockSpec((None, 1, 2 * tt), lambda i: (i, 0, 0), memory_space=pltpu.SMEM),
                  pl.BlockSpec(memory_space=pltpu.SMEM),
                  pl.BlockSpec((tt, d), lambda i: (i, 0))],
        out_specs=pl.BlockSpec(memory_space=pl.ANY),
        out_shape=jax.ShapeDtypeStruct((rows, d), h.dtype),
        scratch_shapes=[pltpu.VMEM((tm, d), h.dtype), pltpu.SemaphoreType.DMA(())],
        compiler_params=_cparams(("arbitrary",)),
        name="moe_dispatch",
    )(pos_tiles, tails, h)


def _glu_grouped_kernel(te_ref, nv_ref, x_ref, w1_ref, w3_ref, o_ref):
    valid = pl.program_id(1) < nv_ref[0]

    @pl.when(valid)
    def _():
        lo, hi = (v.astype(BF16) for v in _unpack_bf16_pair(x_ref[...]))
        half = lo.shape[1]
        up = lambda w_ref: (_dot(lo, w_ref[0:half, :].astype(BF16)) + _dot(hi, w_ref[half:2 * half, :].astype(BF16)))
        o_ref[...] = (_silu(up(w1_ref)) * up(w3_ref)).astype(o_ref.dtype)

    @pl.when(jnp.logical_not(valid))
    def _():
        o_ref[...] = jnp.zeros_like(o_ref)


def _down_grouped_kernel(te_ref, nv_ref, h_ref, w2_ref, o_ref):
    valid = pl.program_id(1) < nv_ref[0]

    @pl.when(valid)
    def _():
        y = _dot(h_ref[...], w2_ref[...].astype(BF16))
        half = y.shape[1] // 2
        o_ref[...] = _pack_bf16_pair(y[:, :half], y[:, half:])

    @pl.when(jnp.logical_not(valid))
    def _():
        o_ref[...] = jnp.zeros_like(o_ref)


def _grouped_call(kernel, x, weights, tile_expert, n_valid, n_col_tiles, tn, out_tn, out_dtype, tm, name):
    rows, d = x.shape
    tile = lambda i, nv: jnp.minimum(i, nv[0] - 1)
    in_specs = [pl.BlockSpec((tm, d), lambda j, i, te, nv: (tile(i, nv), 0))]
    in_specs += [pl.BlockSpec((None, w.shape[1], tn), lambda j, i, te, nv: (te[tile(i, nv)], 0, j)) for w in weights]
    grid_spec = pltpu.PrefetchScalarGridSpec(
        num_scalar_prefetch=2,
        grid=(n_col_tiles, rows // tm),
        in_specs=in_specs,
        out_specs=pl.BlockSpec((tm, out_tn), lambda j, i, te, nv: (i, j)),
    )
    return pl.pallas_call(
        kernel,
        grid_spec=grid_spec,
        out_shape=jax.ShapeDtypeStruct((rows, n_col_tiles * out_tn), out_dtype),
        compiler_params=_cparams(("arbitrary", "arbitrary")),
        name=name,
    )(tile_expert, n_valid, x, *weights)


def _combine_kernel(pos_ref, next_pos_ref, x_ref, slot_ref, yg_ref, o_ref, buf_ref, sems):
    tt = x_ref.shape[0]
    i = pl.program_id(0)
    cur = i % 2

    def gather(p_ref, slot):
        def copies(r):
            return (_row_copy(yg_ref, p_ref[0, r], buf_ref.at[slot, 0], r, sems.at[slot]),
                    _row_copy(yg_ref, p_ref[0, tt + r], buf_ref.at[slot, 1], r, sems.at[slot]))
        return copies

    def start_all(copies):
        def body(r, carry):
            a, b = copies(r)
            a.start(priority=0)
            b.start(priority=1)
            return carry
        lax.fori_loop(0, tt, body, 0, unroll=_ROW_COPY_UNROLL)

    def wait_all(copies):
        def body(r, carry):
            a, b = copies(r)
            a.wait()
            b.wait()
            return carry
        lax.fori_loop(0, tt, body, 0, unroll=_ROW_COPY_UNROLL)

    @pl.when(i == 0)
    def _():
        start_all(gather(pos_ref, cur))

    @pl.when(i + 1 < pl.num_programs(0))
    def _():
        start_all(gather(next_pos_ref, 1 - cur))

    wait_all(gather(pos_ref, cur))
    rows_ref = buf_ref.at[cur]
    slots = slot_ref[...]
    w_a = slots[:, SLOT_W_A:SLOT_W_A + 1]
    w_b = slots[:, SLOT_W_B:SLOT_W_B + 1]
    half = MOE_DOWN_TN // 2
    for j in range(x_ref.shape[1] // MOE_DOWN_TN):
        lo_a, hi_a = _unpack_bf16_pair(rows_ref[0, :, j * half:(j + 1) * half])
        lo_b, hi_b = _unpack_bf16_pair(rows_ref[1, :, j * half:(j + 1) * half])
        c_lo = slice(j * MOE_DOWN_TN, j * MOE_DOWN_TN + half)
        c_hi = slice(j * MOE_DOWN_TN + half, (j + 1) * MOE_DOWN_TN)
        o_ref[:, c_lo] = x_ref[:, c_lo] + w_a * lo_a + w_b * lo_b
        o_ref[:, c_hi] = x_ref[:, c_hi] + w_a * hi_a + w_b * hi_b


def moe_combine(x, slots, pos_tiles, yg):
    m, d = x.shape
    tt = pos_tiles.shape[2] // 2
    n_steps = m // tt
    pos_spec = lambda step: pl.BlockSpec((None, 1, 2 * tt), lambda i: (step(i), 0, 0), memory_space=pltpu.SMEM)
    return pl.pallas_call(
        _combine_kernel,
        grid=(n_steps,),
        in_specs=[pos_spec(lambda i: i),
                  pos_spec(lambda i: jnp.minimum(i + 1, n_steps - 1)),
                  pl.BlockSpec((tt, d), lambda i: (i, 0)),
                  pl.BlockSpec((tt, LANES), lambda i: (i, 0)),
                  pl.BlockSpec(memory_space=pl.ANY)],
        out_specs=pl.BlockSpec((tt, d), lambda i: (i, 0)),
        out_shape=jax.ShapeDtypeStruct((m, d), F32),
        scratch_shapes=[pltpu.VMEM((2, 2, tt, yg.shape[1]), yg.dtype), pltpu.SemaphoreType.DMA((2,))],
        compiler_params=_cparams(("arbitrary",)),
        name="moe_combine",
    )(pos_tiles, pos_tiles, x, slots, yg)


def moe_layer(x2, norm_g, w_router, w1, w3, w2):
    m, d = x2.shape
    tm = min(MOE_TM, m)
    tt = min(MOE_TT, m)
    n_tiles = (TOP_K * m) // tm + N_EXPERTS
    rows = n_tiles * tm
    h, gates, sel = moe_router(x2, norm_g, w_router)
    rank, counts = moe_rank(sel)

    cnt = counts[0, :N_EXPERTS].astype(I32)
    padded = ((cnt + tm - 1) // tm) * tm
    ends = jnp.cumsum(padded)
    starts = ends - padded
    start_row = jnp.zeros((1, LANES), F32).at[0, :N_EXPERTS].set(starts.astype(F32))
    tile_expert = jnp.minimum(jnp.searchsorted(ends, jnp.arange(n_tiles, dtype=I32) * tm, side="right"),
                              N_EXPERTS - 1).astype(I32)
    n_valid = (ends[-1:] // tm).astype(I32)
    unused = ends[-1] + jnp.arange(N_EXPERTS, dtype=I32) * tm
    tails = jnp.concatenate([jnp.where(padded > 0, ends - tm, -1),
                             jnp.where(unused < rows, unused, -1)]).astype(I32)

    slots = moe_slots(rank, sel, gates, start_row)
    pos = slots[:, :2].astype(I32).reshape(m // tt, tt, 2)
    pos_tiles = jnp.swapaxes(pos, 1, 2).reshape(m // tt, 1, 2 * tt)

    xg = moe_dispatch(h, pos_tiles, tails, rows, tm)
    hid = _grouped_call(_glu_grouped_kernel, xg, [w1, w3], tile_expert, n_valid,
                        w1.shape[2] // MOE_UP_TN, MOE_UP_TN, MOE_UP_TN, BF16, tm, "moe_glu")
    yg = _grouped_call(_down_grouped_kernel, hid, [w2], tile_expert, n_valid,
                       d // MOE_DOWN_TN, MOE_DOWN_TN, MOE_DOWN_TN // 2, U32, tm, "moe_down")
    return moe_combine(x2, slots, pos_tiles, yg)


def _split_in_proj(w_in_l):
    seg = lambda lo, hi: w_in_l[:, lo:hi]
    big = jnp.concatenate([
        seg(O_AQ, O_AZ), seg(O_AZ, O_AA), seg(O_BQ, O_BCKV), seg(O_CB, O_DZ),
        seg(O_DZ, O_DXBC), seg(O_DXBC, O_DDT), seg(O_BQI, O_BKI), seg(O_BCKV, O_BQI)], axis=1).astype(BF16)
    pad = jnp.zeros((w_in_l.shape[0], S_COLS - (S_DDT + H_D)), w_in_l.dtype)
    small = jnp.concatenate([
        seg(O_AA, O_AB), seg(O_AB, O_BQ), seg(O_BKI, O_BWI), seg(O_BWI, O_CB), seg(O_DDT, _O_END), pad],
        axis=1).astype(BF16)
    gate = w_in_l[:, MIX_COLS:].astype(BF16)
    return big, small, gate


def _mixer_layer(x2, bsz, t, mix_norm_g, w_in_l, conv_a_w, a_log_a, dt_bias_a, out_norm_a_g, ckv_norm_g, w_uk,
                 q_norm_b_g, k_norm_b_g, w_uv, bias_tables, conv_c_w, conv_d_w, conv_d_b, a_log_d,
                 dt_bias_d, d_skip, out_norm_d_g, w_up_all, w_out_all, layer):
    m = bsz * t
    xn = rmsnorm(x2, mix_norm_g)
    w_big, w_small, w_gate = _split_in_proj(w_in_l)
    p3 = matmul(xn, w_big, out_dtype=BF16, tm=1024, tn=768, name="in_proj").reshape(bsz, t, P_COLS)
    small3 = matmul(xn, w_small, out_dtype=F32, tm=1024, tn=S_COLS, name="in_proj_small").reshape(bsz, t, S_COLS)

    o_a = gated_deltanet(p3, small3, conv_a_w, a_log_a, dt_bias_a, out_norm_a_g)

    qh3, kk3, ct3, ki3 = dsa_prep(p3, small3, w_uk, ckv_norm_g, q_norm_b_g, k_norm_b_g)
    dprev, ddiag = bias_tables
    o_b = dsa_attention(qh3, p3, small3, kk3, ct3, ki3, dprev, ddiag,
                        jnp.swapaxes(w_uv, 1, 2).astype(BF16))

    o_c = _conv_call(_conv_gated_kernel, p3, [P_C, P_C + BRANCH_WIDTH, P_C + 2 * BRANCH_WIDTH], BRANCH_WIDTH,
                     [conv_c_w.astype(F32)], BF16, 512, "conv_gated")

    xbc3 = _conv_call(_conv_xbc_kernel, p3, [P_DXBC], XBC_WIDTH,
                      [conv_d_w.astype(F32), conv_d_b.reshape(1, XBC_WIDTH).astype(F32)], F32, 512, "conv_xbc")
    o_d = mamba2_ssd(xbc3, p3, small3, a_log_d, dt_bias_d, d_skip, out_norm_d_g)

    outs = [o.reshape(m, BRANCH_WIDTH) for o in (o_a, o_b, o_c, o_d)]
    merged = merge_branches(xn, outs, w_gate, w_up_all, layer)
    return matmul_streamed(merged, w_out_all, x2, layer)


def matmul_streamed(a, b_all, residual, layer, tm=1024, tn=512):
    m, kdim = a.shape
    n = b_all.shape[-1]
    tm, tn = min(tm, m), min(tn, n)
    deep = pl.Buffered(3)

    def body(a_ref, b_ref, r_ref, o_ref):
        o_ref[...] = _dot(a_ref[...], b_ref[...].astype(BF16)) + r_ref[...]

    def outer(a_hbm, b_hbm, r_hbm, o_hbm):
        pltpu.emit_pipeline(
            body,
            grid=(m // tm, n // tn),
            in_specs=[pl.BlockSpec((tm, kdim), lambda i, j: (i, 0)),
                      pl.BlockSpec((None, kdim, tn), lambda i, j: (layer, 0, j), pipeline_mode=deep),
                      pl.BlockSpec((tm, tn), lambda i, j: (i, j), pipeline_mode=deep)],
            out_specs=pl.BlockSpec((tm, tn), lambda i, j: (i, j)),
        )(a_hbm, b_hbm, r_hbm, o_hbm)

    any_spec = pl.BlockSpec(memory_space=pl.ANY)
    return pl.pallas_call(
        outer,
        in_specs=[any_spec, any_spec, any_spec],
        out_specs=any_spec,
        out_shape=jax.ShapeDtypeStruct((m, n), F32),
        compiler_params=pltpu.CompilerParams(vmem_limit_bytes=VMEM_LIMIT_MB * 2 ** 20),
        name="out_proj",
    )(a, b_all, residual)


def kernel(x, mix_norm_g, w_in, conv_a_w, a_log_a, dt_bias_a, out_norm_a_g, ckv_norm_g, w_uk, q_norm_b_g, k_norm_b_g, w_uv, rel_bias, conv_c_w, conv_d_w, conv_d_b, a_log_d, dt_bias_d, d_skip, out_norm_d_g, w_up, w_out, ffn_norm_g, w1_dense, w3_dense, w2_dense, w_router, w1_moe, w3_moe, w2_moe):
    bsz, t, d = x.shape
    depth = w_in.shape[0]
    x2 = x.reshape(bsz * t, d)
    bias_tables = relbias_tables(rel_bias)
    for l in range(depth):
        x2 = _mixer_layer(x2, bsz, t, mix_norm_g[l], w_in[l], conv_a_w[l], a_log_a[l], dt_bias_a[l],
                          out_norm_a_g[l], ckv_norm_g[l], w_uk[l], q_norm_b_g[l], k_norm_b_g[l], w_uv[l],
                          bias_tables, conv_c_w[l], conv_d_w[l], conv_d_b[l], a_log_d[l],
                          dt_bias_d[l], d_skip[l], out_norm_d_g[l], w_up, w_out, l)
        j = l // 2
        if l % 2 == 0:
            h = rmsnorm(x2, ffn_norm_g[l])
            hid = glu_dense(h, w1_dense[j], w3_dense[j])
            x2 = matmul(hid, w2_dense[j].astype(BF16), out_dtype=F32, residual=x2,
                        tm=512, tn=512, name="ffn_down")
        else:
            x2 = moe_layer(x2, ffn_norm_g[l], w_router[j], w1_moe[j], w3_moe[j], w2_moe[j])
    return x2.reshape(bsz, t, d)
```
